```python
import jax
import jax.numpy as jnp
from jax import lax
import numpy as np

D_MODEL = 2048
BATCH = 8
SEQ = 2048
DEPTH = 1
DEC_BATCH = 32
DEC_SEQ = 1
PAST_LEN = 8192
PAGE_SIZE = 128

C_CONV = D_MODEL // 2
CONV_W = 31
N_HEADS = 16
HEAD_DIM = 64
KV_HEADS = 4
IDX_HEADS = 16
IDX_DIM = 64
MAX_TOPK = 256
QBLK = 128
N_GROUPS = 8
EXPERTS_PER_GROUP = 8
N_EXPERTS = N_GROUPS * EXPERTS_PER_GROUP
EXPERT_TOPK = 2
D_EXPERT = D_MODEL // 4
MOE_BLK = 128
LN_EPS = 1e-5
DEEPNORM_ALPHA = (2.0 * DEPTH) ** 0.25
DEEPNORM_BETA = (8.0 * DEPTH) ** -0.25
NEG_INF = -1e30
IN_SIZES = (C_CONV, C_CONV, N_HEADS * HEAD_DIM, KV_HEADS * HEAD_DIM, KV_HEADS * HEAD_DIM,
            IDX_HEADS * IDX_DIM, IDX_DIM, IDX_HEADS, D_MODEL, D_MODEL)

kernel_name = "hybrid_conformer_dsa_hmoe_step"


def layer_norm(x, g, b):
    xf = x.astype(jnp.float32)
    mu = jnp.mean(xf, axis=-1, keepdims=True)
    var = jnp.mean(jnp.square(xf - mu), axis=-1, keepdims=True)
    y = (xf - mu) * lax.rsqrt(var + LN_EPS) * g.astype(jnp.float32) + b.astype(jnp.float32)
    return y.astype(x.dtype)


def alibi_slopes():
    return 2.0 ** (-8.0 * jnp.arange(1, N_HEADS + 1, dtype=jnp.float32) / N_HEADS)


def in_proj(x, w_in):
    h = jnp.einsum('ntd,dc->ntc', x, w_in)
    parts = []
    off = 0
    for size in IN_SIZES:
        parts.append(h[..., off:off + size])
        off += size
    return parts


def conv_branch(u_ext, dw_w, dw_b, ln_g, ln_b, w_conv_out):
    y = lax.conv_general_dilated(u_ext, dw_w[:, None, :].astype(u_ext.dtype), window_strides=(1,),
                                 padding='VALID', dimension_numbers=('NWC', 'WIO', 'NWC'),
                                 feature_group_count=C_CONV) + dw_b
    y = jax.nn.silu(layer_norm(y, ln_g, ln_b))
    return jnp.einsum('ntc,cd->ntd', y, w_conv_out)


def sparse_attention(q, qi, wi, ki_all, q_pos, gather_kv, topk):
    N, T = q.shape[:2]
    L = ki_all.shape[1]
    dots = jnp.einsum('nqhd,nsd->nqhs', qi, ki_all).astype(jnp.float32) * IDX_DIM ** -0.5
    score = jnp.einsum('nqh,nqhs->nqs', wi.astype(jnp.float32) * IDX_HEADS ** -0.5, jax.nn.relu(dots))
    admissible = jnp.arange(L, dtype=jnp.int32)[None, :] <= q_pos[:, None]
    score = jnp.where(admissible[None], score, NEG_INF)
    _, idx = lax.top_k(score, topk)
    valid = idx <= q_pos[None, :, None]
    k_sel, v_sel = gather_kv(idx)
    qg = q.reshape(N, T, KV_HEADS, N_HEADS // KV_HEADS, HEAD_DIM)
    s = jnp.einsum('nqgrd,nqkgd->nqgrk', qg, k_sel).astype(jnp.float32) * HEAD_DIM ** -0.5
    dist = (q_pos[None, :, None] - idx).astype(jnp.float32)
    slopes = alibi_slopes().reshape(KV_HEADS, N_HEADS // KV_HEADS)
    s = s - slopes[None, None, :, :, None] * dist[:, :, None, None, :]
    s = jnp.where(valid[:, :, None, None, :], s, NEG_INF)
    p = jax.nn.softmax(s, axis=-1).astype(v_sel.dtype)
    o = jnp.einsum('nqgrk,nqkgd->nqgrd', p, v_sel)
    return o.reshape(N, T, N_HEADS, HEAD_DIM)


def attend_prompt(q, qi, wi, k, v, ki):
    N, T = q.shape[:2]
    nb = T // QBLK
    topk = min(MAX_TOPK, T // 4)
    bidx = jnp.arange(N)[:, None, None]

    def gather(idx):
        return k[bidx, idx], v[bidx, idx]

    def block_fn(args):
        qb, qib, wib, pos = args
        return sparse_attention(qb, qib, wib, ki, pos, gather, topk)

    def to_blocks(a):
        return a.reshape((N, nb, QBLK) + a.shape[2:]).swapaxes(0, 1)

    pos_blocks = jnp.arange(T, dtype=jnp.int32).reshape(nb, QBLK)
    out = lax.map(block_fn, (to_blocks(q), to_blocks(qi), to_blocks(wi), pos_blocks))
    return out.swapaxes(0, 1).reshape(N, T, N_HEADS, HEAD_DIM)


def make_attend_sample(cache_k, cache_v, cache_kidx, page_table):
    def attend(q, qi, wi, k, v, ki):
        N, T = q.shape[:2]
        past = page_table.shape[1] * PAGE_SIZE
        topk = min(MAX_TOPK, (past + T) // 4)
        ki_past = cache_kidx[page_table].reshape(N, past, IDX_DIM)
        ki_all = jnp.concatenate([ki_past.astype(ki.dtype), ki], axis=1)
        bidx = jnp.arange(N)[:, None, None]

        def gather(idx):
            pidx = jnp.minimum(idx, past - 1)
            phys = page_table[bidx, pidx // PAGE_SIZE]
            off = pidx % PAGE_SIZE
            nidx = jnp.clip(idx - past, 0, T - 1)
            is_past = (idx < past)[..., None, None]
            k_sel = jnp.where(is_past, cache_k[phys, off].astype(k.dtype), k[bidx, nidx])
            v_sel = jnp.where(is_past, cache_v[phys, off].astype(v.dtype), v[bidx, nidx])
            return k_sel, v_sel

        q_pos = past + jnp.arange(T, dtype=jnp.int32)
        return sparse_attention(q, qi, wi, ki_all, q_pos, gather, topk)
    return attend


def moe_ffn(h, w_rg, w_re, w_g, w_u, w_d):
    T = h.shape[0]
    hf = h.astype(jnp.float32)
    p_grp = jax.nn.softmax(hf @ w_rg.astype(jnp.float32), axis=-1)
    g_top, g_idx = lax.top_k(p_grp, 1)
    logits_e = (hf @ w_re.astype(jnp.float32)).reshape(T, N_GROUPS, EXPERTS_PER_GROUP)
    le = logits_e[jnp.arange(T), g_idx[:, 0]]
    e_top, e_idx = lax.top_k(jax.nn.softmax(le, axis=-1), EXPERT_TOPK)
    gates = g_top * e_top / jnp.sum(e_top, axis=-1, keepdims=True)
    A = T * EXPERT_TOPK
    eid = (g_idx * EXPERTS_PER_GROUP + e_idx).reshape(A).astype(jnp.int32)
    tok = jnp.broadcast_to(jnp.arange(T, dtype=jnp.int32)[:, None], (T, EXPERT_TOPK)).reshape(A)
    wt = gates.reshape(A)
    order = jnp.argsort(eid)
    eid_s, tok_s, wt_s = eid[order], tok[order], wt[order]
    counts = jnp.zeros((N_EXPERTS,), jnp.int32).at[eid].add(1)
    start = jnp.cumsum(counts) - counts
    padded = (counts + MOE_BLK - 1) // MOE_BLK * MOE_BLK
    pad_end = jnp.cumsum(padded)
    pad_start = pad_end - padded
    dest = pad_start[eid_s] + (jnp.arange(A, dtype=jnp.int32) - start[eid_s])
    n_blocks = (A + N_EXPERTS * (MOE_BLK - 1) + MOE_BLK - 1) // MOE_BLK
    n_rows = n_blocks * MOE_BLK
    row_tok = jnp.full((n_rows,), T, jnp.int32).at[dest].set(tok_s)
    row_w = jnp.zeros((n_rows,), h.dtype).at[dest].set(wt_s.astype(h.dtype))
    block_exp = jnp.minimum(jnp.searchsorted(pad_end, jnp.arange(n_blocks, dtype=jnp.int32) * MOE_BLK,
                                             side='right'), N_EXPERTS - 1)
    h_pad = jnp.concatenate([h, jnp.zeros((1, h.shape[1]), h.dtype)], axis=0)
    xb = h_pad[row_tok].reshape(n_blocks, MOE_BLK, h.shape[1])

    def expert_block(args):
        xblk, e = args
        return (jax.nn.silu(xblk @ w_g[e]) * (xblk @ w_u[e])) @ w_d[e]

    yb = lax.map(expert_block, (xb, block_exp)).reshape(n_rows, h.shape[1])
    return jax.ops.segment_sum(yb * row_w[:, None], row_tok, num_segments=T + 1)[:T]


def trunk_layer(x, conv_buf, attend, w_in, conv_dw_w, conv_dw_b, conv_ln_g, conv_ln_b, w_conv_out,
                w_attn_out, w_out, ln1_g, ln1_b, w_router_group, w_router_expert, w_expert_gate,
                w_expert_up, w_expert_down, ln2_g, ln2_b):
    N, T, D = x.shape
    glu_a, glu_g, q, k, v, qi, ki, wi, gate_c, gate_a = in_proj(x, w_in)
    u = glu_a * jax.nn.sigmoid(glu_g)
    u_ext = jnp.concatenate([conv_buf.astype(u.dtype), u], axis=1)
    conv_out = conv_branch(u_ext, conv_dw_w, conv_dw_b, conv_ln_g, conv_ln_b, w_conv_out)
    q = q.reshape(N, T, N_HEADS, HEAD_DIM)
    k = k.reshape(N, T, KV_HEADS, HEAD_DIM)
    v = v.reshape(N, T, KV_HEADS, HEAD_DIM)
    qi = qi.reshape(N, T, IDX_HEADS, IDX_DIM)
    o = attend(q, qi, wi, k, v, ki).reshape(N, T, N_HEADS * HEAD_DIM)
    attn_out = jnp.einsum('nte,ed->ntd', o, w_attn_out)
    merged = jax.nn.sigmoid(gate_c) * conv_out + jax.nn.sigmoid(gate_a) * attn_out
    mix = jnp.einsum('ntd,de->nte', merged, w_out)
    x1 = layer_norm(DEEPNORM_ALPHA * x + mix, ln1_g, ln1_b)
    ffn = moe_ffn(x1.reshape(N * T, D), w_router_group, w_router_expert, w_expert_gate,
                  w_expert_up, w_expert_down).reshape(N, T, D)
    y = layer_norm(DEEPNORM_ALPHA * x1 + ffn, ln2_g, ln2_b)
    return y, k, v, ki, u_ext[:, -(CONV_W - 1):]


def setup_inputs(seed: int = 0) -> dict:
    key = jax.random.key(seed)
    ks = jax.random.split(key, 32)
    nrm = jax.random.normal
    n_pages = PAST_LEN // PAGE_SIZE
    n_used = DEC_BATCH * n_pages
    n_pool = n_used + max(1, n_used // 4)
    x_prompt = nrm(ks[0], (BATCH, SEQ, D_MODEL), jnp.float32)
    x_sample = nrm(ks[1], (DEC_BATCH, DEC_SEQ, D_MODEL), jnp.float32)
    cache_k = nrm(ks[2], (DEPTH, n_pool, PAGE_SIZE, KV_HEADS, HEAD_DIM), jnp.float32)
    cache_v = nrm(ks[3], (DEPTH, n_pool, PAGE_SIZE, KV_HEADS, HEAD_DIM), jnp.float32)
    cache_kidx = nrm(ks[4], (DEPTH, n_pool, PAGE_SIZE, IDX_DIM), jnp.float32)
    state_conv = 0.5 * nrm(ks[5], (DEPTH, DEC_BATCH, CONV_W - 1, C_CONV), jnp.float32)
    page_table = jax.random.permutation(ks[6], n_pool)[:n_used].reshape(DEC_BATCH, n_pages).astype(jnp.int32)
    col_scales = (DEEPNORM_BETA, 1.0, 1.0, 1.0, DEEPNORM_BETA, 1.0, 1.0, 1.0, 1.0, 1.0)
    col_scale = jnp.concatenate([jnp.full((s,), c, jnp.float32) for s, c in zip(IN_SIZES, col_scales)])
    w_in = nrm(ks[7], (DEPTH, D_MODEL, sum(IN_SIZES)), jnp.float32) * D_MODEL ** -0.5 * col_scale
    conv_dw_w = nrm(ks[8], (DEPTH, CONV_W, C_CONV), jnp.float32) * CONV_W ** -0.5
    conv_dw_b = 0.02 * nrm(ks[9], (DEPTH, C_CONV), jnp.float32)
    conv_ln_g = 1.0 + 0.02 * nrm(ks[10], (DEPTH, C_CONV), jnp.float32)
    conv_ln_b = 0.02 * nrm(ks[11], (DEPTH, C_CONV), jnp.float32)
    w_conv_out = nrm(ks[12], (DEPTH, C_CONV, D_MODEL), jnp.float32) * C_CONV ** -0.5 * DEEPNORM_BETA
    w_attn_out = nrm(ks[13], (DEPTH, N_HEADS * HEAD_DIM, D_MODEL), jnp.float32) * (N_HEADS * HEAD_DIM) ** -0.5 * DEEPNORM_BETA
    w_out = nrm(ks[14], (DEPTH, D_MODEL, D_MODEL), jnp.float32) * D_MODEL ** -0.5 * DEEPNORM_BETA
    ln1_g = 1.0 + 0.02 * nrm(ks[15], (DEPTH, D_MODEL), jnp.float32)
    ln1_b = 0.02 * nrm(ks[16], (DEPTH, D_MODEL), jnp.float32)
    w_router_group = nrm(ks[17], (DEPTH, D_MODEL, N_GROUPS), jnp.float32) * D_MODEL ** -0.5
    w_router_expert = nrm(ks[18], (DEPTH, D_MODEL, N_EXPERTS), jnp.float32) * D_MODEL ** -0.5
    w_expert_gate = nrm(ks[19], (DEPTH, N_EXPERTS, D_MODEL, D_EXPERT), jnp.float32) * D_MODEL ** -0.5
    w_expert_up = nrm(ks[20], (DEPTH, N_EXPERTS, D_MODEL, D_EXPERT), jnp.float32) * D_MODEL ** -0.5 * DEEPNORM_BETA
    w_expert_down = nrm(ks[21], (DEPTH, N_EXPERTS, D_EXPERT, D_MODEL), jnp.float32) * D_EXPERT ** -0.5 * DEEPNORM_BETA
    ln2_g = 1.0 + 0.02 * nrm(ks[22], (DEPTH, D_MODEL), jnp.float32)
    ln2_b = 0.02 * nrm(ks[23], (DEPTH, D_MODEL), jnp.float32)
    return {"x_prompt": x_prompt, "x_sample": x_sample, "cache_k": cache_k, "cache_v": cache_v,
            "cache_kidx": cache_kidx, "state_conv": state_conv, "page_table": page_table,
            "w_in": w_in, "conv_dw_w": conv_dw_w, "conv_dw_b": conv_dw_b, "conv_ln_g": conv_ln_g,
            "conv_ln_b": conv_ln_b, "w_conv_out": w_conv_out, "w_attn_out": w_attn_out, "w_out": w_out,
            "ln1_g": ln1_g, "ln1_b": ln1_b, "w_router_group": w_router_group,
            "w_router_expert": w_router_expert, "w_expert_gate": w_expert_gate,
            "w_expert_up": w_expert_up, "w_expert_down": w_expert_down, "ln2_g": ln2_g, "ln2_b": ln2_b}


def reference(x_prompt, x_sample, cache_k, cache_v, cache_kidx, state_conv, page_table, w_in, conv_dw_w,
              conv_dw_b, conv_ln_g, conv_ln_b, w_conv_out, w_attn_out, w_out, ln1_g, ln1_b, w_router_group,
              w_router_expert, w_expert_gate, w_expert_up, w_expert_down, ln2_g, ln2_b):
    xp = x_prompt
    xs = x_sample
    kp, vp, kip, cp, ksl, vsl, kisl, csl = [], [], [], [], [], [], [], []
    for l in range(DEPTH):
        lw = (w_in[l], conv_dw_w[l], conv_dw_b[l], conv_ln_g[l], conv_ln_b[l], w_conv_out[l], w_attn_out[l],
              w_out[l], ln1_g[l], ln1_b[l], w_router_group[l], w_router_expert[l], w_expert_gate[l],
              w_expert_up[l], w_expert_down[l], ln2_g[l], ln2_b[l])
        buf0 = jnp.zeros((xp.shape[0], CONV_W - 1, C_CONV), xp.dtype)
        xp, k1, v1, ki1, c1 = trunk_layer(xp, buf0, attend_prompt, *lw)
        attend_s = make_attend_sample(cache_k[l], cache_v[l], cache_kidx[l], page_table)
        xs, k2, v2, ki2, c2 = trunk_layer(xs, state_conv[l], attend_s, *lw)
        kp.append(k1); vp.append(v1); kip.append(ki1); cp.append(c1)
        ksl.append(k2); vsl.append(v2); kisl.append(ki2); csl.append(c2)
    y_prompt = xp
    y_sample = xs
    k_prompt = jnp.stack(kp)
    v_prompt = jnp.stack(vp)
    kidx_prompt = jnp.stack(kip)
    conv_prompt = jnp.stack(cp)
    k_sample = jnp.stack(ksl)
    v_sample = jnp.stack(vsl)
    kidx_sample = jnp.stack(kisl)
    conv_sample = jnp.stack(csl)
    return (y_prompt, y_sample, k_prompt, v_prompt, kidx_prompt, conv_prompt, k_sample, v_sample, kidx_sample, conv_sample)
```

```python
import functools

import jax
import jax.numpy as jnp
import numpy as np
from jax import lax
from jax.experimental import pallas as pl
from jax.experimental.pallas import tpu as pltpu

F32 = jnp.float32
BF16 = jnp.bfloat16
I32 = jnp.int32

D_MODEL = 2048
C_CONV = 1024
CONV_W = 31
N_HEADS = 16
HEAD_DIM = 64
KV_HEADS = 4
HEADS_PER_KV = N_HEADS // KV_HEADS
IDX_HEADS = 16
IDX_DIM = 64
MAX_TOPK = 256
N_GROUPS = 8
EXPERTS_PER_GROUP = 8
N_EXPERTS = 64
D_EXPERT = 512
MOE_BLK = 128
PAGE_SIZE = 128
LN_EPS = 1e-5
DEEPNORM_ALPHA = 2.0 ** 0.25
NEG_INF = -1e30
INT_MIN = -(2 ** 31)
NEG_KEY = int(INT_MIN - int(np.array(NEG_INF, np.float32).view(np.int32)))

LANES = 128
IN_TN = 512
VMEM_LIMIT = 56 * 1024 * 1024


def _cparams(sem, vmem=VMEM_LIMIT):
    return pltpu.CompilerParams(dimension_semantics=sem, vmem_limit_bytes=vmem)


def _float_key(x):
    b = pltpu.bitcast(x, I32)
    return jnp.where(b < 0, INT_MIN - b, b)


def _layer_norm(z, g, b):
    mu = jnp.mean(z, axis=-1, keepdims=True)
    zc = z - mu
    var = jnp.mean(zc * zc, axis=-1, keepdims=True)
    return zc * lax.rsqrt(var + LN_EPS) * g + b


def _dot_nt(a, b):
    return lax.dot_general(a, b, (((1,), (1,)), ((), ())), preferred_element_type=F32)


_T_GLU, _T_Q, _T_QI, _T_KV, _T_GC, _T_GA, _T_KI, _N_TILES = 0, 4, 6, 8, 9, 13, 17, 18


def _arrange_w_in(w):
    sizes = (C_CONV, C_CONV, N_HEADS * HEAD_DIM, KV_HEADS * HEAD_DIM, KV_HEADS * HEAD_DIM,
             IDX_HEADS * IDX_DIM, IDX_DIM, IDX_HEADS, D_MODEL, D_MODEL)
    offs = np.cumsum((0,) + sizes)
    a, g, q, k, v, qi, ki, wi, gc, ga = [w[:, offs[i]:offs[i + 1]] for i in range(10)]
    half = IN_TN // 2
    glu = [jnp.concatenate([a[:, t * half:(t + 1) * half], g[:, t * half:(t + 1) * half]], axis=1)
           for t in range(C_CONV // half)]
    tail = jnp.concatenate([ki, wi, jnp.zeros((w.shape[0], IN_TN - IDX_DIM - IDX_HEADS), w.dtype)], axis=1)
    return jnp.concatenate(glu + [q, qi, k, v, gc, ga, tail], axis=1).astype(BF16)


def _in_proj_kernel(x_ref, w_ref, u_ref, q_ref, qi_ref, kv_ref, kvb_ref, gc_ref, ga_ref, kiwi_ref, kib_ref,
                    xb_ref):
    j = pl.program_id(1)

    @pl.when(j == 0)
    def _():
        xb_ref[...] = x_ref[...].astype(BF16)

    def mm(ncols=IN_TN):
        return jnp.dot(xb_ref[...], w_ref[:, :ncols], preferred_element_type=F32)

    @pl.when(j < _T_Q)
    def _():
        acc = mm()
        u_ref[...] = acc[:, :IN_TN // 2] * jax.nn.sigmoid(acc[:, IN_TN // 2:])

    @pl.when((j >= _T_Q) & (j < _T_QI))
    def _():
        q_ref[...] = (mm() * (HEAD_DIM ** -0.5)).astype(BF16)

    @pl.when((j >= _T_QI) & (j < _T_KV))
    def _():
        qi_ref[...] = mm().astype(BF16)

    @pl.when(j == _T_KV)
    def _():
        acc = mm()
        kv_ref[...] = acc
        kvb_ref[...] = acc.astype(BF16)

    @pl.when((j >= _T_GC) & (j < _T_GA))
    def _():
        gc_ref[...] = jax.nn.sigmoid(mm()).astype(BF16)

    @pl.when((j >= _T_GA) & (j < _T_KI))
    def _():
        ga_ref[...] = jax.nn.sigmoid(mm()).astype(BF16)

    @pl.when(j == _T_KI)
    def _():
        acc = mm(LANES)
        kiwi_ref[...] = acc
        kib_ref[...] = acc.astype(BF16)


def _in_proj(x, w_arr, tm):
    m = x.shape[0]
    half = IN_TN // 2

    def cl(lo, n):
        return lambda i, j: (i, jnp.clip(j - lo, 0, n - 1))

    out_shape = (
        jax.ShapeDtypeStruct((m, C_CONV), F32),
        jax.ShapeDtypeStruct((m, N_HEADS * HEAD_DIM), BF16),
        jax.ShapeDtypeStruct((m, IDX_HEADS * IDX_DIM), BF16),
        jax.ShapeDtypeStruct((m, IN_TN), F32),
        jax.ShapeDtypeStruct((m, IN_TN), BF16),
        jax.ShapeDtypeStruct((m, D_MODEL), BF16),
        jax.ShapeDtypeStruct((m, D_MODEL), BF16),
        jax.ShapeDtypeStruct((m, LANES), F32),
        jax.ShapeDtypeStruct((m, LANES), BF16),
    )
    out_specs = (
        pl.BlockSpec((tm, half), cl(_T_GLU, 4)),
        pl.BlockSpec((tm, IN_TN), cl(_T_Q, 2)),
        pl.BlockSpec((tm, IN_TN), cl(_T_QI, 2)),
        pl.BlockSpec((tm, IN_TN), lambda i, j: (i, 0)),
        pl.BlockSpec((tm, IN_TN), lambda i, j: (i, 0)),
        pl.BlockSpec((tm, IN_TN), cl(_T_GC, 4)),
        pl.BlockSpec((tm, IN_TN), cl(_T_GA, 4)),
        pl.BlockSpec((tm, LANES), lambda i, j: (i, 0)),
        pl.BlockSpec((tm, LANES), lambda i, j: (i, 0)),
    )
    return pl.pallas_call(
        _in_proj_kernel,
        grid=(m // tm, _N_TILES),
        in_specs=[pl.BlockSpec((tm, D_MODEL), lambda i, j: (i, 0)),
                  pl.BlockSpec((D_MODEL, IN_TN), lambda i, j: (0, j))],
        out_specs=out_specs,
        out_shape=out_shape,
        scratch_shapes=[pltpu.VMEM((tm, D_MODEL), BF16)],
        compiler_params=_cparams(("arbitrary", "arbitrary")),
        name="in_proj",
    )(x, w_arr)


CONV_HALO = 32
CONV_STRIP = 32


def _conv_strip(ext_ref, row0, dww_ref, dwb, lng, lnb):
    shift = CONV_HALO - (CONV_W - 1)
    acc = jnp.broadcast_to(dwb, (CONV_STRIP, C_CONV))
    for j in range(CONV_W):
        acc = acc + dww_ref[j:j + 1, :] * ext_ref[row0 + j + shift:row0 + j + shift + CONV_STRIP, :]
    yn = _layer_norm(acc, lng, lnb)
    return yn * jax.nn.sigmoid(yn)


def _conv_prompt_kernel(ucur_ref, uprev_ref, dww_ref, dwb_ref, lng_ref, lnb_ref, y_ref, ext_ref, *, tt):
    tb = pl.program_id(1)
    ext_ref[0:CONV_HALO, :] = jnp.where(tb > 0, uprev_ref[0], 0.0)
    ext_ref[CONV_HALO:, :] = ucur_ref[0]
    dwb, lng, lnb = dwb_ref[...], lng_ref[...], lnb_ref[...]
    for r in range(tt // CONV_STRIP):
        y = _conv_strip(ext_ref, r * CONV_STRIP, dww_ref, dwb, lng, lnb)
        y_ref[0, r * CONV_STRIP:(r + 1) * CONV_STRIP, :] = y.astype(BF16)


def _conv_prompt(u, dww, dwb, lng, lnb, tt=128):
    n, t, c = u.shape
    per = tt // CONV_HALO
    vec = pl.BlockSpec((1, c), lambda b, i: (0, 0))
    return pl.pallas_call(
        functools.partial(_conv_prompt_kernel, tt=tt),
        grid=(n, t // tt),
        in_specs=[pl.BlockSpec((1, tt, c), lambda b, i: (b, i, 0)),
                  pl.BlockSpec((1, CONV_HALO, c), lambda b, i: (b, jnp.maximum(i * per - 1, 0), 0)),
                  pl.BlockSpec((CONV_W, c), lambda b, i: (0, 0)), vec, vec, vec],
        out_specs=pl.BlockSpec((1, tt, c), lambda b, i: (b, i, 0)),
        out_shape=jax.ShapeDtypeStruct((n, t, c), BF16),
        scratch_shapes=[pltpu.VMEM((tt + CONV_HALO, c), F32)],
        compiler_params=_cparams(("arbitrary", "arbitrary")),
        name="conv_prompt",
    )(u, u, dww, dwb, lng, lnb)


def _conv_sample_kernel(state_ref, u_ref, dww_ref, dwb_ref, lng_ref, lnb_ref, y_ref):
    acc = dwb_ref[...] + dww_ref[CONV_W - 1:CONV_W, :] * u_ref[...]
    for j in range(CONV_W - 1):
        acc = acc + dww_ref[j:j + 1, :] * state_ref[j]
    yn = _layer_norm(acc, lng_ref[...], lnb_ref[...])
    y_ref[...] = (yn * jax.nn.sigmoid(yn)).astype(BF16)


def _conv_sample(state, u, dww, dwb, lng, lnb):
    n = u.shape[0]
    return pl.pallas_call(
        _conv_sample_kernel,
        out_shape=jax.ShapeDtypeStruct((n, C_CONV), BF16),
        compiler_params=pltpu.CompilerParams(vmem_limit_bytes=VMEM_LIMIT),
        name="conv_sample",
    )(state, u, dww, dwb, lng, lnb)


ATT_TQ = 256
ATT_CK = 256


def _count_ge(key_ref, nch, cand):
    def body(c, acc):
        m = jnp.where(key_ref[c] >= cand, 1.0, 0.0)
        return acc + m[:, :LANES] + m[:, LANES:]
    acc = lax.fori_loop(0, nch, body, jnp.zeros((cand.shape[0], LANES), F32))
    return jnp.sum(acc, axis=1, keepdims=True)


def _select_threshold(count_fn, shape):
    def body(i, prefix):
        cand = prefix + jnp.left_shift(jnp.int32(1), 31 - i)
        return jnp.where(count_fn(cand) >= float(MAX_TOPK), cand, prefix)
    return lax.fori_loop(0, 32, body, jnp.full(shape, INT_MIN, I32))


def _attn_prompt_kernel(qi_ref, kiwi_ref, kib_ref, q_ref, kvb_ref, o_ref, key_ref, bias_ref):
    tq, ck = ATT_TQ, ATT_CK
    qb = pl.program_id(1)
    nch = qb + 1
    row = qb * tq + lax.broadcasted_iota(I32, (tq, ck), 0)
    coll = lax.broadcasted_iota(I32, (tq, ck), 1)
    wsc = kiwi_ref[0][:, IDX_DIM:IDX_DIM + IDX_HEADS] * (IDX_HEADS ** -0.5 * IDX_DIM ** -0.5)

    def score_chunk(c, carry):
        k0 = pl.multiple_of(c * ck, ck)
        kic = kib_ref[0, pl.ds(k0, ck), :][:, :IDX_DIM]
        acc = jnp.zeros((tq, ck), F32)
        for h in range(IDX_HEADS):
            d = _dot_nt(qi_ref[0, :, h * IDX_DIM:(h + 1) * IDX_DIM], kic)
            acc = acc + wsc[:, h:h + 1] * jnp.maximum(d, 0.0)
        acc = jnp.where(c * ck + coll <= row, acc, NEG_INF)
        key_ref[c] = _float_key(acc)
        return carry

    lax.fori_loop(0, nch, score_chunk, 0)

    thr = _select_threshold(lambda cand: _count_ge(key_ref, nch, cand), (tq, 1))
    n_gt = _count_ge(key_ref, nch, thr + 1)
    n_tie_take = float(MAX_TOPK) - n_gt
    tri = jnp.where(lax.broadcasted_iota(I32, (ck, ck), 0) <= lax.broadcasted_iota(I32, (ck, ck), 1),
                    1.0, 0.0).astype(BF16)

    def bias_chunk(c, seen):
        key = key_ref[c]
        eq = key == thr
        eqf = jnp.where(eq, 1.0, 0.0)
        incl = jnp.dot(eqf.astype(BF16), tri, preferred_element_type=F32)
        before = seen + incl - eqf
        sel = (key > thr) | (eq & (before < n_tie_take))
        bias_ref[c] = jnp.where(sel & (key > NEG_KEY), 0.0, NEG_INF)
        return seen + incl[:, ck - 1:ck]

    lax.fori_loop(0, nch, bias_chunk, jnp.zeros((tq, 1), F32))

    for h in range(N_HEADS):
        g = h // HEADS_PER_KV
        slope = float(2.0 ** (-8.0 * (h + 1) / N_HEADS))
        qh = q_ref[0, :, h * HEAD_DIM:(h + 1) * HEAD_DIM]

        def body(c, carry, g=g, slope=slope, qh=qh):
            m, l, acc = carry
            k0 = pl.multiple_of(c * ck, ck)
            kv = kvb_ref[0, pl.ds(k0, ck), :]
            kc = kv[:, g * HEAD_DIM:(g + 1) * HEAD_DIM]
            vc = kv[:, (KV_HEADS + g) * HEAD_DIM:(KV_HEADS + g + 1) * HEAD_DIM]
            dist = (row - (c * ck + coll)).astype(F32)
            s = _dot_nt(qh, kc) - slope * dist + bias_ref[c]
            m_new = jnp.maximum(m, jnp.max(s, axis=1, keepdims=True))
            p = jnp.exp(s - m_new)
            alpha = jnp.exp(m - m_new)
            l = alpha * l + jnp.sum(p, axis=1, keepdims=True)
            acc = alpha * acc + jnp.dot(p.astype(BF16), vc, preferred_element_type=F32)
            return m_new, l, acc

        m, l, acc = lax.fori_loop(
            0, nch, body,
            (jnp.full((tq, 1), -jnp.inf, F32), jnp.zeros((tq, 1), F32), jnp.zeros((tq, HEAD_DIM), F32)))
        o_ref[0, :, h * HEAD_DIM:(h + 1) * HEAD_DIM] = (acc / l).astype(BF16)


def _attn_prompt(qi, kiwi, kib, q, kvb):
    n, t, _ = q.shape
    tq = ATT_TQ
    return pl.pallas_call(
        _attn_prompt_kernel,
        grid=(n, t // tq),
        in_specs=[pl.BlockSpec((1, tq, IDX_HEADS * IDX_DIM), lambda b, i: (b, i, 0)),
                  pl.BlockSpec((1, tq, LANES), lambda b, i: (b, i, 0)),
                  pl.BlockSpec((1, t, LANES), lambda b, i: (b, 0, 0)),
                  pl.BlockSpec((1, tq, N_HEADS * HEAD_DIM), lambda b, i: (b, i, 0)),
                  pl.BlockSpec((1, t, IN_TN), lambda b, i: (b, 0, 0))],
        out_specs=pl.BlockSpec((1, tq, N_HEADS * HEAD_DIM), lambda b, i: (b, i, 0)),
        out_shape=jax.ShapeDtypeStruct((n, t, N_HEADS * HEAD_DIM), BF16),
        scratch_shapes=[pltpu.VMEM((t // ATT_CK, tq, ATT_CK), I32),
                        pltpu.VMEM((t // ATT_CK, tq, ATT_CK), F32)],
        compiler_params=_cparams(("arbitrary", "arbitrary")),
        name="attn_prompt",
    )(qi, kiwi, kib, q, kvb)


S_ROWS = 128


def _sample_score_kernel(pt_ref, qi_ref, wcol_ref, kinew_ref, cki_hbm, sc_ref, kibuf, sem, *, n_pages):
    b = pl.program_id(0)
    nb = pl.num_programs(0)

    def page_copy(bb, p, slot):
        return pltpu.make_async_copy(cki_hbm.at[0, pt_ref[bb * n_pages + p]], kibuf.at[slot, p], sem.at[slot])

    def start_all(bb, slot):
        def body(p, carry):
            page_copy(bb, p, slot).start()
            return carry
        lax.fori_loop(0, n_pages, body, 0)

    slot = b % 2

    @pl.when(b == 0)
    def _():
        start_all(0, 0)

    @pl.when(b + 1 < nb)
    def _():
        start_all(b + 1, 1 - slot)

    pltpu.make_async_copy(kibuf.at[slot], kibuf.at[slot], sem.at[slot]).wait()

    qi = qi_ref[0]
    wsc = wcol_ref[0] * (IDX_HEADS ** -0.5 * IDX_DIM ** -0.5)

    def body(p, carry):
        kp = kibuf[slot, p].astype(BF16)
        d = _dot_nt(qi, kp)
        sc_ref[0, pl.ds(p, 1), :] = jnp.sum(wsc * jnp.maximum(d, 0.0), axis=0, keepdims=True)
        return carry

    lax.fori_loop(0, n_pages, body, 0)
    dn = jnp.sum(qi.astype(F32) * kinew_ref[0], axis=1, keepdims=True)
    s_new = jnp.sum(wsc * jnp.maximum(dn, 0.0), axis=0, keepdims=True)
    tail_rows = S_ROWS - n_pages
    r = lax.broadcasted_iota(I32, (tail_rows, LANES), 0)
    ln = lax.broadcasted_iota(I32, (tail_rows, LANES), 1)
    sc_ref[0, n_pages:, :] = jnp.where((r == 0) & (ln == 0), s_new, NEG_INF)


def _sample_scores(page_table, qi3, wcol, kinew, cache_kidx):
    n, n_pages = page_table.shape
    grid_spec = pltpu.PrefetchScalarGridSpec(
        num_scalar_prefetch=1,
        grid=(n,),
        in_specs=[pl.BlockSpec((1, IDX_HEADS, IDX_DIM), lambda b, pt: (b, 0, 0)),
                  pl.BlockSpec((1, IDX_HEADS, 1), lambda b, pt: (b, 0, 0)),
                  pl.BlockSpec((1, 1, IDX_DIM), lambda b, pt: (b, 0, 0)),
                  pl.BlockSpec(memory_space=pl.ANY)],
        out_specs=pl.BlockSpec((1, S_ROWS, LANES), lambda b, pt: (b, 0, 0)),
        scratch_shapes=[pltpu.VMEM((2, n_pages, PAGE_SIZE, IDX_DIM), F32),
                        pltpu.SemaphoreType.DMA((2,))],
    )
    return pl.pallas_call(
        functools.partial(_sample_score_kernel, n_pages=n_pages),
        grid_spec=grid_spec,
        out_shape=jax.ShapeDtypeStruct((n, S_ROWS, LANES), F32),
        compiler_params=_cparams(("arbitrary",)),
        name="sample_scores",
    )(page_table.reshape(-1), qi3, wcol, kinew, cache_kidx)


def _prefix_rows(mask_f, tri_l, tri_r):
    incl = jnp.dot(mask_f.astype(BF16), tri_l, preferred_element_type=F32)
    tot = jnp.broadcast_to(incl[:, LANES - 1:LANES], incl.shape).astype(BF16)
    before_rows = jnp.dot(tri_r, tot, preferred_element_type=F32)
    return before_rows + incl - mask_f


def _sample_attn_kernel(pt_ref, sc_ref, q_ref, knew_ref, vnew_ref, ck_hbm, cv_hbm, o_ref,
                        key_ref, thr_ref, take_ref, pos_ref, sel_ref, idxv_ref, idxs_ref, kbuf, vbuf, dist_ref,
                        sem, *, n_pages):
    b = pl.program_id(0)
    n = sc_ref.shape[0]
    past = n_pages * PAGE_SIZE

    @pl.when(b == 0)
    def _():
        key_ref[...] = _float_key(sc_ref[...])

        def count(cand):
            m = jnp.where(key_ref[...] >= cand, 1.0, 0.0)
            return jnp.sum(jnp.sum(m, axis=2, keepdims=True), axis=1, keepdims=True)

        thr = _select_threshold(count, (n, 1, 1))
        thr_ref[...] = jnp.broadcast_to(thr, thr_ref.shape)
        take_ref[...] = jnp.broadcast_to(float(MAX_TOPK) - count(thr + 1), take_ref.shape)

    key = key_ref[b]
    thr = thr_ref[b][:1, :]
    take = take_ref[b][:1, :]
    li = lax.broadcasted_iota(I32, (LANES, LANES), 0)
    lj = lax.broadcasted_iota(I32, (LANES, LANES), 1)
    tri_l = jnp.where(li <= lj, 1.0, 0.0).astype(BF16)
    ri = lax.broadcasted_iota(I32, (S_ROWS, S_ROWS), 0)
    rj = lax.broadcasted_iota(I32, (S_ROWS, S_ROWS), 1)
    tri_r = jnp.where(rj < ri, 1.0, 0.0).astype(BF16)
    eqf = jnp.where(key == thr, 1.0, 0.0)
    eq_before = _prefix_rows(eqf, tri_l, tri_r)
    sel = ((key > thr) | ((key == thr) & (eq_before < take))) & (key > NEG_KEY)
    self_ = jnp.where(sel, 1.0, 0.0)
    sel_ref[...] = self_
    pos_ref[...] = _prefix_rows(self_, tri_l, tri_r)

    slot_iota = lax.broadcasted_iota(I32, (MAX_TOPK, LANES), 0).astype(F32)
    lane_val = lax.broadcasted_iota(I32, (8, LANES), 1)
    row_sel = lax.broadcasted_iota(I32, (8, LANES), 0)
    vals = jnp.where(row_sel == 0, lane_val, jnp.where(row_sel == 1, 1, 0)).astype(F32).astype(BF16)

    def compact(p, acc):
        hit = (pos_ref[pl.ds(p, 1), :] == slot_iota) & (sel_ref[pl.ds(p, 1), :] > 0.5)
        r = _dot_nt(vals, jnp.where(hit, 1.0, 0.0).astype(BF16))
        return acc + r[0:1, :] + lax.convert_element_type(p * LANES, F32) * r[1:2, :]

    idx_row = lax.fori_loop(0, n_pages + 1, compact, jnp.zeros((1, MAX_TOPK), F32))
    idxv_ref[...] = idx_row.astype(I32)
    cp = pltpu.make_async_copy(idxv_ref, idxs_ref, sem.at[0])
    cp.start()
    cp.wait()

    def row_copies(j):
        idx = idxs_ref[0, j]
        pidx = jnp.minimum(idx, past - 1)
        page = pt_ref[b * n_pages + pidx // PAGE_SIZE]
        off = pidx % PAGE_SIZE
        return (pltpu.make_async_copy(ck_hbm.at[0, page, off], kbuf.at[j], sem.at[1]),
                pltpu.make_async_copy(cv_hbm.at[0, page, off], vbuf.at[j], sem.at[1]), idx)

    def issue(j, carry):
        ck_cp, cv_cp, idx = row_copies(j)
        ck_cp.start()
        cv_cp.start()
        dist_ref[j] = jnp.full((1, 1), past - idx, I32).astype(F32)
        return carry

    lax.fori_loop(0, MAX_TOPK, issue, 0)
    pltpu.make_async_copy(kbuf, kbuf, sem.at[1]).wait()
    pltpu.make_async_copy(vbuf, vbuf, sem.at[1]).wait()

    dist = dist_ref[...]
    is_new = dist == 0.0
    kb = jnp.where(is_new, knew_ref[...], kbuf[...])
    vb = jnp.where(is_new, vnew_ref[...], vbuf[...])
    gidx = lax.broadcasted_iota(I32, (1, KV_HEADS, 1), 1)
    for r in range(HEADS_PER_KV):
        qr = q_ref[0, r].astype(F32)
        slope = jnp.exp2(-8.0 * (gidx * HEADS_PER_KV + r + 1).astype(F32) / N_HEADS)
        s = jnp.sum(kb * qr[None], axis=2, keepdims=True) - slope * dist
        s = jnp.where(dist >= 0.0, s, NEG_INF)
        m = jnp.max(s, axis=0, keepdims=True)
        p = jnp.exp(s - m)
        l = jnp.sum(p, axis=0, keepdims=True)
        o = jnp.sum(p * vb, axis=0, keepdims=True) / l
        o_ref[0, r] = o[0]


def _sample_attn(page_table, scores, q4, knew, vnew, cache_k, cache_v):
    n, n_pages = page_table.shape
    grid_spec = pltpu.PrefetchScalarGridSpec(
        num_scalar_prefetch=1,
        grid=(n,),
        in_specs=[pl.BlockSpec((n, S_ROWS, LANES), lambda b, pt: (0, 0, 0)),
                  pl.BlockSpec((1, HEADS_PER_KV, KV_HEADS, HEAD_DIM), lambda b, pt: (b, 0, 0, 0)),
                  pl.BlockSpec((1, KV_HEADS, HEAD_DIM), lambda b, pt: (b, 0, 0)),
                  pl.BlockSpec((1, KV_HEADS, HEAD_DIM), lambda b, pt: (b, 0, 0)),
                  pl.BlockSpec(memory_space=pl.ANY),
                  pl.BlockSpec(memory_space=pl.ANY)],
        out_specs=pl.BlockSpec((1, HEADS_PER_KV, KV_HEADS, HEAD_DIM), lambda b, pt: (b, 0, 0, 0)),
        scratch_shapes=[pltpu.VMEM((n, S_ROWS, LANES), I32),
                        pltpu.VMEM((n, 8, LANES), I32),
                        pltpu.VMEM((n, 8, LANES), F32),
                        pltpu.VMEM((S_ROWS, LANES), F32),
                        pltpu.VMEM((S_ROWS, LANES), F32),
                        pltpu.VMEM((1, MAX_TOPK), I32),
                        pltpu.SMEM((1, MAX_TOPK), I32),
                        pltpu.VMEM((MAX_TOPK, KV_HEADS, HEAD_DIM), F32),
                        pltpu.VMEM((MAX_TOPK, KV_HEADS, HEAD_DIM), F32),
                        pltpu.VMEM((MAX_TOPK, 1, 1), F32),
                        pltpu.SemaphoreType.DMA((2,))],
    )
    return pl.pallas_call(
        functools.partial(_sample_attn_kernel, n_pages=n_pages),
        grid_spec=grid_spec,
        out_shape=jax.ShapeDtypeStruct((n, HEADS_PER_KV, KV_HEADS, HEAD_DIM), F32),
        compiler_params=_cparams(("arbitrary",)),
        name="sample_attn",
    )(page_table.reshape(-1), scores, q4, knew, vnew, cache_k, cache_v)


def _merge_ln1_kernel(y_ref, o_ref, gc_ref, ga_ref, x_ref, wc_ref, wa_ref, wo_ref, g_ref, b_ref, wr_ref,
                      x1_ref, ei_ref, gt_ref):
    conv_out = jnp.dot(y_ref[...], wc_ref[...], preferred_element_type=F32)
    attn_out = jnp.dot(o_ref[...], wa_ref[...], preferred_element_type=F32)
    merged = gc_ref[...].astype(F32) * conv_out + ga_ref[...].astype(F32) * attn_out
    mix = jnp.dot(merged.astype(BF16), wo_ref[...], preferred_element_type=F32)
    x1 = _layer_norm(DEEPNORM_ALPHA * x_ref[...] + mix, g_ref[...], b_ref[...])
    x1_ref[...] = x1

    logits = jnp.dot(x1, wr_ref[...], preferred_element_type=F32, precision=lax.Precision.HIGHEST)
    lane = lax.broadcasted_iota(I32, logits.shape, 1)
    is_g = lane < N_GROUPS
    lg = jnp.where(is_g, logits, -jnp.inf)
    eg = jnp.exp(lg - jnp.max(lg, axis=1, keepdims=True))
    pg = eg / jnp.sum(eg, axis=1, keepdims=True)
    g_top = jnp.max(pg, axis=1, keepdims=True)
    g_idx = jnp.min(jnp.where(is_g & (pg == g_top), lane, LANES), axis=1, keepdims=True)
    lo = N_GROUPS + g_idx * EXPERTS_PER_GROUP
    is_e = (lane >= lo) & (lane < lo + EXPERTS_PER_GROUP)
    le = jnp.where(is_e, logits, -jnp.inf)
    ee = jnp.exp(le - jnp.max(le, axis=1, keepdims=True))
    pe = ee / jnp.sum(ee, axis=1, keepdims=True)
    p1 = jnp.max(jnp.where(is_e, pe, -1.0), axis=1, keepdims=True)
    i1 = jnp.min(jnp.where(is_e & (pe == p1), lane, 2 * LANES), axis=1, keepdims=True)
    rest = is_e & (lane != i1)
    p2 = jnp.max(jnp.where(rest, pe, -1.0), axis=1, keepdims=True)
    i2 = jnp.min(jnp.where(rest & (pe == p2), lane, 2 * LANES), axis=1, keepdims=True)
    den = p1 + p2
    ei_ref[...] = jnp.where(lane == 0, i1 - N_GROUPS, jnp.where(lane == 1, i2 - N_GROUPS, 0))
    gt_ref[...] = jnp.where(lane == 0, g_top * p1 / den, jnp.where(lane == 1, g_top * p2 / den, 0.0))


def _merge_ln1(yact, o, gc, ga, x, wc, wa, wo, g1, b1, wr, tm):
    m = x.shape[0]
    row = lambda w: pl.BlockSpec((tm, w), lambda i: (i, 0))
    res = lambda a: pl.BlockSpec(a.shape, lambda i: (0, 0), pipeline_mode=pl.Buffered(1))
    return pl.pallas_call(
        _merge_ln1_kernel,
        grid=(m // tm,),
        in_specs=[row(C_CONV), row(N_HEADS * HEAD_DIM), row(D_MODEL), row(D_MODEL), row(D_MODEL),
                  res(wc), res(wa), res(wo), res(g1), res(b1), res(wr)],
        out_specs=(row(D_MODEL), row(LANES), row(LANES)),
        out_shape=(jax.ShapeDtypeStruct((m, D_MODEL), F32),
                   jax.ShapeDtypeStruct((m, LANES), I32),
                   jax.ShapeDtypeStruct((m, LANES), F32)),
        compiler_params=_cparams(("arbitrary",)),
        name="merge_ln1",
    )(yact, o, gc, ga, x, wc, wa, wo, g1, b1, wr)


def _experts_kernel(bexp_ref, nused_ref, rtok_ref, x_hbm, roww_ref, wg_ref, wu_ref, wd_ref, y_ref,
                    xbuf, wgb, wub, wdb, sem):
    b = pl.program_id(0)
    n_used = nused_ref[0]
    slot = b % 2

    def gather(blk, sl):
        def body(i, carry):
            tok = rtok_ref[blk * MOE_BLK + i]
            pltpu.make_async_copy(x_hbm.at[pl.ds(tok, 1)], xbuf.at[sl, pl.ds(i, 1)], sem.at[sl]).start()
            return carry
        lax.fori_loop(0, MOE_BLK, body, 0)

    @pl.when((b == 0) & (n_used > 0))
    def _():
        gather(0, 0)

    @pl.when(b + 1 < n_used)
    def _():
        gather(b + 1, 1 - slot)

    @pl.when(b < n_used)
    def _():
        new_expert = (b == 0) | (bexp_ref[b] != bexp_ref[jnp.maximum(b - 1, 0)])

        @pl.when(new_expert)
        def _():
            wgb[...] = wg_ref[0].astype(BF16)
            wub[...] = wu_ref[0].astype(BF16)
            wdb[...] = wd_ref[0].astype(BF16)

        pltpu.make_async_copy(xbuf.at[slot], xbuf.at[slot], sem.at[slot]).wait()
        xb = xbuf[slot].astype(BF16)
        hg = jnp.dot(xb, wgb[...], preferred_element_type=F32)
        hu = jnp.dot(xb, wub[...], preferred_element_type=F32)
        h = (hg * jax.nn.sigmoid(hg) * hu).astype(BF16)
        y_ref[...] = jnp.dot(h, wdb[...], preferred_element_type=F32) * roww_ref[...]

    @pl.when(b >= n_used)
    def _():
        y_ref[...] = jnp.zeros_like(y_ref)


def _experts(block_exp, n_used, row_tok, x1_all, row_w, wg, wu, wd):
    n_blocks = block_exp.shape[0]
    n_rows = n_blocks * MOE_BLK
    grid_spec = pltpu.PrefetchScalarGridSpec(
        num_scalar_prefetch=3,
        grid=(n_blocks,),
        in_specs=[pl.BlockSpec(memory_space=pl.ANY),
                  pl.BlockSpec((MOE_BLK, 1), lambda b, be, nu, rt: (b, 0)),
                  pl.BlockSpec((1, D_MODEL, D_EXPERT), lambda b, be, nu, rt: (be[b], 0, 0)),
                  pl.BlockSpec((1, D_MODEL, D_EXPERT), lambda b, be, nu, rt: (be[b], 0, 0)),
                  pl.BlockSpec((1, D_EXPERT, D_MODEL), lambda b, be, nu, rt: (be[b], 0, 0))],
        out_specs=pl.BlockSpec((MOE_BLK, D_MODEL), lambda b, be, nu, rt: (b, 0)),
        scratch_shapes=[pltpu.VMEM((2, MOE_BLK, D_MODEL), F32),
                        pltpu.VMEM((D_MODEL, D_EXPERT), BF16),
                        pltpu.VMEM((D_MODEL, D_EXPERT), BF16),
                        pltpu.VMEM((D_EXPERT, D_MODEL), BF16),
                        pltpu.SemaphoreType.DMA((2,))],
    )
    return pl.pallas_call(
        _experts_kernel,
        grid_spec=grid_spec,
        out_shape=jax.ShapeDtypeStruct((n_rows, D_MODEL), F32),
        compiler_params=_cparams(("arbitrary",)),
        name="experts",
    )(block_exp, n_used, row_tok, x1_all, row_w, wg, wu, wd)


def _combine_ln2_kernel(dest_ref, x1_ref, yb_hbm, g_ref, b_ref, y_ref, ybuf, sem, *, tm, tok0):
    i = pl.program_id(0)
    nt = pl.num_programs(0)
    slot = i % 2

    def gather(tile, sl):
        def body(r, carry):
            a = 2 * (tok0 + tile * tm + r)
            pltpu.make_async_copy(yb_hbm.at[pl.ds(dest_ref[a], 1)], ybuf.at[sl, 0, pl.ds(r, 1)], sem.at[sl]).start()
            pltpu.make_async_copy(yb_hbm.at[pl.ds(dest_ref[a + 1], 1)], ybuf.at[sl, 1, pl.ds(r, 1)],
                                  sem.at[sl]).start()
            return carry
        lax.fori_loop(0, tm, body, 0)

    @pl.when(i == 0)
    def _():
        gather(0, 0)

    @pl.when(i + 1 < nt)
    def _():
        gather(i + 1, 1 - slot)

    pltpu.make_async_copy(ybuf.at[slot], ybuf.at[slot], sem.at[slot]).wait()
    z = DEEPNORM_ALPHA * x1_ref[...] + (ybuf[slot, 0] + ybuf[slot, 1])
    y_ref[...] = _layer_norm(z, g_ref[...], b_ref[...])


def _combine_ln2(dest, x1, yb, g2, b2, tm, tok0):
    m = x1.shape[0]
    grid_spec = pltpu.PrefetchScalarGridSpec(
        num_scalar_prefetch=1,
        grid=(m // tm,),
        in_specs=[pl.BlockSpec((tm, D_MODEL), lambda i, d: (i, 0)),
                  pl.BlockSpec(memory_space=pl.ANY),
                  pl.BlockSpec((1, D_MODEL), lambda i, d: (0, 0)),
                  pl.BlockSpec((1, D_MODEL), lambda i, d: (0, 0))],
        out_specs=pl.BlockSpec((tm, D_MODEL), lambda i, d: (i, 0)),
        scratch_shapes=[pltpu.VMEM((2, 2, tm, D_MODEL), F32),
                        pltpu.SemaphoreType.DMA((2,))],
    )
    return pl.pallas_call(
        functools.partial(_combine_ln2_kernel, tm=tm, tok0=tok0),
        grid_spec=grid_spec,
        out_shape=jax.ShapeDtypeStruct((m, D_MODEL), F32),
        compiler_params=_cparams(("arbitrary",)),
        name="combine_ln2",
    )(dest, x1, yb, g2, b2)


def _dispatch_tables(eid, gate):
    t = eid.shape[0]
    a = 2 * t
    eid = eid.reshape(a)
    order = jnp.argsort(eid)
    eid_s = eid[order]
    counts = jnp.zeros((N_EXPERTS,), I32).at[eid].add(1)
    start = jnp.cumsum(counts) - counts
    padded = (counts + MOE_BLK - 1) // MOE_BLK * MOE_BLK
    pad_end = jnp.cumsum(padded)
    pad_start = pad_end - padded
    dest_s = pad_start[eid_s] + (jnp.arange(a, dtype=I32) - start[eid_s])
    n_blocks = (a + N_EXPERTS * (MOE_BLK - 1) + MOE_BLK - 1) // MOE_BLK
    n_rows = n_blocks * MOE_BLK
    row_tok = jnp.zeros((n_rows,), I32).at[dest_s].set((order // 2).astype(I32))
    row_w = jnp.zeros((n_rows,), F32).at[dest_s].set(gate.reshape(a)[order])
    block_exp = jnp.minimum(jnp.searchsorted(pad_end, jnp.arange(n_blocks, dtype=I32) * MOE_BLK, side='right'),
                            N_EXPERTS - 1).astype(I32)
    dest = jnp.zeros((a,), I32).at[order].set(dest_s.astype(I32))
    n_used = (pad_end[-1:] // MOE_BLK).astype(I32)
    return row_tok, row_w, block_exp, n_used, dest


def kernel(x_prompt, x_sample, cache_k, cache_v, cache_kidx, state_conv, page_table, w_in, conv_dw_w, conv_dw_b,
           conv_ln_g, conv_ln_b, w_conv_out, w_attn_out, w_out, ln1_g, ln1_b, w_router_group, w_router_expert,
           w_expert_gate, w_expert_up, w_expert_down, ln2_g, ln2_b):
    assert w_in.shape[0] == 1, "single-layer trunk"
    nb, t, d = x_prompt.shape
    ns = x_sample.shape[0]
    mp = nb * t
    row2 = lambda a: a.reshape(1, -1)

    w_arr = _arrange_w_in(w_in[0])
    dww, dwb, lng, lnb = conv_dw_w[0], row2(conv_dw_b[0]), row2(conv_ln_g[0]), row2(conv_ln_b[0])
    wc, wa, wo = w_conv_out[0].astype(BF16), w_attn_out[0].astype(BF16), w_out[0].astype(BF16)
    g1, b1, g2, b2 = row2(ln1_g[0]), row2(ln1_b[0]), row2(ln2_g[0]), row2(ln2_b[0])
    wr = jnp.concatenate([w_router_group[0], w_router_expert[0],
                          jnp.zeros((d, LANES - N_GROUPS - N_EXPERTS), F32)], axis=1)

    xp = x_prompt.reshape(mp, d)
    u_p, q_p, qi_p, kv_p, kvb_p, gc_p, ga_p, kiwi_p, kib_p = _in_proj(xp, w_arr, tm=512)
    yact_p = _conv_prompt(u_p.reshape(nb, t, C_CONV), dww, dwb, lng, lnb)
    o_p = _attn_prompt(qi_p.reshape(nb, t, -1), kiwi_p.reshape(nb, t, LANES), kib_p.reshape(nb, t, LANES),
                       q_p.reshape(nb, t, -1), kvb_p.reshape(nb, t, IN_TN))
    x1_p, ei_p, gt_p = _merge_ln1(yact_p.reshape(mp, C_CONV), o_p.reshape(mp, -1), gc_p, ga_p, xp,
                                  wc, wa, wo, g1, b1, wr, tm=256)

    xs = x_sample.reshape(ns, d)
    u_s, q_s, qi_s, kv_s, _, gc_s, ga_s, kiwi_s, _ = _in_proj(xs, w_arr, tm=ns)
    yact_s = _conv_sample(state_conv[0].transpose(1, 0, 2), u_s, dww, dwb, lng, lnb)
    scores_s = _sample_scores(page_table, qi_s.reshape(ns, IDX_HEADS, IDX_DIM),
                              kiwi_s[:, IDX_DIM:IDX_DIM + IDX_HEADS].reshape(ns, IDX_HEADS, 1),
                              kiwi_s[:, :IDX_DIM].reshape(ns, 1, IDX_DIM), cache_kidx)
    kvh = KV_HEADS * HEAD_DIM
    q4 = q_s.reshape(ns, KV_HEADS, HEADS_PER_KV, HEAD_DIM).transpose(0, 2, 1, 3)
    o4 = _sample_attn(page_table, scores_s, q4, kv_s[:, :kvh].reshape(ns, KV_HEADS, HEAD_DIM),
                      kv_s[:, kvh:].reshape(ns, KV_HEADS, HEAD_DIM), cache_k, cache_v)
    o_s = o4.transpose(0, 2, 1, 3).reshape(ns, N_HEADS * HEAD_DIM).astype(BF16)
    x1_s, ei_s, gt_s = _merge_ln1(yact_s, o_s, gc_s, ga_s, xs, wc, wa, wo, g1, b1, wr, tm=ns)

    x1_all = jnp.concatenate([x1_p, x1_s], axis=0)
    eid = jnp.concatenate([ei_p[:, :2], ei_s[:, :2]], axis=0)
    gate = jnp.concatenate([gt_p[:, :2], gt_s[:, :2]], axis=0)
    row_tok, row_w, block_exp, n_used, dest = _dispatch_tables(eid, gate)
    yb = _experts(block_exp, n_used, row_tok, x1_all, row_w.reshape(-1, 1),
                  w_expert_gate[0], w_expert_up[0], w_expert_down[0])
    y_p = _combine_ln2(dest, x1_p, yb, g2, b2, tm=256, tok0=0)
    y_s = _combine_ln2(dest, x1_s, yb, g2, b2, tm=ns, tok0=mp)

    y_prompt = y_p.reshape(nb, t, d)
    y_sample = y_s.reshape(ns, 1, d)
    k_prompt = kv_p[:, :kvh].reshape(1, nb, t, KV_HEADS, HEAD_DIM)
    v_prompt = kv_p[:, kvh:].reshape(1, nb, t, KV_HEADS, HEAD_DIM)
    kidx_prompt = kiwi_p[:, :IDX_DIM].reshape(1, nb, t, IDX_DIM)
    conv_prompt = u_p.reshape(nb, t, C_CONV)[:, t - (CONV_W - 1):][None]
    k_sample = kv_s[:, :kvh].reshape(1, ns, 1, KV_HEADS, HEAD_DIM)
    v_sample = kv_s[:, kvh:].reshape(1, ns, 1, KV_HEADS, HEAD_DIM)
    kidx_sample = kiwi_s[:, :IDX_DIM].reshape(1, ns, 1, IDX_DIM)
    conv_sample = jnp.concatenate([state_conv[0][:, 1:], u_s[:, None, :]], axis=1)[None]
    return (y_prompt, y_sample, k_prompt, v_prompt, kidx_prompt, conv_prompt, k_sample, v_sample, kidx_sample,
            conv_sample)
```

```python
import functools

import jax
import jax.numpy as jnp
import numpy as np
from jax import lax
from jax.experimental import pallas as pl
from jax.experimental.pallas import tpu as pltpu

F32 = jnp.float32
BF16 = jnp.bfloat16
I32 = jnp.int32

D_MODEL = 2048
C_CONV = 1024
CONV_W = 31
N_HEADS = 16
HEAD_DIM = 64
KV_HEADS = 4
HEADS_PER_KV = N_HEADS // KV_HEADS
KV_DIM = KV_HEADS * HEAD_DIM
IDX_HEADS = 16
IDX_DIM = 64
MAX_TOPK = 256
N_GROUPS = 8
EXPERTS_PER_GROUP = 8
N_EXPERTS = 64
D_EXPERT = 512
MOE_BLK = 128
PAGE_SIZE = 128
LN_EPS = 1e-5
DEEPNORM_ALPHA = 2.0 ** 0.25
NEG_INF = -1e30
INT_MIN = -(2 ** 31)
NEG_KEY = int(INT_MIN - int(np.array(NEG_INF, np.float32).view(np.int32)))
LOG2E = 1.4426950408889634
Q_SCALE = HEAD_DIM ** -0.5 * LOG2E
IDX_SCALE = IDX_HEADS ** -0.5 * IDX_DIM ** -0.5

LANES = 128
HALF = LANES // 2
IN_TN = 512
VMEM_LIMIT = 56 * 1024 * 1024


def _cparams(sem, vmem=VMEM_LIMIT):
    return pltpu.CompilerParams(dimension_semantics=sem, vmem_limit_bytes=vmem)


def _float_key(x):
    b = pltpu.bitcast(x, I32)
    return jnp.where(b < 0, INT_MIN - b, b)


def _layer_norm(z, g, b):
    mu = jnp.mean(z, axis=-1, keepdims=True)
    zc = z - mu
    var = jnp.mean(zc * zc, axis=-1, keepdims=True)
    return zc * lax.rsqrt(var + LN_EPS) * g + b


def _dot_nt(a, b):
    return lax.dot_general(a, b, (((1,), (1,)), ((), ())), preferred_element_type=F32)


def _alibi_slope(h):
    return float(2.0 ** (-8.0 * (h + 1) / N_HEADS))


_T_GLU, _T_Q, _T_QI, _T_KV, _T_GC, _T_GA, _T_KI, _N_TILES = 0, 4, 6, 8, 9, 13, 17, 18


def _arrange_w_in(w):
    sizes = (C_CONV, C_CONV, N_HEADS * HEAD_DIM, KV_DIM, KV_DIM,
             IDX_HEADS * IDX_DIM, IDX_DIM, IDX_HEADS, D_MODEL, D_MODEL)
    offs = np.cumsum((0,) + sizes)
    a, g, q, k, v, qi, ki, wi, gc, ga = [w[:, offs[i]:offs[i + 1]] for i in range(10)]
    half = IN_TN // 2
    glu = [jnp.concatenate([a[:, t * half:(t + 1) * half], g[:, t * half:(t + 1) * half]], axis=1)
           for t in range(C_CONV // half)]
    tail = jnp.concatenate([ki, wi, jnp.zeros((w.shape[0], LANES - IDX_DIM - IDX_HEADS), w.dtype), ki, ki,
                            jnp.zeros((w.shape[0], IN_TN - 2 * LANES), w.dtype)], axis=1)
    return jnp.concatenate(glu + [q, qi, k, v, gc, ga, tail], axis=1).astype(BF16)


def _in_proj_kernel(x_ref, w_ref, u_ref, q_ref, qi_ref, kv_ref, kvb_ref, gc_ref, ga_ref, kiwi_ref, kib_ref,
                    xb_ref):
    j = pl.program_id(1)

    @pl.when(j == 0)
    def _():
        xb_ref[...] = x_ref[...].astype(BF16)

    def mm(ncols=IN_TN):
        return jnp.dot(xb_ref[...], w_ref[:, :ncols], preferred_element_type=F32)

    @pl.when(j < _T_Q)
    def _():
        acc = mm()
        u_ref[...] = acc[:, :IN_TN // 2] * jax.nn.sigmoid(acc[:, IN_TN // 2:])

    @pl.when((j >= _T_Q) & (j < _T_QI))
    def _():
        q_ref[...] = (mm() * Q_SCALE).astype(BF16)

    @pl.when((j >= _T_QI) & (j < _T_KV))
    def _():
        qi_ref[...] = mm().astype(BF16)

    @pl.when(j == _T_KV)
    def _():
        acc = mm()
        kv_ref[...] = acc
        kvb_ref[...] = acc.astype(BF16)

    @pl.when((j >= _T_GC) & (j < _T_GA))
    def _():
        gc_ref[...] = jax.nn.sigmoid(mm()).astype(BF16)

    @pl.when((j >= _T_GA) & (j < _T_KI))
    def _():
        ga_ref[...] = jax.nn.sigmoid(mm()).astype(BF16)

    @pl.when(j == _T_KI)
    def _():
        acc = mm(2 * LANES)
        kiwi_ref[...] = acc[:, :LANES]
        kib_ref[...] = acc[:, LANES:].astype(BF16)


def _in_proj(x, w_arr, tm):
    m = x.shape[0]
    half = IN_TN // 2

    def cl(lo, n):
        return lambda i, j: (i, jnp.clip(j - lo, 0, n - 1))

    out_shape = (
        jax.ShapeDtypeStruct((m, C_CONV), F32),
        jax.ShapeDtypeStruct((m, N_HEADS * HEAD_DIM), BF16),
        jax.ShapeDtypeStruct((m, IDX_HEADS * IDX_DIM), BF16),
        jax.ShapeDtypeStruct((m, IN_TN), F32),
        jax.ShapeDtypeStruct((m, IN_TN), BF16),
        jax.ShapeDtypeStruct((m, D_MODEL), BF16),
        jax.ShapeDtypeStruct((m, D_MODEL), BF16),
        jax.ShapeDtypeStruct((m, LANES), F32),
        jax.ShapeDtypeStruct((m, LANES), BF16),
    )
    out_specs = (
        pl.BlockSpec((tm, half), cl(_T_GLU, 4)),
        pl.BlockSpec((tm, IN_TN), cl(_T_Q, 2)),
        pl.BlockSpec((tm, IN_TN), cl(_T_QI, 2)),
        pl.BlockSpec((tm, IN_TN), lambda i, j: (i, 0)),
        pl.BlockSpec((tm, IN_TN), lambda i, j: (i, 0)),
        pl.BlockSpec((tm, IN_TN), cl(_T_GC, 4)),
        pl.BlockSpec((tm, IN_TN), cl(_T_GA, 4)),
        pl.BlockSpec((tm, LANES), lambda i, j: (i, 0)),
        pl.BlockSpec((tm, LANES), lambda i, j: (i, 0)),
    )
    return pl.pallas_call(
        _in_proj_kernel,
        grid=(m // tm, _N_TILES),
        in_specs=[pl.BlockSpec((tm, D_MODEL), lambda i, j: (i, 0)),
                  pl.BlockSpec((D_MODEL, IN_TN), lambda i, j: (0, j))],
        out_specs=out_specs,
        out_shape=out_shape,
        scratch_shapes=[pltpu.VMEM((tm, D_MODEL), BF16)],
        compiler_params=_cparams(("arbitrary", "arbitrary")),
        name="in_proj",
    )(x, w_arr)


CONV_HALO = 32
CONV_STRIP = 32


def _conv_strip(ext_ref, row0, dww_ref, dwb, lng, lnb):
    shift = CONV_HALO - (CONV_W - 1)
    acc = jnp.broadcast_to(dwb, (CONV_STRIP, C_CONV))
    for j in range(CONV_W):
        acc = acc + dww_ref[j:j + 1, :] * ext_ref[row0 + j + shift:row0 + j + shift + CONV_STRIP, :]
    yn = _layer_norm(acc, lng, lnb)
    return yn * jax.nn.sigmoid(yn)


def _conv_prompt_kernel(ucur_ref, uprev_ref, dww_ref, dwb_ref, lng_ref, lnb_ref, y_ref, ext_ref, *, tt):
    tb = pl.program_id(1)
    ext_ref[0:CONV_HALO, :] = jnp.where(tb > 0, uprev_ref[0], 0.0)
    ext_ref[CONV_HALO:, :] = ucur_ref[0]
    dwb, lng, lnb = dwb_ref[...], lng_ref[...], lnb_ref[...]
    for r in range(tt // CONV_STRIP):
        y = _conv_strip(ext_ref, r * CONV_STRIP, dww_ref, dwb, lng, lnb)
        y_ref[0, r * CONV_STRIP:(r + 1) * CONV_STRIP, :] = y.astype(BF16)


def _conv_prompt(u, dww, dwb, lng, lnb, tt=128):
    n, t, c = u.shape
    per = tt // CONV_HALO
    vec = pl.BlockSpec((1, c), lambda b, i: (0, 0))
    return pl.pallas_call(
        functools.partial(_conv_prompt_kernel, tt=tt),
        grid=(n, t // tt),
        in_specs=[pl.BlockSpec((1, tt, c), lambda b, i: (b, i, 0)),
                  pl.BlockSpec((1, CONV_HALO, c), lambda b, i: (b, jnp.maximum(i * per - 1, 0), 0)),
                  pl.BlockSpec((CONV_W, c), lambda b, i: (0, 0)), vec, vec, vec],
        out_specs=pl.BlockSpec((1, tt, c), lambda b, i: (b, i, 0)),
        out_shape=jax.ShapeDtypeStruct((n, t, c), BF16),
        scratch_shapes=[pltpu.VMEM((tt + CONV_HALO, c), F32)],
        compiler_params=_cparams(("arbitrary", "arbitrary")),
        name="conv_prompt",
    )(u, u, dww, dwb, lng, lnb)


def _conv_sample_kernel(state_ref, u_ref, dww_ref, dwb_ref, lng_ref, lnb_ref, y_ref):
    acc = dwb_ref[...] + dww_ref[CONV_W - 1:CONV_W, :] * u_ref[...]
    for j in range(CONV_W - 1):
        acc = acc + dww_ref[j:j + 1, :] * state_ref[j]
    yn = _layer_norm(acc, lng_ref[...], lnb_ref[...])
    y_ref[...] = (yn * jax.nn.sigmoid(yn)).astype(BF16)


def _conv_sample(state_t, u, dww, dwb, lng, lnb):
    n = u.shape[0]
    return pl.pallas_call(
        _conv_sample_kernel,
        out_shape=jax.ShapeDtypeStruct((n, C_CONV), BF16),
        compiler_params=pltpu.CompilerParams(vmem_limit_bytes=VMEM_LIMIT),
        name="conv_sample",
    )(state_t, u, dww, dwb, lng, lnb)


ATT_TQ = 256
ATT_CK = 256


def _count_ge(key_ref, nch, cand):
    def body(c, acc):
        m = jnp.where(key_ref[c] >= cand, 1.0, 0.0)
        return acc + m[:, :LANES] + m[:, LANES:]
    acc = lax.fori_loop(0, nch, body, jnp.zeros((cand.shape[0], LANES), F32))
    return jnp.sum(acc, axis=1, keepdims=True)


def _select_threshold(count_fn, shape):
    def body(i, prefix):
        cand = prefix + jnp.left_shift(jnp.int32(1), 31 - i)
        return jnp.where(count_fn(cand) >= float(MAX_TOPK), cand, prefix)
    return lax.fori_loop(0, 32, body, jnp.full(shape, INT_MIN, I32))


def _selection_bias(key_ref, bias_ref, nch):
    rows, ck = key_ref.shape[1:]
    thr = _select_threshold(lambda cand: _count_ge(key_ref, nch, cand), (rows, 1))
    n_tie_take = float(MAX_TOPK) - _count_ge(key_ref, nch, thr + 1)
    tri = jnp.where(lax.broadcasted_iota(I32, (ck, ck), 0) <= lax.broadcasted_iota(I32, (ck, ck), 1),
                    1.0, 0.0).astype(BF16)

    def bias_chunk(c, seen):
        key = key_ref[c]
        eq = key == thr
        eqf = jnp.where(eq, 1.0, 0.0)
        incl = jnp.dot(eqf.astype(BF16), tri, preferred_element_type=F32)
        sel = (key > thr) | (eq & (seen + incl - eqf < n_tie_take))
        bias_ref[c] = jnp.where(sel & (key > NEG_KEY), 0.0, NEG_INF)
        return seen + incl[:, ck - 1:ck]

    lax.fori_loop(0, nch, bias_chunk, jnp.zeros((rows, 1), F32))


def _swap_halves(x):
    return jnp.concatenate([x[:, HALF:], x[:, :HALF]], axis=1)


def _attn_prompt_kernel(qi_ref, kiwi_ref, kib_ref, q_ref, kvb_ref, o_ref,
                        key_ref, bias_ref, wb_ref, m_ref, l_ref, acc_ref):
    tq, ck = ATT_TQ, ATT_CK
    qb = pl.program_id(1)
    nch = qb + 1
    row = qb * tq + lax.broadcasted_iota(I32, (tq, ck), 0)
    coll = lax.broadcasted_iota(I32, (tq, ck), 1)
    lo_k = lax.broadcasted_iota(I32, (ck, LANES), 1) < HALF
    lo_q = lax.broadcasted_iota(I32, (tq, LANES), 1) < HALF

    w = kiwi_ref[0][:, IDX_DIM:IDX_DIM + IDX_HEADS] * IDX_SCALE
    for h in range(IDX_HEADS):
        wb_ref[h] = jnp.broadcast_to(w[:, h:h + 1], (tq, LANES))

    def score_chunk(c, carry):
        k0 = pl.multiple_of(c * ck, ck)
        kk = kib_ref[0, pl.ds(k0, ck), :]
        zero = jnp.zeros_like(kk)
        k_lo, k_hi = jnp.where(lo_k, kk, zero), jnp.where(lo_k, zero, kk)
        acc = jnp.zeros((tq, ck), F32)
        for p in range(IDX_HEADS // 2):
            qp = qi_ref[0, :, p * LANES:(p + 1) * LANES]
            w_lo, w_hi = wb_ref[2 * p], wb_ref[2 * p + 1]
            acc = acc + jnp.concatenate([w_lo, w_lo], axis=1) * jnp.maximum(_dot_nt(qp, k_lo), 0.0)
            acc = acc + jnp.concatenate([w_hi, w_hi], axis=1) * jnp.maximum(_dot_nt(qp, k_hi), 0.0)
        acc = jnp.where(c * ck + coll <= row, acc, NEG_INF)
        key_ref[c] = _float_key(acc)
        return carry

    lax.fori_loop(0, nch, score_chunk, 0)
    _selection_bias(key_ref, bias_ref, nch)

    m_ref[...] = jnp.full(m_ref.shape, -jnp.inf, F32)
    l_ref[...] = jnp.zeros(l_ref.shape, F32)
    acc_ref[...] = jnp.zeros(acc_ref.shape, F32)

    def attn_chunk(c, carry):
        k0 = pl.multiple_of(c * ck, ck)
        kv = kvb_ref[0, pl.ds(k0, ck), :]
        bias = bias_ref[c]
        rel = (c * ck - (qb + 1) * tq + 1 + lax.broadcasted_iota(I32, (1, ck), 1)).astype(F32)
        for j in range(KV_HEADS // 2):
            kt = kv[:, j * LANES:(j + 1) * LANES]
            vt = kv[:, KV_DIM + j * LANES:KV_DIM + (j + 1) * LANES]
            kts, vts = _swap_halves(kt), _swap_halves(vt)
            zero = jnp.zeros_like(kt)
            for gg in range(2):
                g = 2 * j + gg
                k_own, k_swp = (kt, kts) if gg == 0 else (kts, kt)
                v_own, v_swp = (vt, vts) if gg == 0 else (vts, vt)
                k_lo, k_hi = jnp.where(lo_k, k_own, zero), jnp.where(lo_k, zero, k_swp)
                v_lo, v_hi = jnp.where(lo_k, v_own, zero), jnp.where(lo_k, zero, v_swp)
                for mm in range(HEADS_PER_KV // 2):
                    pidx = (HEADS_PER_KV // 2) * g + mm
                    qp = q_ref[0, :, pidx * LANES:(pidx + 1) * LANES]
                    pv, alphas = None, []
                    for half, (kmat, vmat) in enumerate(((k_lo, v_lo), (k_hi, v_hi))):
                        h = 2 * pidx + half
                        s = _dot_nt(qp, kmat) + (_alibi_slope(h) * LOG2E) * rel + bias
                        m_old = m_ref[h]
                        m_new = jnp.maximum(m_old, jnp.max(s, axis=1, keepdims=True))
                        p = jnp.exp2(s - jnp.concatenate([m_new, m_new], axis=1))
                        alpha = jnp.exp2(m_old - m_new)
                        l_ref[h] = alpha * l_ref[h] + jnp.sum(p, axis=1, keepdims=True)
                        m_ref[h] = m_new
                        d = jnp.dot(p.astype(BF16), vmat, preferred_element_type=F32)
                        pv = d if pv is None else pv + d
                        alphas.append(alpha)
                    acc_ref[pidx] = jnp.where(lo_q, alphas[0], alphas[1]) * acc_ref[pidx] + pv
        return carry

    lax.fori_loop(0, nch, attn_chunk, 0)
    for pidx in range(N_HEADS // 2):
        l_pair = jnp.where(lo_q, l_ref[2 * pidx], l_ref[2 * pidx + 1])
        o_ref[0, :, pidx * LANES:(pidx + 1) * LANES] = (acc_ref[pidx] / l_pair).astype(BF16)


def _attn_prompt(qi, kiwi, kib, q, kvb):
    n, t, _ = q.shape
    tq = ATT_TQ
    return pl.pallas_call(
        _attn_prompt_kernel,
        grid=(n, t // tq),
        in_specs=[pl.BlockSpec((1, tq, IDX_HEADS * IDX_DIM), lambda b, i: (b, i, 0)),
                  pl.BlockSpec((1, tq, LANES), lambda b, i: (b, i, 0)),
                  pl.BlockSpec((1, t, LANES), lambda b, i: (b, 0, 0)),
                  pl.BlockSpec((1, tq, N_HEADS * HEAD_DIM), lambda b, i: (b, i, 0)),
                  pl.BlockSpec((1, t, IN_TN), lambda b, i: (b, 0, 0))],
        out_specs=pl.BlockSpec((1, tq, N_HEADS * HEAD_DIM), lambda b, i: (b, i, 0)),
        out_shape=jax.ShapeDtypeStruct((n, t, N_HEADS * HEAD_DIM), BF16),
        scratch_shapes=[pltpu.VMEM((t // ATT_CK, tq, ATT_CK), I32),
                        pltpu.VMEM((t // ATT_CK, tq, ATT_CK), F32),
                        pltpu.VMEM((IDX_HEADS, tq, LANES), F32),
                        pltpu.VMEM((N_HEADS, tq, LANES), F32),
                        pltpu.VMEM((N_HEADS, tq, LANES), F32),
                        pltpu.VMEM((N_HEADS // 2, tq, LANES), F32)],
        compiler_params=_cparams(("arbitrary", "arbitrary")),
        name="attn_prompt",
    )(qi, kiwi, kib, q, kvb)


SAMPLE_KC = 2048


def _start_page_copies(pt_ref, src_hbm, dst, sem, base, n_pages):
    def body(p, carry):
        col = pl.multiple_of(p * PAGE_SIZE, PAGE_SIZE)
        pltpu.make_async_copy(src_hbm.at[pt_ref[base + p]], dst.at[:, pl.ds(col, PAGE_SIZE)], sem).start()
        return carry
    lax.fori_loop(0, n_pages, body, 0)


def _sample_score_kernel(pt_ref, qi_ref, wcol_ref, kinew_ref, cki_hbm, sc_ref, kibuf, sem, *, n_pages):
    b = pl.program_id(0)
    nb = pl.num_programs(0)
    past = n_pages * PAGE_SIZE
    slot = b % 2

    @pl.when(b == 0)
    def _():
        _start_page_copies(pt_ref, cki_hbm, kibuf.at[0], sem.at[0], 0, n_pages)

    @pl.when(b + 1 < nb)
    def _():
        _start_page_copies(pt_ref, cki_hbm, kibuf.at[1 - slot], sem.at[1 - slot], (b + 1) * n_pages, n_pages)

    pltpu.make_async_copy(kibuf.at[slot], kibuf.at[slot], sem.at[slot]).wait()

    qi = qi_ref[0]
    wsc = wcol_ref[0] * IDX_SCALE
    for c in range(past // SAMPLE_KC):
        kc = kibuf[slot, :, c * SAMPLE_KC:(c + 1) * SAMPLE_KC].astype(BF16)
        d = jnp.dot(qi, kc, preferred_element_type=F32)
        sc_ref[0, :, c * SAMPLE_KC:(c + 1) * SAMPLE_KC] = jnp.sum(wsc * jnp.maximum(d, 0.0), axis=0, keepdims=True)
    dn = jnp.sum(qi.astype(F32) * kinew_ref[0], axis=1, keepdims=True)
    s_new = jnp.sum(wsc * jnp.maximum(dn, 0.0), axis=0, keepdims=True)
    ln = lax.broadcasted_iota(I32, (1, ATT_CK), 1)
    sc_ref[0, :, past:] = jnp.where(ln == 0, s_new, NEG_INF)


def _sample_scores(page_table, qi3, wcol, kinew, cki2):
    n, n_pages = page_table.shape
    past = n_pages * PAGE_SIZE
    grid_spec = pltpu.PrefetchScalarGridSpec(
        num_scalar_prefetch=1,
        grid=(n,),
        in_specs=[pl.BlockSpec((1, IDX_HEADS, IDX_DIM), lambda b, pt: (b, 0, 0)),
                  pl.BlockSpec((1, IDX_HEADS, 1), lambda b, pt: (b, 0, 0)),
                  pl.BlockSpec((1, 1, IDX_DIM), lambda b, pt: (b, 0, 0)),
                  pl.BlockSpec(memory_space=pl.ANY)],
        out_specs=pl.BlockSpec((1, 1, past + ATT_CK), lambda b, pt: (b, 0, 0)),
        scratch_shapes=[pltpu.VMEM((2, IDX_DIM, past), F32),
                        pltpu.SemaphoreType.DMA((2,))],
    )
    return pl.pallas_call(
        functools.partial(_sample_score_kernel, n_pages=n_pages),
        grid_spec=grid_spec,
        out_shape=jax.ShapeDtypeStruct((n, 1, past + ATT_CK), F32),
        compiler_params=_cparams(("arbitrary",)),
        name="sample_scores",
    )(page_table.reshape(-1), qi3, wcol, kinew, cki2)


def _sample_attn_kernel(pt_ref, sc_ref, qbd_ref, knew_ref, vnew_ref, ck_hbm, cv_hbm, o_ref,
                        key_ref, bias_ref, kbuf, vbuf, semk, semv, *, n_pages):
    b = pl.program_id(0)
    nb = pl.num_programs(0)
    past = n_pages * PAGE_SIZE
    nch = sc_ref.shape[0]
    slot = b % 2

    def start_all(bb, sl):
        _start_page_copies(pt_ref, ck_hbm, kbuf.at[sl], semk.at[sl], bb * n_pages, n_pages)
        _start_page_copies(pt_ref, cv_hbm, vbuf.at[sl], semv.at[sl], bb * n_pages, n_pages)

    @pl.when(b == 0)
    def _():
        start_all(0, 0)
        key_ref[...] = _float_key(sc_ref[...])
        _selection_bias(key_ref, bias_ref, nch)

    @pl.when(b + 1 < nb)
    def _():
        start_all(b + 1, 1 - slot)

    pltpu.make_async_copy(kbuf.at[slot], kbuf.at[slot], semk.at[slot]).wait()
    pltpu.make_async_copy(vbuf.at[slot], vbuf.at[slot], semv.at[slot]).wait()

    qbd = qbd_ref[0]
    bias_row = jnp.concatenate([bias_ref[c, pl.ds(b, 1), :] for c in range(nch)], axis=1)
    s_parts = [jnp.dot(qbd, kbuf[slot, :, c * SAMPLE_KC:(c + 1) * SAMPLE_KC].astype(BF16),
                       preferred_element_type=F32) for c in range(past // SAMPLE_KC)]
    s_new = jnp.sum(qbd.astype(F32) * knew_ref[0], axis=1, keepdims=True)
    ln = lax.broadcasted_iota(I32, (N_HEADS, ATT_CK), 1)
    s = jnp.concatenate(s_parts + [jnp.where(ln == 0, s_new, 0.0)], axis=1)
    hrow = lax.broadcasted_iota(I32, (N_HEADS, 1), 0)
    slope2 = jnp.exp2(-8.0 * (hrow + 1).astype(F32) / N_HEADS) * LOG2E
    rel = (lax.broadcasted_iota(I32, (1, past + ATT_CK), 1) - past).astype(F32)
    s = s + slope2 * rel + bias_row
    m = jnp.max(s, axis=1, keepdims=True)
    p = jnp.exp2(s - m)
    l = jnp.sum(p, axis=1, keepdims=True)
    o_all = p[:, past:past + 1] * vnew_ref[0]
    for c in range(past // SAMPLE_KC):
        o_all = o_all + _dot_nt(p[:, c * SAMPLE_KC:(c + 1) * SAMPLE_KC].astype(BF16),
                                vbuf[slot, :, c * SAMPLE_KC:(c + 1) * SAMPLE_KC].astype(BF16))
    o_all = o_all / l
    out = jnp.zeros((N_HEADS, HEAD_DIM), F32)
    for g in range(KV_HEADS):
        own = (hrow >= g * HEADS_PER_KV) & (hrow < (g + 1) * HEADS_PER_KV)
        out = out + jnp.where(own, o_all[:, g * HEAD_DIM:(g + 1) * HEAD_DIM], 0.0)
    o_ref[0] = out


def _sample_attn(page_table, scores3, qbd, knew, vnew, ck2, cv2):
    n, n_pages = page_table.shape
    past = n_pages * PAGE_SIZE
    nch = scores3.shape[0]
    grid_spec = pltpu.PrefetchScalarGridSpec(
        num_scalar_prefetch=1,
        grid=(n,),
        in_specs=[pl.BlockSpec((nch, n, ATT_CK), lambda b, pt: (0, 0, 0)),
                  pl.BlockSpec((1, N_HEADS, KV_DIM), lambda b, pt: (b, 0, 0)),
                  pl.BlockSpec((1, 1, KV_DIM), lambda b, pt: (b, 0, 0)),
                  pl.BlockSpec((1, 1, KV_DIM), lambda b, pt: (b, 0, 0)),
                  pl.BlockSpec(memory_space=pl.ANY),
                  pl.BlockSpec(memory_space=pl.ANY)],
        out_specs=pl.BlockSpec((1, N_HEADS, HEAD_DIM), lambda b, pt: (b, 0, 0)),
        scratch_shapes=[pltpu.VMEM((nch, n, ATT_CK), I32),
                        pltpu.VMEM((nch, n, ATT_CK), F32),
                        pltpu.VMEM((2, KV_DIM, past), F32),
                        pltpu.VMEM((2, KV_DIM, past), F32),
                        pltpu.SemaphoreType.DMA((2,)),
                        pltpu.SemaphoreType.DMA((2,))],
    )
    return pl.pallas_call(
        functools.partial(_sample_attn_kernel, n_pages=n_pages),
        grid_spec=grid_spec,
        out_shape=jax.ShapeDtypeStruct((n, N_HEADS, HEAD_DIM), F32),
        compiler_params=_cparams(("arbitrary",)),
        name="sample_attn",
    )(page_table.reshape(-1), scores3, qbd, knew, vnew, ck2, cv2)


def _merge_ln1_kernel(y_ref, o_ref, gc_ref, ga_ref, x_ref, wc_ref, wa_ref, wo_ref, g_ref, b_ref, wr_ref,
                      x1_ref, ei_ref, gt_ref):
    conv_out = jnp.dot(y_ref[...], wc_ref[...], preferred_element_type=F32)
    attn_out = jnp.dot(o_ref[...], wa_ref[...], preferred_element_type=F32)
    merged = gc_ref[...].astype(F32) * conv_out + ga_ref[...].astype(F32) * attn_out
    mix = jnp.dot(merged.astype(BF16), wo_ref[...], preferred_element_type=F32)
    x1 = _layer_norm(DEEPNORM_ALPHA * x_ref[...] + mix, g_ref[...], b_ref[...])
    x1_ref[...] = x1

    logits = jnp.dot(x1, wr_ref[...], preferred_element_type=F32, precision=lax.Precision.HIGHEST)
    lane = lax.broadcasted_iota(I32, logits.shape, 1)
    is_g = lane < N_GROUPS
    lg = jnp.where(is_g, logits, -jnp.inf)
    eg = jnp.exp(lg - jnp.max(lg, axis=1, keepdims=True))
    pg = eg / jnp.sum(eg, axis=1, keepdims=True)
    g_top = jnp.max(pg, axis=1, keepdims=True)
    g_idx = jnp.min(jnp.where(is_g & (pg == g_top), lane, LANES), axis=1, keepdims=True)
    lo = N_GROUPS + g_idx * EXPERTS_PER_GROUP
    is_e = (lane >= lo) & (lane < lo + EXPERTS_PER_GROUP)
    le = jnp.where(is_e, logits, -jnp.inf)
    ee = jnp.exp(le - jnp.max(le, axis=1, keepdims=True))
    pe = ee / jnp.sum(ee, axis=1, keepdims=True)
    p1 = jnp.max(jnp.where(is_e, pe, -1.0), axis=1, keepdims=True)
    i1 = jnp.min(jnp.where(is_e & (pe == p1), lane, 2 * LANES), axis=1, keepdims=True)
    rest = is_e & (lane != i1)
    p2 = jnp.max(jnp.where(rest, pe, -1.0), axis=1, keepdims=True)
    i2 = jnp.min(jnp.where(rest & (pe == p2), lane, 2 * LANES), axis=1, keepdims=True)
    den = p1 + p2
    ei_ref[...] = jnp.where(lane == 0, i1 - N_GROUPS, jnp.where(lane == 1, i2 - N_GROUPS, 0))
    gt_ref[...] = jnp.where(lane == 0, g_top * p1 / den, jnp.where(lane == 1, g_top * p2 / den, 0.0))


def _merge_ln1(yact, o, gc, ga, x, wc, wa, wo, g1, b1, wr, tm):
    m = x.shape[0]
    row = lambda w: pl.BlockSpec((tm, w), lambda i: (i, 0))
    res = lambda a: pl.BlockSpec(a.shape, lambda i: (0, 0), pipeline_mode=pl.Buffered(1))
    return pl.pallas_call(
        _merge_ln1_kernel,
        grid=(m // tm,),
        in_specs=[row(C_CONV), row(N_HEADS * HEAD_DIM), row(D_MODEL), row(D_MODEL), row(D_MODEL),
                  res(wc), res(wa), res(wo), res(g1), res(b1), res(wr)],
        out_specs=(row(D_MODEL), row(LANES), row(LANES)),
        out_shape=(jax.ShapeDtypeStruct((m, D_MODEL), F32),
                   jax.ShapeDtypeStruct((m, LANES), I32),
                   jax.ShapeDtypeStruct((m, LANES), F32)),
        compiler_params=_cparams(("arbitrary",)),
        name="merge_ln1",
    )(yact, o, gc, ga, x, wc, wa, wo, g1, b1, wr)


def _experts_kernel(bexp_ref, nused_ref, rtok_ref, x_hbm, roww_ref, wg_ref, wu_ref, wd_ref, y_ref,
                    xbuf, wgb, wub, wdb, sem):
    b = pl.program_id(0)
    n_used = nused_ref[0]
    slot = b % 2

    def gather(blk, sl):
        def body(i, carry):
            tok = rtok_ref[blk * MOE_BLK + i]
            pltpu.make_async_copy(x_hbm.at[pl.ds(tok, 1)], xbuf.at[sl, pl.ds(i, 1)], sem.at[sl]).start()
            return carry
        lax.fori_loop(0, MOE_BLK, body, 0)

    @pl.when((b == 0) & (n_used > 0))
    def _():
        gather(0, 0)

    @pl.when(b + 1 < n_used)
    def _():
        gather(b + 1, 1 - slot)

    @pl.when(b < n_used)
    def _():
        new_expert = (b == 0) | (bexp_ref[b] != bexp_ref[jnp.maximum(b - 1, 0)])

        @pl.when(new_expert)
        def _():
            wgb[...] = wg_ref[0].astype(BF16)
            wub[...] = wu_ref[0].astype(BF16)
            wdb[...] = wd_ref[0].astype(BF16)

        pltpu.make_async_copy(xbuf.at[slot], xbuf.at[slot], sem.at[slot]).wait()
        xb = xbuf[slot].astype(BF16)
        hg = jnp.dot(xb, wgb[...], preferred_element_type=F32)
        hu = jnp.dot(xb, wub[...], preferred_element_type=F32)
        h = (hg * jax.nn.sigmoid(hg) * hu).astype(BF16)
        y_ref[...] = jnp.dot(h, wdb[...], preferred_element_type=F32) * roww_ref[...]

    @pl.when(b >= n_used)
    def _():
        y_ref[...] = jnp.zeros_like(y_ref)


def _experts(block_exp, n_used, row_tok, x1_all, row_w, wg, wu, wd):
    n_blocks = block_exp.shape[0]
    n_rows = n_blocks * MOE_BLK
    grid_spec = pltpu.PrefetchScalarGridSpec(
        num_scalar_prefetch=3,
        grid=(n_blocks,),
        in_specs=[pl.BlockSpec(memory_space=pl.ANY),
                  pl.BlockSpec((MOE_BLK, 1), lambda b, be, nu, rt: (b, 0)),
                  pl.BlockSpec((1, D_MODEL, D_EXPERT), lambda b, be, nu, rt: (be[b], 0, 0)),
                  pl.BlockSpec((1, D_MODEL, D_EXPERT), lambda b, be, nu, rt: (be[b], 0, 0)),
                  pl.BlockSpec((1, D_EXPERT, D_MODEL), lambda b, be, nu, rt: (be[b], 0, 0))],
        out_specs=pl.BlockSpec((MOE_BLK, D_MODEL), lambda b, be, nu, rt: (b, 0)),
        scratch_shapes=[pltpu.VMEM((2, MOE_BLK, D_MODEL), F32),
                        pltpu.VMEM((D_MODEL, D_EXPERT), BF16),
                        pltpu.VMEM((D_MODEL, D_EXPERT), BF16),
                        pltpu.VMEM((D_EXPERT, D_MODEL), BF16),
                        pltpu.SemaphoreType.DMA((2,))],
    )
    return pl.pallas_call(
        _experts_kernel,
        grid_spec=grid_spec,
        out_shape=jax.ShapeDtypeStruct((n_rows, D_MODEL), F32),
        compiler_params=_cparams(("arbitrary",)),
        name="experts",
    )(block_exp, n_used, row_tok, x1_all, row_w, wg, wu, wd)


def _combine_ln2_kernel(dest_ref, x1_ref, yb_hbm, g_ref, b_ref, y_ref, ybuf, sem, *, tm, tok0):
    i = pl.program_id(0)
    nt = pl.num_programs(0)
    slot = i % 2

    def gather(tile, sl):
        def body(r, carry):
            a = 2 * (tok0 + tile * tm + r)
            pltpu.make_async_copy(yb_hbm.at[pl.ds(dest_ref[a], 1)], ybuf.at[sl, 0, pl.ds(r, 1)], sem.at[sl]).start()
            pltpu.make_async_copy(yb_hbm.at[pl.ds(dest_ref[a + 1], 1)], ybuf.at[sl, 1, pl.ds(r, 1)],
                                  sem.at[sl]).start()
            return carry
        lax.fori_loop(0, tm, body, 0)

    @pl.when(i == 0)
    def _():
        gather(0, 0)

    @pl.when(i + 1 < nt)
    def _():
        gather(i + 1, 1 - slot)

    pltpu.make_async_copy(ybuf.at[slot], ybuf.at[slot], sem.at[slot]).wait()
    z = DEEPNORM_ALPHA * x1_ref[...] + (ybuf[slot, 0] + ybuf[slot, 1])
    y_ref[...] = _layer_norm(z, g_ref[...], b_ref[...])


def _combine_ln2(dest, x1, yb, g2, b2, tm, tok0):
    m = x1.shape[0]
    grid_spec = pltpu.PrefetchScalarGridSpec(
        num_scalar_prefetch=1,
        grid=(m // tm,),
        in_specs=[pl.BlockSpec((tm, D_MODEL), lambda i, d: (i, 0)),
                  pl.BlockSpec(memory_space=pl.ANY),
                  pl.BlockSpec((1, D_MODEL), lambda i, d: (0, 0)),
                  pl.BlockSpec((1, D_MODEL), lambda i, d: (0, 0))],
        out_specs=pl.BlockSpec((tm, D_MODEL), lambda i, d: (i, 0)),
        scratch_shapes=[pltpu.VMEM((2, 2, tm, D_MODEL), F32),
                        pltpu.SemaphoreType.DMA((2,))],
    )
    return pl.pallas_call(
        functools.partial(_combine_ln2_kernel, tm=tm, tok0=tok0),
        grid_spec=grid_spec,
        out_shape=jax.ShapeDtypeStruct((m, D_MODEL), F32),
        compiler_params=_cparams(("arbitrary",)),
        name="combine_ln2",
    )(dest, x1, yb, g2, b2)


def _dispatch_tables(eid, gate):
    t = eid.shape[0]
    a = 2 * t
    eid = eid.reshape(a)
    order = jnp.argsort(eid)
    eid_s = eid[order]
    counts = jnp.zeros((N_EXPERTS,), I32).at[eid].add(1)
    start = jnp.cumsum(counts) - counts
    padded = (counts + MOE_BLK - 1) // MOE_BLK * MOE_BLK
    pad_end = jnp.cumsum(padded)
    pad_start = pad_end - padded
    dest_s = pad_start[eid_s] + (jnp.arange(a, dtype=I32) - start[eid_s])
    n_blocks = (a + N_EXPERTS * (MOE_BLK - 1) + MOE_BLK - 1) // MOE_BLK
    n_rows = n_blocks * MOE_BLK
    row_tok = jnp.zeros((n_rows,), I32).at[dest_s].set((order // 2).astype(I32))
    row_w = jnp.zeros((n_rows,), F32).at[dest_s].set(gate.reshape(a)[order])
    block_exp = jnp.minimum(jnp.searchsorted(pad_end, jnp.arange(n_blocks, dtype=I32) * MOE_BLK, side='right'),
                            N_EXPERTS - 1).astype(I32)
    dest = jnp.zeros((a,), I32).at[order].set(dest_s.astype(I32))
    n_used = (pad_end[-1:] // MOE_BLK).astype(I32)
    return row_tok, row_w, block_exp, n_used, dest


def kernel(x_prompt, x_sample, cache_k, cache_v, cache_kidx, state_conv, page_table, w_in, conv_dw_w, conv_dw_b,
           conv_ln_g, conv_ln_b, w_conv_out, w_attn_out, w_out, ln1_g, ln1_b, w_router_group, w_router_expert,
           w_expert_gate, w_expert_up, w_expert_down, ln2_g, ln2_b):
    assert w_in.shape[0] == 1, "single-layer trunk"
    nb, t, d = x_prompt.shape
    ns = x_sample.shape[0]
    n_pool = cache_k.shape[1]
    mp = nb * t
    row2 = lambda a: a.reshape(1, -1)

    w_arr = _arrange_w_in(w_in[0])
    dww, dwb, lng, lnb = conv_dw_w[0], row2(conv_dw_b[0]), row2(conv_ln_g[0]), row2(conv_ln_b[0])
    wc, wa, wo = w_conv_out[0].astype(BF16), w_attn_out[0].astype(BF16), w_out[0].astype(BF16)
    g1, b1, g2, b2 = row2(ln1_g[0]), row2(ln1_b[0]), row2(ln2_g[0]), row2(ln2_b[0])
    wr = jnp.concatenate([w_router_group[0], w_router_expert[0],
                          jnp.zeros((d, LANES - N_GROUPS - N_EXPERTS), F32)], axis=1)

    xp = x_prompt.reshape(mp, d)
    u_p, q_p, qi_p, kv_p, kvb_p, gc_p, ga_p, kiwi_p, kib_p = _in_proj(xp, w_arr, tm=512)
    yact_p = _conv_prompt(u_p.reshape(nb, t, C_CONV), dww, dwb, lng, lnb)
    o_p = _attn_prompt(qi_p.reshape(nb, t, -1), kiwi_p.reshape(nb, t, LANES), kib_p.reshape(nb, t, LANES),
                       q_p.reshape(nb, t, -1), kvb_p.reshape(nb, t, IN_TN))
    x1_p, ei_p, gt_p = _merge_ln1(yact_p.reshape(mp, C_CONV), o_p.reshape(mp, -1), gc_p, ga_p, xp,
                                  wc, wa, wo, g1, b1, wr, tm=256)

    xs = x_sample.reshape(ns, d)
    u_s, q_s, qi_s, kv_s, _, gc_s, ga_s, kiwi_s, _ = _in_proj(xs, w_arr, tm=ns)
    yact_s = _conv_sample(state_conv[0].transpose(1, 0, 2), u_s, dww, dwb, lng, lnb)
    cki2 = cache_kidx.transpose(0, 1, 3, 2).reshape(n_pool, IDX_DIM, PAGE_SIZE)
    ck2 = cache_k.transpose(0, 1, 3, 4, 2).reshape(n_pool, KV_DIM, PAGE_SIZE)
    cv2 = cache_v.transpose(0, 1, 3, 4, 2).reshape(n_pool, KV_DIM, PAGE_SIZE)
    scores_s = _sample_scores(page_table, qi_s.reshape(ns, IDX_HEADS, IDX_DIM),
                              kiwi_s[:, IDX_DIM:IDX_DIM + IDX_HEADS].reshape(ns, IDX_HEADS, 1),
                              kiwi_s[:, :IDX_DIM].reshape(ns, 1, IDX_DIM), cki2)
    scores3 = scores_s.reshape(ns, -1, ATT_CK).transpose(1, 0, 2)
    own_group = (np.arange(N_HEADS)[:, None] // HEADS_PER_KV) == (np.arange(KV_DIM)[None, :] // HEAD_DIM)
    qbd = jnp.where(own_group[None], jnp.tile(q_s.reshape(ns, N_HEADS, HEAD_DIM), (1, 1, KV_HEADS)), 0).astype(BF16)
    o3 = _sample_attn(page_table, scores3, qbd, kv_s[:, :KV_DIM].reshape(ns, 1, KV_DIM),
                      kv_s[:, KV_DIM:].reshape(ns, 1, KV_DIM), ck2, cv2)
    o_s = o3.reshape(ns, N_HEADS * HEAD_DIM).astype(BF16)
    x1_s, ei_s, gt_s = _merge_ln1(yact_s, o_s, gc_s, ga_s, xs, wc, wa, wo, g1, b1, wr, tm=ns)

    x1_all = jnp.concatenate([x1_p, x1_s], axis=0)
    eid = jnp.concatenate([ei_p[:, :2], ei_s[:, :2]], axis=0)
    gate = jnp.concatenate([gt_p[:, :2], gt_s[:, :2]], axis=0)
    row_tok, row_w, block_exp, n_used, dest = _dispatch_tables(eid, gate)
    yb = _experts(block_exp, n_used, row_tok, x1_all, row_w.reshape(-1, 1),
                  w_expert_gate[0], w_expert_up[0], w_expert_down[0])
    y_p = _combine_ln2(dest, x1_p, yb, g2, b2, tm=256, tok0=0)
    y_s = _combine_ln2(dest, x1_s, yb, g2, b2, tm=ns, tok0=mp)

    y_prompt = y_p.reshape(nb, t, d)
    y_sample = y_s.reshape(ns, 1, d)
    k_prompt = kv_p[:, :KV_DIM].reshape(1, nb, t, KV_HEADS, HEAD_DIM)
    v_prompt = kv_p[:, KV_DIM:].reshape(1, nb, t, KV_HEADS, HEAD_DIM)
    kidx_prompt = kiwi_p[:, :IDX_DIM].reshape(1, nb, t, IDX_DIM)
    conv_prompt = u_p.reshape(nb, t, C_CONV)[:, t - (CONV_W - 1):][None]
    k_sample = kv_s[:, :KV_DIM].reshape(1, ns, 1, KV_HEADS, HEAD_DIM)
    v_sample = kv_s[:, KV_DIM:].reshape(1, ns, 1, KV_HEADS, HEAD_DIM)
    kidx_sample = kiwi_s[:, :IDX_DIM].reshape(1, ns, 1, IDX_DIM)
    conv_sample = jnp.concatenate([state_conv[0][:, 1:], u_s[:, None, :]], axis=1)[None]
    return (y_prompt, y_sample, k_prompt, v_prompt, kidx_prompt, conv_prompt, k_sample, v_sample, kidx_sample,
            conv_sample)
```

```python
import functools

import jax
import jax.numpy as jnp
import numpy as np
from jax import lax
from jax.experimental import pallas as pl
from jax.experimental.pallas import tpu as pltpu

F32 = jnp.float32
BF16 = jnp.bfloat16
I32 = jnp.int32

D_MODEL = 2048
C_CONV = 1024
CONV_W = 31
N_HEADS = 16
HEAD_DIM = 64
KV_HEADS = 4
HEADS_PER_KV = N_HEADS // KV_HEADS
KV_DIM = KV_HEADS * HEAD_DIM
IDX_HEADS = 16
IDX_DIM = 64
MAX_TOPK = 256
N_GROUPS = 8
EXPERTS_PER_GROUP = 8
N_EXPERTS = 64
D_EXPERT = 512
MOE_BLK = 128
PAGE_SIZE = 128
LN_EPS = 1e-5
DEEPNORM_ALPHA = 2.0 ** 0.25
NEG_INF = -1e30
INT_MIN = -(2 ** 31)
NEG_KEY = int(INT_MIN - int(np.array(NEG_INF, np.float32).view(np.int32)))
LOG2E = 1.4426950408889634
Q_SCALE = HEAD_DIM ** -0.5 * LOG2E
IDX_SCALE = IDX_HEADS ** -0.5 * IDX_DIM ** -0.5

LANES = 128
HALF = LANES // 2
IN_TN = 1024
VMEM_LIMIT = 56 * 1024 * 1024


def _cparams(sem, vmem=VMEM_LIMIT):
    return pltpu.CompilerParams(dimension_semantics=sem, vmem_limit_bytes=vmem)


def _float_key(x):
    b = pltpu.bitcast(x, I32)
    return jnp.where(b < 0, INT_MIN - b, b)


def _layer_norm(z, g, b):
    mu = jnp.mean(z, axis=-1, keepdims=True)
    zc = z - mu
    var = jnp.mean(zc * zc, axis=-1, keepdims=True)
    return zc * lax.rsqrt(var + LN_EPS) * g + b


def _dot_nt(a, b):
    return lax.dot_general(a, b, (((1,), (1,)), ((), ())), preferred_element_type=F32)


def _alibi_slope(h):
    return float(2.0 ** (-8.0 * (h + 1) / N_HEADS))


_T_GLU, _T_Q, _T_QI, _T_KV, _T_GC, _T_GA, _N_TILES = 0, 2, 3, 4, 5, 7, 9
KV_TILE_COLS = 2 * KV_DIM + 2 * LANES


def _arrange_w_in(w):
    sizes = (C_CONV, C_CONV, N_HEADS * HEAD_DIM, KV_DIM, KV_DIM,
             IDX_HEADS * IDX_DIM, IDX_DIM, IDX_HEADS, D_MODEL, D_MODEL)
    offs = np.cumsum((0,) + sizes)
    a, g, q, k, v, qi, ki, wi, gc, ga = [w[:, offs[i]:offs[i + 1]] for i in range(10)]
    half = IN_TN // 2
    glu = [jnp.concatenate([a[:, t * half:(t + 1) * half], g[:, t * half:(t + 1) * half]], axis=1)
           for t in range(C_CONV // half)]
    zeros = lambda n: jnp.zeros((w.shape[0], n), w.dtype)
    kv_tile = jnp.concatenate([k, v, ki, wi, zeros(LANES - IDX_DIM - IDX_HEADS), ki, ki,
                               zeros(IN_TN - KV_TILE_COLS)], axis=1)
    return jnp.concatenate(glu + [q, qi, kv_tile, gc, ga], axis=1).astype(BF16)


def _in_proj_kernel(x_ref, w_ref, u_ref, q_ref, qi_ref, kv_ref, kvb_ref, gc_ref, ga_ref, kiwi_ref, kib_ref,
                    xb_ref):
    j = pl.program_id(1)

    @pl.when(j == 0)
    def _():
        xb_ref[...] = x_ref[...].astype(BF16)

    def mm(ncols=IN_TN):
        return jnp.dot(xb_ref[...], w_ref[:, :ncols], preferred_element_type=F32)

    @pl.when(j < _T_Q)
    def _():
        acc = mm()
        u_ref[...] = acc[:, :IN_TN // 2] * jax.nn.sigmoid(acc[:, IN_TN // 2:])

    @pl.when(j == _T_Q)
    def _():
        q_ref[...] = (mm() * Q_SCALE).astype(BF16)

    @pl.when(j == _T_QI)
    def _():
        qi_ref[...] = mm().astype(BF16)

    @pl.when(j == _T_KV)
    def _():
        acc = mm(KV_TILE_COLS)
        kv_ref[...] = acc[:, :2 * KV_DIM]
        kvb_ref[...] = acc[:, :2 * KV_DIM].astype(BF16)
        kiwi_ref[...] = acc[:, 2 * KV_DIM:2 * KV_DIM + LANES]
        kib_ref[...] = acc[:, 2 * KV_DIM + LANES:].astype(BF16)

    @pl.when((j >= _T_GC) & (j < _T_GA))
    def _():
        gc_ref[...] = jax.nn.sigmoid(mm()).astype(BF16)

    @pl.when(j >= _T_GA)
    def _():
        ga_ref[...] = jax.nn.sigmoid(mm()).astype(BF16)


def _in_proj(x, w_arr, tm):
    m = x.shape[0]
    half = IN_TN // 2

    def cl(lo, n):
        return lambda i, j: (i, jnp.clip(j - lo, 0, n - 1))

    out_shape = (
        jax.ShapeDtypeStruct((m, C_CONV), F32),
        jax.ShapeDtypeStruct((m, N_HEADS * HEAD_DIM), BF16),
        jax.ShapeDtypeStruct((m, IDX_HEADS * IDX_DIM), BF16),
        jax.ShapeDtypeStruct((m, 2 * KV_DIM), F32),
        jax.ShapeDtypeStruct((m, 2 * KV_DIM), BF16),
        jax.ShapeDtypeStruct((m, D_MODEL), BF16),
        jax.ShapeDtypeStruct((m, D_MODEL), BF16),
        jax.ShapeDtypeStruct((m, LANES), F32),
        jax.ShapeDtypeStruct((m, LANES), BF16),
    )
    out_specs = (
        pl.BlockSpec((tm, half), cl(_T_GLU, 2)),
        pl.BlockSpec((tm, IN_TN), lambda i, j: (i, 0)),
        pl.BlockSpec((tm, IN_TN), lambda i, j: (i, 0)),
        pl.BlockSpec((tm, 2 * KV_DIM), lambda i, j: (i, 0)),
        pl.BlockSpec((tm, 2 * KV_DIM), lambda i, j: (i, 0)),
        pl.BlockSpec((tm, IN_TN), cl(_T_GC, 2)),
        pl.BlockSpec((tm, IN_TN), cl(_T_GA, 2)),
        pl.BlockSpec((tm, LANES), lambda i, j: (i, 0)),
        pl.BlockSpec((tm, LANES), lambda i, j: (i, 0)),
    )
    return pl.pallas_call(
        _in_proj_kernel,
        grid=(m // tm, _N_TILES),
        in_specs=[pl.BlockSpec((tm, D_MODEL), lambda i, j: (i, 0)),
                  pl.BlockSpec((D_MODEL, IN_TN), lambda i, j: (0, j))],
        out_specs=out_specs,
        out_shape=out_shape,
        scratch_shapes=[pltpu.VMEM((tm, D_MODEL), BF16)],
        compiler_params=_cparams(("arbitrary", "arbitrary")),
        name="in_proj",
    )(x, w_arr)


CONV_HALO = 32
CONV_STRIP = 32


SUBLANES = 8


def _conv_prompt_kernel(ucur_ref, uprev_ref, dww_ref, dwb_ref, lng_ref, lnb_ref, y_ref, ext_ref, sh_ref, *, tt):
    tb = pl.program_id(1)
    ext_ref[0:CONV_HALO, :] = jnp.where(tb > 0, uprev_ref[0], 0.0)
    ext_ref[CONV_HALO:, :] = ucur_ref[0]
    first = CONV_HALO - (CONV_W - 1)
    for b in range(SUBLANES):
        rows = tt + SUBLANES * ((CONV_W - 1 - b) // SUBLANES)
        sh_ref[b, 0:rows, :] = ext_ref[first + b:first + b + rows, :]
    dwb, lng, lnb = dwb_ref[...], lng_ref[...], lnb_ref[...]
    for r in range(tt // CONV_STRIP):
        acc = jnp.broadcast_to(dwb, (CONV_STRIP, C_CONV))
        for j in range(CONV_W):
            a, b = divmod(j, SUBLANES)
            r0 = r * CONV_STRIP + a * SUBLANES
            acc = acc + dww_ref[j:j + 1, :] * sh_ref[b, r0:r0 + CONV_STRIP, :]
        yn = _layer_norm(acc, lng, lnb)
        y_ref[0, r * CONV_STRIP:(r + 1) * CONV_STRIP, :] = (yn * jax.nn.sigmoid(yn)).astype(BF16)


def _conv_prompt(u, dww, dwb, lng, lnb, tt=128):
    n, t, c = u.shape
    per = tt // CONV_HALO
    vec = pl.BlockSpec((1, c), lambda b, i: (0, 0))
    return pl.pallas_call(
        functools.partial(_conv_prompt_kernel, tt=tt),
        grid=(n, t // tt),
        in_specs=[pl.BlockSpec((1, tt, c), lambda b, i: (b, i, 0)),
                  pl.BlockSpec((1, CONV_HALO, c), lambda b, i: (b, jnp.maximum(i * per - 1, 0), 0)),
                  pl.BlockSpec((CONV_W, c), lambda b, i: (0, 0)), vec, vec, vec],
        out_specs=pl.BlockSpec((1, tt, c), lambda b, i: (b, i, 0)),
        out_shape=jax.ShapeDtypeStruct((n, t, c), BF16),
        scratch_shapes=[pltpu.VMEM((tt + CONV_HALO, c), F32),
                        pltpu.VMEM((SUBLANES, tt + SUBLANES * ((CONV_W - 1) // SUBLANES), c), F32)],
        compiler_params=_cparams(("arbitrary", "arbitrary")),
        name="conv_prompt",
    )(u, u, dww, dwb, lng, lnb)


def _conv_sample_kernel(state_ref, u_ref, dww_ref, dwb_ref, lng_ref, lnb_ref, y_ref):
    acc = dwb_ref[...] + dww_ref[CONV_W - 1:CONV_W, :] * u_ref[...]
    for j in range(CONV_W - 1):
        acc = acc + dww_ref[j:j + 1, :] * state_ref[j]
    yn = _layer_norm(acc, lng_ref[...], lnb_ref[...])
    y_ref[...] = (yn * jax.nn.sigmoid(yn)).astype(BF16)


def _conv_sample(state_t, u, dww, dwb, lng, lnb):
    n = u.shape[0]
    return pl.pallas_call(
        _conv_sample_kernel,
        out_shape=jax.ShapeDtypeStruct((n, C_CONV), BF16),
        compiler_params=pltpu.CompilerParams(vmem_limit_bytes=VMEM_LIMIT),
        name="conv_sample",
    )(state_t, u, dww, dwb, lng, lnb)


ATT_TQ = 256
ATT_CK = 256


def _count_ge(key_ref, nch, cand):
    def body(c, acc):
        m = jnp.where(key_ref[c] >= cand, 1.0, 0.0)
        return acc + m[:, :LANES] + m[:, LANES:]
    acc = lax.fori_loop(0, nch, body, jnp.zeros((cand.shape[0], LANES), F32))
    return jnp.sum(acc, axis=1, keepdims=True)


def _select_threshold(count_fn, shape):
    def body(i, prefix):
        cand = prefix + jnp.left_shift(jnp.int32(1), 31 - i)
        return jnp.where(count_fn(cand) >= float(MAX_TOPK), cand, prefix)
    return lax.fori_loop(0, 32, body, jnp.full(shape, INT_MIN, I32))


def _selection_bias(key_ref, bias_ref, nch):
    rows, ck = key_ref.shape[1:]
    thr = _select_threshold(lambda cand: _count_ge(key_ref, nch, cand), (rows, 1))
    n_tie_take = float(MAX_TOPK) - _count_ge(key_ref, nch, thr + 1)
    tri = jnp.where(lax.broadcasted_iota(I32, (ck, ck), 0) <= lax.broadcasted_iota(I32, (ck, ck), 1),
                    1.0, 0.0).astype(BF16)

    def bias_chunk(c, seen):
        key = key_ref[c]
        eq = key == thr
        eqf = jnp.where(eq, 1.0, 0.0)
        incl = jnp.dot(eqf.astype(BF16), tri, preferred_element_type=F32)
        sel = (key > thr) | (eq & (seen + incl - eqf < n_tie_take))
        bias_ref[c] = jnp.where(sel & (key > NEG_KEY), 0.0, NEG_INF)
        return seen + incl[:, ck - 1:ck]

    lax.fori_loop(0, nch, bias_chunk, jnp.zeros((rows, 1), F32))


def _swap_halves(x):
    return jnp.concatenate([x[:, HALF:], x[:, :HALF]], axis=1)


def _attn_prompt_kernel(qi_ref, kiwi_ref, kib_ref, q_ref, kvb_ref, o_ref,
                        key_ref, bias_ref, wb_ref, m_ref, l_ref, acc_ref):
    tq, ck = ATT_TQ, ATT_CK
    qb = pl.program_id(1)
    nch = qb + 1
    row = qb * tq + lax.broadcasted_iota(I32, (tq, ck), 0)
    coll = lax.broadcasted_iota(I32, (tq, ck), 1)
    lo_k = lax.broadcasted_iota(I32, (ck, LANES), 1) < HALF
    lo_q = lax.broadcasted_iota(I32, (tq, LANES), 1) < HALF

    w = kiwi_ref[0][:, IDX_DIM:IDX_DIM + IDX_HEADS] * IDX_SCALE
    for h in range(IDX_HEADS):
        wb_ref[h] = jnp.broadcast_to(w[:, h:h + 1], (tq, LANES))

    def score_chunk(c, carry):
        k0 = pl.multiple_of(c * ck, ck)
        kk = kib_ref[0, pl.ds(k0, ck), :]
        zero = jnp.zeros_like(kk)
        k_lo, k_hi = jnp.where(lo_k, kk, zero), jnp.where(lo_k, zero, kk)
        acc = jnp.zeros((tq, ck), F32)
        for p in range(IDX_HEADS // 2):
            qp = qi_ref[0, :, p * LANES:(p + 1) * LANES]
            w_lo, w_hi = wb_ref[2 * p], wb_ref[2 * p + 1]
            acc = acc + jnp.concatenate([w_lo, w_lo], axis=1) * jnp.maximum(_dot_nt(qp, k_lo), 0.0)
            acc = acc + jnp.concatenate([w_hi, w_hi], axis=1) * jnp.maximum(_dot_nt(qp, k_hi), 0.0)
        acc = jnp.where(c * ck + coll <= row, acc, NEG_INF)
        key_ref[c] = _float_key(acc)
        return carry

    lax.fori_loop(0, nch, score_chunk, 0)
    _selection_bias(key_ref, bias_ref, nch)

    m_ref[...] = jnp.full(m_ref.shape, -jnp.inf, F32)
    l_ref[...] = jnp.zeros(l_ref.shape, F32)
    acc_ref[...] = jnp.zeros(acc_ref.shape, F32)

    def attn_chunk(c, carry):
        k0 = pl.multiple_of(c * ck, ck)
        kv = kvb_ref[0, pl.ds(k0, ck), :]
        bias = bias_ref[c]
        rel = (c * ck - (qb + 1) * tq + 1 + lax.broadcasted_iota(I32, (1, ck), 1)).astype(F32)
        for j in range(KV_HEADS // 2):
            kt = kv[:, j * LANES:(j + 1) * LANES]
            vt = kv[:, KV_DIM + j * LANES:KV_DIM + (j + 1) * LANES]
            kts, vts = _swap_halves(kt), _swap_halves(vt)
            zero = jnp.zeros_like(kt)
            for gg in range(2):
                g = 2 * j + gg
                k_own, k_swp = (kt, kts) if gg == 0 else (kts, kt)
                v_own, v_swp = (vt, vts) if gg == 0 else (vts, vt)
                k_lo, k_hi = jnp.where(lo_k, k_own, zero), jnp.where(lo_k, zero, k_swp)
                v_lo, v_hi = jnp.where(lo_k, v_own, zero), jnp.where(lo_k, zero, v_swp)
                for mm in range(HEADS_PER_KV // 2):
                    pidx = (HEADS_PER_KV // 2) * g + mm
                    qp = q_ref[0, :, pidx * LANES:(pidx + 1) * LANES]
                    pv, alphas = None, []
                    for half, (kmat, vmat) in enumerate(((k_lo, v_lo), (k_hi, v_hi))):
                        h = 2 * pidx + half
                        s = _dot_nt(qp, kmat) + (_alibi_slope(h) * LOG2E) * rel + bias
                        m_old = m_ref[h]
                        m_new = jnp.maximum(m_old, jnp.max(s, axis=1, keepdims=True))
                        p = jnp.exp2(s - jnp.concatenate([m_new, m_new], axis=1))
                        alpha = jnp.exp2(m_old - m_new)
                        l_ref[h] = alpha * l_ref[h] + jnp.sum(p, axis=1, keepdims=True)
                        m_ref[h] = m_new
                        d = jnp.dot(p.astype(BF16), vmat, preferred_element_type=F32)
                        pv = d if pv is None else pv + d
                        alphas.append(alpha)
                    acc_ref[pidx] = jnp.where(lo_q, alphas[0], alphas[1]) * acc_ref[pidx] + pv
        return carry

    lax.fori_loop(0, nch, attn_chunk, 0)
    for pidx in range(N_HEADS // 2):
        l_pair = jnp.where(lo_q, l_ref[2 * pidx], l_ref[2 * pidx + 1])
        o_ref[0, :, pidx * LANES:(pidx + 1) * LANES] = (acc_ref[pidx] / l_pair).astype(BF16)


def _attn_prompt(qi, kiwi, kib, q, kvb):
    n, t, _ = q.shape
    tq = ATT_TQ
    return pl.pallas_call(
        _attn_prompt_kernel,
        grid=(n, t // tq),
        in_specs=[pl.BlockSpec((1, tq, IDX_HEADS * IDX_DIM), lambda b, i: (b, i, 0)),
                  pl.BlockSpec((1, tq, LANES), lambda b, i: (b, i, 0)),
                  pl.BlockSpec((1, t, LANES), lambda b, i: (b, 0, 0)),
                  pl.BlockSpec((1, tq, N_HEADS * HEAD_DIM), lambda b, i: (b, i, 0)),
                  pl.BlockSpec((1, t, 2 * KV_DIM), lambda b, i: (b, 0, 0))],
        out_specs=pl.BlockSpec((1, tq, N_HEADS * HEAD_DIM), lambda b, i: (b, i, 0)),
        out_shape=jax.ShapeDtypeStruct((n, t, N_HEADS * HEAD_DIM), BF16),
        scratch_shapes=[pltpu.VMEM((t // ATT_CK, tq, ATT_CK), I32),
                        pltpu.VMEM((t // ATT_CK, tq, ATT_CK), F32),
                        pltpu.VMEM((IDX_HEADS, tq, LANES), F32),
                        pltpu.VMEM((N_HEADS, tq, LANES), F32),
                        pltpu.VMEM((N_HEADS, tq, LANES), F32),
                        pltpu.VMEM((N_HEADS // 2, tq, LANES), F32)],
        compiler_params=_cparams(("arbitrary", "arbitrary")),
        name="attn_prompt",
    )(qi, kiwi, kib, q, kvb)


SAMPLE_KC = 2048


def _start_page_copies(pt_ref, src_hbm, dst, sem, base, n_pages):
    def body(p, carry):
        col = pl.multiple_of(p * PAGE_SIZE, PAGE_SIZE)
        pltpu.make_async_copy(src_hbm.at[pt_ref[base + p]], dst.at[:, pl.ds(col, PAGE_SIZE)], sem).start()
        return carry
    lax.fori_loop(0, n_pages, body, 0)


def _sample_score_kernel(pt_ref, qi_ref, wcol_ref, kinew_ref, cki_hbm, sc_ref, kibuf, sem, *, n_pages):
    b = pl.program_id(0)
    nb = pl.num_programs(0)
    past = n_pages * PAGE_SIZE
    slot = b % 2

    @pl.when(b == 0)
    def _():
        _start_page_copies(pt_ref, cki_hbm, kibuf.at[0], sem.at[0], 0, n_pages)

    @pl.when(b + 1 < nb)
    def _():
        _start_page_copies(pt_ref, cki_hbm, kibuf.at[1 - slot], sem.at[1 - slot], (b + 1) * n_pages, n_pages)

    pltpu.make_async_copy(kibuf.at[slot], kibuf.at[slot], sem.at[slot]).wait()

    qi = qi_ref[0]
    wsc = wcol_ref[0] * IDX_SCALE
    for c in range(past // SAMPLE_KC):
        kc = kibuf[slot, :, c * SAMPLE_KC:(c + 1) * SAMPLE_KC].astype(BF16)
        d = jnp.dot(qi, kc, preferred_element_type=F32)
        sc_ref[0, :, c * SAMPLE_KC:(c + 1) * SAMPLE_KC] = jnp.sum(wsc * jnp.maximum(d, 0.0), axis=0, keepdims=True)
    dn = jnp.sum(qi.astype(F32) * kinew_ref[0], axis=1, keepdims=True)
    s_new = jnp.sum(wsc * jnp.maximum(dn, 0.0), axis=0, keepdims=True)
    ln = lax.broadcasted_iota(I32, (1, ATT_CK), 1)
    sc_ref[0, :, past:] = jnp.where(ln == 0, s_new, NEG_INF)


def _sample_scores(page_table, qi3, wcol, kinew, cki2):
    n, n_pages = page_table.shape
    past = n_pages * PAGE_SIZE
    grid_spec = pltpu.PrefetchScalarGridSpec(
        num_scalar_prefetch=1,
        grid=(n,),
        in_specs=[pl.BlockSpec((1, IDX_HEADS, IDX_DIM), lambda b, pt: (b, 0, 0)),
                  pl.BlockSpec((1, IDX_HEADS, 1), lambda b, pt: (b, 0, 0)),
                  pl.BlockSpec((1, 1, IDX_DIM), lambda b, pt: (b, 0, 0)),
                  pl.BlockSpec(memory_space=pl.ANY)],
        out_specs=pl.BlockSpec((1, 1, past + ATT_CK), lambda b, pt: (b, 0, 0)),
        scratch_shapes=[pltpu.VMEM((2, IDX_DIM, past), F32),
                        pltpu.SemaphoreType.DMA((2,))],
    )
    return pl.pallas_call(
        functools.partial(_sample_score_kernel, n_pages=n_pages),
        grid_spec=grid_spec,
        out_shape=jax.ShapeDtypeStruct((n, 1, past + ATT_CK), F32),
        compiler_params=_cparams(("arbitrary",)),
        name="sample_scores",
    )(page_table.reshape(-1), qi3, wcol, kinew, cki2)


def _sample_attn_kernel(pt_ref, sc_ref, qbd_ref, knew_ref, vnew_ref, ck_hbm, cv_hbm, o_ref,
                        key_ref, bias_ref, kbuf, vbuf, semk, semv, *, n_pages):
    b = pl.program_id(0)
    nb = pl.num_programs(0)
    past = n_pages * PAGE_SIZE
    nch = sc_ref.shape[0]
    slot = b % 2

    def start_all(bb, sl):
        _start_page_copies(pt_ref, ck_hbm, kbuf.at[sl], semk.at[sl], bb * n_pages, n_pages)
        _start_page_copies(pt_ref, cv_hbm, vbuf.at[sl], semv.at[sl], bb * n_pages, n_pages)

    @pl.when(b == 0)
    def _():
        start_all(0, 0)
        key_ref[...] = _float_key(sc_ref[...])
        _selection_bias(key_ref, bias_ref, nch)

    @pl.when(b + 1 < nb)
    def _():
        start_all(b + 1, 1 - slot)

    pltpu.make_async_copy(kbuf.at[slot], kbuf.at[slot], semk.at[slot]).wait()
    pltpu.make_async_copy(vbuf.at[slot], vbuf.at[slot], semv.at[slot]).wait()

    qbd = qbd_ref[0]
    bias_row = jnp.concatenate([bias_ref[c, pl.ds(b, 1), :] for c in range(nch)], axis=1)
    s_parts = [jnp.dot(qbd, kbuf[slot, :, c * SAMPLE_KC:(c + 1) * SAMPLE_KC].astype(BF16),
                       preferred_element_type=F32) for c in range(past // SAMPLE_KC)]
    s_new = jnp.sum(qbd.astype(F32) * knew_ref[0], axis=1, keepdims=True)
    ln = lax.broadcasted_iota(I32, (N_HEADS, ATT_CK), 1)
    s = jnp.concatenate(s_parts + [jnp.where(ln == 0, s_new, 0.0)], axis=1)
    hrow = lax.broadcasted_iota(I32, (N_HEADS, 1), 0)
    slope2 = jnp.exp2(-8.0 * (hrow + 1).astype(F32) / N_HEADS) * LOG2E
    rel = (lax.broadcasted_iota(I32, (1, past + ATT_CK), 1) - past).astype(F32)
    s = s + slope2 * rel + bias_row
    m = jnp.max(s, axis=1, keepdims=True)
    p = jnp.exp2(s - m)
    l = jnp.sum(p, axis=1, keepdims=True)
    o_all = p[:, past:past + 1] * vnew_ref[0]
    for c in range(past // SAMPLE_KC):
        o_all = o_all + _dot_nt(p[:, c * SAMPLE_KC:(c + 1) * SAMPLE_KC].astype(BF16),
                                vbuf[slot, :, c * SAMPLE_KC:(c + 1) * SAMPLE_KC].astype(BF16))
    o_all = o_all / l
    out = jnp.zeros((N_HEADS, HEAD_DIM), F32)
    for g in range(KV_HEADS):
        own = (hrow >= g * HEADS_PER_KV) & (hrow < (g + 1) * HEADS_PER_KV)
        out = out + jnp.where(own, o_all[:, g * HEAD_DIM:(g + 1) * HEAD_DIM], 0.0)
    o_ref[0] = out


def _sample_attn(page_table, scores3, qbd, knew, vnew, ck2, cv2):
    n, n_pages = page_table.shape
    past = n_pages * PAGE_SIZE
    nch = scores3.shape[0]
    grid_spec = pltpu.PrefetchScalarGridSpec(
        num_scalar_prefetch=1,
        grid=(n,),
        in_specs=[pl.BlockSpec((nch, n, ATT_CK), lambda b, pt: (0, 0, 0)),
                  pl.BlockSpec((1, N_HEADS, KV_DIM), lambda b, pt: (b, 0, 0)),
                  pl.BlockSpec((1, 1, KV_DIM), lambda b, pt: (b, 0, 0)),
                  pl.BlockSpec((1, 1, KV_DIM), lambda b, pt: (b, 0, 0)),
                  pl.BlockSpec(memory_space=pl.ANY),
                  pl.BlockSpec(memory_space=pl.ANY)],
        out_specs=pl.BlockSpec((1, N_HEADS, HEAD_DIM), lambda b, pt: (b, 0, 0)),
        scratch_shapes=[pltpu.VMEM((nch, n, ATT_CK), I32),
                        pltpu.VMEM((nch, n, ATT_CK), F32),
                        pltpu.VMEM((2, KV_DIM, past), F32),
                        pltpu.VMEM((2, KV_DIM, past), F32),
                        pltpu.SemaphoreType.DMA((2,)),
                        pltpu.SemaphoreType.DMA((2,))],
    )
    return pl.pallas_call(
        functools.partial(_sample_attn_kernel, n_pages=n_pages),
        grid_spec=grid_spec,
        out_shape=jax.ShapeDtypeStruct((n, N_HEADS, HEAD_DIM), F32),
        compiler_params=_cparams(("arbitrary",)),
        name="sample_attn",
    )(page_table.reshape(-1), scores3, qbd, knew, vnew, ck2, cv2)


def _merge_ln1_kernel(*refs, aliased, n_tiles):
    n_in = 12
    x1_ref, ei_ref, gt_ref, cnt_ref = refs[n_in + 1:] if aliased else refs[n_in:]

    @pl.when(pl.program_id(0) < n_tiles)
    def _():
        _merge_ln1_tile(*refs[:n_in], x1_ref, ei_ref, gt_ref, cnt_ref)

    @pl.when(pl.program_id(0) >= n_tiles)
    def _():
        x1_ref[...] = jnp.zeros(x1_ref.shape, F32)


def _merge_ln1_tile(y_ref, o_ref, gc_ref, ga_ref, x_ref, wc_ref, wa_ref, wo_ref, g_ref, b_ref, wr_ref,
                    cnt_in_ref, x1_ref, ei_ref, gt_ref, cnt_ref):
    @pl.when(pl.program_id(0) == 0)
    def _():
        cnt_ref[...] = cnt_in_ref[...]

    conv_out = jnp.dot(y_ref[...], wc_ref[...], preferred_element_type=F32)
    attn_out = jnp.dot(o_ref[...], wa_ref[...], preferred_element_type=F32)
    merged = gc_ref[...].astype(F32) * conv_out + ga_ref[...].astype(F32) * attn_out
    mix = jnp.dot(merged.astype(BF16), wo_ref[...], preferred_element_type=F32)
    x1 = _layer_norm(DEEPNORM_ALPHA * x_ref[...] + mix, g_ref[...], b_ref[...])
    x1_ref[...] = x1

    logits = jnp.dot(x1, wr_ref[...], preferred_element_type=F32, precision=lax.Precision.HIGHEST)
    lane = lax.broadcasted_iota(I32, logits.shape, 1)
    is_g = lane < N_GROUPS
    lg = jnp.where(is_g, logits, -jnp.inf)
    eg = jnp.exp(lg - jnp.max(lg, axis=1, keepdims=True))
    pg = eg / jnp.sum(eg, axis=1, keepdims=True)
    g_top = jnp.max(pg, axis=1, keepdims=True)
    g_idx = jnp.min(jnp.where(is_g & (pg == g_top), lane, LANES), axis=1, keepdims=True)
    lo = N_GROUPS + g_idx * EXPERTS_PER_GROUP
    is_e = (lane >= lo) & (lane < lo + EXPERTS_PER_GROUP)
    le = jnp.where(is_e, logits, -jnp.inf)
    ee = jnp.exp(le - jnp.max(le, axis=1, keepdims=True))
    pe = ee / jnp.sum(ee, axis=1, keepdims=True)
    p1 = jnp.max(jnp.where(is_e, pe, -1.0), axis=1, keepdims=True)
    i1 = jnp.min(jnp.where(is_e & (pe == p1), lane, 2 * LANES), axis=1, keepdims=True)
    rest = is_e & (lane != i1)
    p2 = jnp.max(jnp.where(rest, pe, -1.0), axis=1, keepdims=True)
    i2 = jnp.min(jnp.where(rest & (pe == p2), lane, 2 * LANES), axis=1, keepdims=True)
    den = p1 + p2
    gt_ref[...] = jnp.where(lane == 0, g_top * p1 / den, jnp.where(lane == 1, g_top * p2 / den, 0.0))

    e1, e2 = i1 - N_GROUPS, i2 - N_GROUPS
    tm = logits.shape[0]
    oh1 = jnp.where(lane == e1, 1.0, 0.0)
    oh2 = jnp.where(lane == e2, 1.0, 0.0)
    earlier = jnp.where(lax.broadcasted_iota(I32, (tm, tm), 1) < lax.broadcasted_iota(I32, (tm, tm), 0),
                        1.0, 0.0).astype(BF16)
    before1 = jnp.dot(earlier, oh1.astype(BF16), preferred_element_type=F32)
    before2 = jnp.dot(earlier, oh2.astype(BF16), preferred_element_type=F32)
    tot1 = jnp.sum(oh1, axis=0, keepdims=True)
    base = cnt_ref[...]
    r1 = jnp.sum(oh1 * (base + before1), axis=1, keepdims=True)
    r2 = jnp.sum(oh2 * (base + tot1 + before2), axis=1, keepdims=True)
    cnt_ref[...] = base + tot1 + jnp.sum(oh2, axis=0, keepdims=True)
    ei_ref[...] = jnp.where(lane == 0, e1, jnp.where(lane == 1, e2, jnp.where(
        lane == 2, r1.astype(I32), jnp.where(lane == 3, r2.astype(I32), 0))))


def _merge_ln1(yact, o, gc, ga, x, wc, wa, wo, g1, b1, wr, cnt_in, tm, x1_all=None, m_total=None):
    m = x.shape[0]
    aliased = x1_all is not None
    n_tiles = m // tm
    total = x1_all.shape[0] if aliased else m_total
    blk0 = (total - m) // tm if aliased else 0
    n_steps = n_tiles if aliased else pl.cdiv(total, tm)
    row = lambda w: pl.BlockSpec((tm, w), lambda i: (jnp.minimum(i, n_tiles - 1), 0))
    res = lambda a: pl.BlockSpec(a.shape, lambda i: (0, 0), pipeline_mode=pl.Buffered(1))
    cnt_spec = pl.BlockSpec((1, LANES), lambda i: (0, 0))
    args = [yact, o, gc, ga, x, wc, wa, wo, g1, b1, wr, cnt_in]
    in_specs = [row(C_CONV), row(N_HEADS * HEAD_DIM), row(D_MODEL), row(D_MODEL), row(D_MODEL),
                res(wc), res(wa), res(wo), res(g1), res(b1), res(wr), cnt_spec]
    if aliased:
        args.append(x1_all)
        in_specs.append(pl.BlockSpec(memory_space=pl.ANY))
    return pl.pallas_call(
        functools.partial(_merge_ln1_kernel, aliased=aliased, n_tiles=n_tiles),
        grid=(n_steps,),
        in_specs=in_specs,
        out_specs=(pl.BlockSpec((tm, D_MODEL), lambda i: (blk0 + i, 0)), row(LANES), row(LANES), cnt_spec),
        out_shape=(jax.ShapeDtypeStruct((total, D_MODEL), F32),
                   jax.ShapeDtypeStruct((m, LANES), I32),
                   jax.ShapeDtypeStruct((m, LANES), F32),
                   jax.ShapeDtypeStruct((1, LANES), F32)),
        input_output_aliases={len(args) - 1: 0} if aliased else {},
        compiler_params=_cparams(("arbitrary",)),
        name="merge_ln1",
    )(*args)


def _experts_kernel(fb_ref, dest_ref, x_hbm, wg_ref, wu_ref, wd_ref, y_hbm,
                    rowtok, xbuf, ybuf, wgb, wub, wdb, gsem, ysem, *, n_blocks):
    e = pl.program_id(0)
    n_used = fb_ref[N_EXPERTS]
    last = jnp.maximum(n_used - 1, 0)

    def start_gather(blk, sl):
        base = blk * MOE_BLK
        for i in range(MOE_BLK):
            pltpu.make_async_copy(x_hbm.at[pl.ds(rowtok[base + i], 1)], xbuf.at[sl, pl.ds(i, 1)],
                                  gsem.at[sl]).start()

    def wait_gather(sl):
        pltpu.make_async_copy(xbuf.at[sl], xbuf.at[sl], gsem.at[sl]).wait()

    def write_copy(blk, sl):
        r0 = pl.multiple_of(blk * MOE_BLK, MOE_BLK)
        return pltpu.make_async_copy(ybuf.at[sl], y_hbm.at[pl.ds(r0, MOE_BLK)], ysem.at[sl])

    @pl.when(e == 0)
    def _():
        def clear(r, carry):
            rowtok[r] = 0
            return carry
        lax.fori_loop(0, n_blocks * MOE_BLK, clear, 0, unroll=8)

        def place(a, carry):
            rowtok[dest_ref[a]] = lax.shift_right_logical(a, 1)
            return carry
        lax.fori_loop(0, dest_ref.shape[0], place, 0, unroll=8)
        start_gather(0, 0)

    b0, b1 = fb_ref[e], fb_ref[e + 1]

    @pl.when(b1 > b0)
    def _():
        wgb[...] = wg_ref[0].astype(BF16)
        wub[...] = wu_ref[0].astype(BF16)
        wdb[...] = wd_ref[0].astype(BF16)

    def block(b, carry):
        sl = lax.rem(b, 2)

        @pl.when(b >= 2)
        def _():
            write_copy(b - 2, sl).wait()

        wait_gather(sl)
        start_gather(jnp.minimum(b + 1, last), 1 - sl)
        xb = xbuf[sl].astype(BF16)
        hg = jnp.dot(xb, wgb[...], preferred_element_type=F32)
        hu = jnp.dot(xb, wub[...], preferred_element_type=F32)
        h = (hg * jax.nn.sigmoid(hg) * hu).astype(BF16)
        ybuf[sl] = jnp.dot(h, wdb[...], preferred_element_type=F32)
        write_copy(b, sl).start()
        return carry

    lax.fori_loop(b0, b1, block, 0)

    @pl.when(e == pl.num_programs(0) - 1)
    def _():
        wait_gather(lax.rem(n_used, 2))

        @pl.when(n_used >= 1)
        def _():
            write_copy(last, lax.rem(last, 2)).wait()

        @pl.when(n_used >= 2)
        def _():
            write_copy(last - 1, lax.rem(last - 1, 2)).wait()

        ybuf[0] = jnp.zeros(ybuf.shape[1:], F32)

        def fill(blk, carry):
            write_copy(blk, 0).start()
            return carry
        lax.fori_loop(n_used, n_blocks, fill, 0)

        def fill_wait(blk, carry):
            write_copy(blk, 0).wait()
            return carry
        lax.fori_loop(n_used, n_blocks, fill_wait, 0)


def _experts(first_blk, dest, x1_all, wg, wu, wd):
    n_blocks = (dest.shape[0] + N_EXPERTS * (MOE_BLK - 1) + MOE_BLK - 1) // MOE_BLK
    n_rows = n_blocks * MOE_BLK
    grid_spec = pltpu.PrefetchScalarGridSpec(
        num_scalar_prefetch=2,
        grid=(N_EXPERTS,),
        in_specs=[pl.BlockSpec(memory_space=pl.ANY),
                  pl.BlockSpec((1, D_MODEL, D_EXPERT), lambda e, fb, de: (e, 0, 0)),
                  pl.BlockSpec((1, D_MODEL, D_EXPERT), lambda e, fb, de: (e, 0, 0)),
                  pl.BlockSpec((1, D_EXPERT, D_MODEL), lambda e, fb, de: (e, 0, 0))],
        out_specs=pl.BlockSpec(memory_space=pl.ANY),
        scratch_shapes=[pltpu.SMEM((n_rows,), I32),
                        pltpu.VMEM((2, MOE_BLK, D_MODEL), F32),
                        pltpu.VMEM((2, MOE_BLK, D_MODEL), F32),
                        pltpu.VMEM((D_MODEL, D_EXPERT), BF16),
                        pltpu.VMEM((D_MODEL, D_EXPERT), BF16),
                        pltpu.VMEM((D_EXPERT, D_MODEL), BF16),
                        pltpu.SemaphoreType.DMA((2,)),
                        pltpu.SemaphoreType.DMA((2,))],
    )
    return pl.pallas_call(
        functools.partial(_experts_kernel, n_blocks=n_blocks),
        grid_spec=grid_spec,
        out_shape=jax.ShapeDtypeStruct((n_rows, D_MODEL), F32),
        compiler_params=_cparams(("arbitrary",)),
        name="experts",
    )(first_blk, dest, x1_all, wg, wu, wd)


def _combine_ln2_kernel(dest_ref, x1_ref, gt_ref, yb_hbm, g_ref, b_ref, y_ref, ybuf, sem, *, tm, tok0):
    i = pl.program_id(0)
    last = pl.num_programs(0) - 1
    slot = lax.rem(i, 2)

    def row_copy(a, sl, k, r):
        return pltpu.make_async_copy(yb_hbm.at[pl.ds(dest_ref[a], 1)], ybuf.at[sl, k, pl.ds(r, 1)], sem.at[sl])

    def start_gather(tile, sl, unrolled):
        base = 2 * (tok0 + tile * tm)
        if unrolled:
            for r in range(tm):
                row_copy(base + 2 * r, sl, 0, r).start()
                row_copy(base + 2 * r + 1, sl, 1, r).start()
        else:
            def body(r, carry):
                row_copy(base + 2 * r, sl, 0, r).start()
                row_copy(base + 2 * r + 1, sl, 1, r).start()
                return carry
            lax.fori_loop(0, tm, body, 0)

    def wait_gather(sl):
        pltpu.make_async_copy(ybuf.at[sl], ybuf.at[sl], sem.at[sl]).wait()

    @pl.when(i == 0)
    def _():
        start_gather(0, 0, unrolled=False)

    wait_gather(slot)
    start_gather(jnp.minimum(i + 1, last), 1 - slot, unrolled=True)
    gt = gt_ref[...]
    z = DEEPNORM_ALPHA * x1_ref[...] + gt[:, 0:1] * ybuf[slot, 0] + gt[:, 1:2] * ybuf[slot, 1]
    y_ref[...] = _layer_norm(z, g_ref[...], b_ref[...])

    @pl.when(i == last)
    def _():
        wait_gather(1 - slot)


def _combine_ln2(dest, x1_all, gt, yb, g2, b2, tm, tok0):
    m = gt.shape[0]
    blk0 = tok0 // tm
    grid_spec = pltpu.PrefetchScalarGridSpec(
        num_scalar_prefetch=1,
        grid=(m // tm,),
        in_specs=[pl.BlockSpec((tm, D_MODEL), lambda i, d: (blk0 + i, 0)),
                  pl.BlockSpec((tm, LANES), lambda i, d: (i, 0)),
                  pl.BlockSpec(memory_space=pl.ANY),
                  pl.BlockSpec((1, D_MODEL), lambda i, d: (0, 0)),
                  pl.BlockSpec((1, D_MODEL), lambda i, d: (0, 0))],
        out_specs=pl.BlockSpec((tm, D_MODEL), lambda i, d: (i, 0)),
        scratch_shapes=[pltpu.VMEM((2, 2, tm, D_MODEL), F32),
                        pltpu.SemaphoreType.DMA((2,))],
    )
    return pl.pallas_call(
        functools.partial(_combine_ln2_kernel, tm=tm, tok0=tok0),
        grid_spec=grid_spec,
        out_shape=jax.ShapeDtypeStruct((m, D_MODEL), F32),
        compiler_params=_cparams(("arbitrary",)),
        name="combine_ln2",
    )(dest, x1_all, gt, yb, g2, b2)


def _dispatch_tables(eid, rank, counts):
    padded = (counts + MOE_BLK - 1) // MOE_BLK * MOE_BLK
    pad_end = jnp.cumsum(padded)
    pad_start = pad_end - padded
    start_of = jnp.sum(jnp.where(eid[..., None] == jnp.arange(N_EXPERTS, dtype=I32), pad_start, 0), axis=-1)
    dest = (start_of + rank).reshape(-1).astype(I32)
    first_blk = (jnp.concatenate([pad_start, pad_end[-1:]]) // MOE_BLK).astype(I32)
    return dest, first_blk


def kernel(x_prompt, x_sample, cache_k, cache_v, cache_kidx, state_conv, page_table, w_in, conv_dw_w, conv_dw_b,
           conv_ln_g, conv_ln_b, w_conv_out, w_attn_out, w_out, ln1_g, ln1_b, w_router_group, w_router_expert,
           w_expert_gate, w_expert_up, w_expert_down, ln2_g, ln2_b):
    assert w_in.shape[0] == 1, "single-layer trunk"
    nb, t, d = x_prompt.shape
    ns = x_sample.shape[0]
    n_pool = cache_k.shape[1]
    mp = nb * t
    row2 = lambda a: a.reshape(1, -1)

    w_arr = _arrange_w_in(w_in[0])
    dww, dwb, lng, lnb = conv_dw_w[0], row2(conv_dw_b[0]), row2(conv_ln_g[0]), row2(conv_ln_b[0])
    wc, wa, wo = w_conv_out[0].astype(BF16), w_attn_out[0].astype(BF16), w_out[0].astype(BF16)
    g1, b1, g2, b2 = row2(ln1_g[0]), row2(ln1_b[0]), row2(ln2_g[0]), row2(ln2_b[0])
    wr = jnp.concatenate([w_router_group[0], w_router_expert[0],
                          jnp.zeros((d, LANES - N_GROUPS - N_EXPERTS), F32)], axis=1)

    xp = x_prompt.reshape(mp, d)
    u_p, q_p, qi_p, kv_p, kvb_p, gc_p, ga_p, kiwi_p, kib_p = _in_proj(xp, w_arr, tm=512)
    yact_p = _conv_prompt(u_p.reshape(nb, t, C_CONV), dww, dwb, lng, lnb)
    o_p = _attn_prompt(qi_p.reshape(nb, t, -1), kiwi_p.reshape(nb, t, LANES), kib_p.reshape(nb, t, LANES),
                       q_p.reshape(nb, t, -1), kvb_p.reshape(nb, t, 2 * KV_DIM))
    x1_all, ei_p, gt_p, cnt_p = _merge_ln1(yact_p.reshape(mp, C_CONV), o_p.reshape(mp, -1), gc_p, ga_p, xp,
                                           wc, wa, wo, g1, b1, wr, jnp.zeros((1, LANES), F32), tm=256,
                                           m_total=mp + ns)

    xs = x_sample.reshape(ns, d)
    u_s, q_s, qi_s, kv_s, _, gc_s, ga_s, kiwi_s, _ = _in_proj(xs, w_arr, tm=ns)
    yact_s = _conv_sample(state_conv[0].transpose(1, 0, 2), u_s, dww, dwb, lng, lnb)
    cki2 = cache_kidx.transpose(0, 1, 3, 2).reshape(n_pool, IDX_DIM, PAGE_SIZE)
    ck2 = cache_k.transpose(0, 1, 3, 4, 2).reshape(n_pool, KV_DIM, PAGE_SIZE)
    cv2 = cache_v.transpose(0, 1, 3, 4, 2).reshape(n_pool, KV_DIM, PAGE_SIZE)
    scores_s = _sample_scores(page_table, qi_s.reshape(ns, IDX_HEADS, IDX_DIM),
                              kiwi_s[:, IDX_DIM:IDX_DIM + IDX_HEADS].reshape(ns, IDX_HEADS, 1),
                              kiwi_s[:, :IDX_DIM].reshape(ns, 1, IDX_DIM), cki2)
    scores3 = scores_s.reshape(ns, -1, ATT_CK).transpose(1, 0, 2)
    own_group = (np.arange(N_HEADS)[:, None] // HEADS_PER_KV) == (np.arange(KV_DIM)[None, :] // HEAD_DIM)
    qbd = jnp.where(own_group[None], jnp.tile(q_s.reshape(ns, N_HEADS, HEAD_DIM), (1, 1, KV_HEADS)), 0).astype(BF16)
    o3 = _sample_attn(page_table, scores3, qbd, kv_s[:, :KV_DIM].reshape(ns, 1, KV_DIM),
                      kv_s[:, KV_DIM:].reshape(ns, 1, KV_DIM), ck2, cv2)
    o_s = o3.reshape(ns, N_HEADS * HEAD_DIM).astype(BF16)
    x1_all, ei_s, gt_s, cnt = _merge_ln1(yact_s, o_s, gc_s, ga_s, xs, wc, wa, wo, g1, b1, wr, cnt_p, tm=ns,
                                         x1_all=x1_all)

    eid_rank = jnp.concatenate([ei_p[:, :4], ei_s[:, :4]], axis=0)
    dest, first_blk = _dispatch_tables(eid_rank[:, :2], eid_rank[:, 2:], cnt[0, :N_EXPERTS].astype(I32))
    yb = _experts(first_blk, dest, x1_all, w_expert_gate[0], w_expert_up[0], w_expert_down[0])
    y_p = _combine_ln2(dest, x1_all, gt_p, yb, g2, b2, tm=256, tok0=0)
    y_s = _combine_ln2(dest, x1_all, gt_s, yb, g2, b2, tm=ns, tok0=mp)

    y_prompt = y_p.reshape(nb, t, d)
    y_sample = y_s.reshape(ns, 1, d)
    k_prompt = kv_p[:, :KV_DIM].reshape(1, nb, t, KV_HEADS, HEAD_DIM)
    v_prompt = kv_p[:, KV_DIM:].reshape(1, nb, t, KV_HEADS, HEAD_DIM)
    kidx_prompt = kiwi_p[:, :IDX_DIM].reshape(1, nb, t, IDX_DIM)
    conv_prompt = u_p.reshape(nb, t, C_CONV)[:, t - (CONV_W - 1):][None]
    k_sample = kv_s[:, :KV_DIM].reshape(1, ns, 1, KV_HEADS, HEAD_DIM)
    v_sample = kv_s[:, KV_DIM:].reshape(1, ns, 1, KV_HEADS, HEAD_DIM)
    kidx_sample = kiwi_s[:, :IDX_DIM].reshape(1, ns, 1, IDX_DIM)
    conv_sample = jnp.concatenate([state_conv[0][:, 1:], u_s[:, None, :]], axis=1)[None]
    return (y_prompt, y_sample, k_prompt, v_prompt, kidx_prompt, conv_prompt, k_sample, v_sample, kidx_sample,
            conv_sample)
```

```python
import functools

import jax
import jax.numpy as jnp
import numpy as np
from jax import lax
from jax.experimental import pallas as pl
from jax.experimental.pallas import tpu as pltpu

F32 = jnp.float32
BF16 = jnp.bfloat16
I32 = jnp.int32

D_MODEL = 2048
C_CONV = 1024
CONV_W = 31
N_HEADS = 16
HEAD_DIM = 64
KV_HEADS = 4
HEADS_PER_KV = N_HEADS // KV_HEADS
KV_DIM = KV_HEADS * HEAD_DIM
IDX_HEADS = 16
IDX_DIM = 64
MAX_TOPK = 256
N_GROUPS = 8
EXPERTS_PER_GROUP = 8
N_EXPERTS = 64
D_EXPERT = 512
MOE_BLK = 128
PAGE_SIZE = 128
LN_EPS = 1e-5
DEEPNORM_ALPHA = 2.0 ** 0.25
NEG_INF = -1e30
INT_MIN = -(2 ** 31)
NEG_KEY = int(INT_MIN - int(np.array(NEG_INF, np.float32).view(np.int32)))
LOG2E = 1.4426950408889634
Q_SCALE = HEAD_DIM ** -0.5 * LOG2E
IDX_SCALE = IDX_HEADS ** -0.5 * IDX_DIM ** -0.5

LANES = 128
HALF = LANES // 2
IN_TN = 1024
VMEM_LIMIT = 56 * 1024 * 1024


def _cparams(sem, vmem=VMEM_LIMIT):
    return pltpu.CompilerParams(dimension_semantics=sem, vmem_limit_bytes=vmem)


def _float_key(x):
    b = pltpu.bitcast(x, I32)
    return jnp.where(b < 0, INT_MIN - b, b)


def _layer_norm(z, g, b):
    mu = jnp.mean(z, axis=-1, keepdims=True)
    zc = z - mu
    var = jnp.mean(zc * zc, axis=-1, keepdims=True)
    return zc * lax.rsqrt(var + LN_EPS) * g + b


def _dot_nt(a, b):
    return lax.dot_general(a, b, (((1,), (1,)), ((), ())), preferred_element_type=F32)


def _alibi_slope(h):
    return float(2.0 ** (-8.0 * (h + 1) / N_HEADS))


_T_GLU, _T_Q, _T_QI, _T_KV, _T_GC, _T_GA, _N_TILES = 0, 2, 3, 4, 5, 7, 9
KV_TILE_COLS = 2 * KV_DIM + 2 * LANES


def _arrange_w_in(w):
    sizes = (C_CONV, C_CONV, N_HEADS * HEAD_DIM, KV_DIM, KV_DIM,
             IDX_HEADS * IDX_DIM, IDX_DIM, IDX_HEADS, D_MODEL, D_MODEL)
    offs = np.cumsum((0,) + sizes)
    a, g, q, k, v, qi, ki, wi, gc, ga = [w[:, offs[i]:offs[i + 1]] for i in range(10)]
    half = IN_TN // 2
    glu = [jnp.concatenate([a[:, t * half:(t + 1) * half], g[:, t * half:(t + 1) * half]], axis=1)
           for t in range(C_CONV // half)]
    zeros = lambda n: jnp.zeros((w.shape[0], n), w.dtype)
    kv_tile = jnp.concatenate([k, v, ki, wi, zeros(LANES - IDX_DIM - IDX_HEADS), ki, ki,
                               zeros(IN_TN - KV_TILE_COLS)], axis=1)
    return jnp.concatenate(glu + [q, qi, kv_tile, gc, ga], axis=1).astype(BF16)


def _in_proj_kernel(x_ref, w_ref, u_ref, q_ref, qi_ref, kv_ref, kvb_ref, gc_ref, ga_ref, kiwi_ref, kib_ref,
                    xb_ref):
    j = pl.program_id(1)

    @pl.when(j == 0)
    def _():
        xb_ref[...] = x_ref[...].astype(BF16)

    def mm(ncols=IN_TN):
        return jnp.dot(xb_ref[...], w_ref[:, :ncols], preferred_element_type=F32)

    @pl.when(j < _T_Q)
    def _():
        acc = mm()
        u_ref[...] = acc[:, :IN_TN // 2] * jax.nn.sigmoid(acc[:, IN_TN // 2:])

    @pl.when(j == _T_Q)
    def _():
        q_ref[...] = (mm() * Q_SCALE).astype(BF16)

    @pl.when(j == _T_QI)
    def _():
        qi_ref[...] = mm().astype(BF16)

    @pl.when(j == _T_KV)
    def _():
        acc = mm(KV_TILE_COLS)
        kv_ref[...] = acc[:, :2 * KV_DIM]
        kvb_ref[...] = acc[:, :2 * KV_DIM].astype(BF16)
        kiwi_ref[...] = acc[:, 2 * KV_DIM:2 * KV_DIM + LANES]
        kib_ref[...] = acc[:, 2 * KV_DIM + LANES:].astype(BF16)

    @pl.when((j >= _T_GC) & (j < _T_GA))
    def _():
        gc_ref[...] = jax.nn.sigmoid(mm()).astype(BF16)

    @pl.when(j >= _T_GA)
    def _():
        ga_ref[...] = jax.nn.sigmoid(mm()).astype(BF16)


def _in_proj(x, w_arr, tm):
    m = x.shape[0]
    half = IN_TN // 2

    def cl(lo, n):
        return lambda i, j: (i, jnp.clip(j - lo, 0, n - 1))

    out_shape = (
        jax.ShapeDtypeStruct((m, C_CONV), F32),
        jax.ShapeDtypeStruct((m, N_HEADS * HEAD_DIM), BF16),
        jax.ShapeDtypeStruct((m, IDX_HEADS * IDX_DIM), BF16),
        jax.ShapeDtypeStruct((m, 2 * KV_DIM), F32),
        jax.ShapeDtypeStruct((m, 2 * KV_DIM), BF16),
        jax.ShapeDtypeStruct((m, D_MODEL), BF16),
        jax.ShapeDtypeStruct((m, D_MODEL), BF16),
        jax.ShapeDtypeStruct((m, LANES), F32),
        jax.ShapeDtypeStruct((m, LANES), BF16),
    )
    out_specs = (
        pl.BlockSpec((tm, half), cl(_T_GLU, 2)),
        pl.BlockSpec((tm, IN_TN), lambda i, j: (i, 0)),
        pl.BlockSpec((tm, IN_TN), lambda i, j: (i, 0)),
        pl.BlockSpec((tm, 2 * KV_DIM), lambda i, j: (i, 0)),
        pl.BlockSpec((tm, 2 * KV_DIM), lambda i, j: (i, 0)),
        pl.BlockSpec((tm, IN_TN), cl(_T_GC, 2)),
        pl.BlockSpec((tm, IN_TN), cl(_T_GA, 2)),
        pl.BlockSpec((tm, LANES), lambda i, j: (i, 0)),
        pl.BlockSpec((tm, LANES), lambda i, j: (i, 0)),
    )
    return pl.pallas_call(
        _in_proj_kernel,
        grid=(m // tm, _N_TILES),
        in_specs=[pl.BlockSpec((tm, D_MODEL), lambda i, j: (i, 0)),
                  pl.BlockSpec((D_MODEL, IN_TN), lambda i, j: (0, j))],
        out_specs=out_specs,
        out_shape=out_shape,
        scratch_shapes=[pltpu.VMEM((tm, D_MODEL), BF16)],
        compiler_params=_cparams(("arbitrary", "arbitrary")),
        name="in_proj",
    )(x, w_arr)


CONV_HALO = 32
CONV_STRIP = 32


SUBLANES = 8


def _conv_prompt_kernel(ucur_ref, uprev_ref, dww_ref, dwb_ref, lng_ref, lnb_ref, y_ref, ext_ref, sh_ref, *, tt):
    tb = pl.program_id(1)
    ext_ref[0:CONV_HALO, :] = jnp.where(tb > 0, uprev_ref[0], 0.0)
    ext_ref[CONV_HALO:, :] = ucur_ref[0]
    first = CONV_HALO - (CONV_W - 1)
    for b in range(SUBLANES):
        rows = tt + SUBLANES * ((CONV_W - 1 - b) // SUBLANES)
        sh_ref[b, 0:rows, :] = ext_ref[first + b:first + b + rows, :]
    dwb, lng, lnb = dwb_ref[...], lng_ref[...], lnb_ref[...]
    for r in range(tt // CONV_STRIP):
        acc = jnp.broadcast_to(dwb, (CONV_STRIP, C_CONV))
        for j in range(CONV_W):
            a, b = divmod(j, SUBLANES)
            r0 = r * CONV_STRIP + a * SUBLANES
            acc = acc + dww_ref[j:j + 1, :] * sh_ref[b, r0:r0 + CONV_STRIP, :]
        yn = _layer_norm(acc, lng, lnb)
        y_ref[0, r * CONV_STRIP:(r + 1) * CONV_STRIP, :] = (yn * jax.nn.sigmoid(yn)).astype(BF16)


def _conv_prompt(u, dww, dwb, lng, lnb, tt=128):
    n, t, c = u.shape
    per = tt // CONV_HALO
    vec = pl.BlockSpec((1, c), lambda b, i: (0, 0))
    return pl.pallas_call(
        functools.partial(_conv_prompt_kernel, tt=tt),
        grid=(n, t // tt),
        in_specs=[pl.BlockSpec((1, tt, c), lambda b, i: (b, i, 0)),
                  pl.BlockSpec((1, CONV_HALO, c), lambda b, i: (b, jnp.maximum(i * per - 1, 0), 0)),
                  pl.BlockSpec((CONV_W, c), lambda b, i: (0, 0)), vec, vec, vec],
        out_specs=pl.BlockSpec((1, tt, c), lambda b, i: (b, i, 0)),
        out_shape=jax.ShapeDtypeStruct((n, t, c), BF16),
        scratch_shapes=[pltpu.VMEM((tt + CONV_HALO, c), F32),
                        pltpu.VMEM((SUBLANES, tt + SUBLANES * ((CONV_W - 1) // SUBLANES), c), F32)],
        compiler_params=_cparams(("arbitrary", "arbitrary")),
        name="conv_prompt",
    )(u, u, dww, dwb, lng, lnb)


def _conv_sample_kernel(state_ref, u_ref, dww_ref, dwb_ref, lng_ref, lnb_ref, y_ref):
    acc = dwb_ref[...] + dww_ref[CONV_W - 1:CONV_W, :] * u_ref[...]
    for j in range(CONV_W - 1):
        acc = acc + dww_ref[j:j + 1, :] * state_ref[j]
    yn = _layer_norm(acc, lng_ref[...], lnb_ref[...])
    y_ref[...] = (yn * jax.nn.sigmoid(yn)).astype(BF16)


def _conv_sample(state_t, u, dww, dwb, lng, lnb):
    n = u.shape[0]
    return pl.pallas_call(
        _conv_sample_kernel,
        out_shape=jax.ShapeDtypeStruct((n, C_CONV), BF16),
        compiler_params=pltpu.CompilerParams(vmem_limit_bytes=VMEM_LIMIT),
        name="conv_sample",
    )(state_t, u, dww, dwb, lng, lnb)


ATT_TQ = 256
ATT_CK = 256


def _count_ge(key_ref, nch, cand):
    def body(c, acc):
        m = jnp.where(key_ref[c] >= cand, 1.0, 0.0)
        return acc + m[:, :LANES] + m[:, LANES:]
    acc = lax.fori_loop(0, nch, body, jnp.zeros((cand.shape[0], LANES), F32))
    return jnp.sum(acc, axis=1, keepdims=True)


def _select_threshold(key_ref, nch, rows):
    def body(i, prefix):
        cand = prefix + jnp.left_shift(jnp.int32(1), 31 - i)
        return jnp.where(_count_ge(key_ref, nch, cand) >= float(MAX_TOPK), cand, prefix)
    return lax.fori_loop(0, 32, body, jnp.full((rows, 1), INT_MIN, I32))


def _selection_bias(key_ref, bias_ref, nch):
    rows, ck = key_ref.shape[1:]
    thr = _select_threshold(key_ref, nch, rows)
    n_tie_take = float(MAX_TOPK) - _count_ge(key_ref, nch, thr + 1)
    tri = jnp.where(lax.broadcasted_iota(I32, (ck, ck), 0) <= lax.broadcasted_iota(I32, (ck, ck), 1),
                    1.0, 0.0).astype(BF16)

    def bias_chunk(c, seen):
        key = key_ref[c]
        eq = key == thr
        eqf = jnp.where(eq, 1.0, 0.0)
        incl = jnp.dot(eqf.astype(BF16), tri, preferred_element_type=F32)
        sel = (key > thr) | (eq & (seen + incl - eqf < n_tie_take))
        bias_ref[c] = jnp.where(sel & (key > NEG_KEY), 0.0, NEG_INF)
        return seen + incl[:, ck - 1:ck]

    lax.fori_loop(0, nch, bias_chunk, jnp.zeros((rows, 1), F32))


def _swap_halves(x):
    return jnp.concatenate([x[:, HALF:], x[:, :HALF]], axis=1)


def _attn_prompt_kernel(qi_ref, kiwi_ref, kib_ref, q_ref, kvb_ref, o_ref,
                        key_ref, bias_ref, wb_ref, m_ref, ls_ref, acc_ref):
    tq, ck = ATT_TQ, ATT_CK
    qb = pl.program_id(1)
    nch = qb + 1
    row = qb * tq + lax.broadcasted_iota(I32, (tq, ck), 0)
    coll = lax.broadcasted_iota(I32, (tq, ck), 1)
    lo_k = lax.broadcasted_iota(I32, (ck, LANES), 1) < HALF
    lo_q = lax.broadcasted_iota(I32, (tq, LANES), 1) < HALF

    w = kiwi_ref[0][:, IDX_DIM:IDX_DIM + IDX_HEADS] * IDX_SCALE
    for h in range(IDX_HEADS):
        wb_ref[h] = jnp.broadcast_to(w[:, h:h + 1], (tq, LANES))

    def score_chunk(c, carry):
        k0 = pl.multiple_of(c * ck, ck)
        kk = kib_ref[0, pl.ds(k0, ck), :]
        zero = jnp.zeros_like(kk)
        k_lo, k_hi = jnp.where(lo_k, kk, zero), jnp.where(lo_k, zero, kk)
        acc = jnp.zeros((tq, ck), F32)
        for p in range(IDX_HEADS // 2):
            qp = qi_ref[0, :, p * LANES:(p + 1) * LANES]
            w_lo, w_hi = wb_ref[2 * p], wb_ref[2 * p + 1]
            acc = acc + jnp.concatenate([w_lo, w_lo], axis=1) * jnp.maximum(_dot_nt(qp, k_lo), 0.0)
            acc = acc + jnp.concatenate([w_hi, w_hi], axis=1) * jnp.maximum(_dot_nt(qp, k_hi), 0.0)
        acc = jnp.where(c * ck + coll <= row, acc, NEG_INF)
        key_ref[c] = _float_key(acc)
        return carry

    lax.fori_loop(0, nch, score_chunk, 0)
    _selection_bias(key_ref, bias_ref, nch)

    m_ref[...] = jnp.full(m_ref.shape, -jnp.inf, F32)
    ls_ref[...] = jnp.zeros(ls_ref.shape, F32)
    acc_ref[...] = jnp.zeros(acc_ref.shape, F32)

    def attn_chunk(c, carry):
        k0 = pl.multiple_of(c * ck, ck)
        kv = kvb_ref[0, pl.ds(k0, ck), :]
        bias = bias_ref[c]
        rel = (c * ck - (qb + 1) * tq + 1 + lax.broadcasted_iota(I32, (1, ck), 1)).astype(F32)
        for j in range(KV_HEADS // 2):
            kt = kv[:, j * LANES:(j + 1) * LANES]
            vt = kv[:, KV_DIM + j * LANES:KV_DIM + (j + 1) * LANES]
            kts, vts = _swap_halves(kt), _swap_halves(vt)
            zero, one = jnp.zeros_like(kt), jnp.ones_like(kt)
            for gg in range(2):
                g = 2 * j + gg
                k_own, k_swp = (kt, kts) if gg == 0 else (kts, kt)
                v_own, v_swp = (vt, vts) if gg == 0 else (vts, vt)
                k_lo, k_hi = jnp.where(lo_k, k_own, zero), jnp.where(lo_k, zero, k_swp)
                v_lo, v_hi = jnp.where(lo_k, v_own, one), jnp.where(lo_k, one, v_swp)
                for mm in range(HEADS_PER_KV // 2):
                    pidx = (HEADS_PER_KV // 2) * g + mm
                    qp = q_ref[0, :, pidx * LANES:(pidx + 1) * LANES]
                    pvs, alphas = [], []
                    for half, (kmat, vmat) in enumerate(((k_lo, v_lo), (k_hi, v_hi))):
                        h = 2 * pidx + half
                        s = _dot_nt(qp, kmat) + (_alibi_slope(h) * LOG2E) * rel + bias
                        m_old = m_ref[h]
                        m_new = jnp.maximum(m_old, jnp.max(s, axis=1, keepdims=True))
                        p = jnp.exp2(s - jnp.concatenate([m_new, m_new], axis=1))
                        alphas.append(jnp.exp2(m_old - m_new))
                        m_ref[h] = m_new
                        pvs.append(jnp.dot(p.astype(BF16), vmat, preferred_element_type=F32))
                    acc_ref[pidx] = (jnp.where(lo_q, alphas[0], alphas[1]) * acc_ref[pidx]
                                     + jnp.where(lo_q, pvs[0], pvs[1]))
                    ls_ref[pidx] = (jnp.where(lo_q, alphas[1], alphas[0]) * ls_ref[pidx]
                                    + jnp.where(lo_q, pvs[1], pvs[0]))
        return carry

    lax.fori_loop(0, nch, attn_chunk, 0)
    for pidx in range(N_HEADS // 2):
        l_pair = pltpu.roll(ls_ref[pidx], HALF, 1)
        o_ref[0, :, pidx * LANES:(pidx + 1) * LANES] = (acc_ref[pidx] / l_pair).astype(BF16)


def _attn_prompt(qi, kiwi, kib, q, kvb):
    n, t, _ = q.shape
    tq = ATT_TQ
    return pl.pallas_call(
        _attn_prompt_kernel,
        grid=(n, t // tq),
        in_specs=[pl.BlockSpec((1, tq, IDX_HEADS * IDX_DIM), lambda b, i: (b, i, 0)),
                  pl.BlockSpec((1, tq, LANES), lambda b, i: (b, i, 0)),
                  pl.BlockSpec((1, t, LANES), lambda b, i: (b, 0, 0)),
                  pl.BlockSpec((1, tq, N_HEADS * HEAD_DIM), lambda b, i: (b, i, 0)),
                  pl.BlockSpec((1, t, 2 * KV_DIM), lambda b, i: (b, 0, 0))],
        out_specs=pl.BlockSpec((1, tq, N_HEADS * HEAD_DIM), lambda b, i: (b, i, 0)),
        out_shape=jax.ShapeDtypeStruct((n, t, N_HEADS * HEAD_DIM), BF16),
        scratch_shapes=[pltpu.VMEM((t // ATT_CK, tq, ATT_CK), I32),
                        pltpu.VMEM((t // ATT_CK, tq, ATT_CK), F32),
                        pltpu.VMEM((IDX_HEADS, tq, LANES), F32),
                        pltpu.VMEM((N_HEADS, tq, LANES), F32),
                        pltpu.VMEM((N_HEADS // 2, tq, LANES), F32),
                        pltpu.VMEM((N_HEADS // 2, tq, LANES), F32)],
        compiler_params=_cparams(("arbitrary", "arbitrary")),
        name="attn_prompt",
    )(qi, kiwi, kib, q, kvb)


SAMPLE_KC = 2048


def _start_page_copies(pt_ref, src_hbm, dst, sem, base, n_pages):
    def body(p, carry):
        col = pl.multiple_of(p * PAGE_SIZE, PAGE_SIZE)
        pltpu.make_async_copy(src_hbm.at[pt_ref[base + p]], dst.at[:, pl.ds(col, PAGE_SIZE)], sem).start()
        return carry
    lax.fori_loop(0, n_pages, body, 0)


def _sample_score_kernel(pt_ref, qi_ref, wcol_ref, kinew_ref, cki_hbm, sc_ref, kibuf, sem, *, n_pages):
    b = pl.program_id(0)
    nb = pl.num_programs(0)
    past = n_pages * PAGE_SIZE
    slot = b % 2

    @pl.when(b == 0)
    def _():
        _start_page_copies(pt_ref, cki_hbm, kibuf.at[0], sem.at[0], 0, n_pages)

    @pl.when(b + 1 < nb)
    def _():
        _start_page_copies(pt_ref, cki_hbm, kibuf.at[1 - slot], sem.at[1 - slot], (b + 1) * n_pages, n_pages)

    pltpu.make_async_copy(kibuf.at[slot], kibuf.at[slot], sem.at[slot]).wait()

    qi = qi_ref[0]
    wsc = wcol_ref[0] * IDX_SCALE
    for c in range(past // SAMPLE_KC):
        kc = kibuf[slot, :, c * SAMPLE_KC:(c + 1) * SAMPLE_KC].astype(BF16)
        d = jnp.dot(qi, kc, preferred_element_type=F32)
        sc_ref[0, :, c * SAMPLE_KC:(c + 1) * SAMPLE_KC] = jnp.sum(wsc * jnp.maximum(d, 0.0), axis=0, keepdims=True)
    dn = jnp.sum(qi.astype(F32) * kinew_ref[0], axis=1, keepdims=True)
    s_new = jnp.sum(wsc * jnp.maximum(dn, 0.0), axis=0, keepdims=True)
    ln = lax.broadcasted_iota(I32, (1, ATT_CK), 1)
    sc_ref[0, :, past:] = jnp.where(ln == 0, s_new, NEG_INF)


def _sample_scores(page_table, qi3, wcol, kinew, cki2):
    n, n_pages = page_table.shape
    past = n_pages * PAGE_SIZE
    grid_spec = pltpu.PrefetchScalarGridSpec(
        num_scalar_prefetch=1,
        grid=(n,),
        in_specs=[pl.BlockSpec((1, IDX_HEADS, IDX_DIM), lambda b, pt: (b, 0, 0)),
                  pl.BlockSpec((1, IDX_HEADS, 1), lambda b, pt: (b, 0, 0)),
                  pl.BlockSpec((1, 1, IDX_DIM), lambda b, pt: (b, 0, 0)),
                  pl.BlockSpec(memory_space=pl.ANY)],
        out_specs=pl.BlockSpec((1, 1, past + ATT_CK), lambda b, pt: (b, 0, 0)),
        scratch_shapes=[pltpu.VMEM((2, IDX_DIM, past), F32),
                        pltpu.SemaphoreType.DMA((2,))],
    )
    return pl.pallas_call(
        functools.partial(_sample_score_kernel, n_pages=n_pages),
        grid_spec=grid_spec,
        out_shape=jax.ShapeDtypeStruct((n, 1, past + ATT_CK), F32),
        compiler_params=_cparams(("arbitrary",)),
        name="sample_scores",
    )(page_table.reshape(-1), qi3, wcol, kinew, cki2)


def _sample_attn_kernel(pt_ref, sc_ref, qbd_ref, knew_ref, vnew_ref, ck_hbm, cv_hbm, o_ref,
                        key_ref, bias_ref, kbuf, vbuf, semk, semv, *, n_pages):
    b = pl.program_id(0)
    nb = pl.num_programs(0)
    past = n_pages * PAGE_SIZE
    nch = sc_ref.shape[0]
    slot = b % 2

    def start_all(bb, sl):
        _start_page_copies(pt_ref, ck_hbm, kbuf.at[sl], semk.at[sl], bb * n_pages, n_pages)
        _start_page_copies(pt_ref, cv_hbm, vbuf.at[sl], semv.at[sl], bb * n_pages, n_pages)

    @pl.when(b == 0)
    def _():
        start_all(0, 0)
        key_ref[...] = _float_key(sc_ref[...])
        _selection_bias(key_ref, bias_ref, nch)

    @pl.when(b + 1 < nb)
    def _():
        start_all(b + 1, 1 - slot)

    pltpu.make_async_copy(kbuf.at[slot], kbuf.at[slot], semk.at[slot]).wait()
    pltpu.make_async_copy(vbuf.at[slot], vbuf.at[slot], semv.at[slot]).wait()

    qbd = qbd_ref[0]
    bias_row = jnp.concatenate([bias_ref[c, pl.ds(b, 1), :] for c in range(nch)], axis=1)
    s_parts = [jnp.dot(qbd, kbuf[slot, :, c * SAMPLE_KC:(c + 1) * SAMPLE_KC].astype(BF16),
                       preferred_element_type=F32) for c in range(past // SAMPLE_KC)]
    s_new = jnp.sum(qbd.astype(F32) * knew_ref[0], axis=1, keepdims=True)
    ln = lax.broadcasted_iota(I32, (N_HEADS, ATT_CK), 1)
    s = jnp.concatenate(s_parts + [jnp.where(ln == 0, s_new, 0.0)], axis=1)
    hrow = lax.broadcasted_iota(I32, (N_HEADS, 1), 0)
    slope2 = jnp.exp2(-8.0 * (hrow + 1).astype(F32) / N_HEADS) * LOG2E
    rel = (lax.broadcasted_iota(I32, (1, past + ATT_CK), 1) - past).astype(F32)
    s = s + slope2 * rel + bias_row
    m = jnp.max(s, axis=1, keepdims=True)
    p = jnp.exp2(s - m)
    l = jnp.sum(p, axis=1, keepdims=True)
    o_all = p[:, past:past + 1] * vnew_ref[0]
    for c in range(past // SAMPLE_KC):
        o_all = o_all + _dot_nt(p[:, c * SAMPLE_KC:(c + 1) * SAMPLE_KC].astype(BF16),
                                vbuf[slot, :, c * SAMPLE_KC:(c + 1) * SAMPLE_KC].astype(BF16))
    o_all = o_all / l
    out = jnp.zeros((N_HEADS, HEAD_DIM), F32)
    for g in range(KV_HEADS):
        own = (hrow >= g * HEADS_PER_KV) & (hrow < (g + 1) * HEADS_PER_KV)
        out = out + jnp.where(own, o_all[:, g * HEAD_DIM:(g + 1) * HEAD_DIM], 0.0)
    o_ref[0] = out


def _sample_attn(page_table, scores3, qbd, knew, vnew, ck2, cv2):
    n, n_pages = page_table.shape
    past = n_pages * PAGE_SIZE
    nch = scores3.shape[0]
    grid_spec = pltpu.PrefetchScalarGridSpec(
        num_scalar_prefetch=1,
        grid=(n,),
        in_specs=[pl.BlockSpec((nch, n, ATT_CK), lambda b, pt: (0, 0, 0)),
                  pl.BlockSpec((1, N_HEADS, KV_DIM), lambda b, pt: (b, 0, 0)),
                  pl.BlockSpec((1, 1, KV_DIM), lambda b, pt: (b, 0, 0)),
                  pl.BlockSpec((1, 1, KV_DIM), lambda b, pt: (b, 0, 0)),
                  pl.BlockSpec(memory_space=pl.ANY),
                  pl.BlockSpec(memory_space=pl.ANY)],
        out_specs=pl.BlockSpec((1, N_HEADS, HEAD_DIM), lambda b, pt: (b, 0, 0)),
        scratch_shapes=[pltpu.VMEM((nch, n, ATT_CK), I32),
                        pltpu.VMEM((nch, n, ATT_CK), F32),
                        pltpu.VMEM((2, KV_DIM, past), F32),
                        pltpu.VMEM((2, KV_DIM, past), F32),
                        pltpu.SemaphoreType.DMA((2,)),
                        pltpu.SemaphoreType.DMA((2,))],
    )
    return pl.pallas_call(
        functools.partial(_sample_attn_kernel, n_pages=n_pages),
        grid_spec=grid_spec,
        out_shape=jax.ShapeDtypeStruct((n, N_HEADS, HEAD_DIM), F32),
        compiler_params=_cparams(("arbitrary",)),
        name="sample_attn",
    )(page_table.reshape(-1), scores3, qbd, knew, vnew, ck2, cv2)


def _merge_ln1_kernel(*refs, aliased, n_tiles):
    n_in = 12
    x1_ref, ei_ref, gt_ref, cnt_ref = refs[n_in + 1:] if aliased else refs[n_in:]

    @pl.when(pl.program_id(0) < n_tiles)
    def _():
        _merge_ln1_tile(*refs[:n_in], x1_ref, ei_ref, gt_ref, cnt_ref)

    @pl.when(pl.program_id(0) >= n_tiles)
    def _():
        x1_ref[...] = jnp.zeros(x1_ref.shape, F32)


def _merge_ln1_tile(y_ref, o_ref, gc_ref, ga_ref, x_ref, wc_ref, wa_ref, wo_ref, g_ref, b_ref, wr_ref,
                    cnt_in_ref, x1_ref, ei_ref, gt_ref, cnt_ref):
    @pl.when(pl.program_id(0) == 0)
    def _():
        cnt_ref[...] = cnt_in_ref[...]

    conv_out = jnp.dot(y_ref[...], wc_ref[...], preferred_element_type=F32)
    attn_out = jnp.dot(o_ref[...], wa_ref[...], preferred_element_type=F32)
    merged = gc_ref[...].astype(F32) * conv_out + ga_ref[...].astype(F32) * attn_out
    mix = jnp.dot(merged.astype(BF16), wo_ref[...], preferred_element_type=F32)
    x1 = _layer_norm(DEEPNORM_ALPHA * x_ref[...] + mix, g_ref[...], b_ref[...])
    x1_ref[...] = x1

    x_hi = x1.astype(BF16)
    x_lo = (x1 - x_hi.astype(F32)).astype(BF16)
    logits = (jnp.dot(x_hi, wr_ref[0], preferred_element_type=F32)
              + jnp.dot(x_lo, wr_ref[0], preferred_element_type=F32)
              + jnp.dot(x_hi, wr_ref[1], preferred_element_type=F32))
    lane = lax.broadcasted_iota(I32, logits.shape, 1)
    is_g = lane < N_GROUPS
    lg = jnp.where(is_g, logits, -jnp.inf)
    eg = jnp.exp(lg - jnp.max(lg, axis=1, keepdims=True))
    pg = eg / jnp.sum(eg, axis=1, keepdims=True)
    g_top = jnp.max(pg, axis=1, keepdims=True)
    g_idx = jnp.min(jnp.where(is_g & (pg == g_top), lane, LANES), axis=1, keepdims=True)
    lo = N_GROUPS + g_idx * EXPERTS_PER_GROUP
    is_e = (lane >= lo) & (lane < lo + EXPERTS_PER_GROUP)
    le = jnp.where(is_e, logits, -jnp.inf)
    ee = jnp.exp(le - jnp.max(le, axis=1, keepdims=True))
    pe = ee / jnp.sum(ee, axis=1, keepdims=True)
    p1 = jnp.max(jnp.where(is_e, pe, -1.0), axis=1, keepdims=True)
    i1 = jnp.min(jnp.where(is_e & (pe == p1), lane, 2 * LANES), axis=1, keepdims=True)
    rest = is_e & (lane != i1)
    p2 = jnp.max(jnp.where(rest, pe, -1.0), axis=1, keepdims=True)
    i2 = jnp.min(jnp.where(rest & (pe == p2), lane, 2 * LANES), axis=1, keepdims=True)
    den = p1 + p2
    gt_ref[...] = jnp.where(lane == 0, g_top * p1 / den, jnp.where(lane == 1, g_top * p2 / den, 0.0))

    e1, e2 = i1 - N_GROUPS, i2 - N_GROUPS
    tm = logits.shape[0]
    oh1 = jnp.where(lane == e1, 1.0, 0.0)
    oh2 = jnp.where(lane == e2, 1.0, 0.0)
    earlier = jnp.where(lax.broadcasted_iota(I32, (tm, tm), 1) < lax.broadcasted_iota(I32, (tm, tm), 0),
                        1.0, 0.0).astype(BF16)
    before1 = jnp.dot(earlier, oh1.astype(BF16), preferred_element_type=F32)
    before2 = jnp.dot(earlier, oh2.astype(BF16), preferred_element_type=F32)
    tot1 = jnp.sum(oh1, axis=0, keepdims=True)
    base = cnt_ref[...]
    r1 = jnp.sum(oh1 * (base + before1), axis=1, keepdims=True)
    r2 = jnp.sum(oh2 * (base + tot1 + before2), axis=1, keepdims=True)
    cnt_ref[...] = base + tot1 + jnp.sum(oh2, axis=0, keepdims=True)
    ei_ref[...] = jnp.where(lane == 0, e1, jnp.where(lane == 1, e2, jnp.where(
        lane == 2, r1.astype(I32), jnp.where(lane == 3, r2.astype(I32), 0))))


def _merge_ln1(yact, o, gc, ga, x, wc, wa, wo, g1, b1, wr, cnt_in, tm, x1_all=None, m_total=None):
    m = x.shape[0]
    aliased = x1_all is not None
    n_tiles = m // tm
    total = x1_all.shape[0] if aliased else m_total
    blk0 = (total - m) // tm if aliased else 0
    n_steps = n_tiles if aliased else pl.cdiv(total, tm)
    row = lambda w: pl.BlockSpec((tm, w), lambda i: (jnp.minimum(i, n_tiles - 1), 0))
    res = lambda a: pl.BlockSpec(a.shape, lambda i: (0,) * a.ndim, pipeline_mode=pl.Buffered(1))
    cnt_spec = pl.BlockSpec((1, LANES), lambda i: (0, 0))
    args = [yact, o, gc, ga, x, wc, wa, wo, g1, b1, wr, cnt_in]
    in_specs = [row(C_CONV), row(N_HEADS * HEAD_DIM), row(D_MODEL), row(D_MODEL), row(D_MODEL),
                res(wc), res(wa), res(wo), res(g1), res(b1), res(wr), cnt_spec]
    if aliased:
        args.append(x1_all)
        in_specs.append(pl.BlockSpec(memory_space=pl.ANY))
    return pl.pallas_call(
        functools.partial(_merge_ln1_kernel, aliased=aliased, n_tiles=n_tiles),
        grid=(n_steps,),
        in_specs=in_specs,
        out_specs=(pl.BlockSpec((tm, D_MODEL), lambda i: (blk0 + i, 0)), row(LANES), row(LANES), cnt_spec),
        out_shape=(jax.ShapeDtypeStruct((total, D_MODEL), F32),
                   jax.ShapeDtypeStruct((m, LANES), I32),
                   jax.ShapeDtypeStruct((m, LANES), F32),
                   jax.ShapeDtypeStruct((1, LANES), F32)),
        input_output_aliases={len(args) - 1: 0} if aliased else {},
        compiler_params=_cparams(("arbitrary",)),
        name="merge_ln1",
    )(*args)


def _experts_kernel(fb_ref, dest_ref, x_hbm, wg_ref, wu_ref, wd_ref, y_hbm,
                    rowtok, xbuf, ybuf, wgb, wub, wdb, gsem, ysem, *, n_blocks):
    e = pl.program_id(0)
    n_used = fb_ref[N_EXPERTS]
    last = jnp.maximum(n_used - 1, 0)

    def start_gather(blk, sl):
        base = blk * MOE_BLK
        for i in range(MOE_BLK):
            pltpu.make_async_copy(x_hbm.at[rowtok[base + i]], xbuf.at[sl, i], gsem.at[sl]).start()

    def wait_gather(sl):
        pltpu.make_async_copy(xbuf.at[sl], xbuf.at[sl], gsem.at[sl]).wait()

    def write_copy(blk, sl):
        r0 = pl.multiple_of(blk * MOE_BLK, MOE_BLK)
        return pltpu.make_async_copy(ybuf.at[sl], y_hbm.at[pl.ds(r0, MOE_BLK)], ysem.at[sl])

    @pl.when(e == 0)
    def _():
        def clear(r, carry):
            rowtok[r] = 0
            return carry
        lax.fori_loop(0, n_blocks * MOE_BLK, clear, 0, unroll=8)

        def place(a, carry):
            rowtok[dest_ref[a]] = lax.shift_right_logical(a, 1)
            return carry
        lax.fori_loop(0, dest_ref.shape[0], place, 0, unroll=8)
        start_gather(0, 0)

    b0, b1 = fb_ref[e], fb_ref[e + 1]

    @pl.when(b1 > b0)
    def _():
        wgb[...] = wg_ref[0].astype(BF16)
        wub[...] = wu_ref[0].astype(BF16)
        wdb[...] = wd_ref[0].astype(BF16)

    def block(b, carry):
        sl = lax.rem(b, 2)

        @pl.when(b >= 2)
        def _():
            write_copy(b - 2, sl).wait()

        wait_gather(sl)
        start_gather(jnp.minimum(b + 1, last), 1 - sl)
        n_ct = D_MODEL // LANES
        grp = SUBLANES * n_ct
        xg = xbuf[sl].astype(BF16).reshape(MOE_BLK // SUBLANES, grp, LANES)
        r = lax.broadcasted_iota(I32, (grp, grp), 0)
        k = lax.broadcasted_iota(I32, (grp, grp), 1)
        perm = jnp.where((r // SUBLANES == k % n_ct) & (r % SUBLANES == k // n_ct), 1.0, 0.0).astype(BF16)
        xt = jnp.stack([jnp.dot(perm, xg[t], preferred_element_type=F32) for t in range(MOE_BLK // SUBLANES)])
        xb = jnp.concatenate([xt[:, c * SUBLANES:(c + 1) * SUBLANES, :].reshape(MOE_BLK, LANES)
                              for c in range(n_ct)], axis=1).astype(BF16)
        hg = jnp.dot(xb, wgb[...], preferred_element_type=F32)
        hu = jnp.dot(xb, wub[...], preferred_element_type=F32)
        h = (hg * jax.nn.sigmoid(hg) * hu).astype(BF16)
        ybuf[sl] = jnp.dot(h, wdb[...], preferred_element_type=F32)
        write_copy(b, sl).start()
        return carry

    lax.fori_loop(b0, b1, block, 0)

    @pl.when(e == pl.num_programs(0) - 1)
    def _():
        wait_gather(lax.rem(n_used, 2))

        @pl.when(n_used >= 1)
        def _():
            write_copy(last, lax.rem(last, 2)).wait()

        @pl.when(n_used >= 2)
        def _():
            write_copy(last - 1, lax.rem(last - 1, 2)).wait()

        ybuf[0] = jnp.zeros(ybuf.shape[1:], F32)

        def fill(blk, carry):
            write_copy(blk, 0).start()
            return carry
        lax.fori_loop(n_used, n_blocks, fill, 0)

        def fill_wait(blk, carry):
            write_copy(blk, 0).wait()
            return carry
        lax.fori_loop(n_used, n_blocks, fill_wait, 0)


def _experts(first_blk, dest, x1_all, wg, wu, wd):
    n_blocks = (dest.shape[0] + N_EXPERTS * (MOE_BLK - 1) + MOE_BLK - 1) // MOE_BLK
    n_rows = n_blocks * MOE_BLK
    grid_spec = pltpu.PrefetchScalarGridSpec(
        num_scalar_prefetch=2,
        grid=(N_EXPERTS,),
        in_specs=[pl.BlockSpec(memory_space=pl.ANY),
                  pl.BlockSpec((1, D_MODEL, D_EXPERT), lambda e, fb, de: (e, 0, 0)),
                  pl.BlockSpec((1, D_MODEL, D_EXPERT), lambda e, fb, de: (e, 0, 0)),
                  pl.BlockSpec((1, D_EXPERT, D_MODEL), lambda e, fb, de: (e, 0, 0))],
        out_specs=pl.BlockSpec(memory_space=pl.ANY),
        scratch_shapes=[pltpu.SMEM((n_rows,), I32),
                        pltpu.VMEM((2, MOE_BLK, D_MODEL // LANES, LANES), F32),
                        pltpu.VMEM((2, MOE_BLK, D_MODEL), F32),
                        pltpu.VMEM((D_MODEL, D_EXPERT), BF16),
                        pltpu.VMEM((D_MODEL, D_EXPERT), BF16),
                        pltpu.VMEM((D_EXPERT, D_MODEL), BF16),
                        pltpu.SemaphoreType.DMA((2,)),
                        pltpu.SemaphoreType.DMA((2,))],
    )
    return pl.pallas_call(
        functools.partial(_experts_kernel, n_blocks=n_blocks),
        grid_spec=grid_spec,
        out_shape=jax.ShapeDtypeStruct((n_rows, D_MODEL), F32),
        compiler_params=_cparams(("arbitrary",)),
        name="experts",
    )(first_blk, dest, x1_all, wg, wu, wd)


def _combine_ln2_kernel(dest_ref, x1_ref, gt_ref, yb_hbm, g_ref, b_ref, y_ref, ybuf, sem, *, tm, tok0):
    i = pl.program_id(0)
    last = pl.num_programs(0) - 1
    slot = lax.rem(i, 2)

    def row_copy(a, sl, k, r):
        return pltpu.make_async_copy(yb_hbm.at[pl.ds(dest_ref[a], 1)], ybuf.at[sl, k, pl.ds(r, 1)], sem.at[sl])

    def start_gather(tile, sl, unrolled):
        base = 2 * (tok0 + tile * tm)
        if unrolled:
            for r in range(tm):
                row_copy(base + 2 * r, sl, 0, r).start()
                row_copy(base + 2 * r + 1, sl, 1, r).start()
        else:
            def body(r, carry):
                row_copy(base + 2 * r, sl, 0, r).start()
                row_copy(base + 2 * r + 1, sl, 1, r).start()
                return carry
            lax.fori_loop(0, tm, body, 0)

    def wait_gather(sl):
        pltpu.make_async_copy(ybuf.at[sl], ybuf.at[sl], sem.at[sl]).wait()

    @pl.when(i == 0)
    def _():
        start_gather(0, 0, unrolled=False)

    wait_gather(slot)
    start_gather(jnp.minimum(i + 1, last), 1 - slot, unrolled=True)
    gt = gt_ref[...]
    z = DEEPNORM_ALPHA * x1_ref[...] + gt[:, 0:1] * ybuf[slot, 0] + gt[:, 1:2] * ybuf[slot, 1]
    y_ref[...] = _layer_norm(z, g_ref[...], b_ref[...])

    @pl.when(i == last)
    def _():
        wait_gather(1 - slot)


def _combine_ln2(dest, x1_all, gt, yb, g2, b2, tm, tok0):
    m = gt.shape[0]
    blk0 = tok0 // tm
    grid_spec = pltpu.PrefetchScalarGridSpec(
        num_scalar_prefetch=1,
        grid=(m // tm,),
        in_specs=[pl.BlockSpec((tm, D_MODEL), lambda i, d: (blk0 + i, 0)),
                  pl.BlockSpec((tm, LANES), lambda i, d: (i, 0)),
                  pl.BlockSpec(memory_space=pl.ANY),
                  pl.BlockSpec((1, D_MODEL), lambda i, d: (0, 0)),
                  pl.BlockSpec((1, D_MODEL), lambda i, d: (0, 0))],
        out_specs=pl.BlockSpec((tm, D_MODEL), lambda i, d: (i, 0)),
        scratch_shapes=[pltpu.VMEM((2, 2, tm, D_MODEL), F32),
                        pltpu.SemaphoreType.DMA((2,))],
    )
    return pl.pallas_call(
        functools.partial(_combine_ln2_kernel, tm=tm, tok0=tok0),
        grid_spec=grid_spec,
        out_shape=jax.ShapeDtypeStruct((m, D_MODEL), F32),
        compiler_params=_cparams(("arbitrary",)),
        name="combine_ln2",
    )(dest, x1_all, gt, yb, g2, b2)


def _dispatch_tables(eid, rank, counts):
    padded = (counts + MOE_BLK - 1) // MOE_BLK * MOE_BLK
    pad_end = jnp.cumsum(padded)
    pad_start = pad_end - padded
    start_of = jnp.sum(jnp.where(eid[..., None] == jnp.arange(N_EXPERTS, dtype=I32), pad_start, 0), axis=-1)
    dest = (start_of + rank).reshape(-1).astype(I32)
    first_blk = (jnp.concatenate([pad_start, pad_end[-1:]]) // MOE_BLK).astype(I32)
    return dest, first_blk


def kernel(x_prompt, x_sample, cache_k, cache_v, cache_kidx, state_conv, page_table, w_in, conv_dw_w, conv_dw_b,
           conv_ln_g, conv_ln_b, w_conv_out, w_attn_out, w_out, ln1_g, ln1_b, w_router_group, w_router_expert,
           w_expert_gate, w_expert_up, w_expert_down, ln2_g, ln2_b):
    assert w_in.shape[0] == 1, "single-layer trunk"
    nb, t, d = x_prompt.shape
    ns = x_sample.shape[0]
    n_pool = cache_k.shape[1]
    mp = nb * t
    row2 = lambda a: a.reshape(1, -1)

    w_arr = _arrange_w_in(w_in[0])
    dww, dwb, lng, lnb = conv_dw_w[0], row2(conv_dw_b[0]), row2(conv_ln_g[0]), row2(conv_ln_b[0])
    wc, wa, wo = w_conv_out[0].astype(BF16), w_attn_out[0].astype(BF16), w_out[0].astype(BF16)
    g1, b1, g2, b2 = row2(ln1_g[0]), row2(ln1_b[0]), row2(ln2_g[0]), row2(ln2_b[0])
    wr32 = jnp.concatenate([w_router_group[0], w_router_expert[0],
                            jnp.zeros((d, LANES - N_GROUPS - N_EXPERTS), F32)], axis=1)
    wr_hi = wr32.astype(BF16)
    wr = jnp.stack([wr_hi, (wr32 - wr_hi.astype(F32)).astype(BF16)])

    xp = x_prompt.reshape(mp, d)
    u_p, q_p, qi_p, kv_p, kvb_p, gc_p, ga_p, kiwi_p, kib_p = _in_proj(xp, w_arr, tm=512)
    yact_p = _conv_prompt(u_p.reshape(nb, t, C_CONV), dww, dwb, lng, lnb)
    o_p = _attn_prompt(qi_p.reshape(nb, t, -1), kiwi_p.reshape(nb, t, LANES), kib_p.reshape(nb, t, LANES),
                       q_p.reshape(nb, t, -1), kvb_p.reshape(nb, t, 2 * KV_DIM))
    x1_all, ei_p, gt_p, cnt_p = _merge_ln1(yact_p.reshape(mp, C_CONV), o_p.reshape(mp, -1), gc_p, ga_p, xp,
                                           wc, wa, wo, g1, b1, wr, jnp.zeros((1, LANES), F32), tm=256,
                                           m_total=mp + ns)

    xs = x_sample.reshape(ns, d)
    u_s, q_s, qi_s, kv_s, _, gc_s, ga_s, kiwi_s, _ = _in_proj(xs, w_arr, tm=ns)
    yact_s = _conv_sample(state_conv[0].transpose(1, 0, 2), u_s, dww, dwb, lng, lnb)
    cki2 = cache_kidx.transpose(0, 1, 3, 2).reshape(n_pool, IDX_DIM, PAGE_SIZE)
    ck2 = cache_k.transpose(0, 1, 3, 4, 2).reshape(n_pool, KV_DIM, PAGE_SIZE)
    cv2 = cache_v.transpose(0, 1, 3, 4, 2).reshape(n_pool, KV_DIM, PAGE_SIZE)
    scores_s = _sample_scores(page_table, qi_s.reshape(ns, IDX_HEADS, IDX_DIM),
                              kiwi_s[:, IDX_DIM:IDX_DIM + IDX_HEADS].reshape(ns, IDX_HEADS, 1),
                              kiwi_s[:, :IDX_DIM].reshape(ns, 1, IDX_DIM), cki2)
    scores3 = scores_s.reshape(ns, -1, ATT_CK).transpose(1, 0, 2)
    own_group = (np.arange(N_HEADS)[:, None] // HEADS_PER_KV) == (np.arange(KV_DIM)[None, :] // HEAD_DIM)
    qbd = jnp.where(own_group[None], jnp.tile(q_s.reshape(ns, N_HEADS, HEAD_DIM), (1, 1, KV_HEADS)), 0).astype(BF16)
    o3 = _sample_attn(page_table, scores3, qbd, kv_s[:, :KV_DIM].reshape(ns, 1, KV_DIM),
                      kv_s[:, KV_DIM:].reshape(ns, 1, KV_DIM), ck2, cv2)
    o_s = o3.reshape(ns, N_HEADS * HEAD_DIM).astype(BF16)
    x1_all, ei_s, gt_s, cnt = _merge_ln1(yact_s, o_s, gc_s, ga_s, xs, wc, wa, wo, g1, b1, wr, cnt_p, tm=ns,
                                         x1_all=x1_all)

    eid_rank = jnp.concatenate([ei_p[:, :4], ei_s[:, :4]], axis=0)
    dest, first_blk = _dispatch_tables(eid_rank[:, :2], eid_rank[:, 2:], cnt[0, :N_EXPERTS].astype(I32))
    x1_tok = x1_all.reshape(x1_all.shape[0], d // LANES, LANES)
    yb = _experts(first_blk, dest, x1_tok, w_expert_gate[0], w_expert_up[0], w_expert_down[0])
    y_p = _combine_ln2(dest, x1_all, gt_p, yb, g2, b2, tm=256, tok0=0)
    y_s = _combine_ln2(dest, x1_all, gt_s, yb, g2, b2, tm=ns, tok0=mp)

    y_prompt = y_p.reshape(nb, t, d)
    y_sample = y_s.reshape(ns, 1, d)
    k_prompt = kv_p[:, :KV_DIM].reshape(1, nb, t, KV_HEADS, HEAD_DIM)
    v_prompt = kv_p[:, KV_DIM:].reshape(1, nb, t, KV_HEADS, HEAD_DIM)
    kidx_prompt = kiwi_p[:, :IDX_DIM].reshape(1, nb, t, IDX_DIM)
    conv_prompt = u_p.reshape(nb, t, C_CONV)[:, t - (CONV_W - 1):][None]
    k_sample = kv_s[:, :KV_DIM].reshape(1, ns, 1, KV_HEADS, HEAD_DIM)
    v_sample = kv_s[:, KV_DIM:].reshape(1, ns, 1, KV_HEADS, HEAD_DIM)
    kidx_sample = kiwi_s[:, :IDX_DIM].reshape(1, ns, 1, IDX_DIM)
    conv_sample = jnp.concatenate([state_conv[0][:, 1:], u_s[:, None, :]], axis=1)[None]
    return (y_prompt, y_sample, k_prompt, v_prompt, kidx_prompt, conv_prompt, k_sample, v_sample, kidx_sample,
            conv_sample)
```

```python
import functools

import jax
import jax.numpy as jnp
import numpy as np
from jax import lax
from jax.experimental import pallas as pl
from jax.experimental.pallas import tpu as pltpu

F32 = jnp.float32
BF16 = jnp.bfloat16
I32 = jnp.int32

D_MODEL = 2048
C_CONV = 1024
CONV_W = 31
N_HEADS = 16
HEAD_DIM = 64
KV_HEADS = 4
HEADS_PER_KV = N_HEADS // KV_HEADS
KV_DIM = KV_HEADS * HEAD_DIM
IDX_HEADS = 16
IDX_DIM = 64
MAX_TOPK = 256
N_GROUPS = 8
EXPERTS_PER_GROUP = 8
N_EXPERTS = 64
D_EXPERT = 512
MOE_BLK = 128
PAGE_SIZE = 128
LN_EPS = 1e-5
DEEPNORM_ALPHA = 2.0 ** 0.25
NEG_INF = -1e30
INT_MIN = -(2 ** 31)
NEG_KEY = int(INT_MIN - int(np.array(NEG_INF, np.float32).view(np.int32)))
LOG2E = 1.4426950408889634
Q_SCALE = HEAD_DIM ** -0.5 * LOG2E
IDX_SCALE = IDX_HEADS ** -0.5 * IDX_DIM ** -0.5

LANES = 128
HALF = LANES // 2
IN_TN = 1024
VMEM_LIMIT = 56 * 1024 * 1024


def _cparams(sem, vmem=VMEM_LIMIT):
    return pltpu.CompilerParams(dimension_semantics=sem, vmem_limit_bytes=vmem)


def _float_key(x):
    b = pltpu.bitcast(x, I32)
    return jnp.where(b < 0, INT_MIN - b, b)


def _layer_norm(z, g, b):
    mu = jnp.mean(z, axis=-1, keepdims=True)
    zc = z - mu
    var = jnp.mean(zc * zc, axis=-1, keepdims=True)
    return zc * lax.rsqrt(var + LN_EPS) * g + b


def _dot_nt(a, b):
    return lax.dot_general(a, b, (((1,), (1,)), ((), ())), preferred_element_type=F32)


def _alibi_slope(h):
    return float(2.0 ** (-8.0 * (h + 1) / N_HEADS))


_T_GLU, _T_Q, _T_QI, _T_KV, _T_GC, _T_GA, _N_TILES = 0, 2, 3, 4, 5, 7, 9
KV_TILE_COLS = 2 * KV_DIM + 2 * LANES


def _arrange_w_in(w):
    sizes = (C_CONV, C_CONV, N_HEADS * HEAD_DIM, KV_DIM, KV_DIM,
             IDX_HEADS * IDX_DIM, IDX_DIM, IDX_HEADS, D_MODEL, D_MODEL)
    offs = np.cumsum((0,) + sizes)
    a, g, q, k, v, qi, ki, wi, gc, ga = [w[:, offs[i]:offs[i + 1]] for i in range(10)]
    half = IN_TN // 2
    glu = [jnp.concatenate([a[:, t * half:(t + 1) * half], g[:, t * half:(t + 1) * half]], axis=1)
           for t in range(C_CONV // half)]
    zeros = lambda n: jnp.zeros((w.shape[0], n), w.dtype)
    kv_tile = jnp.concatenate([k, v, ki, wi, zeros(LANES - IDX_DIM - IDX_HEADS), ki, ki,
                               zeros(IN_TN - KV_TILE_COLS)], axis=1)
    return jnp.concatenate(glu + [q, qi, kv_tile, gc, ga], axis=1).astype(BF16)


def _in_proj_kernel(x_ref, w_ref, u_ref, q_ref, qi_ref, kv_ref, kvb_ref, gc_ref, ga_ref, kiwi_ref, kib_ref,
                    xb_ref):
    j = pl.program_id(1)

    @pl.when(j == 0)
    def _():
        xb_ref[...] = x_ref[...].astype(BF16)

    def mm(ncols=IN_TN):
        return jnp.dot(xb_ref[...], w_ref[:, :ncols], preferred_element_type=F32)

    @pl.when(j < _T_Q)
    def _():
        acc = mm()
        u_ref[...] = acc[:, :IN_TN // 2] * jax.nn.sigmoid(acc[:, IN_TN // 2:])

    @pl.when(j == _T_Q)
    def _():
        q_ref[...] = (mm() * Q_SCALE).astype(BF16)

    @pl.when(j == _T_QI)
    def _():
        qi_ref[...] = mm().astype(BF16)

    @pl.when(j == _T_KV)
    def _():
        acc = mm(KV_TILE_COLS)
        kv_ref[...] = acc[:, :2 * KV_DIM]
        kvb_ref[...] = acc[:, :2 * KV_DIM].astype(BF16)
        kiwi_ref[...] = acc[:, 2 * KV_DIM:2 * KV_DIM + LANES]
        kib_ref[...] = acc[:, 2 * KV_DIM + LANES:].astype(BF16)

    @pl.when((j >= _T_GC) & (j < _T_GA))
    def _():
        gc_ref[...] = jax.nn.sigmoid(mm()).astype(BF16)

    @pl.when(j >= _T_GA)
    def _():
        ga_ref[...] = jax.nn.sigmoid(mm()).astype(BF16)


def _in_proj(x, w_arr, tm):
    m = x.shape[0]
    half = IN_TN // 2

    def cl(lo, n):
        return lambda i, j: (i, jnp.clip(j - lo, 0, n - 1))

    out_shape = (
        jax.ShapeDtypeStruct((m, C_CONV), F32),
        jax.ShapeDtypeStruct((m, N_HEADS * HEAD_DIM), BF16),
        jax.ShapeDtypeStruct((m, IDX_HEADS * IDX_DIM), BF16),
        jax.ShapeDtypeStruct((m, 2 * KV_DIM), F32),
        jax.ShapeDtypeStruct((m, 2 * KV_DIM), BF16),
        jax.ShapeDtypeStruct((m, D_MODEL), BF16),
        jax.ShapeDtypeStruct((m, D_MODEL), BF16),
        jax.ShapeDtypeStruct((m, LANES), F32),
        jax.ShapeDtypeStruct((m, LANES), BF16),
    )
    out_specs = (
        pl.BlockSpec((tm, half), cl(_T_GLU, 2)),
        pl.BlockSpec((tm, IN_TN), lambda i, j: (i, 0)),
        pl.BlockSpec((tm, IN_TN), lambda i, j: (i, 0)),
        pl.BlockSpec((tm, 2 * KV_DIM), lambda i, j: (i, 0)),
        pl.BlockSpec((tm, 2 * KV_DIM), lambda i, j: (i, 0)),
        pl.BlockSpec((tm, IN_TN), cl(_T_GC, 2)),
        pl.BlockSpec((tm, IN_TN), cl(_T_GA, 2)),
        pl.BlockSpec((tm, LANES), lambda i, j: (i, 0)),
        pl.BlockSpec((tm, LANES), lambda i, j: (i, 0)),
    )
    return pl.pallas_call(
        _in_proj_kernel,
        grid=(m // tm, _N_TILES),
        in_specs=[pl.BlockSpec((tm, D_MODEL), lambda i, j: (i, 0)),
                  pl.BlockSpec((D_MODEL, IN_TN), lambda i, j: (0, j))],
        out_specs=out_specs,
        out_shape=out_shape,
        scratch_shapes=[pltpu.VMEM((tm, D_MODEL), BF16)],
        compiler_params=_cparams(("arbitrary", "arbitrary")),
        name="in_proj",
    )(x, w_arr)


CONV_HALO = 32
CONV_STRIP = 32


SUBLANES = 8


def _conv_prompt_kernel(ucur_ref, uprev_ref, dww_ref, dwb_ref, lng_ref, lnb_ref, y_ref, ext_ref, sh_ref, *, tt):
    tb = pl.program_id(1)
    ext_ref[0:CONV_HALO, :] = jnp.where(tb > 0, uprev_ref[0], 0.0)
    ext_ref[CONV_HALO:, :] = ucur_ref[0]
    first = CONV_HALO - (CONV_W - 1)
    for b in range(SUBLANES):
        rows = tt + SUBLANES * ((CONV_W - 1 - b) // SUBLANES)
        sh_ref[b, 0:rows, :] = ext_ref[first + b:first + b + rows, :]
    dwb, lng, lnb = dwb_ref[...], lng_ref[...], lnb_ref[...]
    for r in range(tt // CONV_STRIP):
        acc = jnp.broadcast_to(dwb, (CONV_STRIP, C_CONV))
        for j in range(CONV_W):
            a, b = divmod(j, SUBLANES)
            r0 = r * CONV_STRIP + a * SUBLANES
            acc = acc + dww_ref[j:j + 1, :] * sh_ref[b, r0:r0 + CONV_STRIP, :]
        yn = _layer_norm(acc, lng, lnb)
        y_ref[0, r * CONV_STRIP:(r + 1) * CONV_STRIP, :] = (yn * jax.nn.sigmoid(yn)).astype(BF16)


def _conv_prompt(u, dww, dwb, lng, lnb, tt=128):
    n, t, c = u.shape
    per = tt // CONV_HALO
    vec = pl.BlockSpec((1, c), lambda b, i: (0, 0))
    return pl.pallas_call(
        functools.partial(_conv_prompt_kernel, tt=tt),
        grid=(n, t // tt),
        in_specs=[pl.BlockSpec((1, tt, c), lambda b, i: (b, i, 0)),
                  pl.BlockSpec((1, CONV_HALO, c), lambda b, i: (b, jnp.maximum(i * per - 1, 0), 0)),
                  pl.BlockSpec((CONV_W, c), lambda b, i: (0, 0)), vec, vec, vec],
        out_specs=pl.BlockSpec((1, tt, c), lambda b, i: (b, i, 0)),
        out_shape=jax.ShapeDtypeStruct((n, t, c), BF16),
        scratch_shapes=[pltpu.VMEM((tt + CONV_HALO, c), F32),
                        pltpu.VMEM((SUBLANES, tt + SUBLANES * ((CONV_W - 1) // SUBLANES), c), F32)],
        compiler_params=_cparams(("arbitrary", "arbitrary")),
        name="conv_prompt",
    )(u, u, dww, dwb, lng, lnb)


def _conv_sample_kernel(state_ref, u_ref, dww_ref, dwb_ref, lng_ref, lnb_ref, y_ref):
    acc = dwb_ref[...] + dww_ref[CONV_W - 1:CONV_W, :] * u_ref[...]
    for j in range(CONV_W - 1):
        acc = acc + dww_ref[j:j + 1, :] * state_ref[j]
    yn = _layer_norm(acc, lng_ref[...], lnb_ref[...])
    y_ref[...] = (yn * jax.nn.sigmoid(yn)).astype(BF16)


def _conv_sample(state_t, u, dww, dwb, lng, lnb):
    n = u.shape[0]
    return pl.pallas_call(
        _conv_sample_kernel,
        out_shape=jax.ShapeDtypeStruct((n, C_CONV), BF16),
        compiler_params=pltpu.CompilerParams(vmem_limit_bytes=VMEM_LIMIT),
        name="conv_sample",
    )(state_t, u, dww, dwb, lng, lnb)


ATT_TQ = 256
ATT_CK = 256


def _count_ge(key_ref, nch, cand):
    def body(c, acc):
        m = jnp.where(key_ref[c] >= cand, 1.0, 0.0)
        return acc + m[:, :LANES] + m[:, LANES:]
    acc = lax.fori_loop(0, nch, body, jnp.zeros((cand.shape[0], LANES), F32))
    return jnp.sum(acc, axis=1, keepdims=True)


def _select_threshold(key_ref, nch, rows):
    def body(i, prefix):
        cand = prefix + jnp.left_shift(jnp.int32(1), 31 - i)
        return jnp.where(_count_ge(key_ref, nch, cand) >= float(MAX_TOPK), cand, prefix)
    return lax.fori_loop(0, 32, body, jnp.full((rows, 1), INT_MIN, I32))


def _selection_bias(key_ref, bias_ref, nch):
    rows, ck = key_ref.shape[1:]
    thr = _select_threshold(key_ref, nch, rows)
    n_tie_take = float(MAX_TOPK) - _count_ge(key_ref, nch, thr + 1)
    tri = jnp.where(lax.broadcasted_iota(I32, (ck, ck), 0) <= lax.broadcasted_iota(I32, (ck, ck), 1),
                    1.0, 0.0).astype(BF16)

    def bias_chunk(c, seen):
        key = key_ref[c]
        eq = key == thr
        eqf = jnp.where(eq, 1.0, 0.0)
        incl = jnp.dot(eqf.astype(BF16), tri, preferred_element_type=F32)
        sel = (key > thr) | (eq & (seen + incl - eqf < n_tie_take))
        bias_ref[c] = jnp.where(sel & (key > NEG_KEY), 0.0, NEG_INF)
        return seen + incl[:, ck - 1:ck]

    lax.fori_loop(0, nch, bias_chunk, jnp.zeros((rows, 1), F32))


def _swap_halves(x):
    return jnp.concatenate([x[:, HALF:], x[:, :HALF]], axis=1)


def _attn_prompt_kernel(qi_ref, kiwi_ref, kib_ref, q_ref, kvb_ref, o_ref,
                        key_ref, bias_ref, wb_ref, m_ref, ls_ref, acc_ref):
    tq, ck = ATT_TQ, ATT_CK
    qb = pl.program_id(1)
    nch = qb + 1
    row = qb * tq + lax.broadcasted_iota(I32, (tq, ck), 0)
    coll = lax.broadcasted_iota(I32, (tq, ck), 1)
    lo_k = lax.broadcasted_iota(I32, (ck, LANES), 1) < HALF
    lo_q = lax.broadcasted_iota(I32, (tq, LANES), 1) < HALF

    w = kiwi_ref[0][:, IDX_DIM:IDX_DIM + IDX_HEADS] * IDX_SCALE
    for h in range(IDX_HEADS):
        wb_ref[h] = jnp.broadcast_to(w[:, h:h + 1], (tq, LANES))

    def score_chunk(c, carry):
        k0 = pl.multiple_of(c * ck, ck)
        kk = kib_ref[0, pl.ds(k0, ck), :]
        zero = jnp.zeros_like(kk)
        k_lo, k_hi = jnp.where(lo_k, kk, zero), jnp.where(lo_k, zero, kk)
        acc = jnp.zeros((tq, ck), F32)
        for p in range(IDX_HEADS // 2):
            qp = qi_ref[0, :, p * LANES:(p + 1) * LANES]
            w_lo, w_hi = wb_ref[2 * p], wb_ref[2 * p + 1]
            acc = acc + jnp.concatenate([w_lo, w_lo], axis=1) * jnp.maximum(_dot_nt(qp, k_lo), 0.0)
            acc = acc + jnp.concatenate([w_hi, w_hi], axis=1) * jnp.maximum(_dot_nt(qp, k_hi), 0.0)
        acc = jnp.where(c * ck + coll <= row, acc, NEG_INF)
        key_ref[c] = _float_key(acc)
        return carry

    lax.fori_loop(0, nch, score_chunk, 0)
    _selection_bias(key_ref, bias_ref, nch)

    m_ref[...] = jnp.full(m_ref.shape, -jnp.inf, F32)
    ls_ref[...] = jnp.zeros(ls_ref.shape, F32)
    acc_ref[...] = jnp.zeros(acc_ref.shape, F32)

    def attn_chunk(c, carry):
        k0 = pl.multiple_of(c * ck, ck)
        kv = kvb_ref[0, pl.ds(k0, ck), :]
        bias = bias_ref[c]
        rel = (c * ck - (qb + 1) * tq + 1 + lax.broadcasted_iota(I32, (1, ck), 1)).astype(F32)
        for j in range(KV_HEADS // 2):
            kt = kv[:, j * LANES:(j + 1) * LANES]
            vt = kv[:, KV_DIM + j * LANES:KV_DIM + (j + 1) * LANES]
            kts, vts = _swap_halves(kt), _swap_halves(vt)
            zero, one = jnp.zeros_like(kt), jnp.ones_like(kt)
            for gg in range(2):
                g = 2 * j + gg
                k_own, k_swp = (kt, kts) if gg == 0 else (kts, kt)
                v_own, v_swp = (vt, vts) if gg == 0 else (vts, vt)
                k_lo, k_hi = jnp.where(lo_k, k_own, zero), jnp.where(lo_k, zero, k_swp)
                v_lo, v_hi = jnp.where(lo_k, v_own, one), jnp.where(lo_k, one, v_swp)
                for mm in range(HEADS_PER_KV // 2):
                    pidx = (HEADS_PER_KV // 2) * g + mm
                    qp = q_ref[0, :, pidx * LANES:(pidx + 1) * LANES]
                    pvs, alphas = [], []
                    for half, (kmat, vmat) in enumerate(((k_lo, v_lo), (k_hi, v_hi))):
                        h = 2 * pidx + half
                        s = _dot_nt(qp, kmat) + (_alibi_slope(h) * LOG2E) * rel + bias
                        m_old = m_ref[h]
                        m_new = jnp.maximum(m_old, jnp.max(s, axis=1, keepdims=True))
                        p = jnp.exp2(s - jnp.concatenate([m_new, m_new], axis=1))
                        alphas.append(jnp.exp2(m_old - m_new))
                        m_ref[h] = m_new
                        pvs.append(jnp.dot(p.astype(BF16), vmat, preferred_element_type=F32))
                    acc_ref[pidx] = (jnp.where(lo_q, alphas[0], alphas[1]) * acc_ref[pidx]
                                     + jnp.where(lo_q, pvs[0], pvs[1]))
                    ls_ref[pidx] = (jnp.where(lo_q, alphas[1], alphas[0]) * ls_ref[pidx]
                                    + jnp.where(lo_q, pvs[1], pvs[0]))
        return carry

    lax.fori_loop(0, nch, attn_chunk, 0)
    for pidx in range(N_HEADS // 2):
        l_pair = pltpu.roll(ls_ref[pidx], HALF, 1)
        o_ref[0, :, pidx * LANES:(pidx + 1) * LANES] = (acc_ref[pidx] / l_pair).astype(BF16)


def _attn_prompt(qi, kiwi, kib, q, kvb):
    n, t, _ = q.shape
    tq = ATT_TQ
    return pl.pallas_call(
        _attn_prompt_kernel,
        grid=(n, t // tq),
        in_specs=[pl.BlockSpec((1, tq, IDX_HEADS * IDX_DIM), lambda b, i: (b, i, 0)),
                  pl.BlockSpec((1, tq, LANES), lambda b, i: (b, i, 0)),
                  pl.BlockSpec((1, t, LANES), lambda b, i: (b, 0, 0)),
                  pl.BlockSpec((1, tq, N_HEADS * HEAD_DIM), lambda b, i: (b, i, 0)),
                  pl.BlockSpec((1, t, 2 * KV_DIM), lambda b, i: (b, 0, 0))],
        out_specs=pl.BlockSpec((1, tq, N_HEADS * HEAD_DIM), lambda b, i: (b, i, 0)),
        out_shape=jax.ShapeDtypeStruct((n, t, N_HEADS * HEAD_DIM), BF16),
        scratch_shapes=[pltpu.VMEM((t // ATT_CK, tq, ATT_CK), I32),
                        pltpu.VMEM((t // ATT_CK, tq, ATT_CK), F32),
                        pltpu.VMEM((IDX_HEADS, tq, LANES), F32),
                        pltpu.VMEM((N_HEADS, tq, LANES), F32),
                        pltpu.VMEM((N_HEADS // 2, tq, LANES), F32),
                        pltpu.VMEM((N_HEADS // 2, tq, LANES), F32)],
        compiler_params=_cparams(("arbitrary", "arbitrary")),
        name="attn_prompt",
    )(qi, kiwi, kib, q, kvb)


SAMPLE_KC = 2048


def _start_page_copies(pt_ref, src_hbm, dst, sem, base, n_pages):
    def body(p, carry):
        col = pl.multiple_of(p * PAGE_SIZE, PAGE_SIZE)
        pltpu.make_async_copy(src_hbm.at[pt_ref[base + p]], dst.at[:, pl.ds(col, PAGE_SIZE)], sem).start()
        return carry
    lax.fori_loop(0, n_pages, body, 0)


def _sample_score_kernel(pt_ref, qi_ref, wcol_ref, kinew_ref, cki_hbm, sc_ref, kibuf, sem, *, n_pages):
    b = pl.program_id(0)
    nb = pl.num_programs(0)
    past = n_pages * PAGE_SIZE
    slot = b % 2

    @pl.when(b == 0)
    def _():
        _start_page_copies(pt_ref, cki_hbm, kibuf.at[0], sem.at[0], 0, n_pages)

    @pl.when(b + 1 < nb)
    def _():
        _start_page_copies(pt_ref, cki_hbm, kibuf.at[1 - slot], sem.at[1 - slot], (b + 1) * n_pages, n_pages)

    pltpu.make_async_copy(kibuf.at[slot], kibuf.at[slot], sem.at[slot]).wait()

    qi = qi_ref[0]
    wsc = wcol_ref[0] * IDX_SCALE
    for c in range(past // SAMPLE_KC):
        kc = kibuf[slot, :, c * SAMPLE_KC:(c + 1) * SAMPLE_KC].astype(BF16)
        d = jnp.dot(qi, kc, preferred_element_type=F32)
        sc_ref[0, :, c * SAMPLE_KC:(c + 1) * SAMPLE_KC] = jnp.sum(wsc * jnp.maximum(d, 0.0), axis=0, keepdims=True)
    dn = jnp.sum(qi.astype(F32) * kinew_ref[0], axis=1, keepdims=True)
    s_new = jnp.sum(wsc * jnp.maximum(dn, 0.0), axis=0, keepdims=True)
    ln = lax.broadcasted_iota(I32, (1, ATT_CK), 1)
    sc_ref[0, :, past:] = jnp.where(ln == 0, s_new, NEG_INF)


def _sample_scores(page_table, qi3, wcol, kinew, cki2):
    n, n_pages = page_table.shape
    past = n_pages * PAGE_SIZE
    grid_spec = pltpu.PrefetchScalarGridSpec(
        num_scalar_prefetch=1,
        grid=(n,),
        in_specs=[pl.BlockSpec((1, IDX_HEADS, IDX_DIM), lambda b, pt: (b, 0, 0)),
                  pl.BlockSpec((1, IDX_HEADS, 1), lambda b, pt: (b, 0, 0)),
                  pl.BlockSpec((1, 1, IDX_DIM), lambda b, pt: (b, 0, 0)),
                  pl.BlockSpec(memory_space=pl.ANY)],
        out_specs=pl.BlockSpec((1, 1, past + ATT_CK), lambda b, pt: (b, 0, 0)),
        scratch_shapes=[pltpu.VMEM((2, IDX_DIM, past), F32),
                        pltpu.SemaphoreType.DMA((2,))],
    )
    return pl.pallas_call(
        functools.partial(_sample_score_kernel, n_pages=n_pages),
        grid_spec=grid_spec,
        out_shape=jax.ShapeDtypeStruct((n, 1, past + ATT_CK), F32),
        compiler_params=_cparams(("arbitrary",)),
        name="sample_scores",
    )(page_table.reshape(-1), qi3, wcol, kinew, cki2)


def _sample_attn_kernel(pt_ref, sc_ref, qbd_ref, knew_ref, vnew_ref, ck_hbm, cv_hbm, o_ref,
                        key_ref, bias_ref, kbuf, vbuf, semk, semv, *, n_pages):
    b = pl.program_id(0)
    nb = pl.num_programs(0)
    past = n_pages * PAGE_SIZE
    nch = sc_ref.shape[0]
    slot = b % 2

    def start_all(bb, sl):
        _start_page_copies(pt_ref, ck_hbm, kbuf.at[sl], semk.at[sl], bb * n_pages, n_pages)
        _start_page_copies(pt_ref, cv_hbm, vbuf.at[sl], semv.at[sl], bb * n_pages, n_pages)

    @pl.when(b == 0)
    def _():
        start_all(0, 0)
        key_ref[...] = _float_key(sc_ref[...])
        _selection_bias(key_ref, bias_ref, nch)

    @pl.when(b + 1 < nb)
    def _():
        start_all(b + 1, 1 - slot)

    pltpu.make_async_copy(kbuf.at[slot], kbuf.at[slot], semk.at[slot]).wait()
    pltpu.make_async_copy(vbuf.at[slot], vbuf.at[slot], semv.at[slot]).wait()

    qbd = qbd_ref[0]
    bias_row = jnp.concatenate([bias_ref[c, pl.ds(b, 1), :] for c in range(nch)], axis=1)
    s_parts = [jnp.dot(qbd, kbuf[slot, :, c * SAMPLE_KC:(c + 1) * SAMPLE_KC].astype(BF16),
                       preferred_element_type=F32) for c in range(past // SAMPLE_KC)]
    s_new = jnp.sum(qbd.astype(F32) * knew_ref[0], axis=1, keepdims=True)
    ln = lax.broadcasted_iota(I32, (N_HEADS, ATT_CK), 1)
    s = jnp.concatenate(s_parts + [jnp.where(ln == 0, s_new, 0.0)], axis=1)
    hrow = lax.broadcasted_iota(I32, (N_HEADS, 1), 0)
    slope2 = jnp.exp2(-8.0 * (hrow + 1).astype(F32) / N_HEADS) * LOG2E
    rel = (lax.broadcasted_iota(I32, (1, past + ATT_CK), 1) - past).astype(F32)
    s = s + slope2 * rel + bias_row
    m = jnp.max(s, axis=1, keepdims=True)
    p = jnp.exp2(s - m)
    l = jnp.sum(p, axis=1, keepdims=True)
    o_all = p[:, past:past + 1] * vnew_ref[0]
    for c in range(past // SAMPLE_KC):
        o_all = o_all + _dot_nt(p[:, c * SAMPLE_KC:(c + 1) * SAMPLE_KC].astype(BF16),
                                vbuf[slot, :, c * SAMPLE_KC:(c + 1) * SAMPLE_KC].astype(BF16))
    o_all = o_all / l
    out = jnp.zeros((N_HEADS, HEAD_DIM), F32)
    for g in range(KV_HEADS):
        own = (hrow >= g * HEADS_PER_KV) & (hrow < (g + 1) * HEADS_PER_KV)
        out = out + jnp.where(own, o_all[:, g * HEAD_DIM:(g + 1) * HEAD_DIM], 0.0)
    o_ref[0] = out


def _sample_attn(page_table, scores3, qbd, knew, vnew, ck2, cv2):
    n, n_pages = page_table.shape
    past = n_pages * PAGE_SIZE
    nch = scores3.shape[0]
    grid_spec = pltpu.PrefetchScalarGridSpec(
        num_scalar_prefetch=1,
        grid=(n,),
        in_specs=[pl.BlockSpec((nch, n, ATT_CK), lambda b, pt: (0, 0, 0)),
                  pl.BlockSpec((1, N_HEADS, KV_DIM), lambda b, pt: (b, 0, 0)),
                  pl.BlockSpec((1, 1, KV_DIM), lambda b, pt: (b, 0, 0)),
                  pl.BlockSpec((1, 1, KV_DIM), lambda b, pt: (b, 0, 0)),
                  pl.BlockSpec(memory_space=pl.ANY),
                  pl.BlockSpec(memory_space=pl.ANY)],
        out_specs=pl.BlockSpec((1, N_HEADS, HEAD_DIM), lambda b, pt: (b, 0, 0)),
        scratch_shapes=[pltpu.VMEM((nch, n, ATT_CK), I32),
                        pltpu.VMEM((nch, n, ATT_CK), F32),
                        pltpu.VMEM((2, KV_DIM, past), F32),
                        pltpu.VMEM((2, KV_DIM, past), F32),
                        pltpu.SemaphoreType.DMA((2,)),
                        pltpu.SemaphoreType.DMA((2,))],
    )
    return pl.pallas_call(
        functools.partial(_sample_attn_kernel, n_pages=n_pages),
        grid_spec=grid_spec,
        out_shape=jax.ShapeDtypeStruct((n, N_HEADS, HEAD_DIM), F32),
        compiler_params=_cparams(("arbitrary",)),
        name="sample_attn",
    )(page_table.reshape(-1), scores3, qbd, knew, vnew, ck2, cv2)


N_CT = D_MODEL // LANES
TOK_GRP = SUBLANES * N_CT


def _tile_perm(to_token_major):
    r = lax.broadcasted_iota(I32, (TOK_GRP, TOK_GRP), 1 if to_token_major else 0)
    k = lax.broadcasted_iota(I32, (TOK_GRP, TOK_GRP), 0 if to_token_major else 1)
    return jnp.where((r // SUBLANES == k % N_CT) & (r % SUBLANES == k // N_CT), 1.0, 0.0).astype(BF16)


def _to_token_major(x, perm):
    rows = x.shape[0]
    out = []
    for t in range(rows // SUBLANES):
        xs = x[t * SUBLANES:(t + 1) * SUBLANES]
        cm = jnp.concatenate([xs[:, c * LANES:(c + 1) * LANES] for c in range(N_CT)], axis=0).astype(BF16)
        out.append(jnp.dot(perm, cm, preferred_element_type=F32).astype(BF16))
    return jnp.stack(out).reshape(rows, N_CT, LANES)


def _from_token_major(xt, perm):
    rows = xt.shape[0]
    xg = xt.reshape(rows // SUBLANES, TOK_GRP, LANES)
    out = jnp.stack([jnp.dot(perm, xg[t], preferred_element_type=F32) for t in range(rows // SUBLANES)])
    return jnp.concatenate([out[:, c * SUBLANES:(c + 1) * SUBLANES, :].reshape(rows, LANES) for c in range(N_CT)],
                           axis=1)


def _merge_ln1_kernel(*refs, aliased, n_tiles):
    n_in = 12
    outs = refs[n_in + 1:] if aliased else refs[n_in:]
    xt_ref = outs[0]

    @pl.when(pl.program_id(0) < n_tiles)
    def _():
        _merge_ln1_tile(*refs[:n_in], *outs)

    @pl.when(pl.program_id(0) >= n_tiles)
    def _():
        xt_ref[...] = jnp.zeros(xt_ref.shape, BF16)


def _merge_ln1_tile(y_ref, o_ref, gc_ref, ga_ref, x_ref, wc_ref, wa_ref, wo_ref, g_ref, b_ref, wr_ref,
                    cnt_in_ref, xt_ref, x1_ref, ei_ref, gt_ref, cnt_ref):
    @pl.when(pl.program_id(0) == 0)
    def _():
        cnt_ref[...] = cnt_in_ref[...]

    conv_out = jnp.dot(y_ref[...], wc_ref[...], preferred_element_type=F32)
    attn_out = jnp.dot(o_ref[...], wa_ref[...], preferred_element_type=F32)
    merged = gc_ref[...].astype(F32) * conv_out + ga_ref[...].astype(F32) * attn_out
    mix = jnp.dot(merged.astype(BF16), wo_ref[...], preferred_element_type=F32)
    x1 = _layer_norm(DEEPNORM_ALPHA * x_ref[...] + mix, g_ref[...], b_ref[...])
    x1_ref[...] = x1
    xt_ref[...] = _to_token_major(x1, _tile_perm(True))

    x_hi = x1.astype(BF16)
    x_lo = (x1 - x_hi.astype(F32)).astype(BF16)
    logits = (jnp.dot(x_hi, wr_ref[0], preferred_element_type=F32)
              + jnp.dot(x_lo, wr_ref[0], preferred_element_type=F32)
              + jnp.dot(x_hi, wr_ref[1], preferred_element_type=F32))
    lane = lax.broadcasted_iota(I32, logits.shape, 1)
    is_g = lane < N_GROUPS
    lg = jnp.where(is_g, logits, -jnp.inf)
    eg = jnp.exp(lg - jnp.max(lg, axis=1, keepdims=True))
    pg = eg / jnp.sum(eg, axis=1, keepdims=True)
    g_top = jnp.max(pg, axis=1, keepdims=True)
    g_idx = jnp.min(jnp.where(is_g & (pg == g_top), lane, LANES), axis=1, keepdims=True)
    lo = N_GROUPS + g_idx * EXPERTS_PER_GROUP
    is_e = (lane >= lo) & (lane < lo + EXPERTS_PER_GROUP)
    le = jnp.where(is_e, logits, -jnp.inf)
    ee = jnp.exp(le - jnp.max(le, axis=1, keepdims=True))
    pe = ee / jnp.sum(ee, axis=1, keepdims=True)
    p1 = jnp.max(jnp.where(is_e, pe, -1.0), axis=1, keepdims=True)
    i1 = jnp.min(jnp.where(is_e & (pe == p1), lane, 2 * LANES), axis=1, keepdims=True)
    rest = is_e & (lane != i1)
    p2 = jnp.max(jnp.where(rest, pe, -1.0), axis=1, keepdims=True)
    i2 = jnp.min(jnp.where(rest & (pe == p2), lane, 2 * LANES), axis=1, keepdims=True)
    den = p1 + p2
    gt_ref[...] = jnp.where(lane == 0, g_top * p1 / den, jnp.where(lane == 1, g_top * p2 / den, 0.0))

    e1, e2 = i1 - N_GROUPS, i2 - N_GROUPS
    tm = logits.shape[0]
    oh1 = jnp.where(lane == e1, 1.0, 0.0)
    oh2 = jnp.where(lane == e2, 1.0, 0.0)
    earlier = jnp.where(lax.broadcasted_iota(I32, (tm, tm), 1) < lax.broadcasted_iota(I32, (tm, tm), 0),
                        1.0, 0.0).astype(BF16)
    before1 = jnp.dot(earlier, oh1.astype(BF16), preferred_element_type=F32)
    before2 = jnp.dot(earlier, oh2.astype(BF16), preferred_element_type=F32)
    tot1 = jnp.sum(oh1, axis=0, keepdims=True)
    base = cnt_ref[...]
    r1 = jnp.sum(oh1 * (base + before1), axis=1, keepdims=True)
    r2 = jnp.sum(oh2 * (base + tot1 + before2), axis=1, keepdims=True)
    cnt_ref[...] = base + tot1 + jnp.sum(oh2, axis=0, keepdims=True)
    ei_ref[...] = jnp.where(lane == 0, e1, jnp.where(lane == 1, e2, jnp.where(
        lane == 2, r1.astype(I32), jnp.where(lane == 3, r2.astype(I32), 0))))


def _merge_ln1(yact, o, gc, ga, x, wc, wa, wo, g1, b1, wr, cnt_in, tm, xt_all=None, m_total=None):
    m = x.shape[0]
    aliased = xt_all is not None
    n_tiles = m // tm
    total = xt_all.shape[0] if aliased else m_total
    blk0 = (total - m) // tm if aliased else 0
    n_steps = n_tiles if aliased else pl.cdiv(total, tm)
    row = lambda w: pl.BlockSpec((tm, w), lambda i: (jnp.minimum(i, n_tiles - 1), 0))
    res = lambda a: pl.BlockSpec(a.shape, lambda i: (0,) * a.ndim, pipeline_mode=pl.Buffered(1))
    cnt_spec = pl.BlockSpec((1, LANES), lambda i: (0, 0))
    args = [yact, o, gc, ga, x, wc, wa, wo, g1, b1, wr, cnt_in]
    in_specs = [row(C_CONV), row(N_HEADS * HEAD_DIM), row(D_MODEL), row(D_MODEL), row(D_MODEL),
                res(wc), res(wa), res(wo), res(g1), res(b1), res(wr), cnt_spec]
    if aliased:
        args.append(xt_all)
        in_specs.append(pl.BlockSpec(memory_space=pl.ANY))
    return pl.pallas_call(
        functools.partial(_merge_ln1_kernel, aliased=aliased, n_tiles=n_tiles),
        grid=(n_steps,),
        in_specs=in_specs,
        out_specs=(pl.BlockSpec((tm, N_CT, LANES), lambda i: (blk0 + i, 0, 0)), row(D_MODEL), row(LANES),
                   row(LANES), cnt_spec),
        out_shape=(jax.ShapeDtypeStruct((total, N_CT, LANES), BF16),
                   jax.ShapeDtypeStruct((m, D_MODEL), F32),
                   jax.ShapeDtypeStruct((m, LANES), I32),
                   jax.ShapeDtypeStruct((m, LANES), F32),
                   jax.ShapeDtypeStruct((1, LANES), F32)),
        input_output_aliases={len(args) - 1: 0} if aliased else {},
        compiler_params=_cparams(("arbitrary",)),
        name="merge_ln1",
    )(*args)


N_GSLOT = 3


def _experts_kernel(fb_ref, dest_ref, x_hbm, wg_ref, wu_ref, wd_ref, y_hbm,
                    rowtok, xbuf, ybuf, wgb, wub, wdb, gsem, ysem, *, n_blocks):
    e = pl.program_id(0)
    n_used = fb_ref[N_EXPERTS]
    last = jnp.maximum(n_used - 1, 0)

    def start_gather(blk, sl):
        base = blk * MOE_BLK
        for i in range(MOE_BLK):
            pltpu.make_async_copy(x_hbm.at[rowtok[base + i]], xbuf.at[sl, i], gsem.at[sl]).start()

    def wait_gather(sl):
        pltpu.make_async_copy(xbuf.at[sl], xbuf.at[sl], gsem.at[sl]).wait()

    def write_copy(blk, sl):
        r0 = pl.multiple_of(blk * MOE_BLK, MOE_BLK)
        return pltpu.make_async_copy(ybuf.at[sl], y_hbm.at[pl.ds(r0, MOE_BLK)], ysem.at[sl])

    @pl.when(e == 0)
    def _():
        def clear(r, carry):
            rowtok[r] = 0
            return carry
        lax.fori_loop(0, n_blocks * MOE_BLK, clear, 0, unroll=8)

        def place(a, carry):
            rowtok[dest_ref[a]] = lax.shift_right_logical(a, 1)
            return carry
        lax.fori_loop(0, dest_ref.shape[0], place, 0, unroll=8)
        start_gather(0, 0)
        start_gather(jnp.minimum(1, last), 1)

    b0, b1 = fb_ref[e], fb_ref[e + 1]

    @pl.when(b1 > b0)
    def _():
        wgb[...] = wg_ref[0].astype(BF16)
        wub[...] = wu_ref[0].astype(BF16)
        wdb[...] = wd_ref[0].astype(BF16)

    def block(b, carry):
        sl = lax.rem(b, N_GSLOT)
        ysl = lax.rem(b, 2)

        @pl.when(b >= 2)
        def _():
            write_copy(b - 2, ysl).wait()

        wait_gather(sl)
        start_gather(jnp.minimum(b + 2, last), lax.rem(b + 2, N_GSLOT))
        xb = _from_token_major(xbuf[sl], _tile_perm(False)).astype(BF16)
        hg = jnp.dot(xb, wgb[...], preferred_element_type=F32)
        hu = jnp.dot(xb, wub[...], preferred_element_type=F32)
        h = (hg * jax.nn.sigmoid(hg) * hu).astype(BF16)
        y = jnp.dot(h, wdb[...], preferred_element_type=F32)
        ybuf[ysl] = _to_token_major(y, _tile_perm(True))
        write_copy(b, ysl).start()
        return carry

    lax.fori_loop(b0, b1, block, 0)

    @pl.when(e == pl.num_programs(0) - 1)
    def _():
        wait_gather(lax.rem(last + 1, N_GSLOT))
        wait_gather(lax.rem(last + 2, N_GSLOT))

        @pl.when(n_used >= 1)
        def _():
            write_copy(last, lax.rem(last, 2)).wait()

        @pl.when(n_used >= 2)
        def _():
            write_copy(last - 1, lax.rem(last - 1, 2)).wait()

        ybuf[0] = jnp.zeros(ybuf.shape[1:], BF16)

        def fill(blk, carry):
            write_copy(blk, 0).start()
            return carry
        lax.fori_loop(n_used, n_blocks, fill, 0)

        def fill_wait(blk, carry):
            write_copy(blk, 0).wait()
            return carry
        lax.fori_loop(n_used, n_blocks, fill_wait, 0)


def _experts(first_blk, dest, xt_all, wg, wu, wd):
    n_blocks = (dest.shape[0] + N_EXPERTS * (MOE_BLK - 1) + MOE_BLK - 1) // MOE_BLK
    n_rows = n_blocks * MOE_BLK
    grid_spec = pltpu.PrefetchScalarGridSpec(
        num_scalar_prefetch=2,
        grid=(N_EXPERTS,),
        in_specs=[pl.BlockSpec(memory_space=pl.ANY),
                  pl.BlockSpec((1, D_MODEL, D_EXPERT), lambda e, fb, de: (e, 0, 0)),
                  pl.BlockSpec((1, D_MODEL, D_EXPERT), lambda e, fb, de: (e, 0, 0)),
                  pl.BlockSpec((1, D_EXPERT, D_MODEL), lambda e, fb, de: (e, 0, 0))],
        out_specs=pl.BlockSpec(memory_space=pl.ANY),
        scratch_shapes=[pltpu.SMEM((n_rows,), I32),
                        pltpu.VMEM((N_GSLOT, MOE_BLK, N_CT, LANES), BF16),
                        pltpu.VMEM((2, MOE_BLK, N_CT, LANES), BF16),
                        pltpu.VMEM((D_MODEL, D_EXPERT), BF16),
                        pltpu.VMEM((D_MODEL, D_EXPERT), BF16),
                        pltpu.VMEM((D_EXPERT, D_MODEL), BF16),
                        pltpu.SemaphoreType.DMA((N_GSLOT,)),
                        pltpu.SemaphoreType.DMA((2,))],
    )
    return pl.pallas_call(
        functools.partial(_experts_kernel, n_blocks=n_blocks),
        grid_spec=grid_spec,
        out_shape=jax.ShapeDtypeStruct((n_rows, N_CT, LANES), BF16),
        compiler_params=_cparams(("arbitrary",)),
        name="experts",
    )(first_blk, dest, xt_all, wg, wu, wd)


def _combine_ln2_kernel(dest_ref, x1_ref, gt_ref, yb_hbm, g_ref, b_ref, y_ref, ybuf, sem, *, tm, tok0):
    i = pl.program_id(0)
    last = pl.num_programs(0) - 1
    slot = lax.rem(i, 2)

    def row_copy(a, sl, k, r):
        return pltpu.make_async_copy(yb_hbm.at[dest_ref[a]], ybuf.at[sl, k, r], sem.at[sl])

    def start_gather(tile, sl, unrolled):
        base = 2 * (tok0 + tile * tm)
        if unrolled:
            for r in range(tm):
                row_copy(base + 2 * r, sl, 0, r).start()
                row_copy(base + 2 * r + 1, sl, 1, r).start()
        else:
            def body(r, carry):
                row_copy(base + 2 * r, sl, 0, r).start()
                row_copy(base + 2 * r + 1, sl, 1, r).start()
                return carry
            lax.fori_loop(0, tm, body, 0)

    def wait_gather(sl):
        pltpu.make_async_copy(ybuf.at[sl], ybuf.at[sl], sem.at[sl]).wait()

    @pl.when(i == 0)
    def _():
        start_gather(0, 0, unrolled=False)

    wait_gather(slot)
    start_gather(jnp.minimum(i + 1, last), 1 - slot, unrolled=True)
    gt = gt_ref[...]
    perm = _tile_perm(False)
    z = (DEEPNORM_ALPHA * x1_ref[...] + gt[:, 0:1] * _from_token_major(ybuf[slot, 0], perm)
         + gt[:, 1:2] * _from_token_major(ybuf[slot, 1], perm))
    y_ref[...] = _layer_norm(z, g_ref[...], b_ref[...])

    @pl.when(i == last)
    def _():
        wait_gather(1 - slot)


def _combine_ln2(dest, x1, gt, yb, g2, b2, tm, tok0):
    m = x1.shape[0]
    grid_spec = pltpu.PrefetchScalarGridSpec(
        num_scalar_prefetch=1,
        grid=(m // tm,),
        in_specs=[pl.BlockSpec((tm, D_MODEL), lambda i, d: (i, 0)),
                  pl.BlockSpec((tm, LANES), lambda i, d: (i, 0)),
                  pl.BlockSpec(memory_space=pl.ANY),
                  pl.BlockSpec((1, D_MODEL), lambda i, d: (0, 0)),
                  pl.BlockSpec((1, D_MODEL), lambda i, d: (0, 0))],
        out_specs=pl.BlockSpec((tm, D_MODEL), lambda i, d: (i, 0)),
        scratch_shapes=[pltpu.VMEM((2, 2, tm, N_CT, LANES), BF16),
                        pltpu.SemaphoreType.DMA((2,))],
    )
    return pl.pallas_call(
        functools.partial(_combine_ln2_kernel, tm=tm, tok0=tok0),
        grid_spec=grid_spec,
        out_shape=jax.ShapeDtypeStruct((m, D_MODEL), F32),
        compiler_params=_cparams(("arbitrary",)),
        name="combine_ln2",
    )(dest, x1, gt, yb, g2, b2)


def _dispatch_tables(eid, rank, counts):
    padded = (counts + MOE_BLK - 1) // MOE_BLK * MOE_BLK
    pad_end = jnp.cumsum(padded)
    pad_start = pad_end - padded
    start_of = jnp.sum(jnp.where(eid[..., None] == jnp.arange(N_EXPERTS, dtype=I32), pad_start, 0), axis=-1)
    dest = (start_of + rank).reshape(-1).astype(I32)
    first_blk = (jnp.concatenate([pad_start, pad_end[-1:]]) // MOE_BLK).astype(I32)
    return dest, first_blk


def kernel(x_prompt, x_sample, cache_k, cache_v, cache_kidx, state_conv, page_table, w_in, conv_dw_w, conv_dw_b,
           conv_ln_g, conv_ln_b, w_conv_out, w_attn_out, w_out, ln1_g, ln1_b, w_router_group, w_router_expert,
           w_expert_gate, w_expert_up, w_expert_down, ln2_g, ln2_b):
    assert w_in.shape[0] == 1, "single-layer trunk"
    nb, t, d = x_prompt.shape
    ns = x_sample.shape[0]
    n_pool = cache_k.shape[1]
    mp = nb * t
    row2 = lambda a: a.reshape(1, -1)

    w_arr = _arrange_w_in(w_in[0])
    dww, dwb, lng, lnb = conv_dw_w[0], row2(conv_dw_b[0]), row2(conv_ln_g[0]), row2(conv_ln_b[0])
    wc, wa, wo = w_conv_out[0].astype(BF16), w_attn_out[0].astype(BF16), w_out[0].astype(BF16)
    g1, b1, g2, b2 = row2(ln1_g[0]), row2(ln1_b[0]), row2(ln2_g[0]), row2(ln2_b[0])
    wr32 = jnp.concatenate([w_router_group[0], w_router_expert[0],
                            jnp.zeros((d, LANES - N_GROUPS - N_EXPERTS), F32)], axis=1)
    wr_hi = wr32.astype(BF16)
    wr = jnp.stack([wr_hi, (wr32 - wr_hi.astype(F32)).astype(BF16)])

    xp = x_prompt.reshape(mp, d)
    u_p, q_p, qi_p, kv_p, kvb_p, gc_p, ga_p, kiwi_p, kib_p = _in_proj(xp, w_arr, tm=512)
    yact_p = _conv_prompt(u_p.reshape(nb, t, C_CONV), dww, dwb, lng, lnb)
    o_p = _attn_prompt(qi_p.reshape(nb, t, -1), kiwi_p.reshape(nb, t, LANES), kib_p.reshape(nb, t, LANES),
                       q_p.reshape(nb, t, -1), kvb_p.reshape(nb, t, 2 * KV_DIM))
    xt_all, x1_p, ei_p, gt_p, cnt_p = _merge_ln1(yact_p.reshape(mp, C_CONV), o_p.reshape(mp, -1), gc_p, ga_p, xp,
                                                 wc, wa, wo, g1, b1, wr, jnp.zeros((1, LANES), F32), tm=256,
                                                 m_total=mp + ns)

    xs = x_sample.reshape(ns, d)
    u_s, q_s, qi_s, kv_s, _, gc_s, ga_s, kiwi_s, _ = _in_proj(xs, w_arr, tm=ns)
    yact_s = _conv_sample(state_conv[0].transpose(1, 0, 2), u_s, dww, dwb, lng, lnb)
    cki2 = cache_kidx.transpose(0, 1, 3, 2).reshape(n_pool, IDX_DIM, PAGE_SIZE)
    ck2 = cache_k.transpose(0, 1, 3, 4, 2).reshape(n_pool, KV_DIM, PAGE_SIZE)
    cv2 = cache_v.transpose(0, 1, 3, 4, 2).reshape(n_pool, KV_DIM, PAGE_SIZE)
    scores_s = _sample_scores(page_table, qi_s.reshape(ns, IDX_HEADS, IDX_DIM),
                              kiwi_s[:, IDX_DIM:IDX_DIM + IDX_HEADS].reshape(ns, IDX_HEADS, 1),
                              kiwi_s[:, :IDX_DIM].reshape(ns, 1, IDX_DIM), cki2)
    scores3 = scores_s.reshape(ns, -1, ATT_CK).transpose(1, 0, 2)
    own_group = (np.arange(N_HEADS)[:, None] // HEADS_PER_KV) == (np.arange(KV_DIM)[None, :] // HEAD_DIM)
    qbd = jnp.where(own_group[None], jnp.tile(q_s.reshape(ns, N_HEADS, HEAD_DIM), (1, 1, KV_HEADS)), 0).astype(BF16)
    o3 = _sample_attn(page_table, scores3, qbd, kv_s[:, :KV_DIM].reshape(ns, 1, KV_DIM),
                      kv_s[:, KV_DIM:].reshape(ns, 1, KV_DIM), ck2, cv2)
    o_s = o3.reshape(ns, N_HEADS * HEAD_DIM).astype(BF16)
    xt_all, x1_s, ei_s, gt_s, cnt = _merge_ln1(yact_s, o_s, gc_s, ga_s, xs, wc, wa, wo, g1, b1, wr, cnt_p, tm=ns,
                                               xt_all=xt_all)

    eid_rank = jnp.concatenate([ei_p[:, :4], ei_s[:, :4]], axis=0)
    dest, first_blk = _dispatch_tables(eid_rank[:, :2], eid_rank[:, 2:], cnt[0, :N_EXPERTS].astype(I32))
    yb = _experts(first_blk, dest, xt_all, w_expert_gate[0], w_expert_up[0], w_expert_down[0])
    y_p = _combine_ln2(dest, x1_p, gt_p, yb, g2, b2, tm=256, tok0=0)
    y_s = _combine_ln2(dest, x1_s, gt_s, yb, g2, b2, tm=ns, tok0=mp)

    y_prompt = y_p.reshape(nb, t, d)
    y_sample = y_s.reshape(ns, 1, d)
    k_prompt = kv_p[:, :KV_DIM].reshape(1, nb, t, KV_HEADS, HEAD_DIM)
    v_prompt = kv_p[:, KV_DIM:].reshape(1, nb, t, KV_HEADS, HEAD_DIM)
    kidx_prompt = kiwi_p[:, :IDX_DIM].reshape(1, nb, t, IDX_DIM)
    conv_prompt = u_p.reshape(nb, t, C_CONV)[:, t - (CONV_W - 1):][None]
    k_sample = kv_s[:, :KV_DIM].reshape(1, ns, 1, KV_HEADS, HEAD_DIM)
    v_sample = kv_s[:, KV_DIM:].reshape(1, ns, 1, KV_HEADS, HEAD_DIM)
    kidx_sample = kiwi_s[:, :IDX_DIM].reshape(1, ns, 1, IDX_DIM)
    conv_sample = jnp.concatenate([state_conv[0][:, 1:], u_s[:, None, :]], axis=1)[None]
    return (y_prompt, y_sample, k_prompt, v_prompt, kidx_prompt, conv_prompt, k_sample, v_sample, kidx_sample,
            conv_sample)
```

```python
import functools

import jax
import jax.numpy as jnp
import numpy as np
from jax import lax
from jax.experimental import pallas as pl
from jax.experimental.pallas import tpu as pltpu

F32 = jnp.float32
BF16 = jnp.bfloat16
I32 = jnp.int32

D_MODEL = 2048
C_CONV = 1024
CONV_W = 31
N_HEADS = 16
HEAD_DIM = 64
KV_HEADS = 4
HEADS_PER_KV = N_HEADS // KV_HEADS
KV_DIM = KV_HEADS * HEAD_DIM
IDX_HEADS = 16
IDX_DIM = 64
MAX_TOPK = 256
N_GROUPS = 8
EXPERTS_PER_GROUP = 8
N_EXPERTS = 64
D_EXPERT = 512
MOE_BLK = 128
PAGE_SIZE = 128
LN_EPS = 1e-5
DEEPNORM_ALPHA = 2.0 ** 0.25
NEG_INF = -1e30
INT_MIN = -(2 ** 31)
NEG_KEY = int(INT_MIN - int(np.array(NEG_INF, np.float32).view(np.int32)))
LOG2E = 1.4426950408889634
Q_SCALE = HEAD_DIM ** -0.5 * LOG2E
IDX_SCALE = IDX_HEADS ** -0.5 * IDX_DIM ** -0.5

LANES = 128
HALF = LANES // 2
IN_TN = 1024
VMEM_LIMIT = 56 * 1024 * 1024


def _cparams(sem, vmem=VMEM_LIMIT):
    return pltpu.CompilerParams(dimension_semantics=sem, vmem_limit_bytes=vmem)


def _float_key(x):
    b = pltpu.bitcast(x, I32)
    return jnp.where(b < 0, INT_MIN - b, b)


def _layer_norm(z, g, b):
    mu = jnp.mean(z, axis=-1, keepdims=True)
    zc = z - mu
    var = jnp.mean(zc * zc, axis=-1, keepdims=True)
    return zc * lax.rsqrt(var + LN_EPS) * g + b


def _dot_nt(a, b):
    return lax.dot_general(a, b, (((1,), (1,)), ((), ())), preferred_element_type=F32)


def _alibi_slope(h):
    return float(2.0 ** (-8.0 * (h + 1) / N_HEADS))


_T_GLU, _T_Q, _T_QI, _T_KV, _T_GC, _T_GA, _N_TILES = 0, 2, 3, 4, 5, 7, 9
KV_TILE_COLS = 2 * KV_DIM + 2 * LANES


def _arrange_w_in(w):
    sizes = (C_CONV, C_CONV, N_HEADS * HEAD_DIM, KV_DIM, KV_DIM,
             IDX_HEADS * IDX_DIM, IDX_DIM, IDX_HEADS, D_MODEL, D_MODEL)
    offs = np.cumsum((0,) + sizes)
    a, g, q, k, v, qi, ki, wi, gc, ga = [w[:, offs[i]:offs[i + 1]] for i in range(10)]
    half = IN_TN // 2
    glu = [jnp.concatenate([a[:, t * half:(t + 1) * half], g[:, t * half:(t + 1) * half]], axis=1)
           for t in range(C_CONV // half)]
    zeros = lambda n: jnp.zeros((w.shape[0], n), w.dtype)
    kv_tile = jnp.concatenate([k, v, ki, wi, zeros(LANES - IDX_DIM - IDX_HEADS), ki, ki,
                               zeros(IN_TN - KV_TILE_COLS)], axis=1)
    return jnp.concatenate(glu + [q, qi, kv_tile, gc, ga], axis=1).astype(BF16)


def _in_proj_kernel(x_ref, w_ref, u_ref, q_ref, qi_ref, kv_ref, kvb_ref, gc_ref, ga_ref, kiwi_ref, kib_ref,
                    xb_ref):
    j = pl.program_id(1)

    @pl.when(j == 0)
    def _():
        xb_ref[...] = x_ref[...].astype(BF16)

    def mm(ncols=IN_TN):
        return jnp.dot(xb_ref[...], w_ref[:, :ncols], preferred_element_type=F32)

    @pl.when(j < _T_Q)
    def _():
        acc = mm()
        u_ref[...] = acc[:, :IN_TN // 2] * jax.nn.sigmoid(acc[:, IN_TN // 2:])

    @pl.when(j == _T_Q)
    def _():
        q_ref[...] = (mm() * Q_SCALE).astype(BF16)

    @pl.when(j == _T_QI)
    def _():
        qi_ref[...] = mm().astype(BF16)

    @pl.when(j == _T_KV)
    def _():
        acc = mm(KV_TILE_COLS)
        kv_ref[...] = acc[:, :2 * KV_DIM]
        kvb_ref[...] = acc[:, :2 * KV_DIM].astype(BF16)
        kiwi_ref[...] = acc[:, 2 * KV_DIM:2 * KV_DIM + LANES]
        kib_ref[...] = acc[:, 2 * KV_DIM + LANES:].astype(BF16)

    @pl.when((j >= _T_GC) & (j < _T_GA))
    def _():
        gc_ref[...] = jax.nn.sigmoid(mm()).astype(BF16)

    @pl.when(j >= _T_GA)
    def _():
        ga_ref[...] = jax.nn.sigmoid(mm()).astype(BF16)


def _in_proj(x, w_arr, tm):
    m = x.shape[0]
    half = IN_TN // 2

    def cl(lo, n):
        return lambda i, j: (i, jnp.clip(j - lo, 0, n - 1))

    out_shape = (
        jax.ShapeDtypeStruct((m, C_CONV), F32),
        jax.ShapeDtypeStruct((m, N_HEADS * HEAD_DIM), BF16),
        jax.ShapeDtypeStruct((m, IDX_HEADS * IDX_DIM), BF16),
        jax.ShapeDtypeStruct((m, 2 * KV_DIM), F32),
        jax.ShapeDtypeStruct((m, 2 * KV_DIM), BF16),
        jax.ShapeDtypeStruct((m, D_MODEL), BF16),
        jax.ShapeDtypeStruct((m, D_MODEL), BF16),
        jax.ShapeDtypeStruct((m, LANES), F32),
        jax.ShapeDtypeStruct((m, LANES), BF16),
    )
    out_specs = (
        pl.BlockSpec((tm, half), cl(_T_GLU, 2)),
        pl.BlockSpec((tm, IN_TN), lambda i, j: (i, 0)),
        pl.BlockSpec((tm, IN_TN), lambda i, j: (i, 0)),
        pl.BlockSpec((tm, 2 * KV_DIM), lambda i, j: (i, 0)),
        pl.BlockSpec((tm, 2 * KV_DIM), lambda i, j: (i, 0)),
        pl.BlockSpec((tm, IN_TN), cl(_T_GC, 2)),
        pl.BlockSpec((tm, IN_TN), cl(_T_GA, 2)),
        pl.BlockSpec((tm, LANES), lambda i, j: (i, 0)),
        pl.BlockSpec((tm, LANES), lambda i, j: (i, 0)),
    )
    return pl.pallas_call(
        _in_proj_kernel,
        grid=(m // tm, _N_TILES),
        in_specs=[pl.BlockSpec((tm, D_MODEL), lambda i, j: (i, 0)),
                  pl.BlockSpec((D_MODEL, IN_TN), lambda i, j: (0, j))],
        out_specs=out_specs,
        out_shape=out_shape,
        scratch_shapes=[pltpu.VMEM((tm, D_MODEL), BF16)],
        compiler_params=_cparams(("arbitrary", "arbitrary")),
        name="in_proj",
    )(x, w_arr)


CONV_HALO = 32
CONV_STRIP = 32


SUBLANES = 8


def _conv_prompt_kernel(ucur_ref, uprev_ref, dww_ref, dwb_ref, lng_ref, lnb_ref, y_ref, ext_ref, sh_ref, *, tt):
    tb = pl.program_id(1)
    ext_ref[0:CONV_HALO, :] = jnp.where(tb > 0, uprev_ref[0], 0.0)
    ext_ref[CONV_HALO:, :] = ucur_ref[0]
    first = CONV_HALO - (CONV_W - 1)
    for b in range(SUBLANES):
        rows = tt + SUBLANES * ((CONV_W - 1 - b) // SUBLANES)
        sh_ref[b, 0:rows, :] = ext_ref[first + b:first + b + rows, :]
    dwb, lng, lnb = dwb_ref[...], lng_ref[...], lnb_ref[...]
    for r in range(tt // CONV_STRIP):
        acc = jnp.broadcast_to(dwb, (CONV_STRIP, C_CONV))
        for j in range(CONV_W):
            a, b = divmod(j, SUBLANES)
            r0 = r * CONV_STRIP + a * SUBLANES
            acc = acc + dww_ref[j:j + 1, :] * sh_ref[b, r0:r0 + CONV_STRIP, :]
        yn = _layer_norm(acc, lng, lnb)
        y_ref[0, r * CONV_STRIP:(r + 1) * CONV_STRIP, :] = (yn * jax.nn.sigmoid(yn)).astype(BF16)


def _conv_prompt(u, dww, dwb, lng, lnb, tt=128):
    n, t, c = u.shape
    per = tt // CONV_HALO
    vec = pl.BlockSpec((1, c), lambda b, i: (0, 0))
    return pl.pallas_call(
        functools.partial(_conv_prompt_kernel, tt=tt),
        grid=(n, t // tt),
        in_specs=[pl.BlockSpec((1, tt, c), lambda b, i: (b, i, 0)),
                  pl.BlockSpec((1, CONV_HALO, c), lambda b, i: (b, jnp.maximum(i * per - 1, 0), 0)),
                  pl.BlockSpec((CONV_W, c), lambda b, i: (0, 0)), vec, vec, vec],
        out_specs=pl.BlockSpec((1, tt, c), lambda b, i: (b, i, 0)),
        out_shape=jax.ShapeDtypeStruct((n, t, c), BF16),
        scratch_shapes=[pltpu.VMEM((tt + CONV_HALO, c), F32),
                        pltpu.VMEM((SUBLANES, tt + SUBLANES * ((CONV_W - 1) // SUBLANES), c), F32)],
        compiler_params=_cparams(("arbitrary", "arbitrary")),
        name="conv_prompt",
    )(u, u, dww, dwb, lng, lnb)


def _conv_sample_kernel(state_ref, u_ref, dww_ref, dwb_ref, lng_ref, lnb_ref, y_ref):
    acc = dwb_ref[...] + dww_ref[CONV_W - 1:CONV_W, :] * u_ref[...]
    for j in range(CONV_W - 1):
        acc = acc + dww_ref[j:j + 1, :] * state_ref[j]
    yn = _layer_norm(acc, lng_ref[...], lnb_ref[...])
    y_ref[...] = (yn * jax.nn.sigmoid(yn)).astype(BF16)


def _conv_sample(state_t, u, dww, dwb, lng, lnb):
    n = u.shape[0]
    return pl.pallas_call(
        _conv_sample_kernel,
        out_shape=jax.ShapeDtypeStruct((n, C_CONV), BF16),
        compiler_params=pltpu.CompilerParams(vmem_limit_bytes=VMEM_LIMIT),
        name="conv_sample",
    )(state_t, u, dww, dwb, lng, lnb)


ATT_TQ = 256
ATT_CK = 256


def _count_ge(key_ref, nch, cand):
    def body(c, acc):
        m = jnp.where(key_ref[c] >= cand, 1.0, 0.0)
        return acc + m[:, :LANES] + m[:, LANES:]
    acc = lax.fori_loop(0, nch, body, jnp.zeros((cand.shape[0], LANES), F32))
    return jnp.sum(acc, axis=1, keepdims=True)


def _select_threshold(key_ref, nch, rows):
    def body(i, prefix):
        cand = prefix + jnp.left_shift(jnp.int32(1), 31 - i)
        return jnp.where(_count_ge(key_ref, nch, cand) >= float(MAX_TOPK), cand, prefix)
    return lax.fori_loop(0, 32, body, jnp.full((rows, 1), INT_MIN, I32))


def _selection_bias(key_ref, bias_ref, nch):
    rows, ck = key_ref.shape[1:]
    thr = _select_threshold(key_ref, nch, rows)
    n_tie_take = float(MAX_TOPK) - _count_ge(key_ref, nch, thr + 1)
    tri = jnp.where(lax.broadcasted_iota(I32, (ck, ck), 0) <= lax.broadcasted_iota(I32, (ck, ck), 1),
                    1.0, 0.0).astype(BF16)

    def bias_chunk(c, seen):
        key = key_ref[c]
        eq = key == thr
        eqf = jnp.where(eq, 1.0, 0.0)
        incl = jnp.dot(eqf.astype(BF16), tri, preferred_element_type=F32)
        sel = (key > thr) | (eq & (seen + incl - eqf < n_tie_take))
        bias_ref[c] = jnp.where(sel & (key > NEG_KEY), 0.0, NEG_INF)
        return seen + incl[:, ck - 1:ck]

    lax.fori_loop(0, nch, bias_chunk, jnp.zeros((rows, 1), F32))


def _count_ge_t(key_ref, nch, cand):
    ck, nq = key_ref.shape[1:]

    def body(c, acc):
        m = jnp.where(key_ref[c] >= cand, 1.0, 0.0)
        return acc + jnp.sum(m.reshape(ck // SUBLANES, SUBLANES, nq), axis=0)

    acc = lax.fori_loop(0, nch, body, jnp.zeros((SUBLANES, nq), F32))
    return jnp.sum(acc, axis=0, keepdims=True)


def _selection_bias_t(key_ref, bias_ref, nch):
    ck, nq = key_ref.shape[1:]

    def bit(i, prefix):
        cand = prefix + jnp.left_shift(jnp.int32(1), 31 - i)
        return jnp.where(_count_ge_t(key_ref, nch, cand) >= float(MAX_TOPK), cand, prefix)

    thr = lax.fori_loop(0, 32, bit, jnp.full((1, nq), INT_MIN, I32))
    n_tie_take = float(MAX_TOPK) - _count_ge_t(key_ref, nch, thr + 1)
    tri = jnp.where(lax.broadcasted_iota(I32, (ck, ck), 1) <= lax.broadcasted_iota(I32, (ck, ck), 0),
                    1.0, 0.0).astype(BF16)

    def bias_chunk(c, seen):
        key = key_ref[c]
        eq = key == thr
        eqf = jnp.where(eq, 1.0, 0.0)
        incl = jnp.dot(tri, eqf.astype(BF16), preferred_element_type=F32)
        sel = (key > thr) | (eq & (seen + incl - eqf < n_tie_take))
        bias_ref[c] = jnp.where(sel & (key > NEG_KEY), 0.0, NEG_INF).T
        return seen + incl[ck - 1:ck, :]

    lax.fori_loop(0, nch, bias_chunk, jnp.zeros((1, nq), F32))


def _swap_halves(x):
    return jnp.concatenate([x[:, HALF:], x[:, :HALF]], axis=1)


def _attn_prompt_kernel(qi_ref, kiwi_ref, kib_ref, q_ref, kvb_ref, o_ref,
                        key_ref, bias_ref, m_ref, ls_ref, acc_ref):
    tq, ck = ATT_TQ, ATT_CK
    qb = pl.program_id(1)
    nch = qb + 1
    lo_k = lax.broadcasted_iota(I32, (ck, LANES), 1) < HALF
    lo_q = lax.broadcasted_iota(I32, (tq, LANES), 1) < HALF

    key_pos = lax.broadcasted_iota(I32, (ck, tq), 0)
    q_pos = qb * tq + lax.broadcasted_iota(I32, (ck, tq), 1)
    w_t = jnp.transpose(kiwi_ref[0])[IDX_DIM:IDX_DIM + IDX_HEADS, :] * IDX_SCALE

    def score_chunk(c, carry):
        k0 = pl.multiple_of(c * ck, ck)
        kk = kib_ref[0, pl.ds(k0, ck), :]
        zero = jnp.zeros_like(kk)
        k_lo, k_hi = jnp.where(lo_k, kk, zero), jnp.where(lo_k, zero, kk)
        acc = jnp.zeros((ck, tq), F32)
        for p in range(IDX_HEADS // 2):
            qp = qi_ref[0, :, p * LANES:(p + 1) * LANES]
            acc = acc + w_t[2 * p:2 * p + 1, :] * jnp.maximum(_dot_nt(k_lo, qp), 0.0)
            acc = acc + w_t[2 * p + 1:2 * p + 2, :] * jnp.maximum(_dot_nt(k_hi, qp), 0.0)
        acc = jnp.where(c * ck + key_pos <= q_pos, acc, NEG_INF)
        key_ref[c] = _float_key(acc)
        return carry

    lax.fori_loop(0, nch, score_chunk, 0)
    _selection_bias_t(key_ref, bias_ref, nch)

    m_ref[...] = jnp.full(m_ref.shape, -jnp.inf, F32)
    ls_ref[...] = jnp.zeros(ls_ref.shape, F32)
    acc_ref[...] = jnp.zeros(acc_ref.shape, F32)

    def attn_chunk(c, carry):
        k0 = pl.multiple_of(c * ck, ck)
        kv = kvb_ref[0, pl.ds(k0, ck), :]
        bias = bias_ref[c]
        rel = (c * ck - (qb + 1) * tq + 1 + lax.broadcasted_iota(I32, (1, ck), 1)).astype(F32)
        for j in range(KV_HEADS // 2):
            kt = kv[:, j * LANES:(j + 1) * LANES]
            vt = kv[:, KV_DIM + j * LANES:KV_DIM + (j + 1) * LANES]
            kts, vts = _swap_halves(kt), _swap_halves(vt)
            zero, one = jnp.zeros_like(kt), jnp.ones_like(kt)
            for gg in range(2):
                g = 2 * j + gg
                k_own, k_swp = (kt, kts) if gg == 0 else (kts, kt)
                v_own, v_swp = (vt, vts) if gg == 0 else (vts, vt)
                k_lo, k_hi = jnp.where(lo_k, k_own, zero), jnp.where(lo_k, zero, k_swp)
                v_lo, v_hi = jnp.where(lo_k, v_own, one), jnp.where(lo_k, one, v_swp)
                for mm in range(HEADS_PER_KV // 2):
                    pidx = (HEADS_PER_KV // 2) * g + mm
                    qp = q_ref[0, :, pidx * LANES:(pidx + 1) * LANES]
                    pvs, alphas = [], []
                    for half, (kmat, vmat) in enumerate(((k_lo, v_lo), (k_hi, v_hi))):
                        h = 2 * pidx + half
                        s = _dot_nt(qp, kmat) + (_alibi_slope(h) * LOG2E) * rel + bias
                        m_old = m_ref[h]
                        m_new = jnp.maximum(m_old, jnp.max(s, axis=1, keepdims=True))
                        p = jnp.exp2(s - jnp.concatenate([m_new, m_new], axis=1))
                        alphas.append(jnp.exp2(m_old - m_new))
                        m_ref[h] = m_new
                        pvs.append(jnp.dot(p.astype(BF16), vmat, preferred_element_type=F32))
                    acc_ref[pidx] = (jnp.where(lo_q, alphas[0], alphas[1]) * acc_ref[pidx]
                                     + jnp.where(lo_q, pvs[0], pvs[1]))
                    ls_ref[pidx] = (jnp.where(lo_q, alphas[1], alphas[0]) * ls_ref[pidx]
                                    + jnp.where(lo_q, pvs[1], pvs[0]))
        return carry

    lax.fori_loop(0, nch, attn_chunk, 0)
    for pidx in range(N_HEADS // 2):
        l_pair = pltpu.roll(ls_ref[pidx], HALF, 1)
        o_ref[0, :, pidx * LANES:(pidx + 1) * LANES] = (acc_ref[pidx] / l_pair).astype(BF16)


def _attn_prompt(qi, kiwi, kib, q, kvb):
    n, t, _ = q.shape
    tq = ATT_TQ
    return pl.pallas_call(
        _attn_prompt_kernel,
        grid=(n, t // tq),
        in_specs=[pl.BlockSpec((1, tq, IDX_HEADS * IDX_DIM), lambda b, i: (b, i, 0)),
                  pl.BlockSpec((1, tq, LANES), lambda b, i: (b, i, 0)),
                  pl.BlockSpec((1, t, LANES), lambda b, i: (b, 0, 0)),
                  pl.BlockSpec((1, tq, N_HEADS * HEAD_DIM), lambda b, i: (b, i, 0)),
                  pl.BlockSpec((1, t, 2 * KV_DIM), lambda b, i: (b, 0, 0))],
        out_specs=pl.BlockSpec((1, tq, N_HEADS * HEAD_DIM), lambda b, i: (b, i, 0)),
        out_shape=jax.ShapeDtypeStruct((n, t, N_HEADS * HEAD_DIM), BF16),
        scratch_shapes=[pltpu.VMEM((t // ATT_CK, tq, ATT_CK), I32),
                        pltpu.VMEM((t // ATT_CK, tq, ATT_CK), F32),
                        pltpu.VMEM((N_HEADS, tq, LANES), F32),
                        pltpu.VMEM((N_HEADS // 2, tq, LANES), F32),
                        pltpu.VMEM((N_HEADS // 2, tq, LANES), F32)],
        compiler_params=_cparams(("arbitrary", "arbitrary")),
        name="attn_prompt",
    )(qi, kiwi, kib, q, kvb)


SAMPLE_KC = 2048


def _start_page_copies(pt_ref, src_hbm, dst, sem, base, n_pages):
    def body(p, carry):
        col = pl.multiple_of(p * PAGE_SIZE, PAGE_SIZE)
        pltpu.make_async_copy(src_hbm.at[pt_ref[base + p]], dst.at[:, pl.ds(col, PAGE_SIZE)], sem).start()
        return carry
    lax.fori_loop(0, n_pages, body, 0)


def _sample_score_kernel(pt_ref, qi_ref, wcol_ref, kinew_ref, cki_hbm, sc_ref, kibuf, sem, *, n_pages):
    b = pl.program_id(0)
    nb = pl.num_programs(0)
    past = n_pages * PAGE_SIZE
    slot = b % 2

    @pl.when(b == 0)
    def _():
        _start_page_copies(pt_ref, cki_hbm, kibuf.at[0], sem.at[0], 0, n_pages)

    @pl.when(b + 1 < nb)
    def _():
        _start_page_copies(pt_ref, cki_hbm, kibuf.at[1 - slot], sem.at[1 - slot], (b + 1) * n_pages, n_pages)

    pltpu.make_async_copy(kibuf.at[slot], kibuf.at[slot], sem.at[slot]).wait()

    qi = qi_ref[0]
    wsc = wcol_ref[0] * IDX_SCALE
    for c in range(past // SAMPLE_KC):
        kc = kibuf[slot, :, c * SAMPLE_KC:(c + 1) * SAMPLE_KC].astype(BF16)
        d = jnp.dot(qi, kc, preferred_element_type=F32)
        sc_ref[0, :, c * SAMPLE_KC:(c + 1) * SAMPLE_KC] = jnp.sum(wsc * jnp.maximum(d, 0.0), axis=0, keepdims=True)
    dn = jnp.sum(qi.astype(F32) * kinew_ref[0], axis=1, keepdims=True)
    s_new = jnp.sum(wsc * jnp.maximum(dn, 0.0), axis=0, keepdims=True)
    ln = lax.broadcasted_iota(I32, (1, ATT_CK), 1)
    sc_ref[0, :, past:] = jnp.where(ln == 0, s_new, NEG_INF)


def _sample_scores(page_table, qi3, wcol, kinew, cki2):
    n, n_pages = page_table.shape
    past = n_pages * PAGE_SIZE
    grid_spec = pltpu.PrefetchScalarGridSpec(
        num_scalar_prefetch=1,
        grid=(n,),
        in_specs=[pl.BlockSpec((1, IDX_HEADS, IDX_DIM), lambda b, pt: (b, 0, 0)),
                  pl.BlockSpec((1, IDX_HEADS, 1), lambda b, pt: (b, 0, 0)),
                  pl.BlockSpec((1, 1, IDX_DIM), lambda b, pt: (b, 0, 0)),
                  pl.BlockSpec(memory_space=pl.ANY)],
        out_specs=pl.BlockSpec((1, 1, past + ATT_CK), lambda b, pt: (b, 0, 0)),
        scratch_shapes=[pltpu.VMEM((2, IDX_DIM, past), F32),
                        pltpu.SemaphoreType.DMA((2,))],
    )
    return pl.pallas_call(
        functools.partial(_sample_score_kernel, n_pages=n_pages),
        grid_spec=grid_spec,
        out_shape=jax.ShapeDtypeStruct((n, 1, past + ATT_CK), F32),
        compiler_params=_cparams(("arbitrary",)),
        name="sample_scores",
    )(page_table.reshape(-1), qi3, wcol, kinew, cki2)


def _sample_attn_kernel(pt_ref, sc_ref, qbd_ref, knew_ref, vnew_ref, ck_hbm, cv_hbm, o_ref,
                        key_ref, bias_ref, kbuf, vbuf, semk, semv, *, n_pages):
    b = pl.program_id(0)
    nb = pl.num_programs(0)
    past = n_pages * PAGE_SIZE
    nch = sc_ref.shape[0]
    slot = b % 2

    def start_all(bb, sl):
        _start_page_copies(pt_ref, ck_hbm, kbuf.at[sl], semk.at[sl], bb * n_pages, n_pages)
        _start_page_copies(pt_ref, cv_hbm, vbuf.at[sl], semv.at[sl], bb * n_pages, n_pages)

    @pl.when(b == 0)
    def _():
        start_all(0, 0)
        key_ref[...] = _float_key(sc_ref[...])
        _selection_bias(key_ref, bias_ref, nch)

    @pl.when(b + 1 < nb)
    def _():
        start_all(b + 1, 1 - slot)

    pltpu.make_async_copy(kbuf.at[slot], kbuf.at[slot], semk.at[slot]).wait()
    pltpu.make_async_copy(vbuf.at[slot], vbuf.at[slot], semv.at[slot]).wait()

    qbd = qbd_ref[0]
    bias_row = jnp.concatenate([bias_ref[c, pl.ds(b, 1), :] for c in range(nch)], axis=1)
    s_parts = [jnp.dot(qbd, kbuf[slot, :, c * SAMPLE_KC:(c + 1) * SAMPLE_KC].astype(BF16),
                       preferred_element_type=F32) for c in range(past // SAMPLE_KC)]
    s_new = jnp.sum(qbd.astype(F32) * knew_ref[0], axis=1, keepdims=True)
    ln = lax.broadcasted_iota(I32, (N_HEADS, ATT_CK), 1)
    s = jnp.concatenate(s_parts + [jnp.where(ln == 0, s_new, 0.0)], axis=1)
    hrow = lax.broadcasted_iota(I32, (N_HEADS, 1), 0)
    slope2 = jnp.exp2(-8.0 * (hrow + 1).astype(F32) / N_HEADS) * LOG2E
    rel = (lax.broadcasted_iota(I32, (1, past + ATT_CK), 1) - past).astype(F32)
    s = s + slope2 * rel + bias_row
    m = jnp.max(s, axis=1, keepdims=True)
    p = jnp.exp2(s - m)
    l = jnp.sum(p, axis=1, keepdims=True)
    o_all = p[:, past:past + 1] * vnew_ref[0]
    for c in range(past // SAMPLE_KC):
        o_all = o_all + _dot_nt(p[:, c * SAMPLE_KC:(c + 1) * SAMPLE_KC].astype(BF16),
                                vbuf[slot, :, c * SAMPLE_KC:(c + 1) * SAMPLE_KC].astype(BF16))
    o_all = o_all / l
    out = jnp.zeros((N_HEADS, HEAD_DIM), F32)
    for g in range(KV_HEADS):
        own = (hrow >= g * HEADS_PER_KV) & (hrow < (g + 1) * HEADS_PER_KV)
        out = out + jnp.where(own, o_all[:, g * HEAD_DIM:(g + 1) * HEAD_DIM], 0.0)
    o_ref[0] = out


def _sample_attn(page_table, scores3, qbd, knew, vnew, ck2, cv2):
    n, n_pages = page_table.shape
    past = n_pages * PAGE_SIZE
    nch = scores3.shape[0]
    grid_spec = pltpu.PrefetchScalarGridSpec(
        num_scalar_prefetch=1,
        grid=(n,),
        in_specs=[pl.BlockSpec((nch, n, ATT_CK), lambda b, pt: (0, 0, 0)),
                  pl.BlockSpec((1, N_HEADS, KV_DIM), lambda b, pt: (b, 0, 0)),
                  pl.BlockSpec((1, 1, KV_DIM), lambda b, pt: (b, 0, 0)),
                  pl.BlockSpec((1, 1, KV_DIM), lambda b, pt: (b, 0, 0)),
                  pl.BlockSpec(memory_space=pl.ANY),
                  pl.BlockSpec(memory_space=pl.ANY)],
        out_specs=pl.BlockSpec((1, N_HEADS, HEAD_DIM), lambda b, pt: (b, 0, 0)),
        scratch_shapes=[pltpu.VMEM((nch, n, ATT_CK), I32),
                        pltpu.VMEM((nch, n, ATT_CK), F32),
                        pltpu.VMEM((2, KV_DIM, past), F32),
                        pltpu.VMEM((2, KV_DIM, past), F32),
                        pltpu.SemaphoreType.DMA((2,)),
                        pltpu.SemaphoreType.DMA((2,))],
    )
    return pl.pallas_call(
        functools.partial(_sample_attn_kernel, n_pages=n_pages),
        grid_spec=grid_spec,
        out_shape=jax.ShapeDtypeStruct((n, N_HEADS, HEAD_DIM), F32),
        compiler_params=_cparams(("arbitrary",)),
        name="sample_attn",
    )(page_table.reshape(-1), scores3, qbd, knew, vnew, ck2, cv2)


N_CT = D_MODEL // LANES
TOK_GRP = SUBLANES * N_CT


def _tile_perm(to_token_major):
    r = lax.broadcasted_iota(I32, (TOK_GRP, TOK_GRP), 1 if to_token_major else 0)
    k = lax.broadcasted_iota(I32, (TOK_GRP, TOK_GRP), 0 if to_token_major else 1)
    return jnp.where((r // SUBLANES == k % N_CT) & (r % SUBLANES == k // N_CT), 1.0, 0.0).astype(BF16)


def _to_token_major(x, perm):
    rows = x.shape[0]
    out = []
    for t in range(rows // SUBLANES):
        xs = x[t * SUBLANES:(t + 1) * SUBLANES]
        cm = jnp.concatenate([xs[:, c * LANES:(c + 1) * LANES] for c in range(N_CT)], axis=0).astype(BF16)
        out.append(jnp.dot(perm, cm, preferred_element_type=F32).astype(BF16))
    return jnp.stack(out).reshape(rows, N_CT, LANES)


def _from_token_major(xt, perm):
    rows = xt.shape[0]
    xg = xt.reshape(rows // SUBLANES, TOK_GRP, LANES)
    out = jnp.stack([jnp.dot(perm, xg[t], preferred_element_type=F32) for t in range(rows // SUBLANES)])
    return jnp.concatenate([out[:, c * SUBLANES:(c + 1) * SUBLANES, :].reshape(rows, LANES) for c in range(N_CT)],
                           axis=1)


def _merge_ln1_kernel(*refs, aliased, n_tiles):
    n_in = 12
    outs = refs[n_in + 1:] if aliased else refs[n_in:]
    xt_ref = outs[0]

    @pl.when(pl.program_id(0) < n_tiles)
    def _():
        _merge_ln1_tile(*refs[:n_in], *outs)

    @pl.when(pl.program_id(0) >= n_tiles)
    def _():
        xt_ref[...] = jnp.zeros(xt_ref.shape, BF16)


def _merge_ln1_tile(y_ref, o_ref, gc_ref, ga_ref, x_ref, wc_ref, wa_ref, wo_ref, g_ref, b_ref, wr_ref,
                    cnt_in_ref, xt_ref, x1_ref, ei_ref, gt_ref, cnt_ref):
    @pl.when(pl.program_id(0) == 0)
    def _():
        cnt_ref[...] = cnt_in_ref[...]

    conv_out = jnp.dot(y_ref[...], wc_ref[...], preferred_element_type=F32)
    attn_out = jnp.dot(o_ref[...], wa_ref[...], preferred_element_type=F32)
    merged = gc_ref[...].astype(F32) * conv_out + ga_ref[...].astype(F32) * attn_out
    mix = jnp.dot(merged.astype(BF16), wo_ref[...], preferred_element_type=F32)
    x1 = _layer_norm(DEEPNORM_ALPHA * x_ref[...] + mix, g_ref[...], b_ref[...])
    x1_ref[...] = x1
    xt_ref[...] = _to_token_major(x1, _tile_perm(True))

    x_hi = x1.astype(BF16)
    x_lo = (x1 - x_hi.astype(F32)).astype(BF16)
    logits = (jnp.dot(x_hi, wr_ref[0], preferred_element_type=F32)
              + jnp.dot(x_lo, wr_ref[0], preferred_element_type=F32)
              + jnp.dot(x_hi, wr_ref[1], preferred_element_type=F32))
    lane = lax.broadcasted_iota(I32, logits.shape, 1)
    is_g = lane < N_GROUPS
    lg = jnp.where(is_g, logits, -jnp.inf)
    eg = jnp.exp(lg - jnp.max(lg, axis=1, keepdims=True))
    pg = eg / jnp.sum(eg, axis=1, keepdims=True)
    g_top = jnp.max(pg, axis=1, keepdims=True)
    g_idx = jnp.min(jnp.where(is_g & (pg == g_top), lane, LANES), axis=1, keepdims=True)
    lo = N_GROUPS + g_idx * EXPERTS_PER_GROUP
    is_e = (lane >= lo) & (lane < lo + EXPERTS_PER_GROUP)
    le = jnp.where(is_e, logits, -jnp.inf)
    ee = jnp.exp(le - jnp.max(le, axis=1, keepdims=True))
    pe = ee / jnp.sum(ee, axis=1, keepdims=True)
    p1 = jnp.max(jnp.where(is_e, pe, -1.0), axis=1, keepdims=True)
    i1 = jnp.min(jnp.where(is_e & (pe == p1), lane, 2 * LANES), axis=1, keepdims=True)
    rest = is_e & (lane != i1)
    p2 = jnp.max(jnp.where(rest, pe, -1.0), axis=1, keepdims=True)
    i2 = jnp.min(jnp.where(rest & (pe == p2), lane, 2 * LANES), axis=1, keepdims=True)
    den = p1 + p2
    gt_ref[...] = jnp.where(lane == 0, g_top * p1 / den, jnp.where(lane == 1, g_top * p2 / den, 0.0))

    e1, e2 = i1 - N_GROUPS, i2 - N_GROUPS
    tm = logits.shape[0]
    oh1 = jnp.where(lane == e1, 1.0, 0.0)
    oh2 = jnp.where(lane == e2, 1.0, 0.0)
    earlier = jnp.where(lax.broadcasted_iota(I32, (tm, tm), 1) < lax.broadcasted_iota(I32, (tm, tm), 0),
                        1.0, 0.0).astype(BF16)
    before1 = jnp.dot(earlier, oh1.astype(BF16), preferred_element_type=F32)
    before2 = jnp.dot(earlier, oh2.astype(BF16), preferred_element_type=F32)
    tot1 = jnp.sum(oh1, axis=0, keepdims=True)
    base = cnt_ref[...]
    r1 = jnp.sum(oh1 * (base + before1), axis=1, keepdims=True)
    r2 = jnp.sum(oh2 * (base + tot1 + before2), axis=1, keepdims=True)
    cnt_ref[...] = base + tot1 + jnp.sum(oh2, axis=0, keepdims=True)
    ei_ref[...] = jnp.where(lane == 0, e1, jnp.where(lane == 1, e2, jnp.where(
        lane == 2, r1.astype(I32), jnp.where(lane == 3, r2.astype(I32), 0))))


def _merge_ln1(yact, o, gc, ga, x, wc, wa, wo, g1, b1, wr, cnt_in, tm, xt_all=None, m_total=None):
    m = x.shape[0]
    aliased = xt_all is not None
    n_tiles = m // tm
    total = xt_all.shape[0] if aliased else m_total
    blk0 = (total - m) // tm if aliased else 0
    n_steps = n_tiles if aliased else pl.cdiv(total, tm)
    row = lambda w: pl.BlockSpec((tm, w), lambda i: (jnp.minimum(i, n_tiles - 1), 0))
    res = lambda a: pl.BlockSpec(a.shape, lambda i: (0,) * a.ndim, pipeline_mode=pl.Buffered(1))
    cnt_spec = pl.BlockSpec((1, LANES), lambda i: (0, 0))
    args = [yact, o, gc, ga, x, wc, wa, wo, g1, b1, wr, cnt_in]
    in_specs = [row(C_CONV), row(N_HEADS * HEAD_DIM), row(D_MODEL), row(D_MODEL), row(D_MODEL),
                res(wc), res(wa), res(wo), res(g1), res(b1), res(wr), cnt_spec]
    if aliased:
        args.append(xt_all)
        in_specs.append(pl.BlockSpec(memory_space=pl.ANY))
    return pl.pallas_call(
        functools.partial(_merge_ln1_kernel, aliased=aliased, n_tiles=n_tiles),
        grid=(n_steps,),
        in_specs=in_specs,
        out_specs=(pl.BlockSpec((tm, N_CT, LANES), lambda i: (blk0 + i, 0, 0)), row(D_MODEL), row(LANES),
                   row(LANES), cnt_spec),
        out_shape=(jax.ShapeDtypeStruct((total, N_CT, LANES), BF16),
                   jax.ShapeDtypeStruct((m, D_MODEL), F32),
                   jax.ShapeDtypeStruct((m, LANES), I32),
                   jax.ShapeDtypeStruct((m, LANES), F32),
                   jax.ShapeDtypeStruct((1, LANES), F32)),
        input_output_aliases={len(args) - 1: 0} if aliased else {},
        compiler_params=_cparams(("arbitrary",)),
        name="merge_ln1",
    )(*args)


N_GSLOT = 3


def _experts_kernel(fb_ref, dest_ref, x_hbm, wg_ref, wu_ref, wd_ref, y_hbm,
                    rowtok, xbuf, ybuf, wgb, wub, wdb, gsem, ysem, *, n_blocks):
    e = pl.program_id(0)
    n_used = fb_ref[N_EXPERTS]
    last = jnp.maximum(n_used - 1, 0)

    def start_gather(blk, sl):
        base = blk * MOE_BLK
        for i in range(MOE_BLK):
            pltpu.make_async_copy(x_hbm.at[rowtok[base + i]], xbuf.at[sl, i], gsem.at[sl]).start()

    def wait_gather(sl):
        pltpu.make_async_copy(xbuf.at[sl], xbuf.at[sl], gsem.at[sl]).wait()

    def write_copy(blk, sl):
        r0 = pl.multiple_of(blk * MOE_BLK, MOE_BLK)
        return pltpu.make_async_copy(ybuf.at[sl], y_hbm.at[pl.ds(r0, MOE_BLK)], ysem.at[sl])

    @pl.when(e == 0)
    def _():
        def clear(r, carry):
            rowtok[r] = 0
            return carry
        lax.fori_loop(0, n_blocks * MOE_BLK, clear, 0, unroll=8)

        def place(a, carry):
            rowtok[dest_ref[a]] = lax.shift_right_logical(a, 1)
            return carry
        lax.fori_loop(0, dest_ref.shape[0], place, 0, unroll=8)
        start_gather(0, 0)
        start_gather(jnp.minimum(1, last), 1)

    b0, b1 = fb_ref[e], fb_ref[e + 1]

    @pl.when(b1 > b0)
    def _():
        wgb[...] = wg_ref[0].astype(BF16)
        wub[...] = wu_ref[0].astype(BF16)
        wdb[...] = wd_ref[0].astype(BF16)

    def block(b, carry):
        sl = lax.rem(b, N_GSLOT)
        ysl = lax.rem(b, 2)

        @pl.when(b >= 2)
        def _():
            write_copy(b - 2, ysl).wait()

        wait_gather(sl)
        start_gather(jnp.minimum(b + 2, last), lax.rem(b + 2, N_GSLOT))
        xb = _from_token_major(xbuf[sl], _tile_perm(False)).astype(BF16)
        hg = jnp.dot(xb, wgb[...], preferred_element_type=F32)
        hu = jnp.dot(xb, wub[...], preferred_element_type=F32)
        h = (hg * jax.nn.sigmoid(hg) * hu).astype(BF16)
        y = jnp.dot(h, wdb[...], preferred_element_type=F32)
        ybuf[ysl] = _to_token_major(y, _tile_perm(True))
        write_copy(b, ysl).start()
        return carry

    lax.fori_loop(b0, b1, block, 0)

    @pl.when(e == pl.num_programs(0) - 1)
    def _():
        wait_gather(lax.rem(last + 1, N_GSLOT))
        wait_gather(lax.rem(last + 2, N_GSLOT))

        @pl.when(n_used >= 1)
        def _():
            write_copy(last, lax.rem(last, 2)).wait()

        @pl.when(n_used >= 2)
        def _():
            write_copy(last - 1, lax.rem(last - 1, 2)).wait()

        ybuf[0] = jnp.zeros(ybuf.shape[1:], BF16)

        def fill(blk, carry):
            write_copy(blk, 0).start()
            return carry
        lax.fori_loop(n_used, n_blocks, fill, 0)

        def fill_wait(blk, carry):
            write_copy(blk, 0).wait()
            return carry
        lax.fori_loop(n_used, n_blocks, fill_wait, 0)


def _experts(first_blk, dest, xt_all, wg, wu, wd):
    n_blocks = (dest.shape[0] + N_EXPERTS * (MOE_BLK - 1) + MOE_BLK - 1) // MOE_BLK
    n_rows = n_blocks * MOE_BLK
    grid_spec = pltpu.PrefetchScalarGridSpec(
        num_scalar_prefetch=2,
        grid=(N_EXPERTS,),
        in_specs=[pl.BlockSpec(memory_space=pl.ANY),
                  pl.BlockSpec((1, D_MODEL, D_EXPERT), lambda e, fb, de: (e, 0, 0)),
                  pl.BlockSpec((1, D_MODEL, D_EXPERT), lambda e, fb, de: (e, 0, 0)),
                  pl.BlockSpec((1, D_EXPERT, D_MODEL), lambda e, fb, de: (e, 0, 0))],
        out_specs=pl.BlockSpec(memory_space=pl.ANY),
        scratch_shapes=[pltpu.SMEM((n_rows,), I32),
                        pltpu.VMEM((N_GSLOT, MOE_BLK, N_CT, LANES), BF16),
                        pltpu.VMEM((2, MOE_BLK, N_CT, LANES), BF16),
                        pltpu.VMEM((D_MODEL, D_EXPERT), BF16),
                        pltpu.VMEM((D_MODEL, D_EXPERT), BF16),
                        pltpu.VMEM((D_EXPERT, D_MODEL), BF16),
                        pltpu.SemaphoreType.DMA((N_GSLOT,)),
                        pltpu.SemaphoreType.DMA((2,))],
    )
    return pl.pallas_call(
        functools.partial(_experts_kernel, n_blocks=n_blocks),
        grid_spec=grid_spec,
        out_shape=jax.ShapeDtypeStruct((n_rows, N_CT, LANES), BF16),
        compiler_params=_cparams(("arbitrary",)),
        name="experts",
    )(first_blk, dest, xt_all, wg, wu, wd)


def _combine_ln2_kernel(dest_ref, x1_ref, gt_ref, yb_hbm, g_ref, b_ref, y_ref, ybuf, sem, *, tm, tok0):
    i = pl.program_id(0)
    last = pl.num_programs(0) - 1
    slot = lax.rem(i, 2)

    def row_copy(a, sl, k, r):
        return pltpu.make_async_copy(yb_hbm.at[dest_ref[a]], ybuf.at[sl, k, r], sem.at[sl])

    def start_gather(tile, sl, unrolled):
        base = 2 * (tok0 + tile * tm)
        if unrolled:
            for r in range(tm):
                row_copy(base + 2 * r, sl, 0, r).start()
                row_copy(base + 2 * r + 1, sl, 1, r).start()
        else:
            def body(r, carry):
                row_copy(base + 2 * r, sl, 0, r).start()
                row_copy(base + 2 * r + 1, sl, 1, r).start()
                return carry
            lax.fori_loop(0, tm, body, 0)

    def wait_gather(sl):
        pltpu.make_async_copy(ybuf.at[sl], ybuf.at[sl], sem.at[sl]).wait()

    @pl.when(i == 0)
    def _():
        start_gather(0, 0, unrolled=False)

    wait_gather(slot)
    start_gather(jnp.minimum(i + 1, last), 1 - slot, unrolled=True)
    gt = gt_ref[...]
    perm = _tile_perm(False)
    z = (DEEPNORM_ALPHA * x1_ref[...] + gt[:, 0:1] * _from_token_major(ybuf[slot, 0], perm)
         + gt[:, 1:2] * _from_token_major(ybuf[slot, 1], perm))
    y_ref[...] = _layer_norm(z, g_ref[...], b_ref[...])

    @pl.when(i == last)
    def _():
        wait_gather(1 - slot)


def _combine_ln2(dest, x1, gt, yb, g2, b2, tm, tok0):
    m = x1.shape[0]
    grid_spec = pltpu.PrefetchScalarGridSpec(
        num_scalar_prefetch=1,
        grid=(m // tm,),
        in_specs=[pl.BlockSpec((tm, D_MODEL), lambda i, d: (i, 0)),
                  pl.BlockSpec((tm, LANES), lambda i, d: (i, 0)),
                  pl.BlockSpec(memory_space=pl.ANY),
                  pl.BlockSpec((1, D_MODEL), lambda i, d: (0, 0)),
                  pl.BlockSpec((1, D_MODEL), lambda i, d: (0, 0))],
        out_specs=pl.BlockSpec((tm, D_MODEL), lambda i, d: (i, 0)),
        scratch_shapes=[pltpu.VMEM((2, 2, tm, N_CT, LANES), BF16),
                        pltpu.SemaphoreType.DMA((2,))],
    )
    return pl.pallas_call(
        functools.partial(_combine_ln2_kernel, tm=tm, tok0=tok0),
        grid_spec=grid_spec,
        out_shape=jax.ShapeDtypeStruct((m, D_MODEL), F32),
        compiler_params=_cparams(("arbitrary",)),
        name="combine_ln2",
    )(dest, x1, gt, yb, g2, b2)


def _dispatch_tables(eid, rank, counts):
    padded = (counts + MOE_BLK - 1) // MOE_BLK * MOE_BLK
    pad_end = jnp.cumsum(padded)
    pad_start = pad_end - padded
    start_of = jnp.sum(jnp.where(eid[..., None] == jnp.arange(N_EXPERTS, dtype=I32), pad_start, 0), axis=-1)
    dest = (start_of + rank).reshape(-1).astype(I32)
    first_blk = (jnp.concatenate([pad_start, pad_end[-1:]]) // MOE_BLK).astype(I32)
    return dest, first_blk


def kernel(x_prompt, x_sample, cache_k, cache_v, cache_kidx, state_conv, page_table, w_in, conv_dw_w, conv_dw_b,
           conv_ln_g, conv_ln_b, w_conv_out, w_attn_out, w_out, ln1_g, ln1_b, w_router_group, w_router_expert,
           w_expert_gate, w_expert_up, w_expert_down, ln2_g, ln2_b):
    assert w_in.shape[0] == 1, "single-layer trunk"
    nb, t, d = x_prompt.shape
    ns = x_sample.shape[0]
    n_pool = cache_k.shape[1]
    mp = nb * t
    row2 = lambda a: a.reshape(1, -1)

    w_arr = _arrange_w_in(w_in[0])
    dww, dwb, lng, lnb = conv_dw_w[0], row2(conv_dw_b[0]), row2(conv_ln_g[0]), row2(conv_ln_b[0])
    wc, wa, wo = w_conv_out[0].astype(BF16), w_attn_out[0].astype(BF16), w_out[0].astype(BF16)
    g1, b1, g2, b2 = row2(ln1_g[0]), row2(ln1_b[0]), row2(ln2_g[0]), row2(ln2_b[0])
    wr32 = jnp.concatenate([w_router_group[0], w_router_expert[0],
                            jnp.zeros((d, LANES - N_GROUPS - N_EXPERTS), F32)], axis=1)
    wr_hi = wr32.astype(BF16)
    wr = jnp.stack([wr_hi, (wr32 - wr_hi.astype(F32)).astype(BF16)])

    xp = x_prompt.reshape(mp, d)
    u_p, q_p, qi_p, kv_p, kvb_p, gc_p, ga_p, kiwi_p, kib_p = _in_proj(xp, w_arr, tm=512)
    yact_p = _conv_prompt(u_p.reshape(nb, t, C_CONV), dww, dwb, lng, lnb)
    o_p = _attn_prompt(qi_p.reshape(nb, t, -1), kiwi_p.reshape(nb, t, LANES), kib_p.reshape(nb, t, LANES),
                       q_p.reshape(nb, t, -1), kvb_p.reshape(nb, t, 2 * KV_DIM))
    xt_all, x1_p, ei_p, gt_p, cnt_p = _merge_ln1(yact_p.reshape(mp, C_CONV), o_p.reshape(mp, -1), gc_p, ga_p, xp,
                                                 wc, wa, wo, g1, b1, wr, jnp.zeros((1, LANES), F32), tm=256,
                                                 m_total=mp + ns)

    xs = x_sample.reshape(ns, d)
    u_s, q_s, qi_s, kv_s, _, gc_s, ga_s, kiwi_s, _ = _in_proj(xs, w_arr, tm=ns)
    yact_s = _conv_sample(state_conv[0].transpose(1, 0, 2), u_s, dww, dwb, lng, lnb)
    cki2 = cache_kidx.transpose(0, 1, 3, 2).reshape(n_pool, IDX_DIM, PAGE_SIZE)
    ck2 = cache_k.transpose(0, 1, 3, 4, 2).reshape(n_pool, KV_DIM, PAGE_SIZE)
    cv2 = cache_v.transpose(0, 1, 3, 4, 2).reshape(n_pool, KV_DIM, PAGE_SIZE)
    scores_s = _sample_scores(page_table, qi_s.reshape(ns, IDX_HEADS, IDX_DIM),
                              kiwi_s[:, IDX_DIM:IDX_DIM + IDX_HEADS].reshape(ns, IDX_HEADS, 1),
                              kiwi_s[:, :IDX_DIM].reshape(ns, 1, IDX_DIM), cki2)
    scores3 = scores_s.reshape(ns, -1, ATT_CK).transpose(1, 0, 2)
    own_group = (np.arange(N_HEADS)[:, None] // HEADS_PER_KV) == (np.arange(KV_DIM)[None, :] // HEAD_DIM)
    qbd = jnp.where(own_group[None], jnp.tile(q_s.reshape(ns, N_HEADS, HEAD_DIM), (1, 1, KV_HEADS)), 0).astype(BF16)
    o3 = _sample_attn(page_table, scores3, qbd, kv_s[:, :KV_DIM].reshape(ns, 1, KV_DIM),
                      kv_s[:, KV_DIM:].reshape(ns, 1, KV_DIM), ck2, cv2)
    o_s = o3.reshape(ns, N_HEADS * HEAD_DIM).astype(BF16)
    xt_all, x1_s, ei_s, gt_s, cnt = _merge_ln1(yact_s, o_s, gc_s, ga_s, xs, wc, wa, wo, g1, b1, wr, cnt_p, tm=ns,
                                               xt_all=xt_all)

    eid_rank = jnp.concatenate([ei_p[:, :4], ei_s[:, :4]], axis=0)
    dest, first_blk = _dispatch_tables(eid_rank[:, :2], eid_rank[:, 2:], cnt[0, :N_EXPERTS].astype(I32))
    yb = _experts(first_blk, dest, xt_all, w_expert_gate[0], w_expert_up[0], w_expert_down[0])
    y_p = _combine_ln2(dest, x1_p, gt_p, yb, g2, b2, tm=256, tok0=0)
    y_s = _combine_ln2(dest, x1_s, gt_s, yb, g2, b2, tm=ns, tok0=mp)

    y_prompt = y_p.reshape(nb, t, d)
    y_sample = y_s.reshape(ns, 1, d)
    k_prompt = kv_p[:, :KV_DIM].reshape(1, nb, t, KV_HEADS, HEAD_DIM)
    v_prompt = kv_p[:, KV_DIM:].reshape(1, nb, t, KV_HEADS, HEAD_DIM)
    kidx_prompt = kiwi_p[:, :IDX_DIM].reshape(1, nb, t, IDX_DIM)
    conv_prompt = u_p.reshape(nb, t, C_CONV)[:, t - (CONV_W - 1):][None]
    k_sample = kv_s[:, :KV_DIM].reshape(1, ns, 1, KV_HEADS, HEAD_DIM)
    v_sample = kv_s[:, KV_DIM:].reshape(1, ns, 1, KV_HEADS, HEAD_DIM)
    kidx_sample = kiwi_s[:, :IDX_DIM].reshape(1, ns, 1, IDX_DIM)
    conv_sample = jnp.concatenate([state_conv[0][:, 1:], u_s[:, None, :]], axis=1)[None]
    return (y_prompt, y_sample, k_prompt, v_prompt, kidx_prompt, conv_prompt, k_sample, v_sample, kidx_sample,
            conv_sample)
```

```python
import functools

import jax
import jax.numpy as jnp
import numpy as np
from jax import lax
from jax.experimental import pallas as pl
from jax.experimental.pallas import tpu as pltpu

F32 = jnp.float32
BF16 = jnp.bfloat16
I32 = jnp.int32

D_MODEL = 2048
C_CONV = 1024
CONV_W = 31
N_HEADS = 16
HEAD_DIM = 64
KV_HEADS = 4
HEADS_PER_KV = N_HEADS // KV_HEADS
KV_DIM = KV_HEADS * HEAD_DIM
IDX_HEADS = 16
IDX_DIM = 64
MAX_TOPK = 256
N_GROUPS = 8
EXPERTS_PER_GROUP = 8
N_EXPERTS = 64
D_EXPERT = 512
MOE_BLK = 128
PAGE_SIZE = 128
LN_EPS = 1e-5
DEEPNORM_ALPHA = 2.0 ** 0.25
NEG_INF = -1e30
INT_MIN = -(2 ** 31)
NEG_KEY = int(INT_MIN - int(np.array(NEG_INF, np.float32).view(np.int32)))
LOG2E = 1.4426950408889634
Q_SCALE = HEAD_DIM ** -0.5 * LOG2E
IDX_SCALE = IDX_HEADS ** -0.5 * IDX_DIM ** -0.5

LANES = 128
SUBLANES = 8
HALF = LANES // 2
IN_TN = 1024
VMEM_LIMIT = 56 * 1024 * 1024


def _cparams(sem, vmem=VMEM_LIMIT):
    return pltpu.CompilerParams(dimension_semantics=sem, vmem_limit_bytes=vmem)


def _float_key(x):
    b = pltpu.bitcast(x, I32)
    return jnp.where(b < 0, INT_MIN - b, b)


def _layer_norm(z, g, b):
    mu = jnp.mean(z, axis=-1, keepdims=True)
    zc = z - mu
    var = jnp.mean(zc * zc, axis=-1, keepdims=True)
    return zc * lax.rsqrt(var + LN_EPS) * g + b


def _dot_nt(a, b):
    return lax.dot_general(a, b, (((1,), (1,)), ((), ())), preferred_element_type=F32)


def _alibi_slope(h):
    return float(2.0 ** (-8.0 * (h + 1) / N_HEADS))


_T_GLU, _T_Q, _T_QI, _T_KV, _T_GC, _T_GA, _N_TILES = 0, 2, 3, 4, 5, 7, 9
KV_TILE_COLS = 2 * KV_DIM + 2 * LANES
GLU_HALF = IN_TN // 2
CONV_HALO = 32
CONV_STRIP = 32
CONV_ROWS = 256


def _arrange_w_in(w):
    sizes = (C_CONV, C_CONV, N_HEADS * HEAD_DIM, KV_DIM, KV_DIM,
             IDX_HEADS * IDX_DIM, IDX_DIM, IDX_HEADS, D_MODEL, D_MODEL)
    offs = np.cumsum((0,) + sizes)
    a, g, q, k, v, qi, ki, wi, gc, ga = [w[:, offs[i]:offs[i + 1]] for i in range(10)]
    glu = [jnp.concatenate([a[:, t * GLU_HALF:(t + 1) * GLU_HALF], g[:, t * GLU_HALF:(t + 1) * GLU_HALF]], axis=1)
           for t in range(C_CONV // GLU_HALF)]
    zeros = lambda n: jnp.zeros((w.shape[0], n), w.dtype)
    kv_tile = jnp.concatenate([k, v, ki, wi, zeros(LANES - IDX_DIM - IDX_HEADS), ki, ki,
                               zeros(IN_TN - KV_TILE_COLS)], axis=1)
    return jnp.concatenate(glu + [q, qi, kv_tile, gc, ga], axis=1).astype(BF16)


def _conv_half(jj, new_seq, ubuf, sh_ref, ypre, dww_ref, dwb_ref):
    tm = ypre.shape[0]
    cs = slice(jj * GLU_HALF, (jj + 1) * GLU_HALF)
    ubuf[jj, 0:CONV_HALO, :] = jnp.where(new_seq, 0.0, ubuf[jj, 0:CONV_HALO, :])
    first = CONV_HALO - (CONV_W - 1)
    dwb = dwb_ref[:, cs]
    for rb in range(tm // CONV_ROWS):
        for b in range(SUBLANES):
            rows = CONV_ROWS + SUBLANES * ((CONV_W - 1 - b) // SUBLANES)
            src0 = rb * CONV_ROWS + first + b
            sh_ref[b, 0:rows, :] = ubuf[jj, src0:src0 + rows, :]
        for r in range(CONV_ROWS // CONV_STRIP):
            acc = jnp.broadcast_to(dwb, (CONV_STRIP, GLU_HALF))
            for j in range(CONV_W):
                a, b = divmod(j, SUBLANES)
                r0 = r * CONV_STRIP + a * SUBLANES
                acc = acc + dww_ref[j:j + 1, cs] * sh_ref[b, r0:r0 + CONV_STRIP, :]
            o0 = rb * CONV_ROWS + r * CONV_STRIP
            ypre[o0:o0 + CONV_STRIP, cs] = acc
    ubuf[jj, 0:CONV_HALO, :] = ubuf[jj, tm:tm + CONV_HALO, :]


def _in_proj_kernel(*refs, fuse_conv, tiles_per_seq):
    if fuse_conv:
        (x_ref, w_ref, dww_ref, dwb_ref, lng_ref, lnb_ref,
         u_ref, q_ref, qi_ref, kv_ref, kvb_ref, gc_ref, ga_ref, kiwi_ref, kib_ref, y_ref,
         xb_ref, ubuf, sh_ref, ypre) = refs
    else:
        (x_ref, w_ref, u_ref, q_ref, qi_ref, kv_ref, kvb_ref, gc_ref, ga_ref, kiwi_ref, kib_ref, xb_ref) = refs
    i = pl.program_id(0)
    j = pl.program_id(1)

    @pl.when(j == 0)
    def _():
        xb_ref[...] = x_ref[...].astype(BF16)

    if fuse_conv:
        new_seq = lax.rem(i, tiles_per_seq) == 0

        @pl.when((i == 0) & (j == 0))
        def _():
            ubuf[:, 0:CONV_HALO, :] = jnp.zeros((ubuf.shape[0], CONV_HALO, GLU_HALF), F32)

    def mm(ncols=IN_TN):
        return jnp.dot(xb_ref[...], w_ref[:, :ncols], preferred_element_type=F32)

    for jj in range(2):
        @pl.when(j == jj)
        def _(jj=jj):
            if fuse_conv and jj == 1:
                _conv_half(0, new_seq, ubuf, sh_ref, ypre, dww_ref, dwb_ref)
            acc = mm()
            u = acc[:, :GLU_HALF] * jax.nn.sigmoid(acc[:, GLU_HALF:])
            u_ref[...] = u
            if fuse_conv:
                ubuf[jj, CONV_HALO:, :] = u

    @pl.when(j == _T_Q)
    def _():
        if fuse_conv:
            _conv_half(1, new_seq, ubuf, sh_ref, ypre, dww_ref, dwb_ref)
        q_ref[...] = (mm() * Q_SCALE).astype(BF16)

    @pl.when(j == _T_QI)
    def _():
        if fuse_conv:
            lng, lnb = lng_ref[...], lnb_ref[...]
            for r in range(ypre.shape[0] // CONV_STRIP):
                rs = slice(r * CONV_STRIP, (r + 1) * CONV_STRIP)
                yn = _layer_norm(ypre[rs, :], lng, lnb)
                y_ref[rs, :] = (yn * jax.nn.sigmoid(yn)).astype(BF16)
        qi_ref[...] = mm().astype(BF16)

    @pl.when(j == _T_KV)
    def _():
        acc = mm(KV_TILE_COLS)
        kv_ref[...] = acc[:, :2 * KV_DIM]
        kvb_ref[...] = acc[:, :2 * KV_DIM].astype(BF16)
        kiwi_ref[...] = acc[:, 2 * KV_DIM:2 * KV_DIM + LANES]
        kib_ref[...] = acc[:, 2 * KV_DIM + LANES:].astype(BF16)

    @pl.when((j >= _T_GC) & (j < _T_GA))
    def _():
        gc_ref[...] = jax.nn.sigmoid(mm()).astype(BF16)

    @pl.when(j >= _T_GA)
    def _():
        ga_ref[...] = jax.nn.sigmoid(mm()).astype(BF16)


def _in_proj(x, w_arr, tm, conv=None):
    m = x.shape[0]
    fuse_conv = conv is not None

    def cl(lo, n):
        return lambda i, j: (i, jnp.clip(j - lo, 0, n - 1))

    row_blk = lambda w: pl.BlockSpec((tm, w), lambda i, j: (i, 0))
    out_shape = [
        jax.ShapeDtypeStruct((m, C_CONV), F32),
        jax.ShapeDtypeStruct((m, N_HEADS * HEAD_DIM), BF16),
        jax.ShapeDtypeStruct((m, IDX_HEADS * IDX_DIM), BF16),
        jax.ShapeDtypeStruct((m, 2 * KV_DIM), F32),
        jax.ShapeDtypeStruct((m, 2 * KV_DIM), BF16),
        jax.ShapeDtypeStruct((m, D_MODEL), BF16),
        jax.ShapeDtypeStruct((m, D_MODEL), BF16),
        jax.ShapeDtypeStruct((m, LANES), F32),
        jax.ShapeDtypeStruct((m, LANES), BF16),
    ]
    out_specs = [
        pl.BlockSpec((tm, GLU_HALF), cl(_T_GLU, 2)),
        row_blk(IN_TN), row_blk(IN_TN), row_blk(2 * KV_DIM), row_blk(2 * KV_DIM),
        pl.BlockSpec((tm, IN_TN), cl(_T_GC, 2)),
        pl.BlockSpec((tm, IN_TN), cl(_T_GA, 2)),
        row_blk(LANES), row_blk(LANES),
    ]
    args = [x, w_arr]
    in_specs = [pl.BlockSpec((tm, D_MODEL), lambda i, j: (i, 0)),
                pl.BlockSpec((D_MODEL, IN_TN), lambda i, j: (0, j))]
    scratch = [pltpu.VMEM((tm, D_MODEL), BF16)]
    tiles_per_seq = 1
    if fuse_conv:
        dww, dwb, lng, lnb, seq = conv
        assert seq % tm == 0 and tm % CONV_ROWS == 0
        tiles_per_seq = seq // tm
        const = lambda a: pl.BlockSpec(a.shape, lambda i, j: (0, 0))
        args += [dww, dwb, lng, lnb]
        in_specs += [const(dww), const(dwb), const(lng), const(lnb)]
        out_shape.append(jax.ShapeDtypeStruct((m, C_CONV), BF16))
        out_specs.append(row_blk(C_CONV))
        scratch += [pltpu.VMEM((2, CONV_HALO + tm, GLU_HALF), F32),
                    pltpu.VMEM((SUBLANES, CONV_ROWS + SUBLANES * ((CONV_W - 1) // SUBLANES), GLU_HALF), F32),
                    pltpu.VMEM((tm, C_CONV), F32)]
    return pl.pallas_call(
        functools.partial(_in_proj_kernel, fuse_conv=fuse_conv, tiles_per_seq=tiles_per_seq),
        grid=(m // tm, _N_TILES),
        in_specs=in_specs,
        out_specs=tuple(out_specs),
        out_shape=tuple(out_shape),
        scratch_shapes=scratch,
        compiler_params=_cparams(("arbitrary", "arbitrary")),
        name="in_proj",
    )(*args)


def _conv_sample_kernel(state_ref, u_ref, dww_ref, dwb_ref, lng_ref, lnb_ref, y_ref):
    acc = dwb_ref[...] + dww_ref[CONV_W - 1:CONV_W, :] * u_ref[...]
    for j in range(CONV_W - 1):
        acc = acc + dww_ref[j:j + 1, :] * state_ref[j]
    yn = _layer_norm(acc, lng_ref[...], lnb_ref[...])
    y_ref[...] = (yn * jax.nn.sigmoid(yn)).astype(BF16)


def _conv_sample(state_t, u, dww, dwb, lng, lnb):
    n = u.shape[0]
    return pl.pallas_call(
        _conv_sample_kernel,
        out_shape=jax.ShapeDtypeStruct((n, C_CONV), BF16),
        compiler_params=pltpu.CompilerParams(vmem_limit_bytes=VMEM_LIMIT),
        name="conv_sample",
    )(state_t, u, dww, dwb, lng, lnb)


ATT_TQ = 256
ATT_CK = 256


def _count_ge(key_ref, nch, cand):
    def body(c, acc):
        m = jnp.where(key_ref[c] >= cand, 1.0, 0.0)
        return acc + m[:, :LANES] + m[:, LANES:]
    acc = lax.fori_loop(0, nch, body, jnp.zeros((cand.shape[0], LANES), F32))
    return jnp.sum(acc, axis=1, keepdims=True)


def _select_threshold(key_ref, nch, rows):
    def body(i, prefix):
        cand = prefix + jnp.left_shift(jnp.int32(1), 31 - i)
        return jnp.where(_count_ge(key_ref, nch, cand) >= float(MAX_TOPK), cand, prefix)
    return lax.fori_loop(0, 32, body, jnp.full((rows, 1), INT_MIN, I32))


def _selection_bias(key_ref, bias_ref, nch):
    rows, ck = key_ref.shape[1:]
    thr = _select_threshold(key_ref, nch, rows)
    n_tie_take = float(MAX_TOPK) - _count_ge(key_ref, nch, thr + 1)
    tri = jnp.where(lax.broadcasted_iota(I32, (ck, ck), 0) <= lax.broadcasted_iota(I32, (ck, ck), 1),
                    1.0, 0.0).astype(BF16)

    def bias_chunk(c, seen):
        key = key_ref[c]
        eq = key == thr
        eqf = jnp.where(eq, 1.0, 0.0)
        incl = jnp.dot(eqf.astype(BF16), tri, preferred_element_type=F32)
        sel = (key > thr) | (eq & (seen + incl - eqf < n_tie_take))
        bias_ref[c] = jnp.where(sel & (key > NEG_KEY), 0.0, NEG_INF)
        return seen + incl[:, ck - 1:ck]

    lax.fori_loop(0, nch, bias_chunk, jnp.zeros((rows, 1), F32))


def _count_ge_t(key_ref, nch, cand):
    ck, nq = key_ref.shape[1:]

    def body(c, acc):
        m = jnp.where(key_ref[c] >= cand, 1.0, 0.0)
        return acc + jnp.sum(m.reshape(ck // SUBLANES, SUBLANES, nq), axis=0)

    acc = lax.fori_loop(0, nch, body, jnp.zeros((SUBLANES, nq), F32))
    return jnp.sum(acc, axis=0, keepdims=True)


def _selection_bias_t(key_ref, bias_ref, nch):
    ck, nq = key_ref.shape[1:]

    def bit(i, prefix):
        cand = prefix + jnp.left_shift(jnp.int32(1), 31 - i)
        return jnp.where(_count_ge_t(key_ref, nch, cand) >= float(MAX_TOPK), cand, prefix)

    thr = lax.fori_loop(0, 32, bit, jnp.full((1, nq), INT_MIN, I32))
    n_tie_take = float(MAX_TOPK) - _count_ge_t(key_ref, nch, thr + 1)
    tri = jnp.where(lax.broadcasted_iota(I32, (ck, ck), 1) <= lax.broadcasted_iota(I32, (ck, ck), 0),
                    1.0, 0.0).astype(BF16)

    def bias_chunk(c, seen):
        key = key_ref[c]
        eq = key == thr
        eqf = jnp.where(eq, 1.0, 0.0)
        incl = jnp.dot(tri, eqf.astype(BF16), preferred_element_type=F32)
        sel = (key > thr) | (eq & (seen + incl - eqf < n_tie_take))
        bias_ref[c] = jnp.where(sel & (key > NEG_KEY), 0.0, NEG_INF).T
        return seen + incl[ck - 1:ck, :]

    lax.fori_loop(0, nch, bias_chunk, jnp.zeros((1, nq), F32))


def _swap_halves(x):
    return jnp.concatenate([x[:, HALF:], x[:, :HALF]], axis=1)


def _attn_prompt_kernel(qi_ref, kiwi_ref, kib_ref, q_ref, kvb_ref, o_ref,
                        key_ref, bias_ref, m_ref, ls_ref, acc_ref):
    tq, ck = ATT_TQ, ATT_CK
    qb = pl.program_id(1)
    nch = qb + 1
    lo_k = lax.broadcasted_iota(I32, (ck, LANES), 1) < HALF
    lo_q = lax.broadcasted_iota(I32, (tq, LANES), 1) < HALF

    key_pos = lax.broadcasted_iota(I32, (ck, tq), 0)
    q_pos = qb * tq + lax.broadcasted_iota(I32, (ck, tq), 1)
    w_t = jnp.transpose(kiwi_ref[0])[IDX_DIM:IDX_DIM + IDX_HEADS, :] * IDX_SCALE

    def score_chunk(c, carry):
        k0 = pl.multiple_of(c * ck, ck)
        kk = kib_ref[0, pl.ds(k0, ck), :]
        zero = jnp.zeros_like(kk)
        k_lo, k_hi = jnp.where(lo_k, kk, zero), jnp.where(lo_k, zero, kk)
        acc = jnp.zeros((ck, tq), F32)
        for p in range(IDX_HEADS // 2):
            qp = qi_ref[0, :, p * LANES:(p + 1) * LANES]
            acc = acc + w_t[2 * p:2 * p + 1, :] * jnp.maximum(_dot_nt(k_lo, qp), 0.0)
            acc = acc + w_t[2 * p + 1:2 * p + 2, :] * jnp.maximum(_dot_nt(k_hi, qp), 0.0)
        acc = jnp.where(c * ck + key_pos <= q_pos, acc, NEG_INF)
        key_ref[c] = _float_key(acc)
        return carry

    lax.fori_loop(0, nch, score_chunk, 0)
    _selection_bias_t(key_ref, bias_ref, nch)

    m_ref[...] = jnp.full(m_ref.shape, -jnp.inf, F32)
    ls_ref[...] = jnp.zeros(ls_ref.shape, F32)
    acc_ref[...] = jnp.zeros(acc_ref.shape, F32)

    def attn_chunk(c, carry):
        k0 = pl.multiple_of(c * ck, ck)
        kv = kvb_ref[0, pl.ds(k0, ck), :]
        bias = bias_ref[c]
        rel = (c * ck - (qb + 1) * tq + 1 + lax.broadcasted_iota(I32, (1, ck), 1)).astype(F32)
        for j in range(KV_HEADS // 2):
            kt = kv[:, j * LANES:(j + 1) * LANES]
            vt = kv[:, KV_DIM + j * LANES:KV_DIM + (j + 1) * LANES]
            kts, vts = _swap_halves(kt), _swap_halves(vt)
            zero, one = jnp.zeros_like(kt), jnp.ones_like(kt)
            for gg in range(2):
                g = 2 * j + gg
                k_own, k_swp = (kt, kts) if gg == 0 else (kts, kt)
                v_own, v_swp = (vt, vts) if gg == 0 else (vts, vt)
                k_lo, k_hi = jnp.where(lo_k, k_own, zero), jnp.where(lo_k, zero, k_swp)
                v_lo, v_hi = jnp.where(lo_k, v_own, one), jnp.where(lo_k, one, v_swp)
                for mm in range(HEADS_PER_KV // 2):
                    pidx = (HEADS_PER_KV // 2) * g + mm
                    qp = q_ref[0, :, pidx * LANES:(pidx + 1) * LANES]
                    pvs, alphas = [], []
                    for half, (kmat, vmat) in enumerate(((k_lo, v_lo), (k_hi, v_hi))):
                        h = 2 * pidx + half
                        s = _dot_nt(qp, kmat) + (_alibi_slope(h) * LOG2E) * rel + bias
                        m_old = m_ref[h]
                        m_new = jnp.maximum(m_old, jnp.max(s, axis=1, keepdims=True))
                        p = jnp.exp2(s - jnp.concatenate([m_new, m_new], axis=1))
                        alphas.append(jnp.exp2(m_old - m_new))
                        m_ref[h] = m_new
                        pvs.append(jnp.dot(p.astype(BF16), vmat, preferred_element_type=F32))
                    acc_ref[pidx] = (jnp.where(lo_q, alphas[0], alphas[1]) * acc_ref[pidx]
                                     + jnp.where(lo_q, pvs[0], pvs[1]))
                    ls_ref[pidx] = (jnp.where(lo_q, alphas[1], alphas[0]) * ls_ref[pidx]
                                    + jnp.where(lo_q, pvs[1], pvs[0]))
        return carry

    lax.fori_loop(0, nch, attn_chunk, 0)
    for pidx in range(N_HEADS // 2):
        l_pair = pltpu.roll(ls_ref[pidx], HALF, 1)
        o_ref[0, :, pidx * LANES:(pidx + 1) * LANES] = (acc_ref[pidx] / l_pair).astype(BF16)


def _attn_prompt(qi, kiwi, kib, q, kvb):
    n, t, _ = q.shape
    tq = ATT_TQ
    return pl.pallas_call(
        _attn_prompt_kernel,
        grid=(n, t // tq),
        in_specs=[pl.BlockSpec((1, tq, IDX_HEADS * IDX_DIM), lambda b, i: (b, i, 0)),
                  pl.BlockSpec((1, tq, LANES), lambda b, i: (b, i, 0)),
                  pl.BlockSpec((1, t, LANES), lambda b, i: (b, 0, 0)),
                  pl.BlockSpec((1, tq, N_HEADS * HEAD_DIM), lambda b, i: (b, i, 0)),
                  pl.BlockSpec((1, t, 2 * KV_DIM), lambda b, i: (b, 0, 0))],
        out_specs=pl.BlockSpec((1, tq, N_HEADS * HEAD_DIM), lambda b, i: (b, i, 0)),
        out_shape=jax.ShapeDtypeStruct((n, t, N_HEADS * HEAD_DIM), BF16),
        scratch_shapes=[pltpu.VMEM((t // ATT_CK, tq, ATT_CK), I32),
                        pltpu.VMEM((t // ATT_CK, tq, ATT_CK), F32),
                        pltpu.VMEM((N_HEADS, tq, LANES), F32),
                        pltpu.VMEM((N_HEADS // 2, tq, LANES), F32),
                        pltpu.VMEM((N_HEADS // 2, tq, LANES), F32)],
        compiler_params=_cparams(("arbitrary", "arbitrary")),
        name="attn_prompt",
    )(qi, kiwi, kib, q, kvb)


SAMPLE_KC = 2048


def _start_page_copies(pt_ref, src_hbm, dst, sem, base, n_pages):
    def body(p, carry):
        col = pl.multiple_of(p * PAGE_SIZE, PAGE_SIZE)
        pltpu.make_async_copy(src_hbm.at[pt_ref[base + p]], dst.at[:, pl.ds(col, PAGE_SIZE)], sem).start()
        return carry
    lax.fori_loop(0, n_pages, body, 0)


def _sample_score_kernel(pt_ref, qi_ref, wcol_ref, kinew_ref, cki_hbm, sc_ref, kibuf, sem, *, n_pages):
    b = pl.program_id(0)
    nb = pl.num_programs(0)
    past = n_pages * PAGE_SIZE
    slot = b % 2

    @pl.when(b == 0)
    def _():
        _start_page_copies(pt_ref, cki_hbm, kibuf.at[0], sem.at[0], 0, n_pages)

    @pl.when(b + 1 < nb)
    def _():
        _start_page_copies(pt_ref, cki_hbm, kibuf.at[1 - slot], sem.at[1 - slot], (b + 1) * n_pages, n_pages)

    pltpu.make_async_copy(kibuf.at[slot], kibuf.at[slot], sem.at[slot]).wait()

    qi = qi_ref[0]
    wsc = wcol_ref[0] * IDX_SCALE
    for c in range(past // SAMPLE_KC):
        kc = kibuf[slot, :, c * SAMPLE_KC:(c + 1) * SAMPLE_KC].astype(BF16)
        d = jnp.dot(qi, kc, preferred_element_type=F32)
        sc_ref[0, :, c * SAMPLE_KC:(c + 1) * SAMPLE_KC] = jnp.sum(wsc * jnp.maximum(d, 0.0), axis=0, keepdims=True)
    dn = jnp.sum(qi.astype(F32) * kinew_ref[0], axis=1, keepdims=True)
    s_new = jnp.sum(wsc * jnp.maximum(dn, 0.0), axis=0, keepdims=True)
    ln = lax.broadcasted_iota(I32, (1, ATT_CK), 1)
    sc_ref[0, :, past:] = jnp.where(ln == 0, s_new, NEG_INF)


def _sample_scores(page_table, qi3, wcol, kinew, cki2):
    n, n_pages = page_table.shape
    past = n_pages * PAGE_SIZE
    grid_spec = pltpu.PrefetchScalarGridSpec(
        num_scalar_prefetch=1,
        grid=(n,),
        in_specs=[pl.BlockSpec((1, IDX_HEADS, IDX_DIM), lambda b, pt: (b, 0, 0)),
                  pl.BlockSpec((1, IDX_HEADS, 1), lambda b, pt: (b, 0, 0)),
                  pl.BlockSpec((1, 1, IDX_DIM), lambda b, pt: (b, 0, 0)),
                  pl.BlockSpec(memory_space=pl.ANY)],
        out_specs=pl.BlockSpec((1, 1, past + ATT_CK), lambda b, pt: (b, 0, 0)),
        scratch_shapes=[pltpu.VMEM((2, IDX_DIM, past), F32),
                        pltpu.SemaphoreType.DMA((2,))],
    )
    return pl.pallas_call(
        functools.partial(_sample_score_kernel, n_pages=n_pages),
        grid_spec=grid_spec,
        out_shape=jax.ShapeDtypeStruct((n, 1, past + ATT_CK), F32),
        compiler_params=_cparams(("arbitrary",)),
        name="sample_scores",
    )(page_table.reshape(-1), qi3, wcol, kinew, cki2)


def _sample_attn_kernel(pt_ref, sc_ref, qbd_ref, knew_ref, vnew_ref, ck_hbm, cv_hbm, o_ref,
                        key_ref, bias_ref, kbuf, vbuf, semk, semv, *, n_pages):
    b = pl.program_id(0)
    nb = pl.num_programs(0)
    past = n_pages * PAGE_SIZE
    nch = sc_ref.shape[0]
    slot = b % 2

    def start_all(bb, sl):
        _start_page_copies(pt_ref, ck_hbm, kbuf.at[sl], semk.at[sl], bb * n_pages, n_pages)
        _start_page_copies(pt_ref, cv_hbm, vbuf.at[sl], semv.at[sl], bb * n_pages, n_pages)

    @pl.when(b == 0)
    def _():
        start_all(0, 0)
        key_ref[...] = _float_key(sc_ref[...])
        _selection_bias(key_ref, bias_ref, nch)

    @pl.when(b + 1 < nb)
    def _():
        start_all(b + 1, 1 - slot)

    pltpu.make_async_copy(kbuf.at[slot], kbuf.at[slot], semk.at[slot]).wait()
    pltpu.make_async_copy(vbuf.at[slot], vbuf.at[slot], semv.at[slot]).wait()

    qbd = qbd_ref[0]
    bias_row = jnp.concatenate([bias_ref[c, pl.ds(b, 1), :] for c in range(nch)], axis=1)
    s_parts = [jnp.dot(qbd, kbuf[slot, :, c * SAMPLE_KC:(c + 1) * SAMPLE_KC].astype(BF16),
                       preferred_element_type=F32) for c in range(past // SAMPLE_KC)]
    s_new = jnp.sum(qbd.astype(F32) * knew_ref[0], axis=1, keepdims=True)
    ln = lax.broadcasted_iota(I32, (N_HEADS, ATT_CK), 1)
    s = jnp.concatenate(s_parts + [jnp.where(ln == 0, s_new, 0.0)], axis=1)
    hrow = lax.broadcasted_iota(I32, (N_HEADS, 1), 0)
    slope2 = jnp.exp2(-8.0 * (hrow + 1).astype(F32) / N_HEADS) * LOG2E
    rel = (lax.broadcasted_iota(I32, (1, past + ATT_CK), 1) - past).astype(F32)
    s = s + slope2 * rel + bias_row
    m = jnp.max(s, axis=1, keepdims=True)
    p = jnp.exp2(s - m)
    l = jnp.sum(p, axis=1, keepdims=True)
    o_all = p[:, past:past + 1] * vnew_ref[0]
    for c in range(past // SAMPLE_KC):
        o_all = o_all + _dot_nt(p[:, c * SAMPLE_KC:(c + 1) * SAMPLE_KC].astype(BF16),
                                vbuf[slot, :, c * SAMPLE_KC:(c + 1) * SAMPLE_KC].astype(BF16))
    o_all = o_all / l
    out = jnp.zeros((N_HEADS, HEAD_DIM), F32)
    for g in range(KV_HEADS):
        own = (hrow >= g * HEADS_PER_KV) & (hrow < (g + 1) * HEADS_PER_KV)
        out = out + jnp.where(own, o_all[:, g * HEAD_DIM:(g + 1) * HEAD_DIM], 0.0)
    o_ref[0] = out


def _sample_attn(page_table, scores3, qbd, knew, vnew, ck2, cv2):
    n, n_pages = page_table.shape
    past = n_pages * PAGE_SIZE
    nch = scores3.shape[0]
    grid_spec = pltpu.PrefetchScalarGridSpec(
        num_scalar_prefetch=1,
        grid=(n,),
        in_specs=[pl.BlockSpec((nch, n, ATT_CK), lambda b, pt: (0, 0, 0)),
                  pl.BlockSpec((1, N_HEADS, KV_DIM), lambda b, pt: (b, 0, 0)),
                  pl.BlockSpec((1, 1, KV_DIM), lambda b, pt: (b, 0, 0)),
                  pl.BlockSpec((1, 1, KV_DIM), lambda b, pt: (b, 0, 0)),
                  pl.BlockSpec(memory_space=pl.ANY),
                  pl.BlockSpec(memory_space=pl.ANY)],
        out_specs=pl.BlockSpec((1, N_HEADS, HEAD_DIM), lambda b, pt: (b, 0, 0)),
        scratch_shapes=[pltpu.VMEM((nch, n, ATT_CK), I32),
                        pltpu.VMEM((nch, n, ATT_CK), F32),
                        pltpu.VMEM((2, KV_DIM, past), F32),
                        pltpu.VMEM((2, KV_DIM, past), F32),
                        pltpu.SemaphoreType.DMA((2,)),
                        pltpu.SemaphoreType.DMA((2,))],
    )
    return pl.pallas_call(
        functools.partial(_sample_attn_kernel, n_pages=n_pages),
        grid_spec=grid_spec,
        out_shape=jax.ShapeDtypeStruct((n, N_HEADS, HEAD_DIM), F32),
        compiler_params=_cparams(("arbitrary",)),
        name="sample_attn",
    )(page_table.reshape(-1), scores3, qbd, knew, vnew, ck2, cv2)


N_CT = D_MODEL // LANES
TOK_GRP = SUBLANES * N_CT


def _tile_perm(to_token_major):
    r = lax.broadcasted_iota(I32, (TOK_GRP, TOK_GRP), 1 if to_token_major else 0)
    k = lax.broadcasted_iota(I32, (TOK_GRP, TOK_GRP), 0 if to_token_major else 1)
    return jnp.where((r // SUBLANES == k % N_CT) & (r % SUBLANES == k // N_CT), 1.0, 0.0).astype(BF16)


def _to_token_major(x, perm):
    rows = x.shape[0]
    out = []
    for t in range(rows // SUBLANES):
        xs = x[t * SUBLANES:(t + 1) * SUBLANES]
        cm = jnp.concatenate([xs[:, c * LANES:(c + 1) * LANES] for c in range(N_CT)], axis=0).astype(BF16)
        out.append(jnp.dot(perm, cm, preferred_element_type=F32).astype(BF16))
    return jnp.stack(out).reshape(rows, N_CT, LANES)


def _from_token_major(xt, perm):
    rows = xt.shape[0]
    xg = xt.reshape(rows // SUBLANES, TOK_GRP, LANES)
    out = jnp.stack([jnp.dot(perm, xg[t], preferred_element_type=F32) for t in range(rows // SUBLANES)])
    return jnp.concatenate([out[:, c * SUBLANES:(c + 1) * SUBLANES, :].reshape(rows, LANES) for c in range(N_CT)],
                           axis=1)


def _merge_ln1_kernel(*refs, aliased, n_tiles):
    n_in = 12
    outs = refs[n_in + 1:] if aliased else refs[n_in:]
    xt_ref = outs[0]

    @pl.when(pl.program_id(0) < n_tiles)
    def _():
        _merge_ln1_tile(*refs[:n_in], *outs)

    @pl.when(pl.program_id(0) >= n_tiles)
    def _():
        xt_ref[...] = jnp.zeros(xt_ref.shape, BF16)


def _merge_ln1_tile(y_ref, o_ref, gc_ref, ga_ref, x_ref, wc_ref, wa_ref, wo_ref, g_ref, b_ref, wr_ref,
                    cnt_in_ref, xt_ref, x1_ref, ei_ref, gt_ref, cnt_ref):
    @pl.when(pl.program_id(0) == 0)
    def _():
        cnt_ref[...] = cnt_in_ref[...]

    conv_out = jnp.dot(y_ref[...], wc_ref[...], preferred_element_type=F32)
    attn_out = jnp.dot(o_ref[...], wa_ref[...], preferred_element_type=F32)
    merged = gc_ref[...].astype(F32) * conv_out + ga_ref[...].astype(F32) * attn_out
    mix = jnp.dot(merged.astype(BF16), wo_ref[...], preferred_element_type=F32)
    x1 = _layer_norm(DEEPNORM_ALPHA * x_ref[...] + mix, g_ref[...], b_ref[...])
    x1_ref[...] = x1
    xt_ref[...] = _to_token_major(x1, _tile_perm(True))

    x_hi = x1.astype(BF16)
    x_lo = (x1 - x_hi.astype(F32)).astype(BF16)
    logits = (jnp.dot(x_hi, wr_ref[0], preferred_element_type=F32)
              + jnp.dot(x_lo, wr_ref[0], preferred_element_type=F32)
              + jnp.dot(x_hi, wr_ref[1], preferred_element_type=F32))
    lane = lax.broadcasted_iota(I32, logits.shape, 1)
    is_g = lane < N_GROUPS
    lg = jnp.where(is_g, logits, -jnp.inf)
    eg = jnp.exp(lg - jnp.max(lg, axis=1, keepdims=True))
    pg = eg / jnp.sum(eg, axis=1, keepdims=True)
    g_top = jnp.max(pg, axis=1, keepdims=True)
    g_idx = jnp.min(jnp.where(is_g & (pg == g_top), lane, LANES), axis=1, keepdims=True)
    lo = N_GROUPS + g_idx * EXPERTS_PER_GROUP
    is_e = (lane >= lo) & (lane < lo + EXPERTS_PER_GROUP)
    le = jnp.where(is_e, logits, -jnp.inf)
    ee = jnp.exp(le - jnp.max(le, axis=1, keepdims=True))
    pe = ee / jnp.sum(ee, axis=1, keepdims=True)
    p1 = jnp.max(jnp.where(is_e, pe, -1.0), axis=1, keepdims=True)
    i1 = jnp.min(jnp.where(is_e & (pe == p1), lane, 2 * LANES), axis=1, keepdims=True)
    rest = is_e & (lane != i1)
    p2 = jnp.max(jnp.where(rest, pe, -1.0), axis=1, keepdims=True)
    i2 = jnp.min(jnp.where(rest & (pe == p2), lane, 2 * LANES), axis=1, keepdims=True)
    den = p1 + p2
    gt_ref[...] = jnp.where(lane == 0, g_top * p1 / den, jnp.where(lane == 1, g_top * p2 / den, 0.0))

    e1, e2 = i1 - N_GROUPS, i2 - N_GROUPS
    tm = logits.shape[0]
    oh1 = jnp.where(lane == e1, 1.0, 0.0)
    oh2 = jnp.where(lane == e2, 1.0, 0.0)
    earlier = jnp.where(lax.broadcasted_iota(I32, (tm, tm), 1) < lax.broadcasted_iota(I32, (tm, tm), 0),
                        1.0, 0.0).astype(BF16)
    before1 = jnp.dot(earlier, oh1.astype(BF16), preferred_element_type=F32)
    before2 = jnp.dot(earlier, oh2.astype(BF16), preferred_element_type=F32)
    tot1 = jnp.sum(oh1, axis=0, keepdims=True)
    base = cnt_ref[...]
    r1 = jnp.sum(oh1 * (base + before1), axis=1, keepdims=True)
    r2 = jnp.sum(oh2 * (base + tot1 + before2), axis=1, keepdims=True)
    cnt_ref[...] = base + tot1 + jnp.sum(oh2, axis=0, keepdims=True)
    ei_ref[...] = jnp.where(lane == 0, e1, jnp.where(lane == 1, e2, jnp.where(
        lane == 2, r1.astype(I32), jnp.where(lane == 3, r2.astype(I32), 0))))


def _merge_ln1(yact, o, gc, ga, x, wc, wa, wo, g1, b1, wr, cnt_in, tm, xt_all=None, m_total=None):
    m = x.shape[0]
    aliased = xt_all is not None
    n_tiles = m // tm
    total = xt_all.shape[0] if aliased else m_total
    blk0 = (total - m) // tm if aliased else 0
    n_steps = n_tiles if aliased else pl.cdiv(total, tm)
    row = lambda w: pl.BlockSpec((tm, w), lambda i: (jnp.minimum(i, n_tiles - 1), 0))
    res = lambda a: pl.BlockSpec(a.shape, lambda i: (0,) * a.ndim, pipeline_mode=pl.Buffered(1))
    cnt_spec = pl.BlockSpec((1, LANES), lambda i: (0, 0))
    args = [yact, o, gc, ga, x, wc, wa, wo, g1, b1, wr, cnt_in]
    in_specs = [row(C_CONV), row(N_HEADS * HEAD_DIM), row(D_MODEL), row(D_MODEL), row(D_MODEL),
                res(wc), res(wa), res(wo), res(g1), res(b1), res(wr), cnt_spec]
    if aliased:
        args.append(xt_all)
        in_specs.append(pl.BlockSpec(memory_space=pl.ANY))
    return pl.pallas_call(
        functools.partial(_merge_ln1_kernel, aliased=aliased, n_tiles=n_tiles),
        grid=(n_steps,),
        in_specs=in_specs,
        out_specs=(pl.BlockSpec((tm, N_CT, LANES), lambda i: (blk0 + i, 0, 0)), row(D_MODEL), row(LANES),
                   row(LANES), cnt_spec),
        out_shape=(jax.ShapeDtypeStruct((total, N_CT, LANES), BF16),
                   jax.ShapeDtypeStruct((m, D_MODEL), F32),
                   jax.ShapeDtypeStruct((m, LANES), I32),
                   jax.ShapeDtypeStruct((m, LANES), F32),
                   jax.ShapeDtypeStruct((1, LANES), F32)),
        input_output_aliases={len(args) - 1: 0} if aliased else {},
        compiler_params=_cparams(("arbitrary",)),
        name="merge_ln1",
    )(*args)


N_GSLOT = 3


def _experts_kernel(fb_ref, dest_ref, x_hbm, wg_ref, wu_ref, wd_ref, y_hbm,
                    rowtok, xbuf, ybuf, wgb, wub, wdb, gsem, ysem, *, n_blocks):
    e = pl.program_id(0)
    n_used = fb_ref[N_EXPERTS]
    last = jnp.maximum(n_used - 1, 0)

    def start_gather(blk, sl):
        base = blk * MOE_BLK
        for i in range(MOE_BLK):
            pltpu.make_async_copy(x_hbm.at[rowtok[base + i]], xbuf.at[sl, i], gsem.at[sl]).start()

    def wait_gather(sl):
        pltpu.make_async_copy(xbuf.at[sl], xbuf.at[sl], gsem.at[sl]).wait()

    def write_copy(blk, sl):
        r0 = pl.multiple_of(blk * MOE_BLK, MOE_BLK)
        return pltpu.make_async_copy(ybuf.at[sl], y_hbm.at[pl.ds(r0, MOE_BLK)], ysem.at[sl])

    @pl.when(e == 0)
    def _():
        def clear(r, carry):
            rowtok[r] = 0
            return carry
        lax.fori_loop(0, n_blocks * MOE_BLK, clear, 0, unroll=8)

        def place(a, carry):
            rowtok[dest_ref[a]] = lax.shift_right_logical(a, 1)
            return carry
        lax.fori_loop(0, dest_ref.shape[0], place, 0, unroll=8)
        start_gather(0, 0)
        start_gather(jnp.minimum(1, last), 1)

    b0, b1 = fb_ref[e], fb_ref[e + 1]

    @pl.when(b1 > b0)
    def _():
        wgb[...] = wg_ref[0].astype(BF16)
        wub[...] = wu_ref[0].astype(BF16)
        wdb[...] = wd_ref[0].astype(BF16)

    def block(b, carry):
        sl = lax.rem(b, N_GSLOT)
        ysl = lax.rem(b, 2)

        @pl.when(b >= 2)
        def _():
            write_copy(b - 2, ysl).wait()

        wait_gather(sl)
        start_gather(jnp.minimum(b + 2, last), lax.rem(b + 2, N_GSLOT))
        xb = _from_token_major(xbuf[sl], _tile_perm(False)).astype(BF16)
        hg = jnp.dot(xb, wgb[...], preferred_element_type=F32)
        hu = jnp.dot(xb, wub[...], preferred_element_type=F32)
        h = (hg * jax.nn.sigmoid(hg) * hu).astype(BF16)
        y = jnp.dot(h, wdb[...], preferred_element_type=F32)
        ybuf[ysl] = _to_token_major(y, _tile_perm(True))
        write_copy(b, ysl).start()
        return carry

    lax.fori_loop(b0, b1, block, 0)

    @pl.when(e == pl.num_programs(0) - 1)
    def _():
        wait_gather(lax.rem(last + 1, N_GSLOT))
        wait_gather(lax.rem(last + 2, N_GSLOT))

        @pl.when(n_used >= 1)
        def _():
            write_copy(last, lax.rem(last, 2)).wait()

        @pl.when(n_used >= 2)
        def _():
            write_copy(last - 1, lax.rem(last - 1, 2)).wait()

        ybuf[0] = jnp.zeros(ybuf.shape[1:], BF16)

        def fill(blk, carry):
            write_copy(blk, 0).start()
            return carry
        lax.fori_loop(n_used, n_blocks, fill, 0)

        def fill_wait(blk, carry):
            write_copy(blk, 0).wait()
            return carry
        lax.fori_loop(n_used, n_blocks, fill_wait, 0)


def _experts(first_blk, dest, xt_all, wg, wu, wd):
    n_blocks = (dest.shape[0] + N_EXPERTS * (MOE_BLK - 1) + MOE_BLK - 1) // MOE_BLK
    n_rows = n_blocks * MOE_BLK
    grid_spec = pltpu.PrefetchScalarGridSpec(
        num_scalar_prefetch=2,
        grid=(N_EXPERTS,),
        in_specs=[pl.BlockSpec(memory_space=pl.ANY),
                  pl.BlockSpec((1, D_MODEL, D_EXPERT), lambda e, fb, de: (e, 0, 0)),
                  pl.BlockSpec((1, D_MODEL, D_EXPERT), lambda e, fb, de: (e, 0, 0)),
                  pl.BlockSpec((1, D_EXPERT, D_MODEL), lambda e, fb, de: (e, 0, 0))],
        out_specs=pl.BlockSpec(memory_space=pl.ANY),
        scratch_shapes=[pltpu.SMEM((n_rows,), I32),
                        pltpu.VMEM((N_GSLOT, MOE_BLK, N_CT, LANES), BF16),
                        pltpu.VMEM((2, MOE_BLK, N_CT, LANES), BF16),
                        pltpu.VMEM((D_MODEL, D_EXPERT), BF16),
                        pltpu.VMEM((D_MODEL, D_EXPERT), BF16),
                        pltpu.VMEM((D_EXPERT, D_MODEL), BF16),
                        pltpu.SemaphoreType.DMA((N_GSLOT,)),
                        pltpu.SemaphoreType.DMA((2,))],
    )
    return pl.pallas_call(
        functools.partial(_experts_kernel, n_blocks=n_blocks),
        grid_spec=grid_spec,
        out_shape=jax.ShapeDtypeStruct((n_rows, N_CT, LANES), BF16),
        compiler_params=_cparams(("arbitrary",)),
        name="experts",
    )(first_blk, dest, xt_all, wg, wu, wd)


def _combine_ln2_kernel(dest_ref, x1_ref, gt_ref, yb_hbm, g_ref, b_ref, y_ref, ybuf, sem, *, tm, tok0):
    i = pl.program_id(0)
    last = pl.num_programs(0) - 1
    slot = lax.rem(i, 2)

    def row_copy(a, sl, k, r):
        return pltpu.make_async_copy(yb_hbm.at[dest_ref[a]], ybuf.at[sl, k, r], sem.at[sl])

    def start_gather(tile, sl, unrolled):
        base = 2 * (tok0 + tile * tm)
        if unrolled:
            for r in range(tm):
                row_copy(base + 2 * r, sl, 0, r).start()
                row_copy(base + 2 * r + 1, sl, 1, r).start()
        else:
            def body(r, carry):
                row_copy(base + 2 * r, sl, 0, r).start()
                row_copy(base + 2 * r + 1, sl, 1, r).start()
                return carry
            lax.fori_loop(0, tm, body, 0)

    def wait_gather(sl):
        pltpu.make_async_copy(ybuf.at[sl], ybuf.at[sl], sem.at[sl]).wait()

    @pl.when(i == 0)
    def _():
        start_gather(0, 0, unrolled=False)

    wait_gather(slot)
    start_gather(jnp.minimum(i + 1, last), 1 - slot, unrolled=True)
    gt = gt_ref[...]
    perm = _tile_perm(False)
    z = (DEEPNORM_ALPHA * x1_ref[...] + gt[:, 0:1] * _from_token_major(ybuf[slot, 0], perm)
         + gt[:, 1:2] * _from_token_major(ybuf[slot, 1], perm))
    y_ref[...] = _layer_norm(z, g_ref[...], b_ref[...])

    @pl.when(i == last)
    def _():
        wait_gather(1 - slot)


def _combine_ln2(dest, x1, gt, yb, g2, b2, tm, tok0):
    m = x1.shape[0]
    grid_spec = pltpu.PrefetchScalarGridSpec(
        num_scalar_prefetch=1,
        grid=(m // tm,),
        in_specs=[pl.BlockSpec((tm, D_MODEL), lambda i, d: (i, 0)),
                  pl.BlockSpec((tm, LANES), lambda i, d: (i, 0)),
                  pl.BlockSpec(memory_space=pl.ANY),
                  pl.BlockSpec((1, D_MODEL), lambda i, d: (0, 0)),
                  pl.BlockSpec((1, D_MODEL), lambda i, d: (0, 0))],
        out_specs=pl.BlockSpec((tm, D_MODEL), lambda i, d: (i, 0)),
        scratch_shapes=[pltpu.VMEM((2, 2, tm, N_CT, LANES), BF16),
                        pltpu.SemaphoreType.DMA((2,))],
    )
    return pl.pallas_call(
        functools.partial(_combine_ln2_kernel, tm=tm, tok0=tok0),
        grid_spec=grid_spec,
        out_shape=jax.ShapeDtypeStruct((m, D_MODEL), F32),
        compiler_params=_cparams(("arbitrary",)),
        name="combine_ln2",
    )(dest, x1, gt, yb, g2, b2)


def _dispatch_tables(eid, rank, counts):
    padded = (counts + MOE_BLK - 1) // MOE_BLK * MOE_BLK
    pad_end = jnp.cumsum(padded)
    pad_start = pad_end - padded
    start_of = jnp.sum(jnp.where(eid[..., None] == jnp.arange(N_EXPERTS, dtype=I32), pad_start, 0), axis=-1)
    dest = (start_of + rank).reshape(-1).astype(I32)
    first_blk = (jnp.concatenate([pad_start, pad_end[-1:]]) // MOE_BLK).astype(I32)
    return dest, first_blk


def kernel(x_prompt, x_sample, cache_k, cache_v, cache_kidx, state_conv, page_table, w_in, conv_dw_w, conv_dw_b,
           conv_ln_g, conv_ln_b, w_conv_out, w_attn_out, w_out, ln1_g, ln1_b, w_router_group, w_router_expert,
           w_expert_gate, w_expert_up, w_expert_down, ln2_g, ln2_b):
    assert w_in.shape[0] == 1, "single-layer trunk"
    nb, t, d = x_prompt.shape
    ns = x_sample.shape[0]
    n_pool = cache_k.shape[1]
    mp = nb * t
    row2 = lambda a: a.reshape(1, -1)

    w_arr = _arrange_w_in(w_in[0])
    dww, dwb, lng, lnb = conv_dw_w[0], row2(conv_dw_b[0]), row2(conv_ln_g[0]), row2(conv_ln_b[0])
    wc, wa, wo = w_conv_out[0].astype(BF16), w_attn_out[0].astype(BF16), w_out[0].astype(BF16)
    g1, b1, g2, b2 = row2(ln1_g[0]), row2(ln1_b[0]), row2(ln2_g[0]), row2(ln2_b[0])
    wr32 = jnp.concatenate([w_router_group[0], w_router_expert[0],
                            jnp.zeros((d, LANES - N_GROUPS - N_EXPERTS), F32)], axis=1)
    wr_hi = wr32.astype(BF16)
    wr = jnp.stack([wr_hi, (wr32 - wr_hi.astype(F32)).astype(BF16)])

    xp = x_prompt.reshape(mp, d)
    u_p, q_p, qi_p, kv_p, kvb_p, gc_p, ga_p, kiwi_p, kib_p, yact_p = _in_proj(
        xp, w_arr, tm=512, conv=(dww, dwb, lng, lnb, t))
    o_p = _attn_prompt(qi_p.reshape(nb, t, -1), kiwi_p.reshape(nb, t, LANES), kib_p.reshape(nb, t, LANES),
                       q_p.reshape(nb, t, -1), kvb_p.reshape(nb, t, 2 * KV_DIM))
    xt_all, x1_p, ei_p, gt_p, cnt_p = _merge_ln1(yact_p, o_p.reshape(mp, -1), gc_p, ga_p, xp,
                                                 wc, wa, wo, g1, b1, wr, jnp.zeros((1, LANES), F32), tm=256,
                                                 m_total=mp + ns)

    xs = x_sample.reshape(ns, d)
    u_s, q_s, qi_s, kv_s, _, gc_s, ga_s, kiwi_s, _ = _in_proj(xs, w_arr, tm=ns)
    yact_s = _conv_sample(state_conv[0].transpose(1, 0, 2), u_s, dww, dwb, lng, lnb)
    cki2 = cache_kidx.transpose(0, 1, 3, 2).reshape(n_pool, IDX_DIM, PAGE_SIZE)
    ck2 = cache_k.transpose(0, 1, 3, 4, 2).reshape(n_pool, KV_DIM, PAGE_SIZE)
    cv2 = cache_v.transpose(0, 1, 3, 4, 2).reshape(n_pool, KV_DIM, PAGE_SIZE)
    scores_s = _sample_scores(page_table, qi_s.reshape(ns, IDX_HEADS, IDX_DIM),
                              kiwi_s[:, IDX_DIM:IDX_DIM + IDX_HEADS].reshape(ns, IDX_HEADS, 1),
                              kiwi_s[:, :IDX_DIM].reshape(ns, 1, IDX_DIM), cki2)
    scores3 = scores_s.reshape(ns, -1, ATT_CK).transpose(1, 0, 2)
    own_group = (np.arange(N_HEADS)[:, None] // HEADS_PER_KV) == (np.arange(KV_DIM)[None, :] // HEAD_DIM)
    qbd = jnp.where(own_group[None], jnp.tile(q_s.reshape(ns, N_HEADS, HEAD_DIM), (1, 1, KV_HEADS)), 0).astype(BF16)
    o3 = _sample_attn(page_table, scores3, qbd, kv_s[:, :KV_DIM].reshape(ns, 1, KV_DIM),
                      kv_s[:, KV_DIM:].reshape(ns, 1, KV_DIM), ck2, cv2)
    o_s = o3.reshape(ns, N_HEADS * HEAD_DIM).astype(BF16)
    xt_all, x1_s, ei_s, gt_s, cnt = _merge_ln1(yact_s, o_s, gc_s, ga_s, xs, wc, wa, wo, g1, b1, wr, cnt_p, tm=ns,
                                               xt_all=xt_all)

    eid_rank = jnp.concatenate([ei_p[:, :4], ei_s[:, :4]], axis=0)
    dest, first_blk = _dispatch_tables(eid_rank[:, :2], eid_rank[:, 2:], cnt[0, :N_EXPERTS].astype(I32))
    yb = _experts(first_blk, dest, xt_all, w_expert_gate[0], w_expert_up[0], w_expert_down[0])
    y_p = _combine_ln2(dest, x1_p, gt_p, yb, g2, b2, tm=256, tok0=0)
    y_s = _combine_ln2(dest, x1_s, gt_s, yb, g2, b2, tm=ns, tok0=mp)

    y_prompt = y_p.reshape(nb, t, d)
    y_sample = y_s.reshape(ns, 1, d)
    k_prompt = kv_p[:, :KV_DIM].reshape(1, nb, t, KV_HEADS, HEAD_DIM)
    v_prompt = kv_p[:, KV_DIM:].reshape(1, nb, t, KV_HEADS, HEAD_DIM)
    kidx_prompt = kiwi_p[:, :IDX_DIM].reshape(1, nb, t, IDX_DIM)
    conv_prompt = u_p.reshape(nb, t, C_CONV)[:, t - (CONV_W - 1):][None]
    k_sample = kv_s[:, :KV_DIM].reshape(1, ns, 1, KV_HEADS, HEAD_DIM)
    v_sample = kv_s[:, KV_DIM:].reshape(1, ns, 1, KV_HEADS, HEAD_DIM)
    kidx_sample = kiwi_s[:, :IDX_DIM].reshape(1, ns, 1, IDX_DIM)
    conv_sample = jnp.concatenate([state_conv[0][:, 1:], u_s[:, None, :]], axis=1)[None]
    return (y_prompt, y_sample, k_prompt, v_prompt, kidx_prompt, conv_prompt, k_sample, v_sample, kidx_sample,
            conv_sample)
```

```python
import functools

import jax
import jax.numpy as jnp
import numpy as np
from jax import lax
from jax.experimental import pallas as pl
from jax.experimental.pallas import tpu as pltpu

F32 = jnp.float32
BF16 = jnp.bfloat16
I32 = jnp.int32

D_MODEL = 2048
C_CONV = 1024
CONV_W = 31
N_HEADS = 16
HEAD_DIM = 64
KV_HEADS = 4
HEADS_PER_KV = N_HEADS // KV_HEADS
KV_DIM = KV_HEADS * HEAD_DIM
IDX_HEADS = 16
IDX_DIM = 64
MAX_TOPK = 256
N_GROUPS = 8
EXPERTS_PER_GROUP = 8
N_EXPERTS = 64
D_EXPERT = 512
MOE_BLK = 128
PAGE_SIZE = 128
LN_EPS = 1e-5
DEEPNORM_ALPHA = 2.0 ** 0.25
NEG_INF = -1e30
INT_MIN = -(2 ** 31)
NEG_KEY = int(INT_MIN - int(np.array(NEG_INF, np.float32).view(np.int32)))
LOG2E = 1.4426950408889634
Q_SCALE = HEAD_DIM ** -0.5 * LOG2E
IDX_SCALE = IDX_HEADS ** -0.5 * IDX_DIM ** -0.5

LANES = 128
SUBLANES = 8
HALF = LANES // 2
IN_TN = 1024
VMEM_LIMIT = 56 * 1024 * 1024


def _cparams(sem, vmem=VMEM_LIMIT):
    return pltpu.CompilerParams(dimension_semantics=sem, vmem_limit_bytes=vmem)


def _float_key(x):
    b = pltpu.bitcast(x, I32)
    return jnp.where(b < 0, INT_MIN - b, b)


def _layer_norm(z, g, b):
    mu = jnp.mean(z, axis=-1, keepdims=True)
    zc = z - mu
    var = jnp.mean(zc * zc, axis=-1, keepdims=True)
    return zc * lax.rsqrt(var + LN_EPS) * g + b


def _dot_nt(a, b):
    return lax.dot_general(a, b, (((1,), (1,)), ((), ())), preferred_element_type=F32)


def _alibi_slope(h):
    return float(2.0 ** (-8.0 * (h + 1) / N_HEADS))


_T_GLU, _T_Q, _T_QI, _T_KV, _T_GC, _T_GA, _N_TILES = 0, 2, 3, 4, 5, 7, 9
KV_TILE_COLS = 2 * KV_DIM + 2 * LANES
GLU_HALF = IN_TN // 2
CONV_HALO = 32
CONV_STRIP = 32
CONV_ROWS = 256


def _arrange_w_in(w):
    sizes = (C_CONV, C_CONV, N_HEADS * HEAD_DIM, KV_DIM, KV_DIM,
             IDX_HEADS * IDX_DIM, IDX_DIM, IDX_HEADS, D_MODEL, D_MODEL)
    offs = np.cumsum((0,) + sizes)
    a, g, q, k, v, qi, ki, wi, gc, ga = [w[:, offs[i]:offs[i + 1]] for i in range(10)]
    glu = [jnp.concatenate([a[:, t * GLU_HALF:(t + 1) * GLU_HALF], g[:, t * GLU_HALF:(t + 1) * GLU_HALF]], axis=1)
           for t in range(C_CONV // GLU_HALF)]
    zeros = lambda n: jnp.zeros((w.shape[0], n), w.dtype)
    kv_tile = jnp.concatenate([k, v, ki, wi, zeros(LANES - IDX_DIM - IDX_HEADS), ki, ki,
                               zeros(IN_TN - KV_TILE_COLS)], axis=1)
    return jnp.concatenate(glu + [q, qi, kv_tile, gc, ga], axis=1).astype(BF16)


def _conv_half(jj, new_seq, ubuf, sh_ref, ypre, dww_ref, dwb_ref):
    tm = ypre.shape[0]
    cs = slice(jj * GLU_HALF, (jj + 1) * GLU_HALF)
    ubuf[jj, 0:CONV_HALO, :] = jnp.where(new_seq, 0.0, ubuf[jj, 0:CONV_HALO, :])
    first = CONV_HALO - (CONV_W - 1)
    dwb = dwb_ref[:, cs]
    for rb in range(tm // CONV_ROWS):
        for b in range(SUBLANES):
            rows = CONV_ROWS + SUBLANES * ((CONV_W - 1 - b) // SUBLANES)
            src0 = rb * CONV_ROWS + first + b
            sh_ref[b, 0:rows, :] = ubuf[jj, src0:src0 + rows, :]
        for r in range(CONV_ROWS // CONV_STRIP):
            acc = jnp.broadcast_to(dwb, (CONV_STRIP, GLU_HALF))
            for j in range(CONV_W):
                a, b = divmod(j, SUBLANES)
                r0 = r * CONV_STRIP + a * SUBLANES
                acc = acc + dww_ref[j:j + 1, cs] * sh_ref[b, r0:r0 + CONV_STRIP, :]
            o0 = rb * CONV_ROWS + r * CONV_STRIP
            ypre[o0:o0 + CONV_STRIP, cs] = acc
    ubuf[jj, 0:CONV_HALO, :] = ubuf[jj, tm:tm + CONV_HALO, :]


def _in_proj_kernel(*refs, fuse_conv, tiles_per_seq):
    if fuse_conv:
        (x_ref, w_ref, dww_ref, dwb_ref, lng_ref, lnb_ref,
         u_ref, q_ref, qi_ref, kv_ref, kvb_ref, gc_ref, ga_ref, kiwi_ref, kib_ref, y_ref,
         xb_ref, ubuf, sh_ref, ypre) = refs
    else:
        (x_ref, w_ref, u_ref, q_ref, qi_ref, kv_ref, kvb_ref, gc_ref, ga_ref, kiwi_ref, kib_ref, xb_ref) = refs
    i = pl.program_id(0)
    j = pl.program_id(1)

    @pl.when(j == 0)
    def _():
        xb_ref[...] = x_ref[...].astype(BF16)

    if fuse_conv:
        new_seq = lax.rem(i, tiles_per_seq) == 0

        @pl.when((i == 0) & (j == 0))
        def _():
            ubuf[:, 0:CONV_HALO, :] = jnp.zeros((ubuf.shape[0], CONV_HALO, GLU_HALF), F32)

    def mm(ncols=IN_TN):
        return jnp.dot(xb_ref[...], w_ref[:, :ncols], preferred_element_type=F32)

    for jj in range(2):
        @pl.when(j == jj)
        def _(jj=jj):
            if fuse_conv and jj == 1:
                _conv_half(0, new_seq, ubuf, sh_ref, ypre, dww_ref, dwb_ref)
            acc = mm()
            u = acc[:, :GLU_HALF] * jax.nn.sigmoid(acc[:, GLU_HALF:])
            u_ref[...] = u
            if fuse_conv:
                ubuf[jj, CONV_HALO:, :] = u

    @pl.when(j == _T_Q)
    def _():
        if fuse_conv:
            _conv_half(1, new_seq, ubuf, sh_ref, ypre, dww_ref, dwb_ref)
        q_ref[...] = (mm() * Q_SCALE).astype(BF16)

    @pl.when(j == _T_QI)
    def _():
        if fuse_conv:
            lng, lnb = lng_ref[...], lnb_ref[...]
            for r in range(ypre.shape[0] // CONV_STRIP):
                rs = slice(r * CONV_STRIP, (r + 1) * CONV_STRIP)
                yn = _layer_norm(ypre[rs, :], lng, lnb)
                y_ref[rs, :] = (yn * jax.nn.sigmoid(yn)).astype(BF16)
        qi_ref[...] = mm().astype(BF16)

    @pl.when(j == _T_KV)
    def _():
        acc = mm(KV_TILE_COLS)
        kv_ref[...] = acc[:, :2 * KV_DIM]
        kvb_ref[...] = acc[:, :2 * KV_DIM].astype(BF16)
        kiwi_ref[...] = acc[:, 2 * KV_DIM:2 * KV_DIM + LANES]
        kib_ref[...] = acc[:, 2 * KV_DIM + LANES:].astype(BF16)

    @pl.when((j >= _T_GC) & (j < _T_GA))
    def _():
        gc_ref[...] = jax.nn.sigmoid(mm()).astype(BF16)

    @pl.when(j >= _T_GA)
    def _():
        ga_ref[...] = jax.nn.sigmoid(mm()).astype(BF16)


def _in_proj(x, w_arr, tm, conv=None):
    m = x.shape[0]
    fuse_conv = conv is not None

    def cl(lo, n):
        return lambda i, j: (i, jnp.clip(j - lo, 0, n - 1))

    row_blk = lambda w: pl.BlockSpec((tm, w), lambda i, j: (i, 0))
    out_shape = [
        jax.ShapeDtypeStruct((m, C_CONV), F32),
        jax.ShapeDtypeStruct((m, N_HEADS * HEAD_DIM), BF16),
        jax.ShapeDtypeStruct((m, IDX_HEADS * IDX_DIM), BF16),
        jax.ShapeDtypeStruct((m, 2 * KV_DIM), F32),
        jax.ShapeDtypeStruct((m, 2 * KV_DIM), BF16),
        jax.ShapeDtypeStruct((m, D_MODEL), BF16),
        jax.ShapeDtypeStruct((m, D_MODEL), BF16),
        jax.ShapeDtypeStruct((m, LANES), F32),
        jax.ShapeDtypeStruct((m, LANES), BF16),
    ]
    out_specs = [
        pl.BlockSpec((tm, GLU_HALF), cl(_T_GLU, 2)),
        row_blk(IN_TN), row_blk(IN_TN), row_blk(2 * KV_DIM), row_blk(2 * KV_DIM),
        pl.BlockSpec((tm, IN_TN), cl(_T_GC, 2)),
        pl.BlockSpec((tm, IN_TN), cl(_T_GA, 2)),
        row_blk(LANES), row_blk(LANES),
    ]
    args = [x, w_arr]
    in_specs = [pl.BlockSpec((tm, D_MODEL), lambda i, j: (i, 0)),
                pl.BlockSpec((D_MODEL, IN_TN), lambda i, j: (0, j))]
    scratch = [pltpu.VMEM((tm, D_MODEL), BF16)]
    tiles_per_seq = 1
    if fuse_conv:
        dww, dwb, lng, lnb, seq = conv
        assert seq % tm == 0 and tm % CONV_ROWS == 0
        tiles_per_seq = seq // tm
        const = lambda a: pl.BlockSpec(a.shape, lambda i, j: (0, 0))
        args += [dww, dwb, lng, lnb]
        in_specs += [const(dww), const(dwb), const(lng), const(lnb)]
        out_shape.append(jax.ShapeDtypeStruct((m, C_CONV), BF16))
        out_specs.append(row_blk(C_CONV))
        scratch += [pltpu.VMEM((2, CONV_HALO + tm, GLU_HALF), F32),
                    pltpu.VMEM((SUBLANES, CONV_ROWS + SUBLANES * ((CONV_W - 1) // SUBLANES), GLU_HALF), F32),
                    pltpu.VMEM((tm, C_CONV), F32)]
    return pl.pallas_call(
        functools.partial(_in_proj_kernel, fuse_conv=fuse_conv, tiles_per_seq=tiles_per_seq),
        grid=(m // tm, _N_TILES),
        in_specs=in_specs,
        out_specs=tuple(out_specs),
        out_shape=tuple(out_shape),
        scratch_shapes=scratch,
        compiler_params=_cparams(("arbitrary", "arbitrary")),
        name="in_proj",
    )(*args)


def _conv_sample_kernel(state_ref, u_ref, dww_ref, dwb_ref, lng_ref, lnb_ref, y_ref):
    acc = dwb_ref[...] + dww_ref[CONV_W - 1:CONV_W, :] * u_ref[...]
    for j in range(CONV_W - 1):
        acc = acc + dww_ref[j:j + 1, :] * state_ref[j]
    yn = _layer_norm(acc, lng_ref[...], lnb_ref[...])
    y_ref[...] = (yn * jax.nn.sigmoid(yn)).astype(BF16)


def _conv_sample(state_t, u, dww, dwb, lng, lnb):
    n = u.shape[0]
    return pl.pallas_call(
        _conv_sample_kernel,
        out_shape=jax.ShapeDtypeStruct((n, C_CONV), BF16),
        compiler_params=pltpu.CompilerParams(vmem_limit_bytes=VMEM_LIMIT),
        name="conv_sample",
    )(state_t, u, dww, dwb, lng, lnb)


ATT_TQ = 256
ATT_CK = 256


def _count_ge(key_ref, nch, cand):
    def body(c, acc):
        m = jnp.where(key_ref[c] >= cand, 1.0, 0.0)
        return acc + m[:, :LANES] + m[:, LANES:]
    acc = lax.fori_loop(0, nch, body, jnp.zeros((cand.shape[0], LANES), F32))
    return jnp.sum(acc, axis=1, keepdims=True)


def _select_threshold(key_ref, nch, rows):
    def body(i, prefix):
        cand = prefix + jnp.left_shift(jnp.int32(1), 31 - i)
        return jnp.where(_count_ge(key_ref, nch, cand) >= float(MAX_TOPK), cand, prefix)
    return lax.fori_loop(0, 32, body, jnp.full((rows, 1), INT_MIN, I32))


def _selection_bias(key_ref, bias_ref, nch):
    rows, ck = key_ref.shape[1:]
    thr = _select_threshold(key_ref, nch, rows)
    n_tie_take = float(MAX_TOPK) - _count_ge(key_ref, nch, thr + 1)
    tri = jnp.where(lax.broadcasted_iota(I32, (ck, ck), 0) <= lax.broadcasted_iota(I32, (ck, ck), 1),
                    1.0, 0.0).astype(BF16)

    def bias_chunk(c, seen):
        key = key_ref[c]
        eq = key == thr
        eqf = jnp.where(eq, 1.0, 0.0)
        incl = jnp.dot(eqf.astype(BF16), tri, preferred_element_type=F32)
        sel = (key > thr) | (eq & (seen + incl - eqf < n_tie_take))
        bias_ref[c] = jnp.where(sel & (key > NEG_KEY), 0.0, NEG_INF)
        return seen + incl[:, ck - 1:ck]

    lax.fori_loop(0, nch, bias_chunk, jnp.zeros((rows, 1), F32))


def _count_ge_t(key_ref, nch, cand):
    ck, nq = key_ref.shape[1:]

    def body(c, acc):
        m = jnp.where(key_ref[c] >= cand, 1.0, 0.0)
        return acc + jnp.sum(m.reshape(ck // SUBLANES, SUBLANES, nq), axis=0)

    acc = lax.fori_loop(0, nch, body, jnp.zeros((SUBLANES, nq), F32))
    return jnp.sum(acc, axis=0, keepdims=True)


def _selection_bias_t(key_ref, bias_ref, nch):
    ck, nq = key_ref.shape[1:]

    def bit(i, prefix):
        cand = prefix + jnp.left_shift(jnp.int32(1), 31 - i)
        return jnp.where(_count_ge_t(key_ref, nch, cand) >= float(MAX_TOPK), cand, prefix)

    thr = lax.fori_loop(0, 32, bit, jnp.full((1, nq), INT_MIN, I32))
    n_tie_take = float(MAX_TOPK) - _count_ge_t(key_ref, nch, thr + 1)
    tri = jnp.where(lax.broadcasted_iota(I32, (ck, ck), 1) <= lax.broadcasted_iota(I32, (ck, ck), 0),
                    1.0, 0.0).astype(BF16)

    def bias_chunk(c, seen):
        key = key_ref[c]
        eq = key == thr
        eqf = jnp.where(eq, 1.0, 0.0)
        incl = jnp.dot(tri, eqf.astype(BF16), preferred_element_type=F32)
        sel = (key > thr) | (eq & (seen + incl - eqf < n_tie_take))
        bias_ref[c] = jnp.where(sel & (key > NEG_KEY), 0.0, NEG_INF).T
        return seen + incl[ck - 1:ck, :]

    lax.fori_loop(0, nch, bias_chunk, jnp.zeros((1, nq), F32))


def _swap_halves(x):
    return jnp.concatenate([x[:, HALF:], x[:, :HALF]], axis=1)


def _attn_prompt_kernel(qi_ref, kiwi_ref, kib_ref, q_ref, kvb_ref, o_ref,
                        key_ref, bias_ref, m_ref, acc_ref):
    tq, ck = ATT_TQ, ATT_CK
    qb = pl.program_id(1)
    nch = qb + 1
    lo_k = lax.broadcasted_iota(I32, (ck, LANES), 1) < HALF
    lo_q = lax.broadcasted_iota(I32, (tq, LANES), 1) < HALF

    key_pos = lax.broadcasted_iota(I32, (ck, tq), 0)
    q_pos = qb * tq + lax.broadcasted_iota(I32, (ck, tq), 1)
    w_t = jnp.transpose(kiwi_ref[0])[IDX_DIM:IDX_DIM + IDX_HEADS, :] * IDX_SCALE

    def score_chunk(c, carry):
        k0 = pl.multiple_of(c * ck, ck)
        kk = kib_ref[0, pl.ds(k0, ck), :]
        zero = jnp.zeros_like(kk)
        k_lo, k_hi = jnp.where(lo_k, kk, zero), jnp.where(lo_k, zero, kk)
        acc = jnp.zeros((ck, tq), F32)
        for p in range(IDX_HEADS // 2):
            qp = qi_ref[0, :, p * LANES:(p + 1) * LANES]
            acc = acc + w_t[2 * p:2 * p + 1, :] * jnp.maximum(_dot_nt(k_lo, qp), 0.0)
            acc = acc + w_t[2 * p + 1:2 * p + 2, :] * jnp.maximum(_dot_nt(k_hi, qp), 0.0)
        acc = jnp.where(c * ck + key_pos <= q_pos, acc, NEG_INF)
        key_ref[c] = _float_key(acc)
        return carry

    lax.fori_loop(0, nch, score_chunk, 0)
    _selection_bias_t(key_ref, bias_ref, nch)

    m_ref[...] = jnp.full(m_ref.shape, -jnp.inf, F32)
    acc_ref[...] = jnp.zeros(acc_ref.shape, F32)

    def attn_chunk(c, carry):
        k0 = pl.multiple_of(c * ck, ck)
        kv = kvb_ref[0, pl.ds(k0, ck), :]
        bias = bias_ref[c]
        rel = (c * ck - (qb + 1) * tq + 1 + lax.broadcasted_iota(I32, (1, ck), 1)).astype(F32)
        for j in range(KV_HEADS // 2):
            kt = kv[:, j * LANES:(j + 1) * LANES]
            vt = kv[:, KV_DIM + j * LANES:KV_DIM + (j + 1) * LANES]
            kts, vts = _swap_halves(kt), _swap_halves(vt)
            zero, one = jnp.zeros_like(kt), jnp.ones_like(kt)
            for gg in range(2):
                g = 2 * j + gg
                k_own, k_swp = (kt, kts) if gg == 0 else (kts, kt)
                v_own, v_swp = (vt, vts) if gg == 0 else (vts, vt)
                k_lo, k_hi = jnp.where(lo_k, k_own, zero), jnp.where(lo_k, zero, k_swp)
                v_lo, v_hi = jnp.where(lo_k, v_own, one), jnp.where(lo_k, one, v_swp)
                for mm in range(HEADS_PER_KV // 2):
                    pidx = (HEADS_PER_KV // 2) * g + mm
                    qp = q_ref[0, :, pidx * LANES:(pidx + 1) * LANES]
                    for half, (kmat, vmat) in enumerate(((k_lo, v_lo), (k_hi, v_hi))):
                        h = 2 * pidx + half
                        s = _dot_nt(qp, kmat) + (_alibi_slope(h) * LOG2E) * rel + bias
                        m_old = m_ref[h]
                        m_new = jnp.maximum(m_old, jnp.max(s, axis=1, keepdims=True))
                        p = jnp.exp2(s - jnp.concatenate([m_new, m_new], axis=1))
                        m_ref[h] = m_new
                        acc_ref[h] = (jnp.exp2(m_old - m_new) * acc_ref[h]
                                      + jnp.dot(p.astype(BF16), vmat, preferred_element_type=F32))
        return carry

    lax.fori_loop(0, nch, attn_chunk, 0)
    for pidx in range(N_HEADS // 2):
        a_e, a_o = acc_ref[2 * pidx], acc_ref[2 * pidx + 1]
        o_pair = jnp.where(lo_q, a_e / pltpu.roll(a_e, HALF, 1), a_o / pltpu.roll(a_o, HALF, 1))
        o_ref[0, :, pidx * LANES:(pidx + 1) * LANES] = o_pair.astype(BF16)


def _attn_prompt(qi, kiwi, kib, q, kvb):
    n, t, _ = q.shape
    tq = ATT_TQ
    return pl.pallas_call(
        _attn_prompt_kernel,
        grid=(n, t // tq),
        in_specs=[pl.BlockSpec((1, tq, IDX_HEADS * IDX_DIM), lambda b, i: (b, i, 0)),
                  pl.BlockSpec((1, tq, LANES), lambda b, i: (b, i, 0)),
                  pl.BlockSpec((1, t, LANES), lambda b, i: (b, 0, 0)),
                  pl.BlockSpec((1, tq, N_HEADS * HEAD_DIM), lambda b, i: (b, i, 0)),
                  pl.BlockSpec((1, t, 2 * KV_DIM), lambda b, i: (b, 0, 0))],
        out_specs=pl.BlockSpec((1, tq, N_HEADS * HEAD_DIM), lambda b, i: (b, i, 0)),
        out_shape=jax.ShapeDtypeStruct((n, t, N_HEADS * HEAD_DIM), BF16),
        scratch_shapes=[pltpu.VMEM((t // ATT_CK, tq, ATT_CK), I32),
                        pltpu.VMEM((t // ATT_CK, tq, ATT_CK), F32),
                        pltpu.VMEM((N_HEADS, tq, LANES), F32),
                        pltpu.VMEM((N_HEADS, tq, LANES), F32)],
        compiler_params=_cparams(("arbitrary", "arbitrary")),
        name="attn_prompt",
    )(qi, kiwi, kib, q, kvb)


SAMPLE_KC = 2048


def _start_page_copies(pt_ref, src_hbm, dst, sem, base, n_pages):
    def body(p, carry):
        col = pl.multiple_of(p * PAGE_SIZE, PAGE_SIZE)
        pltpu.make_async_copy(src_hbm.at[pt_ref[base + p]], dst.at[:, pl.ds(col, PAGE_SIZE)], sem).start()
        return carry
    lax.fori_loop(0, n_pages, body, 0)


def _sample_score_kernel(pt_ref, qi_ref, wcol_ref, kinew_ref, cki_hbm, sc_ref, kibuf, sem, *, n_pages):
    b = pl.program_id(0)
    nb = pl.num_programs(0)
    past = n_pages * PAGE_SIZE
    slot = b % 2

    @pl.when(b == 0)
    def _():
        _start_page_copies(pt_ref, cki_hbm, kibuf.at[0], sem.at[0], 0, n_pages)

    @pl.when(b + 1 < nb)
    def _():
        _start_page_copies(pt_ref, cki_hbm, kibuf.at[1 - slot], sem.at[1 - slot], (b + 1) * n_pages, n_pages)

    pltpu.make_async_copy(kibuf.at[slot], kibuf.at[slot], sem.at[slot]).wait()

    qi = qi_ref[0]
    wsc = wcol_ref[0] * IDX_SCALE
    for c in range(past // SAMPLE_KC):
        kc = kibuf[slot, :, c * SAMPLE_KC:(c + 1) * SAMPLE_KC].astype(BF16)
        d = jnp.dot(qi, kc, preferred_element_type=F32)
        sc_ref[0, :, c * SAMPLE_KC:(c + 1) * SAMPLE_KC] = jnp.sum(wsc * jnp.maximum(d, 0.0), axis=0, keepdims=True)
    dn = jnp.sum(qi.astype(F32) * kinew_ref[0], axis=1, keepdims=True)
    s_new = jnp.sum(wsc * jnp.maximum(dn, 0.0), axis=0, keepdims=True)
    ln = lax.broadcasted_iota(I32, (1, ATT_CK), 1)
    sc_ref[0, :, past:] = jnp.where(ln == 0, s_new, NEG_INF)


def _sample_scores(page_table, qi3, wcol, kinew, cki2):
    n, n_pages = page_table.shape
    past = n_pages * PAGE_SIZE
    grid_spec = pltpu.PrefetchScalarGridSpec(
        num_scalar_prefetch=1,
        grid=(n,),
        in_specs=[pl.BlockSpec((1, IDX_HEADS, IDX_DIM), lambda b, pt: (b, 0, 0)),
                  pl.BlockSpec((1, IDX_HEADS, 1), lambda b, pt: (b, 0, 0)),
                  pl.BlockSpec((1, 1, IDX_DIM), lambda b, pt: (b, 0, 0)),
                  pl.BlockSpec(memory_space=pl.ANY)],
        out_specs=pl.BlockSpec((1, 1, past + ATT_CK), lambda b, pt: (b, 0, 0)),
        scratch_shapes=[pltpu.VMEM((2, IDX_DIM, past), F32),
                        pltpu.SemaphoreType.DMA((2,))],
    )
    return pl.pallas_call(
        functools.partial(_sample_score_kernel, n_pages=n_pages),
        grid_spec=grid_spec,
        out_shape=jax.ShapeDtypeStruct((n, 1, past + ATT_CK), F32),
        compiler_params=_cparams(("arbitrary",)),
        name="sample_scores",
    )(page_table.reshape(-1), qi3, wcol, kinew, cki2)


def _sample_attn_kernel(pt_ref, sc_ref, qbd_ref, knew_ref, vnew_ref, ck_hbm, cv_hbm, o_ref,
                        key_ref, bias_ref, kbuf, vbuf, semk, semv, *, n_pages):
    b = pl.program_id(0)
    nb = pl.num_programs(0)
    past = n_pages * PAGE_SIZE
    nch = sc_ref.shape[0]
    slot = b % 2

    def start_all(bb, sl):
        _start_page_copies(pt_ref, ck_hbm, kbuf.at[sl], semk.at[sl], bb * n_pages, n_pages)
        _start_page_copies(pt_ref, cv_hbm, vbuf.at[sl], semv.at[sl], bb * n_pages, n_pages)

    @pl.when(b == 0)
    def _():
        start_all(0, 0)
        key_ref[...] = _float_key(sc_ref[...])
        _selection_bias(key_ref, bias_ref, nch)

    @pl.when(b + 1 < nb)
    def _():
        start_all(b + 1, 1 - slot)

    pltpu.make_async_copy(kbuf.at[slot], kbuf.at[slot], semk.at[slot]).wait()
    pltpu.make_async_copy(vbuf.at[slot], vbuf.at[slot], semv.at[slot]).wait()

    qbd = qbd_ref[0]
    bias_row = jnp.concatenate([bias_ref[c, pl.ds(b, 1), :] for c in range(nch)], axis=1)
    s_parts = [jnp.dot(qbd, kbuf[slot, :, c * SAMPLE_KC:(c + 1) * SAMPLE_KC].astype(BF16),
                       preferred_element_type=F32) for c in range(past // SAMPLE_KC)]
    s_new = jnp.sum(qbd.astype(F32) * knew_ref[0], axis=1, keepdims=True)
    ln = lax.broadcasted_iota(I32, (N_HEADS, ATT_CK), 1)
    s = jnp.concatenate(s_parts + [jnp.where(ln == 0, s_new, 0.0)], axis=1)
    hrow = lax.broadcasted_iota(I32, (N_HEADS, 1), 0)
    slope2 = jnp.exp2(-8.0 * (hrow + 1).astype(F32) / N_HEADS) * LOG2E
    rel = (lax.broadcasted_iota(I32, (1, past + ATT_CK), 1) - past).astype(F32)
    s = s + slope2 * rel + bias_row
    m = jnp.max(s, axis=1, keepdims=True)
    p = jnp.exp2(s - m)
    l = jnp.sum(p, axis=1, keepdims=True)
    o_all = p[:, past:past + 1] * vnew_ref[0]
    for c in range(past // SAMPLE_KC):
        o_all = o_all + _dot_nt(p[:, c * SAMPLE_KC:(c + 1) * SAMPLE_KC].astype(BF16),
                                vbuf[slot, :, c * SAMPLE_KC:(c + 1) * SAMPLE_KC].astype(BF16))
    o_all = o_all / l
    out = jnp.zeros((N_HEADS, HEAD_DIM), F32)
    for g in range(KV_HEADS):
        own = (hrow >= g * HEADS_PER_KV) & (hrow < (g + 1) * HEADS_PER_KV)
        out = out + jnp.where(own, o_all[:, g * HEAD_DIM:(g + 1) * HEAD_DIM], 0.0)
    o_ref[0] = out


def _sample_attn(page_table, scores3, qbd, knew, vnew, ck2, cv2):
    n, n_pages = page_table.shape
    past = n_pages * PAGE_SIZE
    nch = scores3.shape[0]
    grid_spec = pltpu.PrefetchScalarGridSpec(
        num_scalar_prefetch=1,
        grid=(n,),
        in_specs=[pl.BlockSpec((nch, n, ATT_CK), lambda b, pt: (0, 0, 0)),
                  pl.BlockSpec((1, N_HEADS, KV_DIM), lambda b, pt: (b, 0, 0)),
                  pl.BlockSpec((1, 1, KV_DIM), lambda b, pt: (b, 0, 0)),
                  pl.BlockSpec((1, 1, KV_DIM), lambda b, pt: (b, 0, 0)),
                  pl.BlockSpec(memory_space=pl.ANY),
                  pl.BlockSpec(memory_space=pl.ANY)],
        out_specs=pl.BlockSpec((1, N_HEADS, HEAD_DIM), lambda b, pt: (b, 0, 0)),
        scratch_shapes=[pltpu.VMEM((nch, n, ATT_CK), I32),
                        pltpu.VMEM((nch, n, ATT_CK), F32),
                        pltpu.VMEM((2, KV_DIM, past), F32),
                        pltpu.VMEM((2, KV_DIM, past), F32),
                        pltpu.SemaphoreType.DMA((2,)),
                        pltpu.SemaphoreType.DMA((2,))],
    )
    return pl.pallas_call(
        functools.partial(_sample_attn_kernel, n_pages=n_pages),
        grid_spec=grid_spec,
        out_shape=jax.ShapeDtypeStruct((n, N_HEADS, HEAD_DIM), F32),
        compiler_params=_cparams(("arbitrary",)),
        name="sample_attn",
    )(page_table.reshape(-1), scores3, qbd, knew, vnew, ck2, cv2)


N_CT = D_MODEL // LANES
TOK_GRP = SUBLANES * N_CT


def _tile_perm(to_token_major):
    r = lax.broadcasted_iota(I32, (TOK_GRP, TOK_GRP), 1 if to_token_major else 0)
    k = lax.broadcasted_iota(I32, (TOK_GRP, TOK_GRP), 0 if to_token_major else 1)
    return jnp.where((r // SUBLANES == k % N_CT) & (r % SUBLANES == k // N_CT), 1.0, 0.0).astype(BF16)


def _to_token_major(x, perm):
    rows = x.shape[0]
    out = []
    for t in range(rows // SUBLANES):
        xs = x[t * SUBLANES:(t + 1) * SUBLANES]
        cm = jnp.concatenate([xs[:, c * LANES:(c + 1) * LANES] for c in range(N_CT)], axis=0).astype(BF16)
        out.append(jnp.dot(perm, cm, preferred_element_type=F32).astype(BF16))
    return jnp.stack(out).reshape(rows, N_CT, LANES)


def _from_token_major(xt, perm):
    rows = xt.shape[0]
    xg = xt.reshape(rows // SUBLANES, TOK_GRP, LANES)
    out = jnp.stack([jnp.dot(perm, xg[t], preferred_element_type=F32) for t in range(rows // SUBLANES)])
    return jnp.concatenate([out[:, c * SUBLANES:(c + 1) * SUBLANES, :].reshape(rows, LANES) for c in range(N_CT)],
                           axis=1)


def _merge_ln1_kernel(*refs, aliased, n_tiles):
    n_in = 12
    outs = refs[n_in + 1:] if aliased else refs[n_in:]
    xt_ref = outs[0]

    @pl.when(pl.program_id(0) < n_tiles)
    def _():
        _merge_ln1_tile(*refs[:n_in], *outs)

    @pl.when(pl.program_id(0) >= n_tiles)
    def _():
        xt_ref[...] = jnp.zeros(xt_ref.shape, BF16)


def _merge_ln1_tile(y_ref, o_ref, gc_ref, ga_ref, x_ref, wc_ref, wa_ref, wo_ref, g_ref, b_ref, wr_ref,
                    cnt_in_ref, xt_ref, x1_ref, ei_ref, gt_ref, cnt_ref):
    @pl.when(pl.program_id(0) == 0)
    def _():
        cnt_ref[...] = cnt_in_ref[...]

    conv_out = jnp.dot(y_ref[...], wc_ref[...], preferred_element_type=F32)
    attn_out = jnp.dot(o_ref[...], wa_ref[...], preferred_element_type=F32)
    merged = gc_ref[...].astype(F32) * conv_out + ga_ref[...].astype(F32) * attn_out
    mix = jnp.dot(merged.astype(BF16), wo_ref[...], preferred_element_type=F32)
    x1 = _layer_norm(DEEPNORM_ALPHA * x_ref[...] + mix, g_ref[...], b_ref[...])
    x1_ref[...] = x1
    xt_ref[...] = _to_token_major(x1, _tile_perm(True))

    x_hi = x1.astype(BF16)
    x_lo = (x1 - x_hi.astype(F32)).astype(BF16)
    logits = (jnp.dot(x_hi, wr_ref[0], preferred_element_type=F32)
              + jnp.dot(x_lo, wr_ref[0], preferred_element_type=F32)
              + jnp.dot(x_hi, wr_ref[1], preferred_element_type=F32))
    lane = lax.broadcasted_iota(I32, logits.shape, 1)
    is_g = lane < N_GROUPS
    lg = jnp.where(is_g, logits, -jnp.inf)
    eg = jnp.exp(lg - jnp.max(lg, axis=1, keepdims=True))
    pg = eg / jnp.sum(eg, axis=1, keepdims=True)
    g_top = jnp.max(pg, axis=1, keepdims=True)
    g_idx = jnp.min(jnp.where(is_g & (pg == g_top), lane, LANES), axis=1, keepdims=True)
    lo = N_GROUPS + g_idx * EXPERTS_PER_GROUP
    is_e = (lane >= lo) & (lane < lo + EXPERTS_PER_GROUP)
    le = jnp.where(is_e, logits, -jnp.inf)
    ee = jnp.exp(le - jnp.max(le, axis=1, keepdims=True))
    pe = ee / jnp.sum(ee, axis=1, keepdims=True)
    p1 = jnp.max(jnp.where(is_e, pe, -1.0), axis=1, keepdims=True)
    i1 = jnp.min(jnp.where(is_e & (pe == p1), lane, 2 * LANES), axis=1, keepdims=True)
    rest = is_e & (lane != i1)
    p2 = jnp.max(jnp.where(rest, pe, -1.0), axis=1, keepdims=True)
    i2 = jnp.min(jnp.where(rest & (pe == p2), lane, 2 * LANES), axis=1, keepdims=True)
    den = p1 + p2
    gt_ref[...] = jnp.where(lane == 0, g_top * p1 / den, jnp.where(lane == 1, g_top * p2 / den, 0.0))

    e1, e2 = i1 - N_GROUPS, i2 - N_GROUPS
    tm = logits.shape[0]
    oh1 = jnp.where(lane == e1, 1.0, 0.0)
    oh2 = jnp.where(lane == e2, 1.0, 0.0)
    earlier = jnp.where(lax.broadcasted_iota(I32, (tm, tm), 1) < lax.broadcasted_iota(I32, (tm, tm), 0),
                        1.0, 0.0).astype(BF16)
    before1 = jnp.dot(earlier, oh1.astype(BF16), preferred_element_type=F32)
    before2 = jnp.dot(earlier, oh2.astype(BF16), preferred_element_type=F32)
    tot1 = jnp.sum(oh1, axis=0, keepdims=True)
    base = cnt_ref[...]
    r1 = jnp.sum(oh1 * (base + before1), axis=1, keepdims=True)
    r2 = jnp.sum(oh2 * (base + tot1 + before2), axis=1, keepdims=True)
    cnt_ref[...] = base + tot1 + jnp.sum(oh2, axis=0, keepdims=True)
    ei_ref[...] = jnp.where(lane == 0, e1, jnp.where(lane == 1, e2, jnp.where(
        lane == 2, r1.astype(I32), jnp.where(lane == 3, r2.astype(I32), 0))))


def _merge_ln1(yact, o, gc, ga, x, wc, wa, wo, g1, b1, wr, cnt_in, tm, xt_all=None, m_total=None):
    m = x.shape[0]
    aliased = xt_all is not None
    n_tiles = m // tm
    total = xt_all.shape[0] if aliased else m_total
    blk0 = (total - m) // tm if aliased else 0
    n_steps = n_tiles if aliased else pl.cdiv(total, tm)
    row = lambda w: pl.BlockSpec((tm, w), lambda i: (jnp.minimum(i, n_tiles - 1), 0))
    res = lambda a: pl.BlockSpec(a.shape, lambda i: (0,) * a.ndim, pipeline_mode=pl.Buffered(1))
    cnt_spec = pl.BlockSpec((1, LANES), lambda i: (0, 0))
    args = [yact, o, gc, ga, x, wc, wa, wo, g1, b1, wr, cnt_in]
    in_specs = [row(C_CONV), row(N_HEADS * HEAD_DIM), row(D_MODEL), row(D_MODEL), row(D_MODEL),
                res(wc), res(wa), res(wo), res(g1), res(b1), res(wr), cnt_spec]
    if aliased:
        args.append(xt_all)
        in_specs.append(pl.BlockSpec(memory_space=pl.ANY))
    return pl.pallas_call(
        functools.partial(_merge_ln1_kernel, aliased=aliased, n_tiles=n_tiles),
        grid=(n_steps,),
        in_specs=in_specs,
        out_specs=(pl.BlockSpec((tm, N_CT, LANES), lambda i: (blk0 + i, 0, 0)), row(D_MODEL), row(LANES),
                   row(LANES), cnt_spec),
        out_shape=(jax.ShapeDtypeStruct((total, N_CT, LANES), BF16),
                   jax.ShapeDtypeStruct((m, D_MODEL), F32),
                   jax.ShapeDtypeStruct((m, LANES), I32),
                   jax.ShapeDtypeStruct((m, LANES), F32),
                   jax.ShapeDtypeStruct((1, LANES), F32)),
        input_output_aliases={len(args) - 1: 0} if aliased else {},
        compiler_params=_cparams(("arbitrary",)),
        name="merge_ln1",
    )(*args)


N_GSLOT = 3


def _experts_kernel(fb_ref, dest_ref, x_hbm, zeros_hbm, wg_ref, wu_ref, wd_ref, y_hbm,
                    rowtok, xbuf, ybuf, wgb, wub, wdb, gsem, ysem, zsem, *, n_blocks):
    e = pl.program_id(0)
    n_used = fb_ref[N_EXPERTS]
    last = jnp.maximum(n_used - 1, 0)

    def start_gather(blk, sl):
        base = blk * MOE_BLK
        for i in range(MOE_BLK):
            pltpu.make_async_copy(x_hbm.at[rowtok[base + i]], xbuf.at[sl, i], gsem.at[sl]).start()

    def wait_gather(sl):
        pltpu.make_async_copy(xbuf.at[sl], xbuf.at[sl], gsem.at[sl]).wait()

    def write_copy(blk, sl):
        r0 = pl.multiple_of(blk * MOE_BLK, MOE_BLK)
        return pltpu.make_async_copy(ybuf.at[sl], y_hbm.at[pl.ds(r0, MOE_BLK)], ysem.at[sl])

    @pl.when(e == 0)
    def _():
        clear = pltpu.make_async_copy(zeros_hbm, rowtok, zsem.at[0])
        clear.start()
        clear.wait()

        def place(a, carry):
            rowtok[dest_ref[a]] = lax.shift_right_logical(a, 1)
            return carry
        lax.fori_loop(0, dest_ref.shape[0], place, 0, unroll=8)
        start_gather(0, 0)
        start_gather(jnp.minimum(1, last), 1)

    b0, b1 = fb_ref[e], fb_ref[e + 1]

    @pl.when(b1 > b0)
    def _():
        wgb[...] = wg_ref[0].astype(BF16)
        wub[...] = wu_ref[0].astype(BF16)
        wdb[...] = wd_ref[0].astype(BF16)

    def block(b, carry):
        sl = lax.rem(b, N_GSLOT)
        ysl = lax.rem(b, 2)

        @pl.when(b >= 2)
        def _():
            write_copy(b - 2, ysl).wait()

        wait_gather(sl)
        start_gather(jnp.minimum(b + 2, last), lax.rem(b + 2, N_GSLOT))
        xb = _from_token_major(xbuf[sl], _tile_perm(False)).astype(BF16)
        hg = jnp.dot(xb, wgb[...], preferred_element_type=F32)
        hu = jnp.dot(xb, wub[...], preferred_element_type=F32)
        h = (hg * jax.nn.sigmoid(hg) * hu).astype(BF16)
        y = jnp.dot(h, wdb[...], preferred_element_type=F32)
        ybuf[ysl] = _to_token_major(y, _tile_perm(True))
        write_copy(b, ysl).start()
        return carry

    lax.fori_loop(b0, b1, block, 0)

    @pl.when(e == pl.num_programs(0) - 1)
    def _():
        wait_gather(lax.rem(last + 1, N_GSLOT))
        wait_gather(lax.rem(last + 2, N_GSLOT))

        @pl.when(n_used >= 1)
        def _():
            write_copy(last, lax.rem(last, 2)).wait()

        @pl.when(n_used >= 2)
        def _():
            write_copy(last - 1, lax.rem(last - 1, 2)).wait()

        ybuf[0] = jnp.zeros(ybuf.shape[1:], BF16)

        def fill(blk, carry):
            write_copy(blk, 0).start()
            return carry
        lax.fori_loop(n_used, n_blocks, fill, 0)

        def fill_wait(blk, carry):
            write_copy(blk, 0).wait()
            return carry
        lax.fori_loop(n_used, n_blocks, fill_wait, 0)


def _experts(first_blk, dest, xt_all, wg, wu, wd):
    n_blocks = (dest.shape[0] + N_EXPERTS * (MOE_BLK - 1) + MOE_BLK - 1) // MOE_BLK
    n_rows = n_blocks * MOE_BLK
    grid_spec = pltpu.PrefetchScalarGridSpec(
        num_scalar_prefetch=2,
        grid=(N_EXPERTS,),
        in_specs=[pl.BlockSpec(memory_space=pl.ANY),
                  pl.BlockSpec(memory_space=pl.ANY),
                  pl.BlockSpec((1, D_MODEL, D_EXPERT), lambda e, fb, de: (e, 0, 0)),
                  pl.BlockSpec((1, D_MODEL, D_EXPERT), lambda e, fb, de: (e, 0, 0)),
                  pl.BlockSpec((1, D_EXPERT, D_MODEL), lambda e, fb, de: (e, 0, 0))],
        out_specs=pl.BlockSpec(memory_space=pl.ANY),
        scratch_shapes=[pltpu.SMEM((n_rows,), I32),
                        pltpu.VMEM((N_GSLOT, MOE_BLK, N_CT, LANES), BF16),
                        pltpu.VMEM((2, MOE_BLK, N_CT, LANES), BF16),
                        pltpu.VMEM((D_MODEL, D_EXPERT), BF16),
                        pltpu.VMEM((D_MODEL, D_EXPERT), BF16),
                        pltpu.VMEM((D_EXPERT, D_MODEL), BF16),
                        pltpu.SemaphoreType.DMA((N_GSLOT,)),
                        pltpu.SemaphoreType.DMA((2,)),
                        pltpu.SemaphoreType.DMA((1,))],
    )
    return pl.pallas_call(
        functools.partial(_experts_kernel, n_blocks=n_blocks),
        grid_spec=grid_spec,
        out_shape=jax.ShapeDtypeStruct((n_rows, N_CT, LANES), BF16),
        compiler_params=_cparams(("arbitrary",)),
        name="experts",
    )(first_blk, dest, xt_all, jnp.zeros((n_rows,), I32), wg, wu, wd)


def _combine_ln2_kernel(dest_ref, x1_ref, gt_ref, yb_hbm, g_ref, b_ref, y_ref, ybuf, sem, *, tm, tok0):
    i = pl.program_id(0)
    last = pl.num_programs(0) - 1
    slot = lax.rem(i, 2)

    def row_copy(a, sl, k, r):
        return pltpu.make_async_copy(yb_hbm.at[dest_ref[a]], ybuf.at[sl, k, r], sem.at[sl])

    def start_gather(tile, sl, unrolled):
        base = 2 * (tok0 + tile * tm)
        if unrolled:
            for r in range(tm):
                row_copy(base + 2 * r, sl, 0, r).start()
                row_copy(base + 2 * r + 1, sl, 1, r).start()
        else:
            def body(r, carry):
                row_copy(base + 2 * r, sl, 0, r).start()
                row_copy(base + 2 * r + 1, sl, 1, r).start()
                return carry
            lax.fori_loop(0, tm, body, 0)

    def wait_gather(sl):
        pltpu.make_async_copy(ybuf.at[sl], ybuf.at[sl], sem.at[sl]).wait()

    @pl.when(i == 0)
    def _():
        start_gather(0, 0, unrolled=False)

    wait_gather(slot)
    start_gather(jnp.minimum(i + 1, last), 1 - slot, unrolled=True)
    gt = gt_ref[...]
    perm = _tile_perm(False)
    z = (DEEPNORM_ALPHA * x1_ref[...] + gt[:, 0:1] * _from_token_major(ybuf[slot, 0], perm)
         + gt[:, 1:2] * _from_token_major(ybuf[slot, 1], perm))
    y_ref[...] = _layer_norm(z, g_ref[...], b_ref[...])

    @pl.when(i == last)
    def _():
        wait_gather(1 - slot)


def _combine_ln2(dest, x1, gt, yb, g2, b2, tm, tok0):
    m = x1.shape[0]
    grid_spec = pltpu.PrefetchScalarGridSpec(
        num_scalar_prefetch=1,
        grid=(m // tm,),
        in_specs=[pl.BlockSpec((tm, D_MODEL), lambda i, d: (i, 0)),
                  pl.BlockSpec((tm, LANES), lambda i, d: (i, 0)),
                  pl.BlockSpec(memory_space=pl.ANY),
                  pl.BlockSpec((1, D_MODEL), lambda i, d: (0, 0)),
                  pl.BlockSpec((1, D_MODEL), lambda i, d: (0, 0))],
        out_specs=pl.BlockSpec((tm, D_MODEL), lambda i, d: (i, 0)),
        scratch_shapes=[pltpu.VMEM((2, 2, tm, N_CT, LANES), BF16),
                        pltpu.SemaphoreType.DMA((2,))],
    )
    return pl.pallas_call(
        functools.partial(_combine_ln2_kernel, tm=tm, tok0=tok0),
        grid_spec=grid_spec,
        out_shape=jax.ShapeDtypeStruct((m, D_MODEL), F32),
        compiler_params=_cparams(("arbitrary",)),
        name="combine_ln2",
    )(dest, x1, gt, yb, g2, b2)


def _dispatch_tables(eid, rank, counts):
    padded = (counts + MOE_BLK - 1) // MOE_BLK * MOE_BLK
    pad_end = jnp.cumsum(padded)
    pad_start = pad_end - padded
    start_of = jnp.sum(jnp.where(eid[..., None] == jnp.arange(N_EXPERTS, dtype=I32), pad_start, 0), axis=-1)
    dest = (start_of + rank).reshape(-1).astype(I32)
    first_blk = (jnp.concatenate([pad_start, pad_end[-1:]]) // MOE_BLK).astype(I32)
    return dest, first_blk


def kernel(x_prompt, x_sample, cache_k, cache_v, cache_kidx, state_conv, page_table, w_in, conv_dw_w, conv_dw_b,
           conv_ln_g, conv_ln_b, w_conv_out, w_attn_out, w_out, ln1_g, ln1_b, w_router_group, w_router_expert,
           w_expert_gate, w_expert_up, w_expert_down, ln2_g, ln2_b):
    assert w_in.shape[0] == 1, "single-layer trunk"
    nb, t, d = x_prompt.shape
    ns = x_sample.shape[0]
    n_pool = cache_k.shape[1]
    mp = nb * t
    row2 = lambda a: a.reshape(1, -1)

    w_arr = _arrange_w_in(w_in[0])
    dww, dwb, lng, lnb = conv_dw_w[0], row2(conv_dw_b[0]), row2(conv_ln_g[0]), row2(conv_ln_b[0])
    wc, wa, wo = w_conv_out[0].astype(BF16), w_attn_out[0].astype(BF16), w_out[0].astype(BF16)
    g1, b1, g2, b2 = row2(ln1_g[0]), row2(ln1_b[0]), row2(ln2_g[0]), row2(ln2_b[0])
    wr32 = jnp.concatenate([w_router_group[0], w_router_expert[0],
                            jnp.zeros((d, LANES - N_GROUPS - N_EXPERTS), F32)], axis=1)
    wr_hi = wr32.astype(BF16)
    wr = jnp.stack([wr_hi, (wr32 - wr_hi.astype(F32)).astype(BF16)])

    xp = x_prompt.reshape(mp, d)
    u_p, q_p, qi_p, kv_p, kvb_p, gc_p, ga_p, kiwi_p, kib_p, yact_p = _in_proj(
        xp, w_arr, tm=512, conv=(dww, dwb, lng, lnb, t))
    o_p = _attn_prompt(qi_p.reshape(nb, t, -1), kiwi_p.reshape(nb, t, LANES), kib_p.reshape(nb, t, LANES),
                       q_p.reshape(nb, t, -1), kvb_p.reshape(nb, t, 2 * KV_DIM))
    xt_all, x1_p, ei_p, gt_p, cnt_p = _merge_ln1(yact_p, o_p.reshape(mp, -1), gc_p, ga_p, xp,
                                                 wc, wa, wo, g1, b1, wr, jnp.zeros((1, LANES), F32), tm=256,
                                                 m_total=mp + ns)

    xs = x_sample.reshape(ns, d)
    u_s, q_s, qi_s, kv_s, _, gc_s, ga_s, kiwi_s, _ = _in_proj(xs, w_arr, tm=ns)
    yact_s = _conv_sample(state_conv[0].transpose(1, 0, 2), u_s, dww, dwb, lng, lnb)
    cki2 = cache_kidx.transpose(0, 1, 3, 2).reshape(n_pool, IDX_DIM, PAGE_SIZE)
    ck2 = cache_k.transpose(0, 1, 3, 4, 2).reshape(n_pool, KV_DIM, PAGE_SIZE)
    cv2 = cache_v.transpose(0, 1, 3, 4, 2).reshape(n_pool, KV_DIM, PAGE_SIZE)
    scores_s = _sample_scores(page_table, qi_s.reshape(ns, IDX_HEADS, IDX_DIM),
                              kiwi_s[:, IDX_DIM:IDX_DIM + IDX_HEADS].reshape(ns, IDX_HEADS, 1),
                              kiwi_s[:, :IDX_DIM].reshape(ns, 1, IDX_DIM), cki2)
    scores3 = scores_s.reshape(ns, -1, ATT_CK).transpose(1, 0, 2)
    own_group = (np.arange(N_HEADS)[:, None] // HEADS_PER_KV) == (np.arange(KV_DIM)[None, :] // HEAD_DIM)
    qbd = jnp.where(own_group[None], jnp.tile(q_s.reshape(ns, N_HEADS, HEAD_DIM), (1, 1, KV_HEADS)), 0).astype(BF16)
    o3 = _sample_attn(page_table, scores3, qbd, kv_s[:, :KV_DIM].reshape(ns, 1, KV_DIM),
                      kv_s[:, KV_DIM:].reshape(ns, 1, KV_DIM), ck2, cv2)
    o_s = o3.reshape(ns, N_HEADS * HEAD_DIM).astype(BF16)
    xt_all, x1_s, ei_s, gt_s, cnt = _merge_ln1(yact_s, o_s, gc_s, ga_s, xs, wc, wa, wo, g1, b1, wr, cnt_p, tm=ns,
                                               xt_all=xt_all)

    eid_rank = jnp.concatenate([ei_p[:, :4], ei_s[:, :4]], axis=0)
    dest, first_blk = _dispatch_tables(eid_rank[:, :2], eid_rank[:, 2:], cnt[0, :N_EXPERTS].astype(I32))
    yb = _experts(first_blk, dest, xt_all, w_expert_gate[0], w_expert_up[0], w_expert_down[0])
    y_p = _combine_ln2(dest, x1_p, gt_p, yb, g2, b2, tm=256, tok0=0)
    y_s = _combine_ln2(dest, x1_s, gt_s, yb, g2, b2, tm=ns, tok0=mp)

    y_prompt = y_p.reshape(nb, t, d)
    y_sample = y_s.reshape(ns, 1, d)
    k_prompt = kv_p[:, :KV_DIM].reshape(1, nb, t, KV_HEADS, HEAD_DIM)
    v_prompt = kv_p[:, KV_DIM:].reshape(1, nb, t, KV_HEADS, HEAD_DIM)
    kidx_prompt = kiwi_p[:, :IDX_DIM].reshape(1, nb, t, IDX_DIM)
    conv_prompt = u_p.reshape(nb, t, C_CONV)[:, t - (CONV_W - 1):][None]
    k_sample = kv_s[:, :KV_DIM].reshape(1, ns, 1, KV_HEADS, HEAD_DIM)
    v_sample = kv_s[:, KV_DIM:].reshape(1, ns, 1, KV_HEADS, HEAD_DIM)
    kidx_sample = kiwi_s[:, :IDX_DIM].reshape(1, ns, 1, IDX_DIM)
    conv_sample = jnp.concatenate([state_conv[0][:, 1:], u_s[:, None, :]], axis=1)[None]
    return (y_prompt, y_sample, k_prompt, v_prompt, kidx_prompt, conv_prompt, k_sample, v_sample, kidx_sample,
            conv_sample)
```

```python
import functools

import jax
import jax.numpy as jnp
import numpy as np
from jax import lax
from jax.experimental import pallas as pl
from jax.experimental.pallas import tpu as pltpu

F32 = jnp.float32
BF16 = jnp.bfloat16
I32 = jnp.int32

D_MODEL = 2048
C_CONV = 1024
CONV_W = 31
N_HEADS = 16
HEAD_DIM = 64
KV_HEADS = 4
HEADS_PER_KV = N_HEADS // KV_HEADS
KV_DIM = KV_HEADS * HEAD_DIM
IDX_HEADS = 16
IDX_DIM = 64
MAX_TOPK = 256
N_GROUPS = 8
EXPERTS_PER_GROUP = 8
N_EXPERTS = 64
D_EXPERT = 512
MOE_BLK = 128
PAGE_SIZE = 128
LN_EPS = 1e-5
DEEPNORM_ALPHA = 2.0 ** 0.25
NEG_INF = -1e30
INT_MIN = -(2 ** 31)
NEG_KEY = int(INT_MIN - int(np.array(NEG_INF, np.float32).view(np.int32)))
LOG2E = 1.4426950408889634
Q_SCALE = HEAD_DIM ** -0.5 * LOG2E
IDX_SCALE = IDX_HEADS ** -0.5 * IDX_DIM ** -0.5

LANES = 128
SUBLANES = 8
HALF = LANES // 2
IN_TN = 1024
VMEM_LIMIT = 56 * 1024 * 1024


def _cparams(sem, vmem=VMEM_LIMIT):
    return pltpu.CompilerParams(dimension_semantics=sem, vmem_limit_bytes=vmem)


def _float_key(x):
    b = pltpu.bitcast(x, I32)
    return jnp.where(b < 0, INT_MIN - b, b)


def _layer_norm(z, g, b):
    mu = jnp.mean(z, axis=-1, keepdims=True)
    zc = z - mu
    var = jnp.mean(zc * zc, axis=-1, keepdims=True)
    return zc * lax.rsqrt(var + LN_EPS) * g + b


def _dot_nt(a, b):
    return lax.dot_general(a, b, (((1,), (1,)), ((), ())), preferred_element_type=F32)


def _alibi_slope(h):
    return float(2.0 ** (-8.0 * (h + 1) / N_HEADS))


_T_GLU, _T_Q, _T_QI, _T_KV, _T_GC, _T_GA, _N_TILES = 0, 2, 3, 4, 5, 7, 9
KV_TILE_COLS = 2 * KV_DIM + 2 * LANES
GLU_HALF = IN_TN // 2
CONV_HALO = 32
CONV_STRIP = 32
CONV_ROWS = 256


def _arrange_w_in(w):
    sizes = (C_CONV, C_CONV, N_HEADS * HEAD_DIM, KV_DIM, KV_DIM,
             IDX_HEADS * IDX_DIM, IDX_DIM, IDX_HEADS, D_MODEL, D_MODEL)
    offs = np.cumsum((0,) + sizes)
    wt = w.T
    a, g, q, k, v, qi, ki, wi, gc, ga = [wt[offs[i]:offs[i + 1]] for i in range(10)]
    glu = [jnp.concatenate([a[t * GLU_HALF:(t + 1) * GLU_HALF], g[t * GLU_HALF:(t + 1) * GLU_HALF]], axis=0)
           for t in range(C_CONV // GLU_HALF)]
    zeros = lambda n: jnp.zeros((n, w.shape[0]), w.dtype)
    kv_tile = jnp.concatenate([k, v, ki, wi, zeros(LANES - IDX_DIM - IDX_HEADS), ki, ki,
                               zeros(IN_TN - KV_TILE_COLS)], axis=0)
    return jnp.concatenate(glu + [q, qi, kv_tile, gc, ga], axis=0).astype(BF16)


def _conv_half(jj, new_seq, ubuf, sh_ref, ypre, dww_ref, dwb_ref):
    tm = ypre.shape[0]
    cs = slice(jj * GLU_HALF, (jj + 1) * GLU_HALF)
    ubuf[jj, 0:CONV_HALO, :] = jnp.where(new_seq, 0.0, ubuf[jj, 0:CONV_HALO, :])
    first = CONV_HALO - (CONV_W - 1)
    dwb = dwb_ref[:, cs]
    for rb in range(tm // CONV_ROWS):
        for b in range(SUBLANES):
            rows = CONV_ROWS + SUBLANES * ((CONV_W - 1 - b) // SUBLANES)
            src0 = rb * CONV_ROWS + first + b
            sh_ref[b, 0:rows, :] = ubuf[jj, src0:src0 + rows, :]
        for r in range(CONV_ROWS // CONV_STRIP):
            acc = jnp.broadcast_to(dwb, (CONV_STRIP, GLU_HALF))
            for j in range(CONV_W):
                a, b = divmod(j, SUBLANES)
                r0 = r * CONV_STRIP + a * SUBLANES
                acc = acc + dww_ref[j:j + 1, cs] * sh_ref[b, r0:r0 + CONV_STRIP, :]
            o0 = rb * CONV_ROWS + r * CONV_STRIP
            ypre[o0:o0 + CONV_STRIP, cs] = acc
    ubuf[jj, 0:CONV_HALO, :] = ubuf[jj, tm:tm + CONV_HALO, :]


def _in_proj_kernel(*refs, fuse_conv, tiles_per_seq):
    if fuse_conv:
        (x_ref, w_ref, dww_ref, dwb_ref, lng_ref, lnb_ref,
         u_ref, q_ref, qi_ref, kv_ref, kvb_ref, gc_ref, ga_ref, kiwi_ref, kib_ref, y_ref,
         xb_ref, ubuf, sh_ref, ypre) = refs
    else:
        (x_ref, w_ref, u_ref, q_ref, qi_ref, kv_ref, kvb_ref, gc_ref, ga_ref, kiwi_ref, kib_ref, xb_ref) = refs
    i = pl.program_id(0)
    j = pl.program_id(1)

    @pl.when(j == 0)
    def _():
        xb_ref[...] = x_ref[...].astype(BF16)

    if fuse_conv:
        new_seq = lax.rem(i, tiles_per_seq) == 0

        @pl.when((i == 0) & (j == 0))
        def _():
            ubuf[:, 0:CONV_HALO, :] = jnp.zeros((ubuf.shape[0], CONV_HALO, GLU_HALF), F32)

    def mm(ncols=IN_TN):
        return _dot_nt(xb_ref[...], w_ref[:ncols, :])

    for jj in range(2):
        @pl.when(j == jj)
        def _(jj=jj):
            if fuse_conv and jj == 1:
                _conv_half(0, new_seq, ubuf, sh_ref, ypre, dww_ref, dwb_ref)
            acc = mm()
            u = acc[:, :GLU_HALF] * jax.nn.sigmoid(acc[:, GLU_HALF:])
            u_ref[...] = u
            if fuse_conv:
                ubuf[jj, CONV_HALO:, :] = u

    @pl.when(j == _T_Q)
    def _():
        if fuse_conv:
            _conv_half(1, new_seq, ubuf, sh_ref, ypre, dww_ref, dwb_ref)
        q_ref[...] = (mm() * Q_SCALE).astype(BF16)

    @pl.when(j == _T_QI)
    def _():
        if fuse_conv:
            lng, lnb = lng_ref[...], lnb_ref[...]
            for r in range(ypre.shape[0] // CONV_STRIP):
                rs = slice(r * CONV_STRIP, (r + 1) * CONV_STRIP)
                yn = _layer_norm(ypre[rs, :], lng, lnb)
                y_ref[rs, :] = (yn * jax.nn.sigmoid(yn)).astype(BF16)
        qi_ref[...] = mm().astype(BF16)

    @pl.when(j == _T_KV)
    def _():
        acc = mm(KV_TILE_COLS)
        kv_ref[...] = acc[:, :2 * KV_DIM]
        kvb_ref[...] = acc[:, :2 * KV_DIM].astype(BF16)
        kiwi_ref[...] = acc[:, 2 * KV_DIM:2 * KV_DIM + LANES]
        kib_ref[...] = acc[:, 2 * KV_DIM + LANES:].astype(BF16)

    @pl.when((j >= _T_GC) & (j < _T_GA))
    def _():
        gc_ref[...] = jax.nn.sigmoid(mm()).astype(BF16)

    @pl.when(j >= _T_GA)
    def _():
        ga_ref[...] = jax.nn.sigmoid(mm()).astype(BF16)


def _in_proj(x, w_arr, tm, conv=None):
    m = x.shape[0]
    fuse_conv = conv is not None

    def cl(lo, n):
        return lambda i, j: (i, jnp.clip(j - lo, 0, n - 1))

    row_blk = lambda w: pl.BlockSpec((tm, w), lambda i, j: (i, 0))
    out_shape = [
        jax.ShapeDtypeStruct((m, C_CONV), F32),
        jax.ShapeDtypeStruct((m, N_HEADS * HEAD_DIM), BF16),
        jax.ShapeDtypeStruct((m, IDX_HEADS * IDX_DIM), BF16),
        jax.ShapeDtypeStruct((m, 2 * KV_DIM), F32),
        jax.ShapeDtypeStruct((m, 2 * KV_DIM), BF16),
        jax.ShapeDtypeStruct((m, D_MODEL), BF16),
        jax.ShapeDtypeStruct((m, D_MODEL), BF16),
        jax.ShapeDtypeStruct((m, LANES), F32),
        jax.ShapeDtypeStruct((m, LANES), BF16),
    ]
    out_specs = [
        pl.BlockSpec((tm, GLU_HALF), cl(_T_GLU, 2)),
        row_blk(IN_TN), row_blk(IN_TN), row_blk(2 * KV_DIM), row_blk(2 * KV_DIM),
        pl.BlockSpec((tm, IN_TN), cl(_T_GC, 2)),
        pl.BlockSpec((tm, IN_TN), cl(_T_GA, 2)),
        row_blk(LANES), row_blk(LANES),
    ]
    args = [x, w_arr]
    in_specs = [pl.BlockSpec((tm, D_MODEL), lambda i, j: (i, 0)),
                pl.BlockSpec((IN_TN, D_MODEL), lambda i, j: (j, 0))]
    scratch = [pltpu.VMEM((tm, D_MODEL), BF16)]
    tiles_per_seq = 1
    if fuse_conv:
        dww, dwb, lng, lnb, seq = conv
        assert seq % tm == 0 and tm % CONV_ROWS == 0
        tiles_per_seq = seq // tm
        const = lambda a: pl.BlockSpec(a.shape, lambda i, j: (0, 0))
        args += [dww, dwb, lng, lnb]
        in_specs += [const(dww), const(dwb), const(lng), const(lnb)]
        out_shape.append(jax.ShapeDtypeStruct((m, C_CONV), BF16))
        out_specs.append(row_blk(C_CONV))
        scratch += [pltpu.VMEM((2, CONV_HALO + tm, GLU_HALF), F32),
                    pltpu.VMEM((SUBLANES, CONV_ROWS + SUBLANES * ((CONV_W - 1) // SUBLANES), GLU_HALF), F32),
                    pltpu.VMEM((tm, C_CONV), F32)]
    return pl.pallas_call(
        functools.partial(_in_proj_kernel, fuse_conv=fuse_conv, tiles_per_seq=tiles_per_seq),
        grid=(m // tm, _N_TILES),
        in_specs=in_specs,
        out_specs=tuple(out_specs),
        out_shape=tuple(out_shape),
        scratch_shapes=scratch,
        compiler_params=_cparams(("arbitrary", "arbitrary")),
        name="in_proj",
    )(*args)


def _conv_sample_kernel(state_ref, u_ref, dww_ref, dwb_ref, lng_ref, lnb_ref, y_ref):
    acc = dwb_ref[...] + dww_ref[CONV_W - 1:CONV_W, :] * u_ref[...]
    for j in range(CONV_W - 1):
        acc = acc + dww_ref[j:j + 1, :] * state_ref[j]
    yn = _layer_norm(acc, lng_ref[...], lnb_ref[...])
    y_ref[...] = (yn * jax.nn.sigmoid(yn)).astype(BF16)


def _conv_sample(state_t, u, dww, dwb, lng, lnb):
    n = u.shape[0]
    return pl.pallas_call(
        _conv_sample_kernel,
        out_shape=jax.ShapeDtypeStruct((n, C_CONV), BF16),
        compiler_params=pltpu.CompilerParams(vmem_limit_bytes=VMEM_LIMIT),
        name="conv_sample",
    )(state_t, u, dww, dwb, lng, lnb)


ATT_TQ = 256
ATT_CK = 256


def _count_ge(key_ref, nch, cand):
    def body(c, acc):
        m = jnp.where(key_ref[c] >= cand, 1.0, 0.0)
        return acc + m[:, :LANES] + m[:, LANES:]
    acc = lax.fori_loop(0, nch, body, jnp.zeros((cand.shape[0], LANES), F32))
    return jnp.sum(acc, axis=1, keepdims=True)


def _select_threshold(key_ref, nch, rows):
    def body(i, prefix):
        cand = prefix + jnp.left_shift(jnp.int32(1), 31 - i)
        return jnp.where(_count_ge(key_ref, nch, cand) >= float(MAX_TOPK), cand, prefix)
    return lax.fori_loop(0, 32, body, jnp.full((rows, 1), INT_MIN, I32))


def _selection_bias(key_ref, bias_ref, nch):
    rows, ck = key_ref.shape[1:]
    thr = _select_threshold(key_ref, nch, rows)
    n_tie_take = float(MAX_TOPK) - _count_ge(key_ref, nch, thr + 1)
    tri = jnp.where(lax.broadcasted_iota(I32, (ck, ck), 0) <= lax.broadcasted_iota(I32, (ck, ck), 1),
                    1.0, 0.0).astype(BF16)

    def bias_chunk(c, seen):
        key = key_ref[c]
        eq = key == thr
        eqf = jnp.where(eq, 1.0, 0.0)
        incl = jnp.dot(eqf.astype(BF16), tri, preferred_element_type=F32)
        sel = (key > thr) | (eq & (seen + incl - eqf < n_tie_take))
        bias_ref[c] = jnp.where(sel & (key > NEG_KEY), 0.0, NEG_INF)
        return seen + incl[:, ck - 1:ck]

    lax.fori_loop(0, nch, bias_chunk, jnp.zeros((rows, 1), F32))


def _count_ge_t(key_ref, nch, cand):
    ck, nq = key_ref.shape[1:]

    def body(c, acc):
        m = jnp.where(key_ref[c] >= cand, 1.0, 0.0)
        return acc + jnp.sum(m.reshape(ck // SUBLANES, SUBLANES, nq), axis=0)

    acc = lax.fori_loop(0, nch, body, jnp.zeros((SUBLANES, nq), F32))
    return jnp.sum(acc, axis=0, keepdims=True)


def _selection_bias_t(key_ref, bias_ref, nch):
    ck, nq = key_ref.shape[1:]

    def bit(i, prefix):
        cand = prefix + jnp.left_shift(jnp.int32(1), 31 - i)
        return jnp.where(_count_ge_t(key_ref, nch, cand) >= float(MAX_TOPK), cand, prefix)

    thr = lax.fori_loop(0, 32, bit, jnp.full((1, nq), INT_MIN, I32))
    n_tie_take = float(MAX_TOPK) - _count_ge_t(key_ref, nch, thr + 1)
    tri = jnp.where(lax.broadcasted_iota(I32, (ck, ck), 1) <= lax.broadcasted_iota(I32, (ck, ck), 0),
                    1.0, 0.0).astype(BF16)

    def bias_chunk(c, seen):
        key = key_ref[c]
        eq = key == thr
        eqf = jnp.where(eq, 1.0, 0.0)
        incl = jnp.dot(tri, eqf.astype(BF16), preferred_element_type=F32)
        sel = (key > thr) | (eq & (seen + incl - eqf < n_tie_take))
        bias_ref[c] = jnp.where(sel & (key > NEG_KEY), 0.0, NEG_INF).T
        return seen + incl[ck - 1:ck, :]

    lax.fori_loop(0, nch, bias_chunk, jnp.zeros((1, nq), F32))


def _swap_halves(x):
    return jnp.concatenate([x[:, HALF:], x[:, :HALF]], axis=1)


def _attn_prompt_kernel(qi_ref, kiwi_ref, kib_ref, q_ref, kvb_ref, o_ref,
                        key_ref, bias_ref, m_ref, acc_ref):
    tq, ck = ATT_TQ, ATT_CK
    qb = pl.program_id(1)
    nch = qb + 1
    lo_k = lax.broadcasted_iota(I32, (ck, LANES), 1) < HALF
    lo_q = lax.broadcasted_iota(I32, (tq, LANES), 1) < HALF

    key_pos = lax.broadcasted_iota(I32, (ck, tq), 0)
    q_pos = qb * tq + lax.broadcasted_iota(I32, (ck, tq), 1)
    w_t = jnp.transpose(kiwi_ref[0])[IDX_DIM:IDX_DIM + IDX_HEADS, :] * IDX_SCALE

    def score_chunk(c, carry):
        k0 = pl.multiple_of(c * ck, ck)
        kk = kib_ref[0, pl.ds(k0, ck), :]
        zero = jnp.zeros_like(kk)
        k_lo, k_hi = jnp.where(lo_k, kk, zero), jnp.where(lo_k, zero, kk)
        acc = jnp.zeros((ck, tq), F32)
        for p in range(IDX_HEADS // 2):
            qp = qi_ref[0, :, p * LANES:(p + 1) * LANES]
            acc = acc + w_t[2 * p:2 * p + 1, :] * jnp.maximum(_dot_nt(k_lo, qp), 0.0)
            acc = acc + w_t[2 * p + 1:2 * p + 2, :] * jnp.maximum(_dot_nt(k_hi, qp), 0.0)
        acc = jnp.where(c * ck + key_pos <= q_pos, acc, NEG_INF)
        key_ref[c] = _float_key(acc)
        return carry

    lax.fori_loop(0, nch, score_chunk, 0)
    _selection_bias_t(key_ref, bias_ref, nch)

    m_ref[...] = jnp.full(m_ref.shape, -jnp.inf, F32)
    acc_ref[...] = jnp.zeros(acc_ref.shape, F32)

    def attn_chunk(c, carry):
        k0 = pl.multiple_of(c * ck, ck)
        kv = kvb_ref[0, pl.ds(k0, ck), :]
        bias = bias_ref[c]
        rel = (c * ck - (qb + 1) * tq + 1 + lax.broadcasted_iota(I32, (1, ck), 1)).astype(F32)
        for j in range(KV_HEADS // 2):
            kt = kv[:, j * LANES:(j + 1) * LANES]
            vt = kv[:, KV_DIM + j * LANES:KV_DIM + (j + 1) * LANES]
            kts, vts = _swap_halves(kt), _swap_halves(vt)
            zero, one = jnp.zeros_like(kt), jnp.ones_like(kt)
            for gg in range(2):
                g = 2 * j + gg
                k_own, k_swp = (kt, kts) if gg == 0 else (kts, kt)
                v_own, v_swp = (vt, vts) if gg == 0 else (vts, vt)
                k_lo, k_hi = jnp.where(lo_k, k_own, zero), jnp.where(lo_k, zero, k_swp)
                v_lo, v_hi = jnp.where(lo_k, v_own, one), jnp.where(lo_k, one, v_swp)
                for mm in range(HEADS_PER_KV // 2):
                    pidx = (HEADS_PER_KV // 2) * g + mm
                    qp = q_ref[0, :, pidx * LANES:(pidx + 1) * LANES]
                    for half, (kmat, vmat) in enumerate(((k_lo, v_lo), (k_hi, v_hi))):
                        h = 2 * pidx + half
                        s = _dot_nt(qp, kmat) + (_alibi_slope(h) * LOG2E) * rel + bias
                        m_old = m_ref[h]
                        m_new = jnp.maximum(m_old, jnp.max(s, axis=1, keepdims=True))
                        p = jnp.exp2(s - jnp.concatenate([m_new, m_new], axis=1))
                        m_ref[h] = m_new
                        acc_ref[h] = (jnp.exp2(m_old - m_new) * acc_ref[h]
                                      + jnp.dot(p.astype(BF16), vmat, preferred_element_type=F32))
        return carry

    lax.fori_loop(0, nch, attn_chunk, 0)
    for pidx in range(N_HEADS // 2):
        a_e, a_o = acc_ref[2 * pidx], acc_ref[2 * pidx + 1]
        num = jnp.where(lo_q, a_e, a_o)
        den = jnp.where(lo_q, pltpu.roll(a_e, HALF, 1), pltpu.roll(a_o, HALF, 1))
        o_ref[0, :, pidx * LANES:(pidx + 1) * LANES] = (num / den).astype(BF16)


def _attn_prompt(qi, kiwi, kib, q, kvb):
    n, t, _ = q.shape
    tq = ATT_TQ
    return pl.pallas_call(
        _attn_prompt_kernel,
        grid=(n, t // tq),
        in_specs=[pl.BlockSpec((1, tq, IDX_HEADS * IDX_DIM), lambda b, i: (b, i, 0)),
                  pl.BlockSpec((1, tq, LANES), lambda b, i: (b, i, 0)),
                  pl.BlockSpec((1, t, LANES), lambda b, i: (b, 0, 0)),
                  pl.BlockSpec((1, tq, N_HEADS * HEAD_DIM), lambda b, i: (b, i, 0)),
                  pl.BlockSpec((1, t, 2 * KV_DIM), lambda b, i: (b, 0, 0))],
        out_specs=pl.BlockSpec((1, tq, N_HEADS * HEAD_DIM), lambda b, i: (b, i, 0)),
        out_shape=jax.ShapeDtypeStruct((n, t, N_HEADS * HEAD_DIM), BF16),
        scratch_shapes=[pltpu.VMEM((t // ATT_CK, tq, ATT_CK), I32),
                        pltpu.VMEM((t // ATT_CK, tq, ATT_CK), F32),
                        pltpu.VMEM((N_HEADS, tq, LANES), F32),
                        pltpu.VMEM((N_HEADS, tq, LANES), F32)],
        compiler_params=_cparams(("arbitrary", "arbitrary")),
        name="attn_prompt",
    )(qi, kiwi, kib, q, kvb)


SAMPLE_KC = 2048


def _start_page_copies(pt_ref, src_hbm, dst, sem, base, n_pages):
    def body(p, carry):
        col = pl.multiple_of(p * PAGE_SIZE, PAGE_SIZE)
        pltpu.make_async_copy(src_hbm.at[pt_ref[base + p]], dst.at[:, pl.ds(col, PAGE_SIZE)], sem).start()
        return carry
    lax.fori_loop(0, n_pages, body, 0)


def _sample_score_kernel(pt_ref, qi_ref, wcol_ref, kinew_ref, cki_hbm, sc_ref, kibuf, sem, *, n_pages):
    b = pl.program_id(0)
    nb = pl.num_programs(0)
    past = n_pages * PAGE_SIZE
    slot = b % 2

    @pl.when(b == 0)
    def _():
        _start_page_copies(pt_ref, cki_hbm, kibuf.at[0], sem.at[0], 0, n_pages)

    @pl.when(b + 1 < nb)
    def _():
        _start_page_copies(pt_ref, cki_hbm, kibuf.at[1 - slot], sem.at[1 - slot], (b + 1) * n_pages, n_pages)

    pltpu.make_async_copy(kibuf.at[slot], kibuf.at[slot], sem.at[slot]).wait()

    qi = qi_ref[0]
    wsc = wcol_ref[0] * IDX_SCALE
    for c in range(past // SAMPLE_KC):
        kc = kibuf[slot, :, c * SAMPLE_KC:(c + 1) * SAMPLE_KC].astype(BF16)
        d = jnp.dot(qi, kc, preferred_element_type=F32)
        sc_ref[0, :, c * SAMPLE_KC:(c + 1) * SAMPLE_KC] = jnp.sum(wsc * jnp.maximum(d, 0.0), axis=0, keepdims=True)
    dn = jnp.sum(qi.astype(F32) * kinew_ref[0], axis=1, keepdims=True)
    s_new = jnp.sum(wsc * jnp.maximum(dn, 0.0), axis=0, keepdims=True)
    ln = lax.broadcasted_iota(I32, (1, ATT_CK), 1)
    sc_ref[0, :, past:] = jnp.where(ln == 0, s_new, NEG_INF)


def _sample_scores(page_table, qi3, wcol, kinew, cki2):
    n, n_pages = page_table.shape
    past = n_pages * PAGE_SIZE
    grid_spec = pltpu.PrefetchScalarGridSpec(
        num_scalar_prefetch=1,
        grid=(n,),
        in_specs=[pl.BlockSpec((1, IDX_HEADS, IDX_DIM), lambda b, pt: (b, 0, 0)),
                  pl.BlockSpec((1, IDX_HEADS, 1), lambda b, pt: (b, 0, 0)),
                  pl.BlockSpec((1, 1, IDX_DIM), lambda b, pt: (b, 0, 0)),
                  pl.BlockSpec(memory_space=pl.ANY)],
        out_specs=pl.BlockSpec((1, 1, past + ATT_CK), lambda b, pt: (b, 0, 0)),
        scratch_shapes=[pltpu.VMEM((2, IDX_DIM, past), F32),
                        pltpu.SemaphoreType.DMA((2,))],
    )
    return pl.pallas_call(
        functools.partial(_sample_score_kernel, n_pages=n_pages),
        grid_spec=grid_spec,
        out_shape=jax.ShapeDtypeStruct((n, 1, past + ATT_CK), F32),
        compiler_params=_cparams(("arbitrary",)),
        name="sample_scores",
    )(page_table.reshape(-1), qi3, wcol, kinew, cki2)


def _sample_attn_kernel(pt_ref, sc_ref, qbd_ref, knew_ref, vnew_ref, ck_hbm, cv_hbm, o_ref,
                        key_ref, bias_ref, kbuf, vbuf, semk, semv, *, n_pages):
    b = pl.program_id(0)
    nb = pl.num_programs(0)
    past = n_pages * PAGE_SIZE
    nch = sc_ref.shape[0]
    slot = b % 2

    def start_all(bb, sl):
        _start_page_copies(pt_ref, ck_hbm, kbuf.at[sl], semk.at[sl], bb * n_pages, n_pages)
        _start_page_copies(pt_ref, cv_hbm, vbuf.at[sl], semv.at[sl], bb * n_pages, n_pages)

    @pl.when(b == 0)
    def _():
        start_all(0, 0)
        key_ref[...] = _float_key(sc_ref[...])
        _selection_bias(key_ref, bias_ref, nch)

    @pl.when(b + 1 < nb)
    def _():
        start_all(b + 1, 1 - slot)

    pltpu.make_async_copy(kbuf.at[slot], kbuf.at[slot], semk.at[slot]).wait()
    pltpu.make_async_copy(vbuf.at[slot], vbuf.at[slot], semv.at[slot]).wait()

    qbd = qbd_ref[0]
    bias_row = jnp.concatenate([bias_ref[c, pl.ds(b, 1), :] for c in range(nch)], axis=1)
    s_parts = [jnp.dot(qbd, kbuf[slot, :, c * SAMPLE_KC:(c + 1) * SAMPLE_KC].astype(BF16),
                       preferred_element_type=F32) for c in range(past // SAMPLE_KC)]
    s_new = jnp.sum(qbd.astype(F32) * knew_ref[0], axis=1, keepdims=True)
    ln = lax.broadcasted_iota(I32, (N_HEADS, ATT_CK), 1)
    s = jnp.concatenate(s_parts + [jnp.where(ln == 0, s_new, 0.0)], axis=1)
    hrow = lax.broadcasted_iota(I32, (N_HEADS, 1), 0)
    slope2 = jnp.exp2(-8.0 * (hrow + 1).astype(F32) / N_HEADS) * LOG2E
    rel = (lax.broadcasted_iota(I32, (1, past + ATT_CK), 1) - past).astype(F32)
    s = s + slope2 * rel + bias_row
    m = jnp.max(s, axis=1, keepdims=True)
    p = jnp.exp2(s - m)
    l = jnp.sum(p, axis=1, keepdims=True)
    o_all = p[:, past:past + 1] * vnew_ref[0]
    for c in range(past // SAMPLE_KC):
        o_all = o_all + _dot_nt(p[:, c * SAMPLE_KC:(c + 1) * SAMPLE_KC].astype(BF16),
                                vbuf[slot, :, c * SAMPLE_KC:(c + 1) * SAMPLE_KC].astype(BF16))
    o_all = o_all / l
    out = jnp.zeros((N_HEADS, HEAD_DIM), F32)
    for g in range(KV_HEADS):
        own = (hrow >= g * HEADS_PER_KV) & (hrow < (g + 1) * HEADS_PER_KV)
        out = out + jnp.where(own, o_all[:, g * HEAD_DIM:(g + 1) * HEAD_DIM], 0.0)
    o_ref[0] = out


def _sample_attn(page_table, scores3, qbd, knew, vnew, ck2, cv2):
    n, n_pages = page_table.shape
    past = n_pages * PAGE_SIZE
    nch = scores3.shape[0]
    grid_spec = pltpu.PrefetchScalarGridSpec(
        num_scalar_prefetch=1,
        grid=(n,),
        in_specs=[pl.BlockSpec((nch, n, ATT_CK), lambda b, pt: (0, 0, 0)),
                  pl.BlockSpec((1, N_HEADS, KV_DIM), lambda b, pt: (b, 0, 0)),
                  pl.BlockSpec((1, 1, KV_DIM), lambda b, pt: (b, 0, 0)),
                  pl.BlockSpec((1, 1, KV_DIM), lambda b, pt: (b, 0, 0)),
                  pl.BlockSpec(memory_space=pl.ANY),
                  pl.BlockSpec(memory_space=pl.ANY)],
        out_specs=pl.BlockSpec((1, N_HEADS, HEAD_DIM), lambda b, pt: (b, 0, 0)),
        scratch_shapes=[pltpu.VMEM((nch, n, ATT_CK), I32),
                        pltpu.VMEM((nch, n, ATT_CK), F32),
                        pltpu.VMEM((2, KV_DIM, past), F32),
                        pltpu.VMEM((2, KV_DIM, past), F32),
                        pltpu.SemaphoreType.DMA((2,)),
                        pltpu.SemaphoreType.DMA((2,))],
    )
    return pl.pallas_call(
        functools.partial(_sample_attn_kernel, n_pages=n_pages),
        grid_spec=grid_spec,
        out_shape=jax.ShapeDtypeStruct((n, N_HEADS, HEAD_DIM), F32),
        compiler_params=_cparams(("arbitrary",)),
        name="sample_attn",
    )(page_table.reshape(-1), scores3, qbd, knew, vnew, ck2, cv2)


N_CT = D_MODEL // LANES
TOK_GRP = SUBLANES * N_CT


def _tile_perm(to_token_major):
    r = lax.broadcasted_iota(I32, (TOK_GRP, TOK_GRP), 1 if to_token_major else 0)
    k = lax.broadcasted_iota(I32, (TOK_GRP, TOK_GRP), 0 if to_token_major else 1)
    return jnp.where((r // SUBLANES == k % N_CT) & (r % SUBLANES == k // N_CT), 1.0, 0.0).astype(BF16)


def _to_token_major(x, perm):
    rows = x.shape[0]
    out = []
    for t in range(rows // SUBLANES):
        xs = x[t * SUBLANES:(t + 1) * SUBLANES]
        cm = jnp.concatenate([xs[:, c * LANES:(c + 1) * LANES] for c in range(N_CT)], axis=0).astype(BF16)
        out.append(jnp.dot(perm, cm, preferred_element_type=F32).astype(BF16))
    return jnp.stack(out).reshape(rows, N_CT, LANES)


def _from_token_major(xt, perm):
    rows = xt.shape[0]
    xg = xt.reshape(rows // SUBLANES, TOK_GRP, LANES)
    out = jnp.stack([jnp.dot(perm, xg[t], preferred_element_type=F32) for t in range(rows // SUBLANES)])
    return jnp.concatenate([out[:, c * SUBLANES:(c + 1) * SUBLANES, :].reshape(rows, LANES) for c in range(N_CT)],
                           axis=1)


def _merge_ln1_kernel(*refs, aliased, n_tiles):
    n_in = 12
    outs = refs[n_in + 1:] if aliased else refs[n_in:]
    xt_ref = outs[0]

    @pl.when(pl.program_id(0) < n_tiles)
    def _():
        _merge_ln1_tile(*refs[:n_in], *outs)

    @pl.when(pl.program_id(0) >= n_tiles)
    def _():
        xt_ref[...] = jnp.zeros(xt_ref.shape, BF16)


def _merge_ln1_tile(y_ref, o_ref, gc_ref, ga_ref, x_ref, wc_ref, wa_ref, wo_ref, g_ref, b_ref, wr_ref,
                    cnt_in_ref, xt_ref, x1_ref, ei_ref, gt_ref, cnt_ref):
    @pl.when(pl.program_id(0) == 0)
    def _():
        cnt_ref[...] = cnt_in_ref[...]

    conv_out = jnp.dot(y_ref[...], wc_ref[...], preferred_element_type=F32)
    attn_out = jnp.dot(o_ref[...], wa_ref[...], preferred_element_type=F32)
    merged = gc_ref[...].astype(F32) * conv_out + ga_ref[...].astype(F32) * attn_out
    mix = jnp.dot(merged.astype(BF16), wo_ref[...], preferred_element_type=F32)
    x1 = _layer_norm(DEEPNORM_ALPHA * x_ref[...] + mix, g_ref[...], b_ref[...])
    x1_ref[...] = x1
    xt_ref[...] = _to_token_major(x1, _tile_perm(True))

    x_hi = x1.astype(BF16)
    x_lo = (x1 - x_hi.astype(F32)).astype(BF16)
    logits = (jnp.dot(x_hi, wr_ref[0], preferred_element_type=F32)
              + jnp.dot(x_lo, wr_ref[0], preferred_element_type=F32)
              + jnp.dot(x_hi, wr_ref[1], preferred_element_type=F32))
    lane = lax.broadcasted_iota(I32, logits.shape, 1)
    is_g = lane < N_GROUPS
    lg = jnp.where(is_g, logits, -jnp.inf)
    eg = jnp.exp(lg - jnp.max(lg, axis=1, keepdims=True))
    pg = eg / jnp.sum(eg, axis=1, keepdims=True)
    g_top = jnp.max(pg, axis=1, keepdims=True)
    g_idx = jnp.min(jnp.where(is_g & (pg == g_top), lane, LANES), axis=1, keepdims=True)
    lo = N_GROUPS + g_idx * EXPERTS_PER_GROUP
    is_e = (lane >= lo) & (lane < lo + EXPERTS_PER_GROUP)
    le = jnp.where(is_e, logits, -jnp.inf)
    ee = jnp.exp(le - jnp.max(le, axis=1, keepdims=True))
    pe = ee / jnp.sum(ee, axis=1, keepdims=True)
    p1 = jnp.max(jnp.where(is_e, pe, -1.0), axis=1, keepdims=True)
    i1 = jnp.min(jnp.where(is_e & (pe == p1), lane, 2 * LANES), axis=1, keepdims=True)
    rest = is_e & (lane != i1)
    p2 = jnp.max(jnp.where(rest, pe, -1.0), axis=1, keepdims=True)
    i2 = jnp.min(jnp.where(rest & (pe == p2), lane, 2 * LANES), axis=1, keepdims=True)
    den = p1 + p2
    gt_ref[...] = jnp.where(lane == 0, g_top * p1 / den, jnp.where(lane == 1, g_top * p2 / den, 0.0))

    e1, e2 = i1 - N_GROUPS, i2 - N_GROUPS
    tm = logits.shape[0]
    oh1 = jnp.where(lane == e1, 1.0, 0.0)
    oh2 = jnp.where(lane == e2, 1.0, 0.0)
    earlier = jnp.where(lax.broadcasted_iota(I32, (tm, tm), 1) < lax.broadcasted_iota(I32, (tm, tm), 0),
                        1.0, 0.0).astype(BF16)
    before1 = jnp.dot(earlier, oh1.astype(BF16), preferred_element_type=F32)
    before2 = jnp.dot(earlier, oh2.astype(BF16), preferred_element_type=F32)
    tot1 = jnp.sum(oh1, axis=0, keepdims=True)
    base = cnt_ref[...]
    r1 = jnp.sum(oh1 * (base + before1), axis=1, keepdims=True)
    r2 = jnp.sum(oh2 * (base + tot1 + before2), axis=1, keepdims=True)
    cnt_ref[...] = base + tot1 + jnp.sum(oh2, axis=0, keepdims=True)
    ei_ref[...] = jnp.where(lane == 0, e1, jnp.where(lane == 1, e2, jnp.where(
        lane == 2, r1.astype(I32), jnp.where(lane == 3, r2.astype(I32), 0))))


def _merge_ln1(yact, o, gc, ga, x, wc, wa, wo, g1, b1, wr, cnt_in, tm, xt_all=None, m_total=None):
    m = x.shape[0]
    aliased = xt_all is not None
    n_tiles = m // tm
    total = xt_all.shape[0] if aliased else m_total
    blk0 = (total - m) // tm if aliased else 0
    n_steps = n_tiles if aliased else pl.cdiv(total, tm)
    row = lambda w: pl.BlockSpec((tm, w), lambda i: (jnp.minimum(i, n_tiles - 1), 0))
    res = lambda a: pl.BlockSpec(a.shape, lambda i: (0,) * a.ndim, pipeline_mode=pl.Buffered(1))
    cnt_spec = pl.BlockSpec((1, LANES), lambda i: (0, 0))
    args = [yact, o, gc, ga, x, wc, wa, wo, g1, b1, wr, cnt_in]
    in_specs = [row(C_CONV), row(N_HEADS * HEAD_DIM), row(D_MODEL), row(D_MODEL), row(D_MODEL),
                res(wc), res(wa), res(wo), res(g1), res(b1), res(wr), cnt_spec]
    if aliased:
        args.append(xt_all)
        in_specs.append(pl.BlockSpec(memory_space=pl.ANY))
    return pl.pallas_call(
        functools.partial(_merge_ln1_kernel, aliased=aliased, n_tiles=n_tiles),
        grid=(n_steps,),
        in_specs=in_specs,
        out_specs=(pl.BlockSpec((tm, N_CT, LANES), lambda i: (blk0 + i, 0, 0)), row(D_MODEL), row(LANES),
                   row(LANES), cnt_spec),
        out_shape=(jax.ShapeDtypeStruct((total, N_CT, LANES), BF16),
                   jax.ShapeDtypeStruct((m, D_MODEL), F32),
                   jax.ShapeDtypeStruct((m, LANES), I32),
                   jax.ShapeDtypeStruct((m, LANES), F32),
                   jax.ShapeDtypeStruct((1, LANES), F32)),
        input_output_aliases={len(args) - 1: 0} if aliased else {},
        compiler_params=_cparams(("arbitrary",)),
        name="merge_ln1",
    )(*args)


N_GSLOT = 3


def _experts_kernel(fb_ref, dest_ref, x_hbm, zeros_hbm, wg_ref, wu_ref, wd_ref, y_hbm,
                    rowtok, xbuf, ybuf, wgb, wub, wdb, gsem, ysem, zsem, *, n_blocks):
    e = pl.program_id(0)
    n_used = fb_ref[N_EXPERTS]
    last = jnp.maximum(n_used - 1, 0)

    def start_gather(blk, sl):
        base = blk * MOE_BLK
        for i in range(MOE_BLK):
            pltpu.make_async_copy(x_hbm.at[rowtok[base + i]], xbuf.at[sl, i], gsem.at[sl]).start()

    def wait_gather(sl):
        pltpu.make_async_copy(xbuf.at[sl], xbuf.at[sl], gsem.at[sl]).wait()

    def write_copy(blk, sl):
        r0 = pl.multiple_of(blk * MOE_BLK, MOE_BLK)
        return pltpu.make_async_copy(ybuf.at[sl], y_hbm.at[pl.ds(r0, MOE_BLK)], ysem.at[sl])

    @pl.when(e == 0)
    def _():
        clear = pltpu.make_async_copy(zeros_hbm, rowtok, zsem.at[0])
        clear.start()
        clear.wait()

        def place(t, carry):
            rowtok[dest_ref[2 * t]] = t
            rowtok[dest_ref[2 * t + 1]] = t
            return carry
        lax.fori_loop(0, dest_ref.shape[0] // 2, place, 0, unroll=8)
        start_gather(0, 0)
        start_gather(jnp.minimum(1, last), 1)

    b0, b1 = fb_ref[e], fb_ref[e + 1]

    @pl.when(b1 > b0)
    def _():
        wgb[...] = wg_ref[0].astype(BF16)
        wub[...] = wu_ref[0].astype(BF16)
        wdb[...] = wd_ref[0].astype(BF16)

    def block(b, carry):
        sl = lax.rem(b, N_GSLOT)
        ysl = lax.rem(b, 2)

        @pl.when(b >= 2)
        def _():
            write_copy(b - 2, ysl).wait()

        wait_gather(sl)
        start_gather(jnp.minimum(b + 2, last), lax.rem(b + 2, N_GSLOT))
        xb = _from_token_major(xbuf[sl], _tile_perm(False)).astype(BF16)
        hg = jnp.dot(xb, wgb[...], preferred_element_type=F32)
        hu = jnp.dot(xb, wub[...], preferred_element_type=F32)
        h = (hg * jax.nn.sigmoid(hg) * hu).astype(BF16)
        y = jnp.dot(h, wdb[...], preferred_element_type=F32)
        ybuf[ysl] = _to_token_major(y, _tile_perm(True))
        write_copy(b, ysl).start()
        return carry

    lax.fori_loop(b0, b1, block, 0)

    @pl.when(e == pl.num_programs(0) - 1)
    def _():
        wait_gather(lax.rem(last + 1, N_GSLOT))
        wait_gather(lax.rem(last + 2, N_GSLOT))

        @pl.when(n_used >= 1)
        def _():
            write_copy(last, lax.rem(last, 2)).wait()

        @pl.when(n_used >= 2)
        def _():
            write_copy(last - 1, lax.rem(last - 1, 2)).wait()

        ybuf[0] = jnp.zeros(ybuf.shape[1:], BF16)

        def fill(blk, carry):
            write_copy(blk, 0).start()
            return carry
        lax.fori_loop(n_used, n_blocks, fill, 0)

        def fill_wait(blk, carry):
            write_copy(blk, 0).wait()
            return carry
        lax.fori_loop(n_used, n_blocks, fill_wait, 0)


def _experts(first_blk, dest, xt_all, wg, wu, wd):
    n_blocks = (dest.shape[0] + N_EXPERTS * (MOE_BLK - 1) + MOE_BLK - 1) // MOE_BLK
    n_rows = n_blocks * MOE_BLK
    grid_spec = pltpu.PrefetchScalarGridSpec(
        num_scalar_prefetch=2,
        grid=(N_EXPERTS,),
        in_specs=[pl.BlockSpec(memory_space=pl.ANY),
                  pl.BlockSpec(memory_space=pl.ANY),
                  pl.BlockSpec((1, D_MODEL, D_EXPERT), lambda e, fb, de: (e, 0, 0)),
                  pl.BlockSpec((1, D_MODEL, D_EXPERT), lambda e, fb, de: (e, 0, 0)),
                  pl.BlockSpec((1, D_EXPERT, D_MODEL), lambda e, fb, de: (e, 0, 0))],
        out_specs=pl.BlockSpec(memory_space=pl.ANY),
        scratch_shapes=[pltpu.SMEM((n_rows,), I32),
                        pltpu.VMEM((N_GSLOT, MOE_BLK, N_CT, LANES), BF16),
                        pltpu.VMEM((2, MOE_BLK, N_CT, LANES), BF16),
                        pltpu.VMEM((D_MODEL, D_EXPERT), BF16),
                        pltpu.VMEM((D_MODEL, D_EXPERT), BF16),
                        pltpu.VMEM((D_EXPERT, D_MODEL), BF16),
                        pltpu.SemaphoreType.DMA((N_GSLOT,)),
                        pltpu.SemaphoreType.DMA((2,)),
                        pltpu.SemaphoreType.DMA((1,))],
    )
    return pl.pallas_call(
        functools.partial(_experts_kernel, n_blocks=n_blocks),
        grid_spec=grid_spec,
        out_shape=jax.ShapeDtypeStruct((n_rows, N_CT, LANES), BF16),
        compiler_params=_cparams(("arbitrary",)),
        name="experts",
    )(first_blk, dest, xt_all, jnp.zeros((n_rows,), I32), wg, wu, wd)


def _combine_ln2_kernel(dest_ref, x1_ref, gt_ref, yb_hbm, g_ref, b_ref, y_ref, ybuf, sem, *, tm, tok0):
    i = pl.program_id(0)
    last = pl.num_programs(0) - 1
    slot = lax.rem(i, 2)

    def row_copy(a, sl, k, r):
        return pltpu.make_async_copy(yb_hbm.at[dest_ref[a]], ybuf.at[sl, k, r], sem.at[sl])

    def start_gather(tile, sl, unrolled):
        base = 2 * (tok0 + tile * tm)
        if unrolled:
            for r in range(tm):
                row_copy(base + 2 * r, sl, 0, r).start()
                row_copy(base + 2 * r + 1, sl, 1, r).start()
        else:
            def body(r, carry):
                row_copy(base + 2 * r, sl, 0, r).start()
                row_copy(base + 2 * r + 1, sl, 1, r).start()
                return carry
            lax.fori_loop(0, tm, body, 0)

    def wait_gather(sl):
        pltpu.make_async_copy(ybuf.at[sl], ybuf.at[sl], sem.at[sl]).wait()

    @pl.when(i == 0)
    def _():
        start_gather(0, 0, unrolled=False)

    wait_gather(slot)
    start_gather(jnp.minimum(i + 1, last), 1 - slot, unrolled=True)
    gt = gt_ref[...]
    perm = _tile_perm(False)
    z = (DEEPNORM_ALPHA * x1_ref[...] + gt[:, 0:1] * _from_token_major(ybuf[slot, 0], perm)
         + gt[:, 1:2] * _from_token_major(ybuf[slot, 1], perm))
    y_ref[...] = _layer_norm(z, g_ref[...], b_ref[...])

    @pl.when(i == last)
    def _():
        wait_gather(1 - slot)


def _combine_ln2(dest, x1, gt, yb, g2, b2, tm, tok0):
    m = x1.shape[0]
    grid_spec = pltpu.PrefetchScalarGridSpec(
        num_scalar_prefetch=1,
        grid=(m // tm,),
        in_specs=[pl.BlockSpec((tm, D_MODEL), lambda i, d: (i, 0)),
                  pl.BlockSpec((tm, LANES), lambda i, d: (i, 0)),
                  pl.BlockSpec(memory_space=pl.ANY),
                  pl.BlockSpec((1, D_MODEL), lambda i, d: (0, 0)),
                  pl.BlockSpec((1, D_MODEL), lambda i, d: (0, 0))],
        out_specs=pl.BlockSpec((tm, D_MODEL), lambda i, d: (i, 0)),
        scratch_shapes=[pltpu.VMEM((2, 2, tm, N_CT, LANES), BF16),
                        pltpu.SemaphoreType.DMA((2,))],
    )
    return pl.pallas_call(
        functools.partial(_combine_ln2_kernel, tm=tm, tok0=tok0),
        grid_spec=grid_spec,
        out_shape=jax.ShapeDtypeStruct((m, D_MODEL), F32),
        compiler_params=_cparams(("arbitrary",)),
        name="combine_ln2",
    )(dest, x1, gt, yb, g2, b2)


def _dispatch_tables(eid, rank, counts):
    padded = (counts + MOE_BLK - 1) // MOE_BLK * MOE_BLK
    pad_end = jnp.cumsum(padded)
    pad_start = pad_end - padded
    start_of = jnp.sum(jnp.where(eid[..., None] == jnp.arange(N_EXPERTS, dtype=I32), pad_start, 0), axis=-1)
    dest = (start_of + rank).reshape(-1).astype(I32)
    first_blk = (jnp.concatenate([pad_start, pad_end[-1:]]) // MOE_BLK).astype(I32)
    return dest, first_blk


def kernel(x_prompt, x_sample, cache_k, cache_v, cache_kidx, state_conv, page_table, w_in, conv_dw_w, conv_dw_b,
           conv_ln_g, conv_ln_b, w_conv_out, w_attn_out, w_out, ln1_g, ln1_b, w_router_group, w_router_expert,
           w_expert_gate, w_expert_up, w_expert_down, ln2_g, ln2_b):
    assert w_in.shape[0] == 1, "single-layer trunk"
    nb, t, d = x_prompt.shape
    ns = x_sample.shape[0]
    n_pool = cache_k.shape[1]
    mp = nb * t
    row2 = lambda a: a.reshape(1, -1)

    w_arr = _arrange_w_in(w_in[0])
    dww, dwb, lng, lnb = conv_dw_w[0], row2(conv_dw_b[0]), row2(conv_ln_g[0]), row2(conv_ln_b[0])
    wc, wa, wo = w_conv_out[0].astype(BF16), w_attn_out[0].astype(BF16), w_out[0].astype(BF16)
    g1, b1, g2, b2 = row2(ln1_g[0]), row2(ln1_b[0]), row2(ln2_g[0]), row2(ln2_b[0])
    wr32 = jnp.concatenate([w_router_group[0], w_router_expert[0],
                            jnp.zeros((d, LANES - N_GROUPS - N_EXPERTS), F32)], axis=1)
    wr_hi = wr32.astype(BF16)
    wr = jnp.stack([wr_hi, (wr32 - wr_hi.astype(F32)).astype(BF16)])

    xp = x_prompt.reshape(mp, d)
    u_p, q_p, qi_p, kv_p, kvb_p, gc_p, ga_p, kiwi_p, kib_p, yact_p = _in_proj(
        xp, w_arr, tm=512, conv=(dww, dwb, lng, lnb, t))
    o_p = _attn_prompt(qi_p.reshape(nb, t, -1), kiwi_p.reshape(nb, t, LANES), kib_p.reshape(nb, t, LANES),
                       q_p.reshape(nb, t, -1), kvb_p.reshape(nb, t, 2 * KV_DIM))
    xt_all, x1_p, ei_p, gt_p, cnt_p = _merge_ln1(yact_p, o_p.reshape(mp, -1), gc_p, ga_p, xp,
                                                 wc, wa, wo, g1, b1, wr, jnp.zeros((1, LANES), F32), tm=256,
                                                 m_total=mp + ns)

    xs = x_sample.reshape(ns, d)
    u_s, q_s, qi_s, kv_s, _, gc_s, ga_s, kiwi_s, _ = _in_proj(xs, w_arr, tm=ns)
    yact_s = _conv_sample(state_conv[0].transpose(1, 0, 2), u_s, dww, dwb, lng, lnb)
    cki2 = cache_kidx.transpose(0, 1, 3, 2).reshape(n_pool, IDX_DIM, PAGE_SIZE)
    ck2 = cache_k.transpose(0, 1, 3, 4, 2).reshape(n_pool, KV_DIM, PAGE_SIZE)
    cv2 = cache_v.transpose(0, 1, 3, 4, 2).reshape(n_pool, KV_DIM, PAGE_SIZE)
    scores_s = _sample_scores(page_table, qi_s.reshape(ns, IDX_HEADS, IDX_DIM),
                              kiwi_s[:, IDX_DIM:IDX_DIM + IDX_HEADS].reshape(ns, IDX_HEADS, 1),
                              kiwi_s[:, :IDX_DIM].reshape(ns, 1, IDX_DIM), cki2)
    scores3 = scores_s.reshape(ns, -1, ATT_CK).transpose(1, 0, 2)
    own_group = (np.arange(N_HEADS)[:, None] // HEADS_PER_KV) == (np.arange(KV_DIM)[None, :] // HEAD_DIM)
    qbd = jnp.where(own_group[None], jnp.tile(q_s.reshape(ns, N_HEADS, HEAD_DIM), (1, 1, KV_HEADS)), 0).astype(BF16)
    o3 = _sample_attn(page_table, scores3, qbd, kv_s[:, :KV_DIM].reshape(ns, 1, KV_DIM),
                      kv_s[:, KV_DIM:].reshape(ns, 1, KV_DIM), ck2, cv2)
    o_s = o3.reshape(ns, N_HEADS * HEAD_DIM).astype(BF16)
    xt_all, x1_s, ei_s, gt_s, cnt = _merge_ln1(yact_s, o_s, gc_s, ga_s, xs, wc, wa, wo, g1, b1, wr, cnt_p, tm=ns,
                                               xt_all=xt_all)

    eid_rank = jnp.concatenate([ei_p[:, :4], ei_s[:, :4]], axis=0)
    dest, first_blk = _dispatch_tables(eid_rank[:, :2], eid_rank[:, 2:], cnt[0, :N_EXPERTS].astype(I32))
    yb = _experts(first_blk, dest, xt_all, w_expert_gate[0], w_expert_up[0], w_expert_down[0])
    y_p = _combine_ln2(dest, x1_p, gt_p, yb, g2, b2, tm=256, tok0=0)
    y_s = _combine_ln2(dest, x1_s, gt_s, yb, g2, b2, tm=ns, tok0=mp)

    y_prompt = y_p.reshape(nb, t, d)
    y_sample = y_s.reshape(ns, 1, d)
    k_prompt = kv_p[:, :KV_DIM].reshape(1, nb, t, KV_HEADS, HEAD_DIM)
    v_prompt = kv_p[:, KV_DIM:].reshape(1, nb, t, KV_HEADS, HEAD_DIM)
    kidx_prompt = kiwi_p[:, :IDX_DIM].reshape(1, nb, t, IDX_DIM)
    conv_prompt = u_p.reshape(nb, t, C_CONV)[:, t - (CONV_W - 1):][None]
    k_sample = kv_s[:, :KV_DIM].reshape(1, ns, 1, KV_HEADS, HEAD_DIM)
    v_sample = kv_s[:, KV_DIM:].reshape(1, ns, 1, KV_HEADS, HEAD_DIM)
    kidx_sample = kiwi_s[:, :IDX_DIM].reshape(1, ns, 1, IDX_DIM)
    conv_sample = jnp.concatenate([state_conv[0][:, 1:], u_s[:, None, :]], axis=1)[None]
    return (y_prompt, y_sample, k_prompt, v_prompt, kidx_prompt, conv_prompt, k_sample, v_sample, kidx_sample,
            conv_sample)
```

```python
import functools

import jax
import jax.numpy as jnp
import numpy as np
from jax import lax
from jax.experimental import pallas as pl
from jax.experimental.pallas import tpu as pltpu

F32 = jnp.float32
BF16 = jnp.bfloat16
I32 = jnp.int32

D_MODEL = 2048
C_CONV = 1024
CONV_W = 31
N_HEADS = 16
HEAD_DIM = 64
KV_HEADS = 4
HEADS_PER_KV = N_HEADS // KV_HEADS
KV_DIM = KV_HEADS * HEAD_DIM
IDX_HEADS = 16
IDX_DIM = 64
MAX_TOPK = 256
N_GROUPS = 8
EXPERTS_PER_GROUP = 8
N_EXPERTS = 64
D_EXPERT = 512
MOE_BLK = 128
PAGE_SIZE = 128
LN_EPS = 1e-5
DEEPNORM_ALPHA = 2.0 ** 0.25
NEG_INF = -1e30
INT_MIN = -(2 ** 31)
NEG_KEY = int(INT_MIN - int(np.array(NEG_INF, np.float32).view(np.int32)))
LOG2E = 1.4426950408889634
Q_SCALE = HEAD_DIM ** -0.5 * LOG2E
IDX_SCALE = IDX_HEADS ** -0.5 * IDX_DIM ** -0.5

LANES = 128
SUBLANES = 8
HALF = LANES // 2
IN_TN = 1024
VMEM_LIMIT = 56 * 1024 * 1024


def _cparams(sem, vmem=VMEM_LIMIT):
    return pltpu.CompilerParams(dimension_semantics=sem, vmem_limit_bytes=vmem)


def _float_key(x):
    b = pltpu.bitcast(x, I32)
    return jnp.where(b < 0, INT_MIN - b, b)


def _layer_norm(z, g, b):
    mu = jnp.mean(z, axis=-1, keepdims=True)
    zc = z - mu
    var = jnp.mean(zc * zc, axis=-1, keepdims=True)
    return zc * lax.rsqrt(var + LN_EPS) * g + b


def _dot_nt(a, b):
    return lax.dot_general(a, b, (((1,), (1,)), ((), ())), preferred_element_type=F32)


def _alibi_slope(h):
    return float(2.0 ** (-8.0 * (h + 1) / N_HEADS))


_T_GLU, _T_Q, _T_QI, _T_KV, _T_GC, _T_GA, _N_TILES = 0, 2, 3, 4, 5, 7, 9
KV_TILE_COLS = 2 * KV_DIM + 2 * LANES
GLU_HALF = IN_TN // 2
CONV_HALO = 32
CONV_STRIP = 32
CONV_ROWS = 256


def _arrange_w_in(w):
    sizes = (C_CONV, C_CONV, N_HEADS * HEAD_DIM, KV_DIM, KV_DIM,
             IDX_HEADS * IDX_DIM, IDX_DIM, IDX_HEADS, D_MODEL, D_MODEL)
    offs = np.cumsum((0,) + sizes)
    wt = w.T
    a, g, q, k, v, qi, ki, wi, gc, ga = [wt[offs[i]:offs[i + 1]] for i in range(10)]
    glu = [jnp.concatenate([a[t * GLU_HALF:(t + 1) * GLU_HALF], g[t * GLU_HALF:(t + 1) * GLU_HALF]], axis=0)
           for t in range(C_CONV // GLU_HALF)]
    zeros = lambda n: jnp.zeros((n, w.shape[0]), w.dtype)
    kv_tile = jnp.concatenate([k, v, ki, wi, zeros(LANES - IDX_DIM - IDX_HEADS), ki, ki,
                               zeros(IN_TN - KV_TILE_COLS)], axis=0)
    return jnp.concatenate(glu + [q, qi, kv_tile, gc, ga], axis=0).astype(BF16)


def _conv_half(jj, new_seq, ubuf, sh_ref, ypre, dww_ref, dwb_ref):
    tm = ypre.shape[0]
    cs = slice(jj * GLU_HALF, (jj + 1) * GLU_HALF)
    ubuf[jj, 0:CONV_HALO, :] = jnp.where(new_seq, 0.0, ubuf[jj, 0:CONV_HALO, :])
    first = CONV_HALO - (CONV_W - 1)
    dwb = dwb_ref[:, cs]
    for rb in range(tm // CONV_ROWS):
        for b in range(SUBLANES):
            rows = CONV_ROWS + SUBLANES * ((CONV_W - 1 - b) // SUBLANES)
            src0 = rb * CONV_ROWS + first + b
            sh_ref[b, 0:rows, :] = ubuf[jj, src0:src0 + rows, :]
        for r in range(CONV_ROWS // CONV_STRIP):
            acc = jnp.broadcast_to(dwb, (CONV_STRIP, GLU_HALF))
            for j in range(CONV_W):
                a, b = divmod(j, SUBLANES)
                r0 = r * CONV_STRIP + a * SUBLANES
                acc = acc + dww_ref[j:j + 1, cs] * sh_ref[b, r0:r0 + CONV_STRIP, :]
            o0 = rb * CONV_ROWS + r * CONV_STRIP
            ypre[o0:o0 + CONV_STRIP, cs] = acc
    ubuf[jj, 0:CONV_HALO, :] = ubuf[jj, tm:tm + CONV_HALO, :]


def _in_proj_kernel(*refs, fuse_conv, tiles_per_seq):
    if fuse_conv:
        (x_ref, w_ref, dww_ref, dwb_ref, lng_ref, lnb_ref,
         u_ref, q_ref, qi_ref, kv_ref, kvb_ref, gc_ref, ga_ref, kiwi_ref, kib_ref, y_ref, kt_ref, vt_ref, kit_ref,
         xb_ref, ubuf, sh_ref, ypre) = refs
    else:
        (x_ref, w_ref, u_ref, q_ref, qi_ref, kv_ref, kvb_ref, gc_ref, ga_ref, kiwi_ref, kib_ref, xb_ref) = refs
    i = pl.program_id(0)
    j = pl.program_id(1)

    @pl.when(j == 0)
    def _():
        xb_ref[...] = x_ref[...].astype(BF16)

    if fuse_conv:
        new_seq = lax.rem(i, tiles_per_seq) == 0

        @pl.when((i == 0) & (j == 0))
        def _():
            ubuf[:, 0:CONV_HALO, :] = jnp.zeros((ubuf.shape[0], CONV_HALO, GLU_HALF), F32)

    def mm(ncols=IN_TN):
        return _dot_nt(xb_ref[...], w_ref[:ncols, :])

    for jj in range(2):
        @pl.when(j == jj)
        def _(jj=jj):
            if fuse_conv and jj == 1:
                _conv_half(0, new_seq, ubuf, sh_ref, ypre, dww_ref, dwb_ref)
            acc = mm()
            u = acc[:, :GLU_HALF] * jax.nn.sigmoid(acc[:, GLU_HALF:])
            u_ref[...] = u
            if fuse_conv:
                ubuf[jj, CONV_HALO:, :] = u

    @pl.when(j == _T_Q)
    def _():
        if fuse_conv:
            _conv_half(1, new_seq, ubuf, sh_ref, ypre, dww_ref, dwb_ref)
        q_ref[...] = (mm() * Q_SCALE).astype(BF16)

    @pl.when(j == _T_QI)
    def _():
        if fuse_conv:
            lng, lnb = lng_ref[...], lnb_ref[...]
            for r in range(ypre.shape[0] // CONV_STRIP):
                rs = slice(r * CONV_STRIP, (r + 1) * CONV_STRIP)
                yn = _layer_norm(ypre[rs, :], lng, lnb)
                y_ref[rs, :] = (yn * jax.nn.sigmoid(yn)).astype(BF16)
        qi_ref[...] = mm().astype(BF16)

    @pl.when(j == _T_KV)
    def _():
        acc = mm(KV_TILE_COLS)
        kv_ref[...] = acc[:, :2 * KV_DIM]
        kvb_ref[...] = acc[:, :2 * KV_DIM].astype(BF16)
        kiwi_ref[...] = acc[:, 2 * KV_DIM:2 * KV_DIM + LANES]
        kib_ref[...] = acc[:, 2 * KV_DIM + LANES:].astype(BF16)
        if fuse_conv:
            kt_ref[0] = jnp.transpose(acc[:, :KV_DIM])
            vt_ref[0] = jnp.transpose(acc[:, KV_DIM:2 * KV_DIM])
            kit_ref[0] = jnp.transpose(acc[:, 2 * KV_DIM:2 * KV_DIM + LANES])[:IDX_DIM]

    @pl.when((j >= _T_GC) & (j < _T_GA))
    def _():
        gc_ref[...] = jax.nn.sigmoid(mm()).astype(BF16)

    @pl.when(j >= _T_GA)
    def _():
        ga_ref[...] = jax.nn.sigmoid(mm()).astype(BF16)


def _in_proj(x, w_arr, tm, conv=None):
    m = x.shape[0]
    fuse_conv = conv is not None

    def cl(lo, n):
        return lambda i, j: (i, jnp.clip(j - lo, 0, n - 1))

    row_blk = lambda w: pl.BlockSpec((tm, w), lambda i, j: (i, 0))
    out_shape = [
        jax.ShapeDtypeStruct((m, C_CONV), F32),
        jax.ShapeDtypeStruct((m, N_HEADS * HEAD_DIM), BF16),
        jax.ShapeDtypeStruct((m, IDX_HEADS * IDX_DIM), BF16),
        jax.ShapeDtypeStruct((m, 2 * KV_DIM), F32),
        jax.ShapeDtypeStruct((m, 2 * KV_DIM), BF16),
        jax.ShapeDtypeStruct((m, D_MODEL), BF16),
        jax.ShapeDtypeStruct((m, D_MODEL), BF16),
        jax.ShapeDtypeStruct((m, LANES), F32),
        jax.ShapeDtypeStruct((m, LANES), BF16),
    ]
    out_specs = [
        pl.BlockSpec((tm, GLU_HALF), cl(_T_GLU, 2)),
        row_blk(IN_TN), row_blk(IN_TN), row_blk(2 * KV_DIM), row_blk(2 * KV_DIM),
        pl.BlockSpec((tm, IN_TN), cl(_T_GC, 2)),
        pl.BlockSpec((tm, IN_TN), cl(_T_GA, 2)),
        row_blk(LANES), row_blk(LANES),
    ]
    args = [x, w_arr]
    in_specs = [pl.BlockSpec((tm, D_MODEL), lambda i, j: (i, 0)),
                pl.BlockSpec((IN_TN, D_MODEL), lambda i, j: (j, 0))]
    scratch = [pltpu.VMEM((tm, D_MODEL), BF16)]
    tiles_per_seq = 1
    if fuse_conv:
        dww, dwb, lng, lnb, seq = conv
        assert seq % tm == 0 and tm % CONV_ROWS == 0
        tiles_per_seq = seq // tm
        const = lambda a: pl.BlockSpec(a.shape, lambda i, j: (0, 0))
        args += [dww, dwb, lng, lnb]
        in_specs += [const(dww), const(dwb), const(lng), const(lnb)]
        out_shape.append(jax.ShapeDtypeStruct((m, C_CONV), BF16))
        out_specs.append(row_blk(C_CONV))
        for feats in (KV_DIM, KV_DIM, IDX_DIM):
            out_shape.append(jax.ShapeDtypeStruct((m // seq, feats, seq), F32))
            out_specs.append(pl.BlockSpec((1, feats, tm), lambda i, j: (i // tiles_per_seq, 0, i % tiles_per_seq)))
        scratch += [pltpu.VMEM((2, CONV_HALO + tm, GLU_HALF), F32),
                    pltpu.VMEM((SUBLANES, CONV_ROWS + SUBLANES * ((CONV_W - 1) // SUBLANES), GLU_HALF), F32),
                    pltpu.VMEM((tm, C_CONV), F32)]
    return pl.pallas_call(
        functools.partial(_in_proj_kernel, fuse_conv=fuse_conv, tiles_per_seq=tiles_per_seq),
        grid=(m // tm, _N_TILES),
        in_specs=in_specs,
        out_specs=tuple(out_specs),
        out_shape=tuple(out_shape),
        scratch_shapes=scratch,
        compiler_params=_cparams(("arbitrary", "arbitrary")),
        name="in_proj",
    )(*args)


def _conv_sample_kernel(state_ref, u_ref, dww_ref, dwb_ref, lng_ref, lnb_ref, y_ref):
    acc = dwb_ref[...] + dww_ref[CONV_W - 1:CONV_W, :] * u_ref[...]
    for j in range(CONV_W - 1):
        acc = acc + dww_ref[j:j + 1, :] * state_ref[j]
    yn = _layer_norm(acc, lng_ref[...], lnb_ref[...])
    y_ref[...] = (yn * jax.nn.sigmoid(yn)).astype(BF16)


def _conv_sample(state_t, u, dww, dwb, lng, lnb):
    n = u.shape[0]
    return pl.pallas_call(
        _conv_sample_kernel,
        out_shape=jax.ShapeDtypeStruct((n, C_CONV), BF16),
        compiler_params=pltpu.CompilerParams(vmem_limit_bytes=VMEM_LIMIT),
        name="conv_sample",
    )(state_t, u, dww, dwb, lng, lnb)


ATT_TQ = 256
ATT_CK = 256


def _count_ge(key_ref, nch, cand):
    def body(c, acc):
        m = jnp.where(key_ref[c] >= cand, 1.0, 0.0)
        return acc + m[:, :LANES] + m[:, LANES:]
    acc = lax.fori_loop(0, nch, body, jnp.zeros((cand.shape[0], LANES), F32))
    return jnp.sum(acc, axis=1, keepdims=True)


def _select_threshold(key_ref, nch, rows):
    def body(i, prefix):
        cand = prefix + jnp.left_shift(jnp.int32(1), 31 - i)
        return jnp.where(_count_ge(key_ref, nch, cand) >= float(MAX_TOPK), cand, prefix)
    return lax.fori_loop(0, 32, body, jnp.full((rows, 1), INT_MIN, I32))


def _selection_bias(key_ref, bias_ref, nch):
    rows, ck = key_ref.shape[1:]
    thr = _select_threshold(key_ref, nch, rows)
    n_tie_take = float(MAX_TOPK) - _count_ge(key_ref, nch, thr + 1)
    tri = jnp.where(lax.broadcasted_iota(I32, (ck, ck), 0) <= lax.broadcasted_iota(I32, (ck, ck), 1),
                    1.0, 0.0).astype(BF16)

    def bias_chunk(c, seen):
        key = key_ref[c]
        eq = key == thr
        eqf = jnp.where(eq, 1.0, 0.0)
        incl = jnp.dot(eqf.astype(BF16), tri, preferred_element_type=F32)
        sel = (key > thr) | (eq & (seen + incl - eqf < n_tie_take))
        bias_ref[c] = jnp.where(sel & (key > NEG_KEY), 0.0, NEG_INF)
        return seen + incl[:, ck - 1:ck]

    lax.fori_loop(0, nch, bias_chunk, jnp.zeros((rows, 1), F32))


def _count_ge_t(key_ref, nch, cand):
    ck, nq = key_ref.shape[1:]

    def body(c, acc):
        m = jnp.where(key_ref[c] >= cand, 1.0, 0.0)
        return acc + jnp.sum(m.reshape(ck // SUBLANES, SUBLANES, nq), axis=0)

    acc = lax.fori_loop(0, nch, body, jnp.zeros((SUBLANES, nq), F32))
    return jnp.sum(acc, axis=0, keepdims=True)


def _selection_bias_t(key_ref, bias_ref, nch):
    ck, nq = key_ref.shape[1:]

    def bit(i, prefix):
        cand = prefix + jnp.left_shift(jnp.int32(1), 31 - i)
        return jnp.where(_count_ge_t(key_ref, nch, cand) >= float(MAX_TOPK), cand, prefix)

    thr = lax.fori_loop(0, 32, bit, jnp.full((1, nq), INT_MIN, I32))
    n_tie_take = float(MAX_TOPK) - _count_ge_t(key_ref, nch, thr + 1)
    tri = jnp.where(lax.broadcasted_iota(I32, (ck, ck), 1) <= lax.broadcasted_iota(I32, (ck, ck), 0),
                    1.0, 0.0).astype(BF16)

    def bias_chunk(c, seen):
        key = key_ref[c]
        eq = key == thr
        eqf = jnp.where(eq, 1.0, 0.0)
        incl = jnp.dot(tri, eqf.astype(BF16), preferred_element_type=F32)
        sel = (key > thr) | (eq & (seen + incl - eqf < n_tie_take))
        bias_ref[c] = jnp.where(sel & (key > NEG_KEY), 0.0, NEG_INF).T
        return seen + incl[ck - 1:ck, :]

    lax.fori_loop(0, nch, bias_chunk, jnp.zeros((1, nq), F32))


def _swap_halves(x):
    return jnp.concatenate([x[:, HALF:], x[:, :HALF]], axis=1)


def _attn_prompt_kernel(qi_ref, kiwi_ref, kib_ref, q_ref, kvb_ref, o_ref,
                        key_ref, bias_ref, m_ref, acc_ref):
    tq, ck = ATT_TQ, ATT_CK
    qb = pl.program_id(1)
    nch = qb + 1
    lo_k = lax.broadcasted_iota(I32, (ck, LANES), 1) < HALF
    lo_q = lax.broadcasted_iota(I32, (tq, LANES), 1) < HALF

    key_pos = lax.broadcasted_iota(I32, (ck, tq), 0)
    q_pos = qb * tq + lax.broadcasted_iota(I32, (ck, tq), 1)
    w_t = jnp.transpose(kiwi_ref[0])[IDX_DIM:IDX_DIM + IDX_HEADS, :] * IDX_SCALE

    def score_chunk(c, carry):
        k0 = pl.multiple_of(c * ck, ck)
        kk = kib_ref[0, pl.ds(k0, ck), :]
        zero = jnp.zeros_like(kk)
        k_lo, k_hi = jnp.where(lo_k, kk, zero), jnp.where(lo_k, zero, kk)
        acc = jnp.zeros((ck, tq), F32)
        for p in range(IDX_HEADS // 2):
            qp = qi_ref[0, :, p * LANES:(p + 1) * LANES]
            acc = acc + w_t[2 * p:2 * p + 1, :] * jnp.maximum(_dot_nt(k_lo, qp), 0.0)
            acc = acc + w_t[2 * p + 1:2 * p + 2, :] * jnp.maximum(_dot_nt(k_hi, qp), 0.0)
        acc = jnp.where(c * ck + key_pos <= q_pos, acc, NEG_INF)
        key_ref[c] = _float_key(acc)
        return carry

    lax.fori_loop(0, nch, score_chunk, 0)
    _selection_bias_t(key_ref, bias_ref, nch)

    m_ref[...] = jnp.full(m_ref.shape, -jnp.inf, F32)
    acc_ref[...] = jnp.zeros(acc_ref.shape, F32)

    def attn_chunk(c, carry):
        k0 = pl.multiple_of(c * ck, ck)
        kv = kvb_ref[0, pl.ds(k0, ck), :]
        bias = bias_ref[c]
        rel = (c * ck - (qb + 1) * tq + 1 + lax.broadcasted_iota(I32, (1, ck), 1)).astype(F32)
        for j in range(KV_HEADS // 2):
            kt = kv[:, j * LANES:(j + 1) * LANES]
            vt = kv[:, KV_DIM + j * LANES:KV_DIM + (j + 1) * LANES]
            kts, vts = _swap_halves(kt), _swap_halves(vt)
            zero, one = jnp.zeros_like(kt), jnp.ones_like(kt)
            for gg in range(2):
                g = 2 * j + gg
                k_own, k_swp = (kt, kts) if gg == 0 else (kts, kt)
                v_own, v_swp = (vt, vts) if gg == 0 else (vts, vt)
                k_lo, k_hi = jnp.where(lo_k, k_own, zero), jnp.where(lo_k, zero, k_swp)
                v_lo, v_hi = jnp.where(lo_k, v_own, one), jnp.where(lo_k, one, v_swp)
                for mm in range(HEADS_PER_KV // 2):
                    pidx = (HEADS_PER_KV // 2) * g + mm
                    qp = q_ref[0, :, pidx * LANES:(pidx + 1) * LANES]
                    for half, (kmat, vmat) in enumerate(((k_lo, v_lo), (k_hi, v_hi))):
                        h = 2 * pidx + half
                        s = _dot_nt(qp, kmat) + (_alibi_slope(h) * LOG2E) * rel + bias
                        m_old = m_ref[h]
                        m_new = jnp.maximum(m_old, jnp.max(s, axis=1, keepdims=True))
                        p = jnp.exp2(s - jnp.concatenate([m_new, m_new], axis=1))
                        m_ref[h] = m_new
                        acc_ref[h] = (jnp.exp2(m_old - m_new) * acc_ref[h]
                                      + jnp.dot(p.astype(BF16), vmat, preferred_element_type=F32))
        return carry

    lax.fori_loop(0, nch, attn_chunk, 0)
    for pidx in range(N_HEADS // 2):
        a_e, a_o = acc_ref[2 * pidx], acc_ref[2 * pidx + 1]
        num = jnp.where(lo_q, a_e, a_o)
        den = jnp.where(lo_q, pltpu.roll(a_e, HALF, 1), pltpu.roll(a_o, HALF, 1))
        o_ref[0, :, pidx * LANES:(pidx + 1) * LANES] = (num / den).astype(BF16)


def _attn_prompt(qi, kiwi, kib, q, kvb):
    n, t, _ = q.shape
    tq = ATT_TQ
    return pl.pallas_call(
        _attn_prompt_kernel,
        grid=(n, t // tq),
        in_specs=[pl.BlockSpec((1, tq, IDX_HEADS * IDX_DIM), lambda b, i: (b, i, 0)),
                  pl.BlockSpec((1, tq, LANES), lambda b, i: (b, i, 0)),
                  pl.BlockSpec((1, t, LANES), lambda b, i: (b, 0, 0)),
                  pl.BlockSpec((1, tq, N_HEADS * HEAD_DIM), lambda b, i: (b, i, 0)),
                  pl.BlockSpec((1, t, 2 * KV_DIM), lambda b, i: (b, 0, 0))],
        out_specs=pl.BlockSpec((1, tq, N_HEADS * HEAD_DIM), lambda b, i: (b, i, 0)),
        out_shape=jax.ShapeDtypeStruct((n, t, N_HEADS * HEAD_DIM), BF16),
        scratch_shapes=[pltpu.VMEM((t // ATT_CK, tq, ATT_CK), I32),
                        pltpu.VMEM((t // ATT_CK, tq, ATT_CK), F32),
                        pltpu.VMEM((N_HEADS, tq, LANES), F32),
                        pltpu.VMEM((N_HEADS, tq, LANES), F32)],
        compiler_params=_cparams(("arbitrary", "arbitrary")),
        name="attn_prompt",
    )(qi, kiwi, kib, q, kvb)


SAMPLE_KC = 2048


def _start_page_copies(pt_ref, src_hbm, dst, sem, base, n_pages):
    def body(p, carry):
        col = pl.multiple_of(p * PAGE_SIZE, PAGE_SIZE)
        pltpu.make_async_copy(src_hbm.at[pt_ref[base + p]], dst.at[:, pl.ds(col, PAGE_SIZE)], sem).start()
        return carry
    lax.fori_loop(0, n_pages, body, 0)


def _sample_score_kernel(pt_ref, qi_ref, wcol_ref, kinew_ref, cki_hbm, sc_ref, kibuf, sem, *, n_pages):
    b = pl.program_id(0)
    nb = pl.num_programs(0)
    past = n_pages * PAGE_SIZE
    slot = b % 2

    @pl.when(b == 0)
    def _():
        _start_page_copies(pt_ref, cki_hbm, kibuf.at[0], sem.at[0], 0, n_pages)

    @pl.when(b + 1 < nb)
    def _():
        _start_page_copies(pt_ref, cki_hbm, kibuf.at[1 - slot], sem.at[1 - slot], (b + 1) * n_pages, n_pages)

    pltpu.make_async_copy(kibuf.at[slot], kibuf.at[slot], sem.at[slot]).wait()

    qi = qi_ref[0]
    wsc = wcol_ref[0] * IDX_SCALE
    for c in range(past // SAMPLE_KC):
        kc = kibuf[slot, :, c * SAMPLE_KC:(c + 1) * SAMPLE_KC].astype(BF16)
        d = jnp.dot(qi, kc, preferred_element_type=F32)
        sc_ref[0, :, c * SAMPLE_KC:(c + 1) * SAMPLE_KC] = jnp.sum(wsc * jnp.maximum(d, 0.0), axis=0, keepdims=True)
    dn = jnp.sum(qi.astype(F32) * kinew_ref[0], axis=1, keepdims=True)
    s_new = jnp.sum(wsc * jnp.maximum(dn, 0.0), axis=0, keepdims=True)
    ln = lax.broadcasted_iota(I32, (1, ATT_CK), 1)
    sc_ref[0, :, past:] = jnp.where(ln == 0, s_new, NEG_INF)


def _sample_scores(page_table, qi3, wcol, kinew, cki2):
    n, n_pages = page_table.shape
    past = n_pages * PAGE_SIZE
    grid_spec = pltpu.PrefetchScalarGridSpec(
        num_scalar_prefetch=1,
        grid=(n,),
        in_specs=[pl.BlockSpec((1, IDX_HEADS, IDX_DIM), lambda b, pt: (b, 0, 0)),
                  pl.BlockSpec((1, IDX_HEADS, 1), lambda b, pt: (b, 0, 0)),
                  pl.BlockSpec((1, 1, IDX_DIM), lambda b, pt: (b, 0, 0)),
                  pl.BlockSpec(memory_space=pl.ANY)],
        out_specs=pl.BlockSpec((1, 1, past + ATT_CK), lambda b, pt: (b, 0, 0)),
        scratch_shapes=[pltpu.VMEM((2, IDX_DIM, past), F32),
                        pltpu.SemaphoreType.DMA((2,))],
    )
    return pl.pallas_call(
        functools.partial(_sample_score_kernel, n_pages=n_pages),
        grid_spec=grid_spec,
        out_shape=jax.ShapeDtypeStruct((n, 1, past + ATT_CK), F32),
        compiler_params=_cparams(("arbitrary",)),
        name="sample_scores",
    )(page_table.reshape(-1), qi3, wcol, kinew, cki2)


def _sample_attn_kernel(pt_ref, sc_ref, qbd_ref, knew_ref, vnew_ref, ck_hbm, cv_hbm, o_ref,
                        key_ref, bias_ref, kbuf, vbuf, semk, semv, *, n_pages):
    b = pl.program_id(0)
    nb = pl.num_programs(0)
    past = n_pages * PAGE_SIZE
    nch = sc_ref.shape[0]
    slot = b % 2

    def start_all(bb, sl):
        _start_page_copies(pt_ref, ck_hbm, kbuf.at[sl], semk.at[sl], bb * n_pages, n_pages)
        _start_page_copies(pt_ref, cv_hbm, vbuf.at[sl], semv.at[sl], bb * n_pages, n_pages)

    @pl.when(b == 0)
    def _():
        start_all(0, 0)
        key_ref[...] = _float_key(sc_ref[...])
        _selection_bias(key_ref, bias_ref, nch)

    @pl.when(b + 1 < nb)
    def _():
        start_all(b + 1, 1 - slot)

    pltpu.make_async_copy(kbuf.at[slot], kbuf.at[slot], semk.at[slot]).wait()
    pltpu.make_async_copy(vbuf.at[slot], vbuf.at[slot], semv.at[slot]).wait()

    qbd = qbd_ref[0]
    bias_row = jnp.concatenate([bias_ref[c, pl.ds(b, 1), :] for c in range(nch)], axis=1)
    s_parts = [jnp.dot(qbd, kbuf[slot, :, c * SAMPLE_KC:(c + 1) * SAMPLE_KC].astype(BF16),
                       preferred_element_type=F32) for c in range(past // SAMPLE_KC)]
    s_new = jnp.sum(qbd.astype(F32) * knew_ref[0], axis=1, keepdims=True)
    ln = lax.broadcasted_iota(I32, (N_HEADS, ATT_CK), 1)
    s = jnp.concatenate(s_parts + [jnp.where(ln == 0, s_new, 0.0)], axis=1)
    hrow = lax.broadcasted_iota(I32, (N_HEADS, 1), 0)
    slope2 = jnp.exp2(-8.0 * (hrow + 1).astype(F32) / N_HEADS) * LOG2E
    rel = (lax.broadcasted_iota(I32, (1, past + ATT_CK), 1) - past).astype(F32)
    s = s + slope2 * rel + bias_row
    m = jnp.max(s, axis=1, keepdims=True)
    p = jnp.exp2(s - m)
    l = jnp.sum(p, axis=1, keepdims=True)
    o_all = p[:, past:past + 1] * vnew_ref[0]
    for c in range(past // SAMPLE_KC):
        o_all = o_all + _dot_nt(p[:, c * SAMPLE_KC:(c + 1) * SAMPLE_KC].astype(BF16),
                                vbuf[slot, :, c * SAMPLE_KC:(c + 1) * SAMPLE_KC].astype(BF16))
    o_all = o_all / l
    out = jnp.zeros((N_HEADS, HEAD_DIM), F32)
    for g in range(KV_HEADS):
        own = (hrow >= g * HEADS_PER_KV) & (hrow < (g + 1) * HEADS_PER_KV)
        out = out + jnp.where(own, o_all[:, g * HEAD_DIM:(g + 1) * HEAD_DIM], 0.0)
    o_ref[0] = out


def _sample_attn(page_table, scores3, qbd, knew, vnew, ck2, cv2):
    n, n_pages = page_table.shape
    past = n_pages * PAGE_SIZE
    nch = scores3.shape[0]
    grid_spec = pltpu.PrefetchScalarGridSpec(
        num_scalar_prefetch=1,
        grid=(n,),
        in_specs=[pl.BlockSpec((nch, n, ATT_CK), lambda b, pt: (0, 0, 0)),
                  pl.BlockSpec((1, N_HEADS, KV_DIM), lambda b, pt: (b, 0, 0)),
                  pl.BlockSpec((1, 1, KV_DIM), lambda b, pt: (b, 0, 0)),
                  pl.BlockSpec((1, 1, KV_DIM), lambda b, pt: (b, 0, 0)),
                  pl.BlockSpec(memory_space=pl.ANY),
                  pl.BlockSpec(memory_space=pl.ANY)],
        out_specs=pl.BlockSpec((1, N_HEADS, HEAD_DIM), lambda b, pt: (b, 0, 0)),
        scratch_shapes=[pltpu.VMEM((nch, n, ATT_CK), I32),
                        pltpu.VMEM((nch, n, ATT_CK), F32),
                        pltpu.VMEM((2, KV_DIM, past), F32),
                        pltpu.VMEM((2, KV_DIM, past), F32),
                        pltpu.SemaphoreType.DMA((2,)),
                        pltpu.SemaphoreType.DMA((2,))],
    )
    return pl.pallas_call(
        functools.partial(_sample_attn_kernel, n_pages=n_pages),
        grid_spec=grid_spec,
        out_shape=jax.ShapeDtypeStruct((n, N_HEADS, HEAD_DIM), F32),
        compiler_params=_cparams(("arbitrary",)),
        name="sample_attn",
    )(page_table.reshape(-1), scores3, qbd, knew, vnew, ck2, cv2)


N_CT = D_MODEL // LANES
TOK_GRP = SUBLANES * N_CT


def _tile_perm(to_token_major):
    r = lax.broadcasted_iota(I32, (TOK_GRP, TOK_GRP), 1 if to_token_major else 0)
    k = lax.broadcasted_iota(I32, (TOK_GRP, TOK_GRP), 0 if to_token_major else 1)
    return jnp.where((r // SUBLANES == k % N_CT) & (r % SUBLANES == k // N_CT), 1.0, 0.0).astype(BF16)


def _to_token_major(x, perm):
    rows = x.shape[0]
    out = []
    for t in range(rows // SUBLANES):
        xs = x[t * SUBLANES:(t + 1) * SUBLANES]
        cm = jnp.concatenate([xs[:, c * LANES:(c + 1) * LANES] for c in range(N_CT)], axis=0).astype(BF16)
        out.append(jnp.dot(perm, cm, preferred_element_type=F32).astype(BF16))
    return jnp.stack(out).reshape(rows, N_CT, LANES)


def _from_token_major(xt, perm):
    rows = xt.shape[0]
    xg = xt.reshape(rows // SUBLANES, TOK_GRP, LANES)
    out = jnp.stack([jnp.dot(perm, xg[t], preferred_element_type=F32) for t in range(rows // SUBLANES)])
    return jnp.concatenate([out[:, c * SUBLANES:(c + 1) * SUBLANES, :].reshape(rows, LANES) for c in range(N_CT)],
                           axis=1)


def _merge_ln1_kernel(*refs, aliased, n_tiles):
    n_in = 12
    outs = refs[n_in + 1:] if aliased else refs[n_in:]
    xt_ref = outs[0]

    @pl.when(pl.program_id(0) < n_tiles)
    def _():
        _merge_ln1_tile(*refs[:n_in], *outs)

    @pl.when(pl.program_id(0) >= n_tiles)
    def _():
        xt_ref[...] = jnp.zeros(xt_ref.shape, BF16)


def _merge_ln1_tile(y_ref, o_ref, gc_ref, ga_ref, x_ref, wc_ref, wa_ref, wo_ref, g_ref, b_ref, wr_ref,
                    cnt_in_ref, xt_ref, x1_ref, ei_ref, gt_ref, cnt_ref):
    @pl.when(pl.program_id(0) == 0)
    def _():
        cnt_ref[...] = cnt_in_ref[...]

    conv_out = jnp.dot(y_ref[...], wc_ref[...], preferred_element_type=F32)
    attn_out = jnp.dot(o_ref[...], wa_ref[...], preferred_element_type=F32)
    merged = gc_ref[...].astype(F32) * conv_out + ga_ref[...].astype(F32) * attn_out
    mix = jnp.dot(merged.astype(BF16), wo_ref[...], preferred_element_type=F32)
    x1 = _layer_norm(DEEPNORM_ALPHA * x_ref[...] + mix, g_ref[...], b_ref[...])
    x1_ref[...] = x1
    xt_ref[...] = _to_token_major(x1, _tile_perm(True))

    x_hi = x1.astype(BF16)
    x_lo = (x1 - x_hi.astype(F32)).astype(BF16)
    logits = (jnp.dot(x_hi, wr_ref[0], preferred_element_type=F32)
              + jnp.dot(x_lo, wr_ref[0], preferred_element_type=F32)
              + jnp.dot(x_hi, wr_ref[1], preferred_element_type=F32))
    lane = lax.broadcasted_iota(I32, logits.shape, 1)
    is_g = lane < N_GROUPS
    lg = jnp.where(is_g, logits, -jnp.inf)
    eg = jnp.exp(lg - jnp.max(lg, axis=1, keepdims=True))
    pg = eg / jnp.sum(eg, axis=1, keepdims=True)
    g_top = jnp.max(pg, axis=1, keepdims=True)
    g_idx = jnp.min(jnp.where(is_g & (pg == g_top), lane, LANES), axis=1, keepdims=True)
    lo = N_GROUPS + g_idx * EXPERTS_PER_GROUP
    is_e = (lane >= lo) & (lane < lo + EXPERTS_PER_GROUP)
    le = jnp.where(is_e, logits, -jnp.inf)
    ee = jnp.exp(le - jnp.max(le, axis=1, keepdims=True))
    pe = ee / jnp.sum(ee, axis=1, keepdims=True)
    p1 = jnp.max(jnp.where(is_e, pe, -1.0), axis=1, keepdims=True)
    i1 = jnp.min(jnp.where(is_e & (pe == p1), lane, 2 * LANES), axis=1, keepdims=True)
    rest = is_e & (lane != i1)
    p2 = jnp.max(jnp.where(rest, pe, -1.0), axis=1, keepdims=True)
    i2 = jnp.min(jnp.where(rest & (pe == p2), lane, 2 * LANES), axis=1, keepdims=True)
    den = p1 + p2
    gt_ref[...] = jnp.where(lane == 0, g_top * p1 / den, jnp.where(lane == 1, g_top * p2 / den, 0.0))

    e1, e2 = i1 - N_GROUPS, i2 - N_GROUPS
    tm = logits.shape[0]
    oh1 = jnp.where(lane == e1, 1.0, 0.0)
    oh2 = jnp.where(lane == e2, 1.0, 0.0)
    earlier = jnp.where(lax.broadcasted_iota(I32, (tm, tm), 1) < lax.broadcasted_iota(I32, (tm, tm), 0),
                        1.0, 0.0).astype(BF16)
    before1 = jnp.dot(earlier, oh1.astype(BF16), preferred_element_type=F32)
    before2 = jnp.dot(earlier, oh2.astype(BF16), preferred_element_type=F32)
    tot1 = jnp.sum(oh1, axis=0, keepdims=True)
    base = cnt_ref[...]
    r1 = jnp.sum(oh1 * (base + before1), axis=1, keepdims=True)
    r2 = jnp.sum(oh2 * (base + tot1 + before2), axis=1, keepdims=True)
    cnt_ref[...] = base + tot1 + jnp.sum(oh2, axis=0, keepdims=True)
    ei_ref[...] = jnp.where(lane == 0, e1, jnp.where(lane == 1, e2, jnp.where(
        lane == 2, r1.astype(I32), jnp.where(lane == 3, r2.astype(I32), 0))))


def _merge_ln1(yact, o, gc, ga, x, wc, wa, wo, g1, b1, wr, cnt_in, tm, xt_all=None, m_total=None):
    m = x.shape[0]
    aliased = xt_all is not None
    n_tiles = m // tm
    total = xt_all.shape[0] if aliased else m_total
    blk0 = (total - m) // tm if aliased else 0
    n_steps = n_tiles if aliased else pl.cdiv(total, tm)
    row = lambda w: pl.BlockSpec((tm, w), lambda i: (jnp.minimum(i, n_tiles - 1), 0))
    res = lambda a: pl.BlockSpec(a.shape, lambda i: (0,) * a.ndim, pipeline_mode=pl.Buffered(1))
    cnt_spec = pl.BlockSpec((1, LANES), lambda i: (0, 0))
    args = [yact, o, gc, ga, x, wc, wa, wo, g1, b1, wr, cnt_in]
    in_specs = [row(C_CONV), row(N_HEADS * HEAD_DIM), row(D_MODEL), row(D_MODEL), row(D_MODEL),
                res(wc), res(wa), res(wo), res(g1), res(b1), res(wr), cnt_spec]
    if aliased:
        args.append(xt_all)
        in_specs.append(pl.BlockSpec(memory_space=pl.ANY))
    return pl.pallas_call(
        functools.partial(_merge_ln1_kernel, aliased=aliased, n_tiles=n_tiles),
        grid=(n_steps,),
        in_specs=in_specs,
        out_specs=(pl.BlockSpec((tm, N_CT, LANES), lambda i: (blk0 + i, 0, 0)), row(D_MODEL), row(LANES),
                   row(LANES), cnt_spec),
        out_shape=(jax.ShapeDtypeStruct((total, N_CT, LANES), BF16),
                   jax.ShapeDtypeStruct((m, D_MODEL), F32),
                   jax.ShapeDtypeStruct((m, LANES), I32),
                   jax.ShapeDtypeStruct((m, LANES), F32),
                   jax.ShapeDtypeStruct((1, LANES), F32)),
        input_output_aliases={len(args) - 1: 0} if aliased else {},
        compiler_params=_cparams(("arbitrary",)),
        name="merge_ln1",
    )(*args)


N_GSLOT = 3


def _experts_kernel(fb_ref, dest_ref, x_hbm, zeros_hbm, wg_ref, wu_ref, wd_ref, y_hbm,
                    rowtok, xbuf, ybuf, wgb, wub, wdb, gsem, ysem, zsem, *, n_blocks):
    e = pl.program_id(0)
    n_used = fb_ref[N_EXPERTS]
    last = jnp.maximum(n_used - 1, 0)

    def start_gather(blk, sl):
        base = blk * MOE_BLK
        for i in range(MOE_BLK):
            pltpu.make_async_copy(x_hbm.at[rowtok[base + i]], xbuf.at[sl, i], gsem.at[sl]).start()

    def wait_gather(sl):
        pltpu.make_async_copy(xbuf.at[sl], xbuf.at[sl], gsem.at[sl]).wait()

    def write_copy(blk, sl):
        r0 = pl.multiple_of(blk * MOE_BLK, MOE_BLK)
        return pltpu.make_async_copy(ybuf.at[sl], y_hbm.at[pl.ds(r0, MOE_BLK)], ysem.at[sl])

    @pl.when(e == 0)
    def _():
        clear = pltpu.make_async_copy(zeros_hbm, rowtok, zsem.at[0])
        clear.start()
        clear.wait()

        def place(t, carry):
            rowtok[dest_ref[2 * t]] = t
            rowtok[dest_ref[2 * t + 1]] = t
            return carry
        lax.fori_loop(0, dest_ref.shape[0] // 2, place, 0, unroll=8)
        start_gather(0, 0)
        start_gather(jnp.minimum(1, last), 1)

    b0, b1 = fb_ref[e], fb_ref[e + 1]

    @pl.when(b1 > b0)
    def _():
        wgb[...] = wg_ref[0].astype(BF16)
        wub[...] = wu_ref[0].astype(BF16)
        wdb[...] = wd_ref[0].astype(BF16)

    def block(b, carry):
        sl = lax.rem(b, N_GSLOT)
        ysl = lax.rem(b, 2)

        @pl.when(b >= 2)
        def _():
            write_copy(b - 2, ysl).wait()

        wait_gather(sl)
        start_gather(jnp.minimum(b + 2, last), lax.rem(b + 2, N_GSLOT))
        xb = _from_token_major(xbuf[sl], _tile_perm(False)).astype(BF16)
        hg = jnp.dot(xb, wgb[...], preferred_element_type=F32)
        hu = jnp.dot(xb, wub[...], preferred_element_type=F32)
        h = (hg * jax.nn.sigmoid(hg) * hu).astype(BF16)
        y = jnp.dot(h, wdb[...], preferred_element_type=F32)
        ybuf[ysl] = _to_token_major(y, _tile_perm(True))
        write_copy(b, ysl).start()
        return carry

    lax.fori_loop(b0, b1, block, 0)

    @pl.when(e == pl.num_programs(0) - 1)
    def _():
        wait_gather(lax.rem(last + 1, N_GSLOT))
        wait_gather(lax.rem(last + 2, N_GSLOT))

        @pl.when(n_used >= 1)
        def _():
            write_copy(last, lax.rem(last, 2)).wait()

        @pl.when(n_used >= 2)
        def _():
            write_copy(last - 1, lax.rem(last - 1, 2)).wait()

        ybuf[0] = jnp.zeros(ybuf.shape[1:], BF16)

        def fill(blk, carry):
            write_copy(blk, 0).start()
            return carry
        lax.fori_loop(n_used, n_blocks, fill, 0)

        def fill_wait(blk, carry):
            write_copy(blk, 0).wait()
            return carry
        lax.fori_loop(n_used, n_blocks, fill_wait, 0)


def _experts(first_blk, dest, xt_all, wg, wu, wd):
    n_blocks = (dest.shape[0] + N_EXPERTS * (MOE_BLK - 1) + MOE_BLK - 1) // MOE_BLK
    n_rows = n_blocks * MOE_BLK
    grid_spec = pltpu.PrefetchScalarGridSpec(
        num_scalar_prefetch=2,
        grid=(N_EXPERTS,),
        in_specs=[pl.BlockSpec(memory_space=pl.ANY),
                  pl.BlockSpec(memory_space=pl.ANY),
                  pl.BlockSpec((1, D_MODEL, D_EXPERT), lambda e, fb, de: (e, 0, 0)),
                  pl.BlockSpec((1, D_MODEL, D_EXPERT), lambda e, fb, de: (e, 0, 0)),
                  pl.BlockSpec((1, D_EXPERT, D_MODEL), lambda e, fb, de: (e, 0, 0))],
        out_specs=pl.BlockSpec(memory_space=pl.ANY),
        scratch_shapes=[pltpu.SMEM((n_rows,), I32),
                        pltpu.VMEM((N_GSLOT, MOE_BLK, N_CT, LANES), BF16),
                        pltpu.VMEM((2, MOE_BLK, N_CT, LANES), BF16),
                        pltpu.VMEM((D_MODEL, D_EXPERT), BF16),
                        pltpu.VMEM((D_MODEL, D_EXPERT), BF16),
                        pltpu.VMEM((D_EXPERT, D_MODEL), BF16),
                        pltpu.SemaphoreType.DMA((N_GSLOT,)),
                        pltpu.SemaphoreType.DMA((2,)),
                        pltpu.SemaphoreType.DMA((1,))],
    )
    return pl.pallas_call(
        functools.partial(_experts_kernel, n_blocks=n_blocks),
        grid_spec=grid_spec,
        out_shape=jax.ShapeDtypeStruct((n_rows, N_CT, LANES), BF16),
        compiler_params=_cparams(("arbitrary",)),
        name="experts",
    )(first_blk, dest, xt_all, jnp.zeros((n_rows,), I32), wg, wu, wd)


def _combine_ln2_kernel(dest_ref, x1_ref, gt_ref, yb_hbm, g_ref, b_ref, y_ref, ybuf, sem, *, tm, tok0):
    i = pl.program_id(0)
    last = pl.num_programs(0) - 1
    slot = lax.rem(i, 2)

    def row_copy(a, sl, k, r):
        return pltpu.make_async_copy(yb_hbm.at[dest_ref[a]], ybuf.at[sl, k, r], sem.at[sl])

    def start_gather(tile, sl, unrolled):
        base = 2 * (tok0 + tile * tm)
        if unrolled:
            for r in range(tm):
                row_copy(base + 2 * r, sl, 0, r).start()
                row_copy(base + 2 * r + 1, sl, 1, r).start()
        else:
            def body(r, carry):
                row_copy(base + 2 * r, sl, 0, r).start()
                row_copy(base + 2 * r + 1, sl, 1, r).start()
                return carry
            lax.fori_loop(0, tm, body, 0)

    def wait_gather(sl):
        pltpu.make_async_copy(ybuf.at[sl], ybuf.at[sl], sem.at[sl]).wait()

    @pl.when(i == 0)
    def _():
        start_gather(0, 0, unrolled=False)

    wait_gather(slot)
    start_gather(jnp.minimum(i + 1, last), 1 - slot, unrolled=True)
    gt = gt_ref[...]
    perm = _tile_perm(False)
    z = (DEEPNORM_ALPHA * x1_ref[...] + gt[:, 0:1] * _from_token_major(ybuf[slot, 0], perm)
         + gt[:, 1:2] * _from_token_major(ybuf[slot, 1], perm))
    y_ref[...] = _layer_norm(z, g_ref[...], b_ref[...])

    @pl.when(i == last)
    def _():
        wait_gather(1 - slot)


def _combine_ln2(dest, x1, gt, yb, g2, b2, tm, tok0):
    m = x1.shape[0]
    grid_spec = pltpu.PrefetchScalarGridSpec(
        num_scalar_prefetch=1,
        grid=(m // tm,),
        in_specs=[pl.BlockSpec((tm, D_MODEL), lambda i, d: (i, 0)),
                  pl.BlockSpec((tm, LANES), lambda i, d: (i, 0)),
                  pl.BlockSpec(memory_space=pl.ANY),
                  pl.BlockSpec((1, D_MODEL), lambda i, d: (0, 0)),
                  pl.BlockSpec((1, D_MODEL), lambda i, d: (0, 0))],
        out_specs=pl.BlockSpec((tm, D_MODEL), lambda i, d: (i, 0)),
        scratch_shapes=[pltpu.VMEM((2, 2, tm, N_CT, LANES), BF16),
                        pltpu.SemaphoreType.DMA((2,))],
    )
    return pl.pallas_call(
        functools.partial(_combine_ln2_kernel, tm=tm, tok0=tok0),
        grid_spec=grid_spec,
        out_shape=jax.ShapeDtypeStruct((m, D_MODEL), F32),
        compiler_params=_cparams(("arbitrary",)),
        name="combine_ln2",
    )(dest, x1, gt, yb, g2, b2)


def _dispatch_tables(eid, rank, counts):
    padded = (counts + MOE_BLK - 1) // MOE_BLK * MOE_BLK
    pad_end = jnp.cumsum(padded)
    pad_start = pad_end - padded
    start_of = jnp.sum(jnp.where(eid[..., None] == jnp.arange(N_EXPERTS, dtype=I32), pad_start, 0), axis=-1)
    dest = (start_of + rank).reshape(-1).astype(I32)
    first_blk = (jnp.concatenate([pad_start, pad_end[-1:]]) // MOE_BLK).astype(I32)
    return dest, first_blk


def kernel(x_prompt, x_sample, cache_k, cache_v, cache_kidx, state_conv, page_table, w_in, conv_dw_w, conv_dw_b,
           conv_ln_g, conv_ln_b, w_conv_out, w_attn_out, w_out, ln1_g, ln1_b, w_router_group, w_router_expert,
           w_expert_gate, w_expert_up, w_expert_down, ln2_g, ln2_b):
    assert w_in.shape[0] == 1, "single-layer trunk"
    nb, t, d = x_prompt.shape
    ns = x_sample.shape[0]
    n_pool = cache_k.shape[1]
    mp = nb * t
    row2 = lambda a: a.reshape(1, -1)

    w_arr = _arrange_w_in(w_in[0])
    dww, dwb, lng, lnb = conv_dw_w[0], row2(conv_dw_b[0]), row2(conv_ln_g[0]), row2(conv_ln_b[0])
    wc, wa, wo = w_conv_out[0].astype(BF16), w_attn_out[0].astype(BF16), w_out[0].astype(BF16)
    g1, b1, g2, b2 = row2(ln1_g[0]), row2(ln1_b[0]), row2(ln2_g[0]), row2(ln2_b[0])
    wr32 = jnp.concatenate([w_router_group[0], w_router_expert[0],
                            jnp.zeros((d, LANES - N_GROUPS - N_EXPERTS), F32)], axis=1)
    wr_hi = wr32.astype(BF16)
    wr = jnp.stack([wr_hi, (wr32 - wr_hi.astype(F32)).astype(BF16)])

    xp = x_prompt.reshape(mp, d)
    u_p, q_p, qi_p, _, kvb_p, gc_p, ga_p, kiwi_p, kib_p, yact_p, kt_p, vt_p, kit_p = _in_proj(
        xp, w_arr, tm=512, conv=(dww, dwb, lng, lnb, t))
    o_p = _attn_prompt(qi_p.reshape(nb, t, -1), kiwi_p.reshape(nb, t, LANES), kib_p.reshape(nb, t, LANES),
                       q_p.reshape(nb, t, -1), kvb_p.reshape(nb, t, 2 * KV_DIM))
    xt_all, x1_p, ei_p, gt_p, cnt_p = _merge_ln1(yact_p, o_p.reshape(mp, -1), gc_p, ga_p, xp,
                                                 wc, wa, wo, g1, b1, wr, jnp.zeros((1, LANES), F32), tm=256,
                                                 m_total=mp + ns)

    xs = x_sample.reshape(ns, d)
    u_s, q_s, qi_s, kv_s, _, gc_s, ga_s, kiwi_s, _ = _in_proj(xs, w_arr, tm=ns)
    yact_s = _conv_sample(state_conv[0].transpose(1, 0, 2), u_s, dww, dwb, lng, lnb)
    cki2 = cache_kidx.transpose(0, 1, 3, 2).reshape(n_pool, IDX_DIM, PAGE_SIZE)
    ck2 = cache_k.transpose(0, 1, 3, 4, 2).reshape(n_pool, KV_DIM, PAGE_SIZE)
    cv2 = cache_v.transpose(0, 1, 3, 4, 2).reshape(n_pool, KV_DIM, PAGE_SIZE)
    scores_s = _sample_scores(page_table, qi_s.reshape(ns, IDX_HEADS, IDX_DIM),
                              kiwi_s[:, IDX_DIM:IDX_DIM + IDX_HEADS].reshape(ns, IDX_HEADS, 1),
                              kiwi_s[:, :IDX_DIM].reshape(ns, 1, IDX_DIM), cki2)
    scores3 = scores_s.reshape(ns, -1, ATT_CK).transpose(1, 0, 2)
    own_group = (np.arange(N_HEADS)[:, None] // HEADS_PER_KV) == (np.arange(KV_DIM)[None, :] // HEAD_DIM)
    qbd = jnp.where(own_group[None], jnp.tile(q_s.reshape(ns, N_HEADS, HEAD_DIM), (1, 1, KV_HEADS)), 0).astype(BF16)
    o3 = _sample_attn(page_table, scores3, qbd, kv_s[:, :KV_DIM].reshape(ns, 1, KV_DIM),
                      kv_s[:, KV_DIM:].reshape(ns, 1, KV_DIM), ck2, cv2)
    o_s = o3.reshape(ns, N_HEADS * HEAD_DIM).astype(BF16)
    xt_all, x1_s, ei_s, gt_s, cnt = _merge_ln1(yact_s, o_s, gc_s, ga_s, xs, wc, wa, wo, g1, b1, wr, cnt_p, tm=ns,
                                               xt_all=xt_all)

    eid_rank = jnp.concatenate([ei_p[:, :4], ei_s[:, :4]], axis=0)
    dest, first_blk = _dispatch_tables(eid_rank[:, :2], eid_rank[:, 2:], cnt[0, :N_EXPERTS].astype(I32))
    yb = _experts(first_blk, dest, xt_all, w_expert_gate[0], w_expert_up[0], w_expert_down[0])
    y_p = _combine_ln2(dest, x1_p, gt_p, yb, g2, b2, tm=256, tok0=0)
    y_s = _combine_ln2(dest, x1_s, gt_s, yb, g2, b2, tm=ns, tok0=mp)

    y_prompt = y_p.reshape(nb, t, d)
    y_sample = y_s.reshape(ns, 1, d)
    k_prompt = kt_p.reshape(nb, KV_HEADS, HEAD_DIM, t).transpose(0, 3, 1, 2)[None]
    v_prompt = vt_p.reshape(nb, KV_HEADS, HEAD_DIM, t).transpose(0, 3, 1, 2)[None]
    kidx_prompt = kit_p.transpose(0, 2, 1)[None]
    conv_prompt = u_p.reshape(nb, t, C_CONV)[:, t - (CONV_W - 1):][None]
    k_sample = kv_s[:, :KV_DIM].reshape(1, ns, 1, KV_HEADS, HEAD_DIM)
    v_sample = kv_s[:, KV_DIM:].reshape(1, ns, 1, KV_HEADS, HEAD_DIM)
    kidx_sample = kiwi_s[:, :IDX_DIM].reshape(1, ns, 1, IDX_DIM)
    conv_sample = jnp.concatenate([state_conv[0][:, 1:], u_s[:, None, :]], axis=1)[None]
    return (y_prompt, y_sample, k_prompt, v_prompt, kidx_prompt, conv_prompt, k_sample, v_sample, kidx_sample,
            conv_sample)
```

```python
import functools

import jax
import jax.numpy as jnp
import numpy as np
from jax import lax
from jax.experimental import pallas as pl
from jax.experimental.pallas import tpu as pltpu

F32 = jnp.float32
BF16 = jnp.bfloat16
I32 = jnp.int32

D_MODEL = 2048
C_CONV = 1024
CONV_W = 31
N_HEADS = 16
HEAD_DIM = 64
KV_HEADS = 4
HEADS_PER_KV = N_HEADS // KV_HEADS
KV_DIM = KV_HEADS * HEAD_DIM
IDX_HEADS = 16
IDX_DIM = 64
MAX_TOPK = 256
N_GROUPS = 8
EXPERTS_PER_GROUP = 8
N_EXPERTS = 64
D_EXPERT = 512
MOE_BLK = 128
PAGE_SIZE = 128
LN_EPS = 1e-5
DEEPNORM_ALPHA = 2.0 ** 0.25
NEG_INF = -1e30
INT_MIN = -(2 ** 31)
NEG_KEY = int(INT_MIN - int(np.array(NEG_INF, np.float32).view(np.int32)))
LOG2E = 1.4426950408889634
Q_SCALE = HEAD_DIM ** -0.5 * LOG2E
IDX_SCALE = IDX_HEADS ** -0.5 * IDX_DIM ** -0.5

LANES = 128
SUBLANES = 8
HALF = LANES // 2
IN_TN = 1024
VMEM_LIMIT = 56 * 1024 * 1024


def _cparams(sem, vmem=VMEM_LIMIT):
    return pltpu.CompilerParams(dimension_semantics=sem, vmem_limit_bytes=vmem)


def _float_key(x):
    b = pltpu.bitcast(x, I32)
    return jnp.where(b < 0, INT_MIN - b, b)


def _layer_norm(z, g, b):
    mu = jnp.mean(z, axis=-1, keepdims=True)
    zc = z - mu
    var = jnp.mean(zc * zc, axis=-1, keepdims=True)
    return zc * lax.rsqrt(var + LN_EPS) * g + b


def _dot_nt(a, b):
    return lax.dot_general(a, b, (((1,), (1,)), ((), ())), preferred_element_type=F32)


def _alibi_slope(h):
    return float(2.0 ** (-8.0 * (h + 1) / N_HEADS))


_T_GLU, _T_Q, _T_QI, _T_KV, _T_GC, _T_GA, _N_TILES = 0, 2, 3, 4, 5, 7, 9
KV_TILE_COLS = 2 * KV_DIM + 2 * LANES
GLU_HALF = IN_TN // 2
CONV_HALO = 32
CONV_STRIP = 32
CONV_ROWS = 256


def _arrange_w_in(w):
    sizes = (C_CONV, C_CONV, N_HEADS * HEAD_DIM, KV_DIM, KV_DIM,
             IDX_HEADS * IDX_DIM, IDX_DIM, IDX_HEADS, D_MODEL, D_MODEL)
    offs = np.cumsum((0,) + sizes)
    wt = w.T
    a, g, q, k, v, qi, ki, wi, gc, ga = [wt[offs[i]:offs[i + 1]] for i in range(10)]
    glu = [jnp.concatenate([a[t * GLU_HALF:(t + 1) * GLU_HALF], g[t * GLU_HALF:(t + 1) * GLU_HALF]], axis=0)
           for t in range(C_CONV // GLU_HALF)]
    zeros = lambda n: jnp.zeros((n, w.shape[0]), w.dtype)
    kv_tile = jnp.concatenate([k, v, ki, wi, zeros(LANES - IDX_DIM - IDX_HEADS), ki, ki,
                               zeros(IN_TN - KV_TILE_COLS)], axis=0)
    return jnp.concatenate(glu + [q, qi, kv_tile, gc, ga], axis=0).astype(BF16)


def _conv_half(jj, new_seq, ubuf, sh_ref, ypre, dww_ref, dwb_ref):
    tm = ypre.shape[0]
    cs = slice(jj * GLU_HALF, (jj + 1) * GLU_HALF)
    ubuf[jj, 0:CONV_HALO, :] = jnp.where(new_seq, 0.0, ubuf[jj, 0:CONV_HALO, :])
    first = CONV_HALO - (CONV_W - 1)
    dwb = dwb_ref[:, cs]
    for rb in range(tm // CONV_ROWS):
        for b in range(SUBLANES):
            rows = CONV_ROWS + SUBLANES * ((CONV_W - 1 - b) // SUBLANES)
            src0 = rb * CONV_ROWS + first + b
            sh_ref[b, 0:rows, :] = ubuf[jj, src0:src0 + rows, :]
        for r in range(CONV_ROWS // CONV_STRIP):
            acc = jnp.broadcast_to(dwb, (CONV_STRIP, GLU_HALF))
            for j in range(CONV_W):
                a, b = divmod(j, SUBLANES)
                r0 = r * CONV_STRIP + a * SUBLANES
                acc = acc + dww_ref[j:j + 1, cs] * sh_ref[b, r0:r0 + CONV_STRIP, :]
            o0 = rb * CONV_ROWS + r * CONV_STRIP
            ypre[o0:o0 + CONV_STRIP, cs] = acc
    ubuf[jj, 0:CONV_HALO, :] = ubuf[jj, tm:tm + CONV_HALO, :]


def _in_proj_kernel(*refs, fuse_conv, tiles_per_seq):
    if fuse_conv:
        (x_ref, w_ref, dww_ref, dwb_ref, lng_ref, lnb_ref,
         u_ref, q_ref, qi_ref, kv_ref, kvb_ref, gc_ref, ga_ref, kiwi_ref, kib_ref, y_ref, kt_ref, vt_ref, kit_ref,
         xb_ref, ubuf, sh_ref, ypre) = refs
    else:
        (x_ref, w_ref, u_ref, q_ref, qi_ref, kv_ref, kvb_ref, gc_ref, ga_ref, kiwi_ref, kib_ref, xb_ref) = refs
    i = pl.program_id(0)
    j = pl.program_id(1)

    @pl.when(j == 0)
    def _():
        xb_ref[...] = x_ref[...].astype(BF16)

    if fuse_conv:
        new_seq = lax.rem(i, tiles_per_seq) == 0

        @pl.when((i == 0) & (j == 0))
        def _():
            ubuf[:, 0:CONV_HALO, :] = jnp.zeros((ubuf.shape[0], CONV_HALO, GLU_HALF), F32)

    def mm(ncols=IN_TN):
        return _dot_nt(xb_ref[...], w_ref[:ncols, :])

    for jj in range(2):
        @pl.when(j == jj)
        def _(jj=jj):
            if fuse_conv and jj == 1:
                _conv_half(0, new_seq, ubuf, sh_ref, ypre, dww_ref, dwb_ref)
            acc = mm()
            u = acc[:, :GLU_HALF] * jax.nn.sigmoid(acc[:, GLU_HALF:])
            u_ref[...] = u
            if fuse_conv:
                ubuf[jj, CONV_HALO:, :] = u

    @pl.when(j == _T_Q)
    def _():
        if fuse_conv:
            _conv_half(1, new_seq, ubuf, sh_ref, ypre, dww_ref, dwb_ref)
        q_ref[...] = (mm() * Q_SCALE).astype(BF16)

    @pl.when(j == _T_QI)
    def _():
        if fuse_conv:
            lng, lnb = lng_ref[...], lnb_ref[...]
            for r in range(ypre.shape[0] // CONV_STRIP):
                rs = slice(r * CONV_STRIP, (r + 1) * CONV_STRIP)
                yn = _layer_norm(ypre[rs, :], lng, lnb)
                y_ref[rs, :] = (yn * jax.nn.sigmoid(yn)).astype(BF16)
        qi_ref[...] = mm().astype(BF16)

    @pl.when(j == _T_KV)
    def _():
        acc = mm(KV_TILE_COLS)
        kv_ref[...] = acc[:, :2 * KV_DIM]
        kvb_ref[...] = acc[:, :2 * KV_DIM].astype(BF16)
        kiwi_ref[...] = acc[:, 2 * KV_DIM:2 * KV_DIM + LANES]
        kib_ref[...] = acc[:, 2 * KV_DIM + LANES:].astype(BF16)
        if fuse_conv:
            kt_ref[0] = jnp.transpose(acc[:, :KV_DIM])
            vt_ref[0] = jnp.transpose(acc[:, KV_DIM:2 * KV_DIM])
            kit_ref[0] = jnp.transpose(acc[:, 2 * KV_DIM:2 * KV_DIM + LANES])[:IDX_DIM]

    @pl.when((j >= _T_GC) & (j < _T_GA))
    def _():
        gc_ref[...] = jax.nn.sigmoid(mm()).astype(BF16)

    @pl.when(j >= _T_GA)
    def _():
        ga_ref[...] = jax.nn.sigmoid(mm()).astype(BF16)


def _in_proj(x, w_arr, tm, conv=None):
    m = x.shape[0]
    fuse_conv = conv is not None

    def cl(lo, n):
        return lambda i, j: (i, jnp.clip(j - lo, 0, n - 1))

    row_blk = lambda w: pl.BlockSpec((tm, w), lambda i, j: (i, 0))
    out_shape = [
        jax.ShapeDtypeStruct((m, C_CONV), F32),
        jax.ShapeDtypeStruct((m, N_HEADS * HEAD_DIM), BF16),
        jax.ShapeDtypeStruct((m, IDX_HEADS * IDX_DIM), BF16),
        jax.ShapeDtypeStruct((m, 2 * KV_DIM), F32),
        jax.ShapeDtypeStruct((m, 2 * KV_DIM), BF16),
        jax.ShapeDtypeStruct((m, D_MODEL), BF16),
        jax.ShapeDtypeStruct((m, D_MODEL), BF16),
        jax.ShapeDtypeStruct((m, LANES), F32),
        jax.ShapeDtypeStruct((m, LANES), BF16),
    ]
    out_specs = [
        pl.BlockSpec((tm, GLU_HALF), cl(_T_GLU, 2)),
        row_blk(IN_TN), row_blk(IN_TN), row_blk(2 * KV_DIM), row_blk(2 * KV_DIM),
        pl.BlockSpec((tm, IN_TN), cl(_T_GC, 2)),
        pl.BlockSpec((tm, IN_TN), cl(_T_GA, 2)),
        row_blk(LANES), row_blk(LANES),
    ]
    args = [x, w_arr]
    in_specs = [pl.BlockSpec((tm, D_MODEL), lambda i, j: (i, 0)),
                pl.BlockSpec((IN_TN, D_MODEL), lambda i, j: (j, 0))]
    scratch = [pltpu.VMEM((tm, D_MODEL), BF16)]
    tiles_per_seq = 1
    if fuse_conv:
        dww, dwb, lng, lnb, seq = conv
        assert seq % tm == 0 and tm % CONV_ROWS == 0
        tiles_per_seq = seq // tm
        const = lambda a: pl.BlockSpec(a.shape, lambda i, j: (0, 0))
        args += [dww, dwb, lng, lnb]
        in_specs += [const(dww), const(dwb), const(lng), const(lnb)]
        out_shape.append(jax.ShapeDtypeStruct((m, C_CONV), BF16))
        out_specs.append(row_blk(C_CONV))
        for feats in (KV_DIM, KV_DIM, IDX_DIM):
            out_shape.append(jax.ShapeDtypeStruct((m // seq, feats, seq), F32))
            out_specs.append(pl.BlockSpec((1, feats, tm), lambda i, j: (i // tiles_per_seq, 0, i % tiles_per_seq)))
        scratch += [pltpu.VMEM((2, CONV_HALO + tm, GLU_HALF), F32),
                    pltpu.VMEM((SUBLANES, CONV_ROWS + SUBLANES * ((CONV_W - 1) // SUBLANES), GLU_HALF), F32),
                    pltpu.VMEM((tm, C_CONV), F32)]
    return pl.pallas_call(
        functools.partial(_in_proj_kernel, fuse_conv=fuse_conv, tiles_per_seq=tiles_per_seq),
        grid=(m // tm, _N_TILES),
        in_specs=in_specs,
        out_specs=tuple(out_specs),
        out_shape=tuple(out_shape),
        scratch_shapes=scratch,
        compiler_params=_cparams(("arbitrary", "arbitrary")),
        name="in_proj",
    )(*args)


def _conv_sample_kernel(state_ref, u_ref, dww_ref, dwb_ref, lng_ref, lnb_ref, y_ref):
    acc = dwb_ref[...] + dww_ref[CONV_W - 1:CONV_W, :] * u_ref[...]
    for j in range(CONV_W - 1):
        acc = acc + dww_ref[j:j + 1, :] * state_ref[j]
    yn = _layer_norm(acc, lng_ref[...], lnb_ref[...])
    y_ref[...] = (yn * jax.nn.sigmoid(yn)).astype(BF16)


def _conv_sample(state_t, u, dww, dwb, lng, lnb):
    n = u.shape[0]
    return pl.pallas_call(
        _conv_sample_kernel,
        out_shape=jax.ShapeDtypeStruct((n, C_CONV), BF16),
        compiler_params=pltpu.CompilerParams(vmem_limit_bytes=VMEM_LIMIT),
        name="conv_sample",
    )(state_t, u, dww, dwb, lng, lnb)


ATT_TQ = 256
ATT_CK = 256


def _count_ge(key_ref, nch, cand):
    def body(c, acc):
        m = jnp.where(key_ref[c] >= cand, 1.0, 0.0)
        return acc + m[:, :LANES] + m[:, LANES:]
    acc = lax.fori_loop(0, nch, body, jnp.zeros((cand.shape[0], LANES), F32))
    return jnp.sum(acc, axis=1, keepdims=True)


def _select_threshold(key_ref, nch, rows):
    def body(i, prefix):
        cand = prefix + jnp.left_shift(jnp.int32(1), 31 - i)
        return jnp.where(_count_ge(key_ref, nch, cand) >= float(MAX_TOPK), cand, prefix)
    return lax.fori_loop(0, 32, body, jnp.full((rows, 1), INT_MIN, I32))


def _selection_bias(key_ref, bias_ref, nch):
    rows, ck = key_ref.shape[1:]
    thr = _select_threshold(key_ref, nch, rows)
    n_tie_take = float(MAX_TOPK) - _count_ge(key_ref, nch, thr + 1)
    tri = jnp.where(lax.broadcasted_iota(I32, (ck, ck), 0) <= lax.broadcasted_iota(I32, (ck, ck), 1),
                    1.0, 0.0).astype(BF16)

    def bias_chunk(c, seen):
        key = key_ref[c]
        eq = key == thr
        eqf = jnp.where(eq, 1.0, 0.0)
        incl = jnp.dot(eqf.astype(BF16), tri, preferred_element_type=F32)
        sel = (key > thr) | (eq & (seen + incl - eqf < n_tie_take))
        bias_ref[c] = jnp.where(sel & (key > NEG_KEY), 0.0, NEG_INF)
        return seen + incl[:, ck - 1:ck]

    lax.fori_loop(0, nch, bias_chunk, jnp.zeros((rows, 1), F32))


def _count_ge_t(key_ref, nch, cand):
    ck, nq = key_ref.shape[1:]

    def body(c, acc):
        m = jnp.where(key_ref[c] >= cand, 1.0, 0.0)
        return acc + jnp.sum(m.reshape(ck // SUBLANES, SUBLANES, nq), axis=0)

    acc = lax.fori_loop(0, nch, body, jnp.zeros((SUBLANES, nq), F32))
    return jnp.sum(acc, axis=0, keepdims=True)


def _selection_bias_t(key_ref, bias_ref, nch):
    ck, nq = key_ref.shape[1:]

    def bit(i, prefix):
        cand = prefix + jnp.left_shift(jnp.int32(1), 31 - i)
        return jnp.where(_count_ge_t(key_ref, nch, cand) >= float(MAX_TOPK), cand, prefix)

    thr = lax.fori_loop(0, 32, bit, jnp.full((1, nq), INT_MIN, I32))
    n_tie_take = float(MAX_TOPK) - _count_ge_t(key_ref, nch, thr + 1)
    tri = jnp.where(lax.broadcasted_iota(I32, (ck, ck), 1) <= lax.broadcasted_iota(I32, (ck, ck), 0),
                    1.0, 0.0).astype(BF16)

    def bias_chunk(c, seen):
        key = key_ref[c]
        eq = key == thr
        eqf = jnp.where(eq, 1.0, 0.0)
        incl = jnp.dot(tri, eqf.astype(BF16), preferred_element_type=F32)
        sel = (key > thr) | (eq & (seen + incl - eqf < n_tie_take))
        bias_ref[c] = jnp.where(sel & (key > NEG_KEY), 0.0, NEG_INF).T
        return seen + incl[ck - 1:ck, :]

    lax.fori_loop(0, nch, bias_chunk, jnp.zeros((1, nq), F32))


def _swap_halves(x):
    return jnp.concatenate([x[:, HALF:], x[:, :HALF]], axis=1)


def _attn_prompt_kernel(qi_ref, kiwi_ref, kib_ref, q_ref, kvb_ref, o_ref,
                        key_ref, bias_ref, m_ref, acc_ref):
    tq, ck = ATT_TQ, ATT_CK
    qb = pl.program_id(1)
    nch = qb + 1
    lo_k = lax.broadcasted_iota(I32, (ck, LANES), 1) < HALF
    lo_q = lax.broadcasted_iota(I32, (tq, LANES), 1) < HALF

    key_pos = lax.broadcasted_iota(I32, (ck, tq), 0)
    q_pos = qb * tq + lax.broadcasted_iota(I32, (ck, tq), 1)
    w_t = jnp.transpose(kiwi_ref[0])[IDX_DIM:IDX_DIM + IDX_HEADS, :] * IDX_SCALE

    def score_chunk(c, carry):
        k0 = pl.multiple_of(c * ck, ck)
        kk = kib_ref[0, pl.ds(k0, ck), :]
        zero = jnp.zeros_like(kk)
        k_lo, k_hi = jnp.where(lo_k, kk, zero), jnp.where(lo_k, zero, kk)
        acc = jnp.zeros((ck, tq), F32)
        for p in range(IDX_HEADS // 2):
            qp = qi_ref[0, :, p * LANES:(p + 1) * LANES]
            acc = acc + w_t[2 * p:2 * p + 1, :] * jnp.maximum(_dot_nt(k_lo, qp), 0.0)
            acc = acc + w_t[2 * p + 1:2 * p + 2, :] * jnp.maximum(_dot_nt(k_hi, qp), 0.0)
        acc = jnp.where(c * ck + key_pos <= q_pos, acc, NEG_INF)
        key_ref[c] = _float_key(acc)
        return carry

    lax.fori_loop(0, nch, score_chunk, 0)
    _selection_bias_t(key_ref, bias_ref, nch)

    m_ref[...] = jnp.full(m_ref.shape, -jnp.inf, F32)
    acc_ref[...] = jnp.zeros(acc_ref.shape, F32)

    def attn_chunk(c, carry):
        k0 = pl.multiple_of(c * ck, ck)
        kv = kvb_ref[0, pl.ds(k0, ck), :]
        bias = bias_ref[c]
        rel = (c * ck - (qb + 1) * tq + 1 + lax.broadcasted_iota(I32, (1, ck), 1)).astype(F32)
        for j in range(KV_HEADS // 2):
            kt = kv[:, j * LANES:(j + 1) * LANES]
            vt = kv[:, KV_DIM + j * LANES:KV_DIM + (j + 1) * LANES]
            kts, vts = _swap_halves(kt), _swap_halves(vt)
            zero, one = jnp.zeros_like(kt), jnp.ones_like(kt)
            for gg in range(2):
                g = 2 * j + gg
                k_own, k_swp = (kt, kts) if gg == 0 else (kts, kt)
                v_own, v_swp = (vt, vts) if gg == 0 else (vts, vt)
                k_lo, k_hi = jnp.where(lo_k, k_own, zero), jnp.where(lo_k, zero, k_swp)
                v_lo, v_hi = jnp.where(lo_k, v_own, one), jnp.where(lo_k, one, v_swp)
                for mm in range(HEADS_PER_KV // 2):
                    pidx = (HEADS_PER_KV // 2) * g + mm
                    qp = q_ref[0, :, pidx * LANES:(pidx + 1) * LANES]
                    for half, (kmat, vmat) in enumerate(((k_lo, v_lo), (k_hi, v_hi))):
                        h = 2 * pidx + half
                        s = _dot_nt(qp, kmat) + (_alibi_slope(h) * LOG2E) * rel + bias
                        m_old = m_ref[h]
                        m_new = jnp.maximum(m_old, jnp.max(s, axis=1, keepdims=True))
                        p = jnp.exp2(s - jnp.concatenate([m_new, m_new], axis=1))
                        m_ref[h] = m_new
                        acc_ref[h] = (jnp.exp2(m_old - m_new) * acc_ref[h]
                                      + jnp.dot(p.astype(BF16), vmat, preferred_element_type=F32))
        return carry

    lax.fori_loop(0, nch, attn_chunk, 0)
    for pidx in range(N_HEADS // 2):
        a_e, a_o = acc_ref[2 * pidx], acc_ref[2 * pidx + 1]
        num = jnp.where(lo_q, a_e, a_o)
        den = jnp.where(lo_q, pltpu.roll(a_e, HALF, 1), pltpu.roll(a_o, HALF, 1))
        o_ref[0, :, pidx * LANES:(pidx + 1) * LANES] = (num / den).astype(BF16)


def _attn_prompt(qi, kiwi, kib, q, kvb):
    n, t, _ = q.shape
    tq = ATT_TQ
    return pl.pallas_call(
        _attn_prompt_kernel,
        grid=(n, t // tq),
        in_specs=[pl.BlockSpec((1, tq, IDX_HEADS * IDX_DIM), lambda b, i: (b, i, 0)),
                  pl.BlockSpec((1, tq, LANES), lambda b, i: (b, i, 0)),
                  pl.BlockSpec((1, t, LANES), lambda b, i: (b, 0, 0)),
                  pl.BlockSpec((1, tq, N_HEADS * HEAD_DIM), lambda b, i: (b, i, 0)),
                  pl.BlockSpec((1, t, 2 * KV_DIM), lambda b, i: (b, 0, 0))],
        out_specs=pl.BlockSpec((1, tq, N_HEADS * HEAD_DIM), lambda b, i: (b, i, 0)),
        out_shape=jax.ShapeDtypeStruct((n, t, N_HEADS * HEAD_DIM), BF16),
        scratch_shapes=[pltpu.VMEM((t // ATT_CK, tq, ATT_CK), I32),
                        pltpu.VMEM((t // ATT_CK, tq, ATT_CK), F32),
                        pltpu.VMEM((N_HEADS, tq, LANES), F32),
                        pltpu.VMEM((N_HEADS, tq, LANES), F32)],
        compiler_params=_cparams(("arbitrary", "arbitrary")),
        name="attn_prompt",
    )(qi, kiwi, kib, q, kvb)


SAMPLE_KC = 2048


def _start_page_copies(pt_ref, src_hbm, dst, sem, base, n_pages):
    def body(p, carry):
        col = pl.multiple_of(p * PAGE_SIZE, PAGE_SIZE)
        pltpu.make_async_copy(src_hbm.at[pt_ref[base + p]], dst.at[:, pl.ds(col, PAGE_SIZE)], sem).start()
        return carry
    lax.fori_loop(0, n_pages, body, 0)


def _sample_score_kernel(pt_ref, qi_ref, wcol_ref, kinew_ref, cki_hbm, sc_ref, kibuf, sem, *, n_pages):
    b = pl.program_id(0)
    nb = pl.num_programs(0)
    past = n_pages * PAGE_SIZE
    slot = b % 2

    @pl.when(b == 0)
    def _():
        _start_page_copies(pt_ref, cki_hbm, kibuf.at[0], sem.at[0], 0, n_pages)

    @pl.when(b + 1 < nb)
    def _():
        _start_page_copies(pt_ref, cki_hbm, kibuf.at[1 - slot], sem.at[1 - slot], (b + 1) * n_pages, n_pages)

    pltpu.make_async_copy(kibuf.at[slot], kibuf.at[slot], sem.at[slot]).wait()

    qi = qi_ref[0]
    wsc = wcol_ref[0] * IDX_SCALE
    for c in range(past // SAMPLE_KC):
        kc = kibuf[slot, :, c * SAMPLE_KC:(c + 1) * SAMPLE_KC].astype(BF16)
        d = jnp.dot(qi, kc, preferred_element_type=F32)
        sc_ref[0, :, c * SAMPLE_KC:(c + 1) * SAMPLE_KC] = jnp.sum(wsc * jnp.maximum(d, 0.0), axis=0, keepdims=True)
    dn = jnp.sum(qi.astype(F32) * kinew_ref[0], axis=1, keepdims=True)
    s_new = jnp.sum(wsc * jnp.maximum(dn, 0.0), axis=0, keepdims=True)
    ln = lax.broadcasted_iota(I32, (1, ATT_CK), 1)
    sc_ref[0, :, past:] = jnp.where(ln == 0, s_new, NEG_INF)


def _sample_scores(page_table, qi3, wcol, kinew, cki2):
    n, n_pages = page_table.shape
    past = n_pages * PAGE_SIZE
    grid_spec = pltpu.PrefetchScalarGridSpec(
        num_scalar_prefetch=1,
        grid=(n,),
        in_specs=[pl.BlockSpec((1, IDX_HEADS, IDX_DIM), lambda b, pt: (b, 0, 0)),
                  pl.BlockSpec((1, IDX_HEADS, 1), lambda b, pt: (b, 0, 0)),
                  pl.BlockSpec((1, 1, IDX_DIM), lambda b, pt: (b, 0, 0)),
                  pl.BlockSpec(memory_space=pl.ANY)],
        out_specs=pl.BlockSpec((1, 1, past + ATT_CK), lambda b, pt: (b, 0, 0)),
        scratch_shapes=[pltpu.VMEM((2, IDX_DIM, past), F32),
                        pltpu.SemaphoreType.DMA((2,))],
    )
    return pl.pallas_call(
        functools.partial(_sample_score_kernel, n_pages=n_pages),
        grid_spec=grid_spec,
        out_shape=jax.ShapeDtypeStruct((n, 1, past + ATT_CK), F32),
        compiler_params=_cparams(("arbitrary",)),
        name="sample_scores",
    )(page_table.reshape(-1), qi3, wcol, kinew, cki2)


def _sample_attn_kernel(pt_ref, sc_ref, qbd_ref, knew_ref, vnew_ref, ck_hbm, cv_hbm, o_ref,
                        key_ref, bias_ref, kbuf, vbuf, semk, semv, *, n_pages):
    b = pl.program_id(0)
    nb = pl.num_programs(0)
    past = n_pages * PAGE_SIZE
    nch = sc_ref.shape[0]
    slot = b % 2

    def start_all(bb, sl):
        _start_page_copies(pt_ref, ck_hbm, kbuf.at[sl], semk.at[sl], bb * n_pages, n_pages)
        _start_page_copies(pt_ref, cv_hbm, vbuf.at[sl], semv.at[sl], bb * n_pages, n_pages)

    @pl.when(b == 0)
    def _():
        start_all(0, 0)
        key_ref[...] = _float_key(sc_ref[...])
        _selection_bias(key_ref, bias_ref, nch)

    @pl.when(b + 1 < nb)
    def _():
        start_all(b + 1, 1 - slot)

    pltpu.make_async_copy(kbuf.at[slot], kbuf.at[slot], semk.at[slot]).wait()
    pltpu.make_async_copy(vbuf.at[slot], vbuf.at[slot], semv.at[slot]).wait()

    qbd = qbd_ref[0]
    bias_row = jnp.concatenate([bias_ref[c, pl.ds(b, 1), :] for c in range(nch)], axis=1)
    s_parts = [jnp.dot(qbd, kbuf[slot, :, c * SAMPLE_KC:(c + 1) * SAMPLE_KC].astype(BF16),
                       preferred_element_type=F32) for c in range(past // SAMPLE_KC)]
    s_new = jnp.sum(qbd.astype(F32) * knew_ref[0], axis=1, keepdims=True)
    ln = lax.broadcasted_iota(I32, (N_HEADS, ATT_CK), 1)
    s = jnp.concatenate(s_parts + [jnp.where(ln == 0, s_new, 0.0)], axis=1)
    hrow = lax.broadcasted_iota(I32, (N_HEADS, 1), 0)
    slope2 = jnp.exp2(-8.0 * (hrow + 1).astype(F32) / N_HEADS) * LOG2E
    rel = (lax.broadcasted_iota(I32, (1, past + ATT_CK), 1) - past).astype(F32)
    s = s + slope2 * rel + bias_row
    m = jnp.max(s, axis=1, keepdims=True)
    p = jnp.exp2(s - m)
    l = jnp.sum(p, axis=1, keepdims=True)
    o_all = p[:, past:past + 1] * vnew_ref[0]
    for c in range(past // SAMPLE_KC):
        o_all = o_all + _dot_nt(p[:, c * SAMPLE_KC:(c + 1) * SAMPLE_KC].astype(BF16),
                                vbuf[slot, :, c * SAMPLE_KC:(c + 1) * SAMPLE_KC].astype(BF16))
    o_all = o_all / l
    out = jnp.zeros((N_HEADS, HEAD_DIM), F32)
    for g in range(KV_HEADS):
        own = (hrow >= g * HEADS_PER_KV) & (hrow < (g + 1) * HEADS_PER_KV)
        out = out + jnp.where(own, o_all[:, g * HEAD_DIM:(g + 1) * HEAD_DIM], 0.0)
    o_ref[0] = out


def _sample_attn(page_table, scores3, qbd, knew, vnew, ck2, cv2):
    n, n_pages = page_table.shape
    past = n_pages * PAGE_SIZE
    nch = scores3.shape[0]
    grid_spec = pltpu.PrefetchScalarGridSpec(
        num_scalar_prefetch=1,
        grid=(n,),
        in_specs=[pl.BlockSpec((nch, n, ATT_CK), lambda b, pt: (0, 0, 0)),
                  pl.BlockSpec((1, N_HEADS, KV_DIM), lambda b, pt: (b, 0, 0)),
                  pl.BlockSpec((1, 1, KV_DIM), lambda b, pt: (b, 0, 0)),
                  pl.BlockSpec((1, 1, KV_DIM), lambda b, pt: (b, 0, 0)),
                  pl.BlockSpec(memory_space=pl.ANY),
                  pl.BlockSpec(memory_space=pl.ANY)],
        out_specs=pl.BlockSpec((1, N_HEADS, HEAD_DIM), lambda b, pt: (b, 0, 0)),
        scratch_shapes=[pltpu.VMEM((nch, n, ATT_CK), I32),
                        pltpu.VMEM((nch, n, ATT_CK), F32),
                        pltpu.VMEM((2, KV_DIM, past), F32),
                        pltpu.VMEM((2, KV_DIM, past), F32),
                        pltpu.SemaphoreType.DMA((2,)),
                        pltpu.SemaphoreType.DMA((2,))],
    )
    return pl.pallas_call(
        functools.partial(_sample_attn_kernel, n_pages=n_pages),
        grid_spec=grid_spec,
        out_shape=jax.ShapeDtypeStruct((n, N_HEADS, HEAD_DIM), F32),
        compiler_params=_cparams(("arbitrary",)),
        name="sample_attn",
    )(page_table.reshape(-1), scores3, qbd, knew, vnew, ck2, cv2)


N_CT = D_MODEL // LANES
TOK_GRP = SUBLANES * N_CT


def _tile_perm(to_token_major):
    r = lax.broadcasted_iota(I32, (TOK_GRP, TOK_GRP), 1 if to_token_major else 0)
    k = lax.broadcasted_iota(I32, (TOK_GRP, TOK_GRP), 0 if to_token_major else 1)
    return jnp.where((r // SUBLANES == k % N_CT) & (r % SUBLANES == k // N_CT), 1.0, 0.0).astype(BF16)


def _to_token_major(x, perm):
    rows = x.shape[0]
    out = []
    for t in range(rows // SUBLANES):
        xs = x[t * SUBLANES:(t + 1) * SUBLANES]
        cm = jnp.concatenate([xs[:, c * LANES:(c + 1) * LANES] for c in range(N_CT)], axis=0).astype(BF16)
        out.append(jnp.dot(perm, cm, preferred_element_type=F32).astype(BF16))
    return jnp.stack(out).reshape(rows, N_CT, LANES)


def _from_token_major(xt, perm):
    rows = xt.shape[0]
    xg = xt.reshape(rows // SUBLANES, TOK_GRP, LANES)
    out = jnp.stack([jnp.dot(perm, xg[t], preferred_element_type=F32) for t in range(rows // SUBLANES)])
    return jnp.concatenate([out[:, c * SUBLANES:(c + 1) * SUBLANES, :].reshape(rows, LANES) for c in range(N_CT)],
                           axis=1)


def _merge_ln1_kernel(*refs, aliased, n_tiles):
    n_in = 12
    outs = refs[n_in + 1:] if aliased else refs[n_in:]
    xt_ref = outs[0]

    @pl.when(pl.program_id(0) < n_tiles)
    def _():
        _merge_ln1_tile(*refs[:n_in], *outs)

    @pl.when(pl.program_id(0) >= n_tiles)
    def _():
        xt_ref[...] = jnp.zeros(xt_ref.shape, BF16)


def _merge_ln1_tile(y_ref, o_ref, gc_ref, ga_ref, x_ref, wc_ref, wa_ref, wo_ref, g_ref, b_ref, wr_ref,
                    cnt_in_ref, xt_ref, x1_ref, ei_ref, gt_ref, cnt_ref):
    @pl.when(pl.program_id(0) == 0)
    def _():
        cnt_ref[...] = cnt_in_ref[...]

    conv_out = jnp.dot(y_ref[...], wc_ref[...], preferred_element_type=F32)
    attn_out = jnp.dot(o_ref[...], wa_ref[...], preferred_element_type=F32)
    merged = gc_ref[...].astype(F32) * conv_out + ga_ref[...].astype(F32) * attn_out
    mix = jnp.dot(merged.astype(BF16), wo_ref[...], preferred_element_type=F32)
    x1 = _layer_norm(DEEPNORM_ALPHA * x_ref[...] + mix, g_ref[...], b_ref[...])
    x1_ref[...] = x1
    xt_ref[...] = _to_token_major(x1, _tile_perm(True))

    x_hi = x1.astype(BF16)
    x_lo = (x1 - x_hi.astype(F32)).astype(BF16)
    logits = (jnp.dot(x_hi, wr_ref[0], preferred_element_type=F32)
              + jnp.dot(x_lo, wr_ref[0], preferred_element_type=F32)
              + jnp.dot(x_hi, wr_ref[1], preferred_element_type=F32))
    lane = lax.broadcasted_iota(I32, logits.shape, 1)
    is_g = lane < N_GROUPS
    lg = jnp.where(is_g, logits, -jnp.inf)
    eg = jnp.exp(lg - jnp.max(lg, axis=1, keepdims=True))
    pg = eg / jnp.sum(eg, axis=1, keepdims=True)
    g_top = jnp.max(pg, axis=1, keepdims=True)
    g_idx = jnp.min(jnp.where(is_g & (pg == g_top), lane, LANES), axis=1, keepdims=True)
    lo = N_GROUPS + g_idx * EXPERTS_PER_GROUP
    is_e = (lane >= lo) & (lane < lo + EXPERTS_PER_GROUP)
    le = jnp.where(is_e, logits, -jnp.inf)
    ee = jnp.exp(le - jnp.max(le, axis=1, keepdims=True))
    pe = ee / jnp.sum(ee, axis=1, keepdims=True)
    p1 = jnp.max(jnp.where(is_e, pe, -1.0), axis=1, keepdims=True)
    i1 = jnp.min(jnp.where(is_e & (pe == p1), lane, 2 * LANES), axis=1, keepdims=True)
    rest = is_e & (lane != i1)
    p2 = jnp.max(jnp.where(rest, pe, -1.0), axis=1, keepdims=True)
    i2 = jnp.min(jnp.where(rest & (pe == p2), lane, 2 * LANES), axis=1, keepdims=True)
    den = p1 + p2
    gt_ref[...] = jnp.where(lane == 0, g_top * p1 / den, jnp.where(lane == 1, g_top * p2 / den, 0.0))

    e1, e2 = i1 - N_GROUPS, i2 - N_GROUPS
    tm = logits.shape[0]
    oh1 = jnp.where(lane == e1, 1.0, 0.0)
    oh2 = jnp.where(lane == e2, 1.0, 0.0)
    earlier = jnp.where(lax.broadcasted_iota(I32, (tm, tm), 1) < lax.broadcasted_iota(I32, (tm, tm), 0),
                        1.0, 0.0).astype(BF16)
    before1 = jnp.dot(earlier, oh1.astype(BF16), preferred_element_type=F32)
    before2 = jnp.dot(earlier, oh2.astype(BF16), preferred_element_type=F32)
    tot1 = jnp.sum(oh1, axis=0, keepdims=True)
    base = cnt_ref[...]
    r1 = jnp.sum(oh1 * (base + before1), axis=1, keepdims=True)
    r2 = jnp.sum(oh2 * (base + tot1 + before2), axis=1, keepdims=True)
    cnt_ref[...] = base + tot1 + jnp.sum(oh2, axis=0, keepdims=True)
    ei_ref[...] = jnp.where(lane == 0, e1, jnp.where(lane == 1, e2, jnp.where(
        lane == 2, r1.astype(I32), jnp.where(lane == 3, r2.astype(I32), 0))))


def _merge_ln1(yact, o, gc, ga, x, wc, wa, wo, g1, b1, wr, cnt_in, tm, xt_all=None, m_total=None):
    m = x.shape[0]
    aliased = xt_all is not None
    n_tiles = m // tm
    total = xt_all.shape[0] if aliased else m_total
    blk0 = (total - m) // tm if aliased else 0
    n_steps = n_tiles if aliased else pl.cdiv(total, tm)
    row = lambda w: pl.BlockSpec((tm, w), lambda i: (jnp.minimum(i, n_tiles - 1), 0))
    res = lambda a: pl.BlockSpec(a.shape, lambda i: (0,) * a.ndim, pipeline_mode=pl.Buffered(1))
    cnt_spec = pl.BlockSpec((1, LANES), lambda i: (0, 0))
    args = [yact, o, gc, ga, x, wc, wa, wo, g1, b1, wr, cnt_in]
    in_specs = [row(C_CONV), row(N_HEADS * HEAD_DIM), row(D_MODEL), row(D_MODEL), row(D_MODEL),
                res(wc), res(wa), res(wo), res(g1), res(b1), res(wr), cnt_spec]
    if aliased:
        args.append(xt_all)
        in_specs.append(pl.BlockSpec(memory_space=pl.ANY))
    return pl.pallas_call(
        functools.partial(_merge_ln1_kernel, aliased=aliased, n_tiles=n_tiles),
        grid=(n_steps,),
        in_specs=in_specs,
        out_specs=(pl.BlockSpec((tm, N_CT, LANES), lambda i: (blk0 + i, 0, 0)), row(D_MODEL), row(LANES),
                   row(LANES), cnt_spec),
        out_shape=(jax.ShapeDtypeStruct((total, N_CT, LANES), BF16),
                   jax.ShapeDtypeStruct((m, D_MODEL), F32),
                   jax.ShapeDtypeStruct((m, LANES), I32),
                   jax.ShapeDtypeStruct((m, LANES), F32),
                   jax.ShapeDtypeStruct((1, LANES), F32)),
        input_output_aliases={len(args) - 1: 0} if aliased else {},
        compiler_params=_cparams(("arbitrary",)),
        name="merge_ln1",
    )(*args)


N_GSLOT = 4


def _experts_kernel(fb_ref, dest_ref, x_hbm, zeros_hbm, wg_ref, wu_ref, wd_ref, y_hbm,
                    rowtok, xbuf, ybuf, wgb, wub, wdb, gsem, ysem, zsem, *, n_blocks):
    e = pl.program_id(0)
    n_used = fb_ref[N_EXPERTS]
    last = jnp.maximum(n_used - 1, 0)

    def start_gather(blk, sl):
        base = blk * MOE_BLK
        for i in range(MOE_BLK):
            pltpu.make_async_copy(x_hbm.at[rowtok[base + i]], xbuf.at[sl, i], gsem.at[sl]).start()

    def wait_gather(sl):
        pltpu.make_async_copy(xbuf.at[sl], xbuf.at[sl], gsem.at[sl]).wait()

    def write_copy(blk, sl):
        r0 = pl.multiple_of(blk * MOE_BLK, MOE_BLK)
        return pltpu.make_async_copy(ybuf.at[sl], y_hbm.at[pl.ds(r0, MOE_BLK)], ysem.at[sl])

    @pl.when(e == 0)
    def _():
        clear = pltpu.make_async_copy(zeros_hbm, rowtok, zsem.at[0])
        clear.start()
        clear.wait()

        def place(t, carry):
            rowtok[dest_ref[2 * t]] = t
            rowtok[dest_ref[2 * t + 1]] = t
            return carry
        lax.fori_loop(0, dest_ref.shape[0] // 2, place, 0, unroll=8)
        for d in range(N_GSLOT - 1):
            start_gather(jnp.minimum(d, last), d)

    b0, b1 = fb_ref[e], fb_ref[e + 1]

    @pl.when(b1 > b0)
    def _():
        wgb[...] = wg_ref[0].astype(BF16)
        wub[...] = wu_ref[0].astype(BF16)
        wdb[...] = wd_ref[0].astype(BF16)

    def block(b, carry):
        sl = lax.rem(b, N_GSLOT)
        ysl = lax.rem(b, 2)

        @pl.when(b >= 2)
        def _():
            write_copy(b - 2, ysl).wait()

        wait_gather(sl)
        start_gather(jnp.minimum(b + N_GSLOT - 1, last), lax.rem(b + N_GSLOT - 1, N_GSLOT))
        xb = _from_token_major(xbuf[sl], _tile_perm(False)).astype(BF16)
        hg = jnp.dot(xb, wgb[...], preferred_element_type=F32)
        hu = jnp.dot(xb, wub[...], preferred_element_type=F32)
        h = (hg * jax.nn.sigmoid(hg) * hu).astype(BF16)
        y = jnp.dot(h, wdb[...], preferred_element_type=F32)
        ybuf[ysl] = _to_token_major(y, _tile_perm(True))
        write_copy(b, ysl).start()
        return carry

    lax.fori_loop(b0, b1, block, 0)

    @pl.when(e == pl.num_programs(0) - 1)
    def _():
        for d in range(1, N_GSLOT):
            wait_gather(lax.rem(last + d, N_GSLOT))

        @pl.when(n_used >= 1)
        def _():
            write_copy(last, lax.rem(last, 2)).wait()

        @pl.when(n_used >= 2)
        def _():
            write_copy(last - 1, lax.rem(last - 1, 2)).wait()

        ybuf[0] = jnp.zeros(ybuf.shape[1:], BF16)

        def fill(blk, carry):
            write_copy(blk, 0).start()
            return carry
        lax.fori_loop(n_used, n_blocks, fill, 0)

        def fill_wait(blk, carry):
            write_copy(blk, 0).wait()
            return carry
        lax.fori_loop(n_used, n_blocks, fill_wait, 0)


def _experts(first_blk, dest, xt_all, wg, wu, wd):
    n_blocks = (dest.shape[0] + N_EXPERTS * (MOE_BLK - 1) + MOE_BLK - 1) // MOE_BLK
    n_rows = n_blocks * MOE_BLK
    grid_spec = pltpu.PrefetchScalarGridSpec(
        num_scalar_prefetch=2,
        grid=(N_EXPERTS,),
        in_specs=[pl.BlockSpec(memory_space=pl.ANY),
                  pl.BlockSpec(memory_space=pl.ANY),
                  pl.BlockSpec((1, D_MODEL, D_EXPERT), lambda e, fb, de: (e, 0, 0)),
                  pl.BlockSpec((1, D_MODEL, D_EXPERT), lambda e, fb, de: (e, 0, 0)),
                  pl.BlockSpec((1, D_EXPERT, D_MODEL), lambda e, fb, de: (e, 0, 0))],
        out_specs=pl.BlockSpec(memory_space=pl.ANY),
        scratch_shapes=[pltpu.SMEM((n_rows,), I32),
                        pltpu.VMEM((N_GSLOT, MOE_BLK, N_CT, LANES), BF16),
                        pltpu.VMEM((2, MOE_BLK, N_CT, LANES), BF16),
                        pltpu.VMEM((D_MODEL, D_EXPERT), BF16),
                        pltpu.VMEM((D_MODEL, D_EXPERT), BF16),
                        pltpu.VMEM((D_EXPERT, D_MODEL), BF16),
                        pltpu.SemaphoreType.DMA((N_GSLOT,)),
                        pltpu.SemaphoreType.DMA((2,)),
                        pltpu.SemaphoreType.DMA((1,))],
    )
    return pl.pallas_call(
        functools.partial(_experts_kernel, n_blocks=n_blocks),
        grid_spec=grid_spec,
        out_shape=jax.ShapeDtypeStruct((n_rows, N_CT, LANES), BF16),
        compiler_params=_cparams(("arbitrary",)),
        name="experts",
    )(first_blk, dest, xt_all, jnp.zeros((n_rows,), I32), wg, wu, wd)


def _combine_ln2_kernel(dest_ref, x1_ref, gt_ref, yb_hbm, g_ref, b_ref, y_ref, ybuf, sem, *, tm, tok0):
    i = pl.program_id(0)
    last = pl.num_programs(0) - 1
    slot = lax.rem(i, 2)

    def row_copy(a, sl, k, r):
        return pltpu.make_async_copy(yb_hbm.at[dest_ref[a]], ybuf.at[sl, k, r], sem.at[sl])

    def start_gather(tile, sl, unrolled):
        base = 2 * (tok0 + tile * tm)
        if unrolled:
            for r in range(tm):
                row_copy(base + 2 * r, sl, 0, r).start()
                row_copy(base + 2 * r + 1, sl, 1, r).start()
        else:
            def body(r, carry):
                row_copy(base + 2 * r, sl, 0, r).start()
                row_copy(base + 2 * r + 1, sl, 1, r).start()
                return carry
            lax.fori_loop(0, tm, body, 0)

    def wait_gather(sl):
        pltpu.make_async_copy(ybuf.at[sl], ybuf.at[sl], sem.at[sl]).wait()

    @pl.when(i == 0)
    def _():
        start_gather(0, 0, unrolled=False)

    wait_gather(slot)
    start_gather(jnp.minimum(i + 1, last), 1 - slot, unrolled=True)
    gt = gt_ref[...]
    perm = _tile_perm(False)
    z = (DEEPNORM_ALPHA * x1_ref[...] + gt[:, 0:1] * _from_token_major(ybuf[slot, 0], perm)
         + gt[:, 1:2] * _from_token_major(ybuf[slot, 1], perm))
    y_ref[...] = _layer_norm(z, g_ref[...], b_ref[...])

    @pl.when(i == last)
    def _():
        wait_gather(1 - slot)


def _combine_ln2(dest, x1, gt, yb, g2, b2, tm, tok0):
    m = x1.shape[0]
    grid_spec = pltpu.PrefetchScalarGridSpec(
        num_scalar_prefetch=1,
        grid=(m // tm,),
        in_specs=[pl.BlockSpec((tm, D_MODEL), lambda i, d: (i, 0)),
                  pl.BlockSpec((tm, LANES), lambda i, d: (i, 0)),
                  pl.BlockSpec(memory_space=pl.ANY),
                  pl.BlockSpec((1, D_MODEL), lambda i, d: (0, 0)),
                  pl.BlockSpec((1, D_MODEL), lambda i, d: (0, 0))],
        out_specs=pl.BlockSpec((tm, D_MODEL), lambda i, d: (i, 0)),
        scratch_shapes=[pltpu.VMEM((2, 2, tm, N_CT, LANES), BF16),
                        pltpu.SemaphoreType.DMA((2,))],
    )
    return pl.pallas_call(
        functools.partial(_combine_ln2_kernel, tm=tm, tok0=tok0),
        grid_spec=grid_spec,
        out_shape=jax.ShapeDtypeStruct((m, D_MODEL), F32),
        compiler_params=_cparams(("arbitrary",)),
        name="combine_ln2",
    )(dest, x1, gt, yb, g2, b2)


def _dispatch_tables(eid, rank, counts):
    padded = (counts + MOE_BLK - 1) // MOE_BLK * MOE_BLK
    pad_end = jnp.cumsum(padded)
    pad_start = pad_end - padded
    start_of = jnp.sum(jnp.where(eid[..., None] == jnp.arange(N_EXPERTS, dtype=I32), pad_start, 0), axis=-1)
    dest = (start_of + rank).reshape(-1).astype(I32)
    first_blk = (jnp.concatenate([pad_start, pad_end[-1:]]) // MOE_BLK).astype(I32)
    return dest, first_blk


def kernel(x_prompt, x_sample, cache_k, cache_v, cache_kidx, state_conv, page_table, w_in, conv_dw_w, conv_dw_b,
           conv_ln_g, conv_ln_b, w_conv_out, w_attn_out, w_out, ln1_g, ln1_b, w_router_group, w_router_expert,
           w_expert_gate, w_expert_up, w_expert_down, ln2_g, ln2_b):
    assert w_in.shape[0] == 1, "single-layer trunk"
    nb, t, d = x_prompt.shape
    ns = x_sample.shape[0]
    n_pool = cache_k.shape[1]
    mp = nb * t
    row2 = lambda a: a.reshape(1, -1)

    w_arr = _arrange_w_in(w_in[0])
    dww, dwb, lng, lnb = conv_dw_w[0], row2(conv_dw_b[0]), row2(conv_ln_g[0]), row2(conv_ln_b[0])
    wc, wa, wo = w_conv_out[0].astype(BF16), w_attn_out[0].astype(BF16), w_out[0].astype(BF16)
    g1, b1, g2, b2 = row2(ln1_g[0]), row2(ln1_b[0]), row2(ln2_g[0]), row2(ln2_b[0])
    wr32 = jnp.concatenate([w_router_group[0], w_router_expert[0],
                            jnp.zeros((d, LANES - N_GROUPS - N_EXPERTS), F32)], axis=1)
    wr_hi = wr32.astype(BF16)
    wr = jnp.stack([wr_hi, (wr32 - wr_hi.astype(F32)).astype(BF16)])

    xp = x_prompt.reshape(mp, d)
    u_p, q_p, qi_p, _, kvb_p, gc_p, ga_p, kiwi_p, kib_p, yact_p, kt_p, vt_p, kit_p = _in_proj(
        xp, w_arr, tm=512, conv=(dww, dwb, lng, lnb, t))
    o_p = _attn_prompt(qi_p.reshape(nb, t, -1), kiwi_p.reshape(nb, t, LANES), kib_p.reshape(nb, t, LANES),
                       q_p.reshape(nb, t, -1), kvb_p.reshape(nb, t, 2 * KV_DIM))
    xt_all, x1_p, ei_p, gt_p, cnt_p = _merge_ln1(yact_p, o_p.reshape(mp, -1), gc_p, ga_p, xp,
                                                 wc, wa, wo, g1, b1, wr, jnp.zeros((1, LANES), F32), tm=256,
                                                 m_total=mp + ns)

    xs = x_sample.reshape(ns, d)
    u_s, q_s, qi_s, kv_s, _, gc_s, ga_s, kiwi_s, _ = _in_proj(xs, w_arr, tm=ns)
    yact_s = _conv_sample(state_conv[0].transpose(1, 0, 2), u_s, dww, dwb, lng, lnb)
    cki2 = cache_kidx.transpose(0, 1, 3, 2).reshape(n_pool, IDX_DIM, PAGE_SIZE)
    ck2 = cache_k.transpose(0, 1, 3, 4, 2).reshape(n_pool, KV_DIM, PAGE_SIZE)
    cv2 = cache_v.transpose(0, 1, 3, 4, 2).reshape(n_pool, KV_DIM, PAGE_SIZE)
    scores_s = _sample_scores(page_table, qi_s.reshape(ns, IDX_HEADS, IDX_DIM),
                              kiwi_s[:, IDX_DIM:IDX_DIM + IDX_HEADS].reshape(ns, IDX_HEADS, 1),
                              kiwi_s[:, :IDX_DIM].reshape(ns, 1, IDX_DIM), cki2)
    scores3 = scores_s.reshape(ns, -1, ATT_CK).transpose(1, 0, 2)
    own_group = (np.arange(N_HEADS)[:, None] // HEADS_PER_KV) == (np.arange(KV_DIM)[None, :] // HEAD_DIM)
    qbd = jnp.where(own_group[None], jnp.tile(q_s.reshape(ns, N_HEADS, HEAD_DIM), (1, 1, KV_HEADS)), 0).astype(BF16)
    o3 = _sample_attn(page_table, scores3, qbd, kv_s[:, :KV_DIM].reshape(ns, 1, KV_DIM),
                      kv_s[:, KV_DIM:].reshape(ns, 1, KV_DIM), ck2, cv2)
    o_s = o3.reshape(ns, N_HEADS * HEAD_DIM).astype(BF16)
    xt_all, x1_s, ei_s, gt_s, cnt = _merge_ln1(yact_s, o_s, gc_s, ga_s, xs, wc, wa, wo, g1, b1, wr, cnt_p, tm=ns,
                                               xt_all=xt_all)

    eid_rank = jnp.concatenate([ei_p[:, :4], ei_s[:, :4]], axis=0)
    dest, first_blk = _dispatch_tables(eid_rank[:, :2], eid_rank[:, 2:], cnt[0, :N_EXPERTS].astype(I32))
    yb = _experts(first_blk, dest, xt_all, w_expert_gate[0], w_expert_up[0], w_expert_down[0])
    y_p = _combine_ln2(dest, x1_p, gt_p, yb, g2, b2, tm=256, tok0=0)
    y_s = _combine_ln2(dest, x1_s, gt_s, yb, g2, b2, tm=ns, tok0=mp)

    y_prompt = y_p.reshape(nb, t, d)
    y_sample = y_s.reshape(ns, 1, d)
    k_prompt = kt_p.reshape(nb, KV_HEADS, HEAD_DIM, t).transpose(0, 3, 1, 2)[None]
    v_prompt = vt_p.reshape(nb, KV_HEADS, HEAD_DIM, t).transpose(0, 3, 1, 2)[None]
    kidx_prompt = kit_p.transpose(0, 2, 1)[None]
    conv_prompt = u_p.reshape(nb, t, C_CONV)[:, t - (CONV_W - 1):][None]
    k_sample = kv_s[:, :KV_DIM].reshape(1, ns, 1, KV_HEADS, HEAD_DIM)
    v_sample = kv_s[:, KV_DIM:].reshape(1, ns, 1, KV_HEADS, HEAD_DIM)
    kidx_sample = kiwi_s[:, :IDX_DIM].reshape(1, ns, 1, IDX_DIM)
    conv_sample = jnp.concatenate([state_conv[0][:, 1:], u_s[:, None, :]], axis=1)[None]
    return (y_prompt, y_sample, k_prompt, v_prompt, kidx_prompt, conv_prompt, k_sample, v_sample, kidx_sample,
            conv_sample)
```

```python
import functools

import jax
import jax.numpy as jnp
import numpy as np
from jax import lax
from jax.experimental import pallas as pl
from jax.experimental.pallas import tpu as pltpu

F32 = jnp.float32
BF16 = jnp.bfloat16
I32 = jnp.int32

D_MODEL = 2048
C_CONV = 1024
CONV_W = 31
N_HEADS = 16
HEAD_DIM = 64
KV_HEADS = 4
HEADS_PER_KV = N_HEADS // KV_HEADS
KV_DIM = KV_HEADS * HEAD_DIM
IDX_HEADS = 16
IDX_DIM = 64
MAX_TOPK = 256
N_GROUPS = 8
EXPERTS_PER_GROUP = 8
N_EXPERTS = 64
D_EXPERT = 512
MOE_BLK = 128
PAGE_SIZE = 128
LN_EPS = 1e-5
DEEPNORM_ALPHA = 2.0 ** 0.25
NEG_INF = -1e30
INT_MIN = -(2 ** 31)
NEG_KEY = int(INT_MIN - int(np.array(NEG_INF, np.float32).view(np.int32)))
LOG2E = 1.4426950408889634
Q_SCALE = HEAD_DIM ** -0.5 * LOG2E
IDX_SCALE = IDX_HEADS ** -0.5 * IDX_DIM ** -0.5

LANES = 128
SUBLANES = 8
HALF = LANES // 2
IN_TN = 1024
VMEM_LIMIT = 56 * 1024 * 1024


def _cparams(sem, vmem=VMEM_LIMIT):
    return pltpu.CompilerParams(dimension_semantics=sem, vmem_limit_bytes=vmem)


def _float_key(x):
    b = pltpu.bitcast(x, I32)
    return jnp.where(b < 0, INT_MIN - b, b)


def _layer_norm(z, g, b):
    mu = jnp.mean(z, axis=-1, keepdims=True)
    zc = z - mu
    var = jnp.mean(zc * zc, axis=-1, keepdims=True)
    return zc * lax.rsqrt(var + LN_EPS) * g + b


def _dot_nt(a, b):
    return lax.dot_general(a, b, (((1,), (1,)), ((), ())), preferred_element_type=F32)


def _alibi_slope(h):
    return float(2.0 ** (-8.0 * (h + 1) / N_HEADS))


_T_GLU, _T_Q, _T_QI, _T_KV, _T_GC, _T_GA, _N_TILES = 0, 2, 3, 4, 5, 7, 9
KV_TILE_COLS = 2 * KV_DIM + 2 * LANES
GLU_HALF = IN_TN // 2
CONV_HALO = 32
CONV_STRIP = 32
CONV_ROWS = 256


def _arrange_w_in(w):
    sizes = (C_CONV, C_CONV, N_HEADS * HEAD_DIM, KV_DIM, KV_DIM,
             IDX_HEADS * IDX_DIM, IDX_DIM, IDX_HEADS, D_MODEL, D_MODEL)
    offs = np.cumsum((0,) + sizes)
    wt = w.T
    a, g, q, k, v, qi, ki, wi, gc, ga = [wt[offs[i]:offs[i + 1]] for i in range(10)]
    glu = [jnp.concatenate([a[t * GLU_HALF:(t + 1) * GLU_HALF], g[t * GLU_HALF:(t + 1) * GLU_HALF]], axis=0)
           for t in range(C_CONV // GLU_HALF)]
    zeros = lambda n: jnp.zeros((n, w.shape[0]), w.dtype)
    kv_tile = jnp.concatenate([k, v, ki, wi, zeros(LANES - IDX_DIM - IDX_HEADS), ki, ki,
                               zeros(IN_TN - KV_TILE_COLS)], axis=0)
    return jnp.concatenate(glu + [q, qi, kv_tile, gc, ga], axis=0).astype(BF16)


def _conv_half(jj, new_seq, ubuf, sh_ref, ypre, dww_ref, dwb_ref):
    tm = ypre.shape[0]
    cs = slice(jj * GLU_HALF, (jj + 1) * GLU_HALF)
    ubuf[jj, 0:CONV_HALO, :] = jnp.where(new_seq, 0.0, ubuf[jj, 0:CONV_HALO, :])
    first = CONV_HALO - (CONV_W - 1)
    dwb = dwb_ref[:, cs]
    for rb in range(tm // CONV_ROWS):
        for b in range(SUBLANES):
            rows = CONV_ROWS + SUBLANES * ((CONV_W - 1 - b) // SUBLANES)
            src0 = rb * CONV_ROWS + first + b
            sh_ref[b, 0:rows, :] = ubuf[jj, src0:src0 + rows, :]
        for r in range(CONV_ROWS // CONV_STRIP):
            acc = jnp.broadcast_to(dwb, (CONV_STRIP, GLU_HALF))
            for j in range(CONV_W):
                a, b = divmod(j, SUBLANES)
                r0 = r * CONV_STRIP + a * SUBLANES
                acc = acc + dww_ref[j:j + 1, cs] * sh_ref[b, r0:r0 + CONV_STRIP, :]
            o0 = rb * CONV_ROWS + r * CONV_STRIP
            ypre[o0:o0 + CONV_STRIP, cs] = acc
    ubuf[jj, 0:CONV_HALO, :] = ubuf[jj, tm:tm + CONV_HALO, :]


def _in_proj_kernel(*refs, fuse_conv, tiles_per_seq):
    if fuse_conv:
        (x_ref, w_ref, dww_ref, dwb_ref, lng_ref, lnb_ref,
         u_ref, q_ref, qi_ref, kv_ref, kvb_ref, gc_ref, ga_ref, kiwi_ref, kib_ref, y_ref, kt_ref, vt_ref, kit_ref,
         xb_ref, ubuf, sh_ref, ypre) = refs
    else:
        (x_ref, w_ref, u_ref, q_ref, qi_ref, kv_ref, kvb_ref, gc_ref, ga_ref, kiwi_ref, kib_ref, xb_ref) = refs
    i = pl.program_id(0)
    j = pl.program_id(1)

    @pl.when(j == 0)
    def _():
        xb_ref[...] = x_ref[...].astype(BF16)

    if fuse_conv:
        new_seq = lax.rem(i, tiles_per_seq) == 0

        @pl.when((i == 0) & (j == 0))
        def _():
            ubuf[:, 0:CONV_HALO, :] = jnp.zeros((ubuf.shape[0], CONV_HALO, GLU_HALF), F32)

    def mm(ncols=IN_TN):
        return _dot_nt(xb_ref[...], w_ref[:ncols, :])

    for jj in range(2):
        @pl.when(j == jj)
        def _(jj=jj):
            if fuse_conv and jj == 1:
                _conv_half(0, new_seq, ubuf, sh_ref, ypre, dww_ref, dwb_ref)
            acc = mm()
            u = acc[:, :GLU_HALF] * jax.nn.sigmoid(acc[:, GLU_HALF:])
            u_ref[...] = u
            if fuse_conv:
                ubuf[jj, CONV_HALO:, :] = u

    @pl.when(j == _T_Q)
    def _():
        if fuse_conv:
            _conv_half(1, new_seq, ubuf, sh_ref, ypre, dww_ref, dwb_ref)
        q_ref[...] = (mm() * Q_SCALE).astype(BF16)

    @pl.when(j == _T_QI)
    def _():
        if fuse_conv:
            lng, lnb = lng_ref[...], lnb_ref[...]
            for r in range(ypre.shape[0] // CONV_STRIP):
                rs = slice(r * CONV_STRIP, (r + 1) * CONV_STRIP)
                yn = _layer_norm(ypre[rs, :], lng, lnb)
                y_ref[rs, :] = (yn * jax.nn.sigmoid(yn)).astype(BF16)
        qi_ref[...] = mm().astype(BF16)

    @pl.when(j == _T_KV)
    def _():
        acc = mm(KV_TILE_COLS)
        kv_ref[...] = acc[:, :2 * KV_DIM]
        kvb_ref[...] = acc[:, :2 * KV_DIM].astype(BF16)
        kiwi_ref[...] = acc[:, 2 * KV_DIM:2 * KV_DIM + LANES]
        kib_ref[...] = acc[:, 2 * KV_DIM + LANES:].astype(BF16)
        if fuse_conv:
            kt_ref[0] = jnp.transpose(acc[:, :KV_DIM])
            vt_ref[0] = jnp.transpose(acc[:, KV_DIM:2 * KV_DIM])
            kit_ref[0] = jnp.transpose(acc[:, 2 * KV_DIM:2 * KV_DIM + LANES])[:IDX_DIM]

    @pl.when((j >= _T_GC) & (j < _T_GA))
    def _():
        gc_ref[...] = jax.nn.sigmoid(mm()).astype(BF16)

    @pl.when(j >= _T_GA)
    def _():
        ga_ref[...] = jax.nn.sigmoid(mm()).astype(BF16)


def _in_proj(x, w_arr, tm, conv=None):
    m = x.shape[0]
    fuse_conv = conv is not None

    def cl(lo, n):
        return lambda i, j: (i, jnp.clip(j - lo, 0, n - 1))

    row_blk = lambda w: pl.BlockSpec((tm, w), lambda i, j: (i, 0))
    out_shape = [
        jax.ShapeDtypeStruct((m, C_CONV), F32),
        jax.ShapeDtypeStruct((m, N_HEADS * HEAD_DIM), BF16),
        jax.ShapeDtypeStruct((m, IDX_HEADS * IDX_DIM), BF16),
        jax.ShapeDtypeStruct((m, 2 * KV_DIM), F32),
        jax.ShapeDtypeStruct((m, 2 * KV_DIM), BF16),
        jax.ShapeDtypeStruct((m, D_MODEL), BF16),
        jax.ShapeDtypeStruct((m, D_MODEL), BF16),
        jax.ShapeDtypeStruct((m, LANES), F32),
        jax.ShapeDtypeStruct((m, LANES), BF16),
    ]
    out_specs = [
        pl.BlockSpec((tm, GLU_HALF), cl(_T_GLU, 2)),
        row_blk(IN_TN), row_blk(IN_TN), row_blk(2 * KV_DIM), row_blk(2 * KV_DIM),
        pl.BlockSpec((tm, IN_TN), cl(_T_GC, 2)),
        pl.BlockSpec((tm, IN_TN), cl(_T_GA, 2)),
        row_blk(LANES), row_blk(LANES),
    ]
    args = [x, w_arr]
    in_specs = [pl.BlockSpec((tm, D_MODEL), lambda i, j: (i, 0)),
                pl.BlockSpec((IN_TN, D_MODEL), lambda i, j: (j, 0))]
    scratch = [pltpu.VMEM((tm, D_MODEL), BF16)]
    tiles_per_seq = 1
    if fuse_conv:
        dww, dwb, lng, lnb, seq = conv
        assert seq % tm == 0 and tm % CONV_ROWS == 0
        tiles_per_seq = seq // tm
        const = lambda a: pl.BlockSpec(a.shape, lambda i, j: (0, 0))
        args += [dww, dwb, lng, lnb]
        in_specs += [const(dww), const(dwb), const(lng), const(lnb)]
        out_shape.append(jax.ShapeDtypeStruct((m, C_CONV), BF16))
        out_specs.append(row_blk(C_CONV))
        for feats in (KV_DIM, KV_DIM, IDX_DIM):
            out_shape.append(jax.ShapeDtypeStruct((m // seq, feats, seq), F32))
            out_specs.append(pl.BlockSpec((1, feats, tm), lambda i, j: (i // tiles_per_seq, 0, i % tiles_per_seq)))
        scratch += [pltpu.VMEM((2, CONV_HALO + tm, GLU_HALF), F32),
                    pltpu.VMEM((SUBLANES, CONV_ROWS + SUBLANES * ((CONV_W - 1) // SUBLANES), GLU_HALF), F32),
                    pltpu.VMEM((tm, C_CONV), F32)]
    return pl.pallas_call(
        functools.partial(_in_proj_kernel, fuse_conv=fuse_conv, tiles_per_seq=tiles_per_seq),
        grid=(m // tm, _N_TILES),
        in_specs=in_specs,
        out_specs=tuple(out_specs),
        out_shape=tuple(out_shape),
        scratch_shapes=scratch,
        compiler_params=_cparams(("arbitrary", "arbitrary")),
        name="in_proj",
    )(*args)


def _conv_sample_kernel(state_ref, u_ref, dww_ref, dwb_ref, lng_ref, lnb_ref, y_ref):
    acc = dwb_ref[...] + dww_ref[CONV_W - 1:CONV_W, :] * u_ref[...]
    for j in range(CONV_W - 1):
        acc = acc + dww_ref[j:j + 1, :] * state_ref[j]
    yn = _layer_norm(acc, lng_ref[...], lnb_ref[...])
    y_ref[...] = (yn * jax.nn.sigmoid(yn)).astype(BF16)


def _conv_sample(state_t, u, dww, dwb, lng, lnb):
    n = u.shape[0]
    return pl.pallas_call(
        _conv_sample_kernel,
        out_shape=jax.ShapeDtypeStruct((n, C_CONV), BF16),
        compiler_params=pltpu.CompilerParams(vmem_limit_bytes=VMEM_LIMIT),
        name="conv_sample",
    )(state_t, u, dww, dwb, lng, lnb)


ATT_TQ = 256
ATT_CK = 256


def _count_ge(key_ref, nch, cand):
    def body(c, acc):
        m = jnp.where(key_ref[c] >= cand, 1.0, 0.0)
        return acc + m[:, :LANES] + m[:, LANES:]
    acc = lax.fori_loop(0, nch, body, jnp.zeros((cand.shape[0], LANES), F32))
    return jnp.sum(acc, axis=1, keepdims=True)


def _select_threshold(key_ref, nch, rows):
    def body(i, prefix):
        cand = prefix + jnp.left_shift(jnp.int32(1), 31 - i)
        return jnp.where(_count_ge(key_ref, nch, cand) >= float(MAX_TOPK), cand, prefix)
    return lax.fori_loop(0, 32, body, jnp.full((rows, 1), INT_MIN, I32))


def _selection_bias(key_ref, bias_ref, nch):
    rows, ck = key_ref.shape[1:]
    thr = _select_threshold(key_ref, nch, rows)
    n_tie_take = float(MAX_TOPK) - _count_ge(key_ref, nch, thr + 1)
    tri = jnp.where(lax.broadcasted_iota(I32, (ck, ck), 0) <= lax.broadcasted_iota(I32, (ck, ck), 1),
                    1.0, 0.0).astype(BF16)

    def bias_chunk(c, seen):
        key = key_ref[c]
        eq = key == thr
        eqf = jnp.where(eq, 1.0, 0.0)
        incl = jnp.dot(eqf.astype(BF16), tri, preferred_element_type=F32)
        sel = (key > thr) | (eq & (seen + incl - eqf < n_tie_take))
        bias_ref[c] = jnp.where(sel & (key > NEG_KEY), 0.0, NEG_INF)
        return seen + incl[:, ck - 1:ck]

    lax.fori_loop(0, nch, bias_chunk, jnp.zeros((rows, 1), F32))


def _count_ge_t(key_ref, nch, cand):
    ck, nq = key_ref.shape[1:]

    def body(c, acc):
        m = jnp.where(key_ref[c] >= cand, 1.0, 0.0)
        return acc + jnp.sum(m.reshape(ck // SUBLANES, SUBLANES, nq), axis=0)

    acc = lax.fori_loop(0, nch, body, jnp.zeros((SUBLANES, nq), F32))
    return jnp.sum(acc, axis=0, keepdims=True)


def _selection_bias_t(key_ref, bias_ref, nch):
    ck, nq = key_ref.shape[1:]

    def bit(i, prefix):
        cand = prefix + jnp.left_shift(jnp.int32(1), 31 - i)
        return jnp.where(_count_ge_t(key_ref, nch, cand) >= float(MAX_TOPK), cand, prefix)

    thr = lax.fori_loop(0, 32, bit, jnp.full((1, nq), INT_MIN, I32))
    n_tie_take = float(MAX_TOPK) - _count_ge_t(key_ref, nch, thr + 1)
    tri = jnp.where(lax.broadcasted_iota(I32, (ck, ck), 1) <= lax.broadcasted_iota(I32, (ck, ck), 0),
                    1.0, 0.0).astype(BF16)

    def bias_chunk(c, seen):
        key = key_ref[c]
        eq = key == thr
        eqf = jnp.where(eq, 1.0, 0.0)
        incl = jnp.dot(tri, eqf.astype(BF16), preferred_element_type=F32)
        sel = (key > thr) | (eq & (seen + incl - eqf < n_tie_take))
        bias_ref[c] = jnp.where(sel & (key > NEG_KEY), 0.0, NEG_INF).T
        return seen + incl[ck - 1:ck, :]

    lax.fori_loop(0, nch, bias_chunk, jnp.zeros((1, nq), F32))


def _swap_halves(x):
    return jnp.concatenate([x[:, HALF:], x[:, :HALF]], axis=1)


def _attn_prompt_kernel(qi_ref, kiwi_ref, kib_ref, q_ref, kvb_ref, o_ref,
                        key_ref, bias_ref, m_ref, acc_ref):
    tq, ck = ATT_TQ, ATT_CK
    qb = pl.program_id(1)
    nch = qb + 1
    lo_k = lax.broadcasted_iota(I32, (ck, LANES), 1) < HALF
    lo_q = lax.broadcasted_iota(I32, (tq, LANES), 1) < HALF

    key_pos = lax.broadcasted_iota(I32, (ck, tq), 0)
    q_pos = qb * tq + lax.broadcasted_iota(I32, (ck, tq), 1)
    w_t = jnp.transpose(kiwi_ref[0])[IDX_DIM:IDX_DIM + IDX_HEADS, :] * IDX_SCALE

    def score_chunk(c, carry):
        k0 = pl.multiple_of(c * ck, ck)
        kk = kib_ref[0, pl.ds(k0, ck), :]
        zero = jnp.zeros_like(kk)
        k_lo, k_hi = jnp.where(lo_k, kk, zero), jnp.where(lo_k, zero, kk)
        acc = jnp.zeros((ck, tq), F32)
        for p in range(IDX_HEADS // 2):
            qp = qi_ref[0, :, p * LANES:(p + 1) * LANES]
            acc = acc + w_t[2 * p:2 * p + 1, :] * jnp.maximum(_dot_nt(k_lo, qp), 0.0)
            acc = acc + w_t[2 * p + 1:2 * p + 2, :] * jnp.maximum(_dot_nt(k_hi, qp), 0.0)
        acc = jnp.where(c * ck + key_pos <= q_pos, acc, NEG_INF)
        key_ref[c] = _float_key(acc)
        return carry

    lax.fori_loop(0, nch, score_chunk, 0)
    _selection_bias_t(key_ref, bias_ref, nch)

    m_ref[...] = jnp.full(m_ref.shape, -jnp.inf, F32)
    acc_ref[...] = jnp.zeros(acc_ref.shape, F32)

    def attn_chunk(c, carry):
        k0 = pl.multiple_of(c * ck, ck)
        kv = kvb_ref[0, pl.ds(k0, ck), :]
        bias = bias_ref[c]
        rel = (c * ck - (qb + 1) * tq + 1 + lax.broadcasted_iota(I32, (1, ck), 1)).astype(F32)
        for j in range(KV_HEADS // 2):
            kt = kv[:, j * LANES:(j + 1) * LANES]
            vt = kv[:, KV_DIM + j * LANES:KV_DIM + (j + 1) * LANES]
            kts, vts = _swap_halves(kt), _swap_halves(vt)
            zero, one = jnp.zeros_like(kt), jnp.ones_like(kt)
            for gg in range(2):
                g = 2 * j + gg
                k_own, k_swp = (kt, kts) if gg == 0 else (kts, kt)
                v_own, v_swp = (vt, vts) if gg == 0 else (vts, vt)
                k_lo, k_hi = jnp.where(lo_k, k_own, zero), jnp.where(lo_k, zero, k_swp)
                v_lo, v_hi = jnp.where(lo_k, v_own, one), jnp.where(lo_k, one, v_swp)
                for mm in range(HEADS_PER_KV // 2):
                    pidx = (HEADS_PER_KV // 2) * g + mm
                    qp = q_ref[0, :, pidx * LANES:(pidx + 1) * LANES]
                    for half, (kmat, vmat) in enumerate(((k_lo, v_lo), (k_hi, v_hi))):
                        h = 2 * pidx + half
                        s = _dot_nt(qp, kmat) + (_alibi_slope(h) * LOG2E) * rel + bias
                        m_old = m_ref[h]
                        m_new = jnp.maximum(m_old, jnp.max(s, axis=1, keepdims=True))
                        p = jnp.exp2(s - jnp.concatenate([m_new, m_new], axis=1))
                        m_ref[h] = m_new
                        acc_ref[h] = (jnp.exp2(m_old - m_new) * acc_ref[h]
                                      + jnp.dot(p.astype(BF16), vmat, preferred_element_type=F32))
        return carry

    lax.fori_loop(0, nch, attn_chunk, 0)
    for pidx in range(N_HEADS // 2):
        a_e, a_o = acc_ref[2 * pidx], acc_ref[2 * pidx + 1]
        num = jnp.where(lo_q, a_e, a_o)
        den = jnp.where(lo_q, pltpu.roll(a_e, HALF, 1), pltpu.roll(a_o, HALF, 1))
        o_ref[0, :, pidx * LANES:(pidx + 1) * LANES] = (num / den).astype(BF16)


def _attn_prompt(qi, kiwi, kib, q, kvb):
    n, t, _ = q.shape
    tq = ATT_TQ
    return pl.pallas_call(
        _attn_prompt_kernel,
        grid=(n, t // tq),
        in_specs=[pl.BlockSpec((1, tq, IDX_HEADS * IDX_DIM), lambda b, i: (b, i, 0)),
                  pl.BlockSpec((1, tq, LANES), lambda b, i: (b, i, 0)),
                  pl.BlockSpec((1, t, LANES), lambda b, i: (b, 0, 0)),
                  pl.BlockSpec((1, tq, N_HEADS * HEAD_DIM), lambda b, i: (b, i, 0)),
                  pl.BlockSpec((1, t, 2 * KV_DIM), lambda b, i: (b, 0, 0))],
        out_specs=pl.BlockSpec((1, tq, N_HEADS * HEAD_DIM), lambda b, i: (b, i, 0)),
        out_shape=jax.ShapeDtypeStruct((n, t, N_HEADS * HEAD_DIM), BF16),
        scratch_shapes=[pltpu.VMEM((t // ATT_CK, tq, ATT_CK), I32),
                        pltpu.VMEM((t // ATT_CK, tq, ATT_CK), F32),
                        pltpu.VMEM((N_HEADS, tq, LANES), F32),
                        pltpu.VMEM((N_HEADS, tq, LANES), F32)],
        compiler_params=_cparams(("arbitrary", "arbitrary")),
        name="attn_prompt",
    )(qi, kiwi, kib, q, kvb)


SAMPLE_KC = 2048


def _start_page_copies(pt_ref, src_hbm, dst, sem, base, n_pages):
    def body(p, carry):
        col = pl.multiple_of(p * PAGE_SIZE, PAGE_SIZE)
        pltpu.make_async_copy(src_hbm.at[pt_ref[base + p]], dst.at[:, pl.ds(col, PAGE_SIZE)], sem).start()
        return carry
    lax.fori_loop(0, n_pages, body, 0)


def _sample_score_kernel(pt_ref, qi_ref, wcol_ref, kinew_ref, cki_hbm, sc_ref, kibuf, sem, *, n_pages):
    b = pl.program_id(0)
    nb = pl.num_programs(0)
    past = n_pages * PAGE_SIZE
    slot = b % 2

    @pl.when(b == 0)
    def _():
        _start_page_copies(pt_ref, cki_hbm, kibuf.at[0], sem.at[0], 0, n_pages)

    @pl.when(b + 1 < nb)
    def _():
        _start_page_copies(pt_ref, cki_hbm, kibuf.at[1 - slot], sem.at[1 - slot], (b + 1) * n_pages, n_pages)

    pltpu.make_async_copy(kibuf.at[slot], kibuf.at[slot], sem.at[slot]).wait()

    qi = qi_ref[0]
    wsc = wcol_ref[0] * IDX_SCALE
    for c in range(past // SAMPLE_KC):
        kc = kibuf[slot, :, c * SAMPLE_KC:(c + 1) * SAMPLE_KC].astype(BF16)
        d = jnp.dot(qi, kc, preferred_element_type=F32)
        sc_ref[0, :, c * SAMPLE_KC:(c + 1) * SAMPLE_KC] = jnp.sum(wsc * jnp.maximum(d, 0.0), axis=0, keepdims=True)
    dn = jnp.sum(qi.astype(F32) * kinew_ref[0], axis=1, keepdims=True)
    s_new = jnp.sum(wsc * jnp.maximum(dn, 0.0), axis=0, keepdims=True)
    ln = lax.broadcasted_iota(I32, (1, ATT_CK), 1)
    sc_ref[0, :, past:] = jnp.where(ln == 0, s_new, NEG_INF)


def _sample_scores(page_table, qi3, wcol, kinew, cki2):
    n, n_pages = page_table.shape
    past = n_pages * PAGE_SIZE
    grid_spec = pltpu.PrefetchScalarGridSpec(
        num_scalar_prefetch=1,
        grid=(n,),
        in_specs=[pl.BlockSpec((1, IDX_HEADS, IDX_DIM), lambda b, pt: (b, 0, 0)),
                  pl.BlockSpec((1, IDX_HEADS, 1), lambda b, pt: (b, 0, 0)),
                  pl.BlockSpec((1, 1, IDX_DIM), lambda b, pt: (b, 0, 0)),
                  pl.BlockSpec(memory_space=pl.ANY)],
        out_specs=pl.BlockSpec((1, 1, past + ATT_CK), lambda b, pt: (b, 0, 0)),
        scratch_shapes=[pltpu.VMEM((2, IDX_DIM, past), F32),
                        pltpu.SemaphoreType.DMA((2,))],
    )
    return pl.pallas_call(
        functools.partial(_sample_score_kernel, n_pages=n_pages),
        grid_spec=grid_spec,
        out_shape=jax.ShapeDtypeStruct((n, 1, past + ATT_CK), F32),
        compiler_params=_cparams(("arbitrary",)),
        name="sample_scores",
    )(page_table.reshape(-1), qi3, wcol, kinew, cki2)


def _sample_attn_kernel(pt_ref, sc_ref, qbd_ref, knew_ref, vnew_ref, ck_hbm, cv_hbm, o_ref,
                        key_ref, bias_ref, kbuf, vbuf, semk, semv, *, n_pages):
    b = pl.program_id(0)
    nb = pl.num_programs(0)
    past = n_pages * PAGE_SIZE
    nch = sc_ref.shape[0]
    slot = b % 2

    def start_all(bb, sl):
        _start_page_copies(pt_ref, ck_hbm, kbuf.at[sl], semk.at[sl], bb * n_pages, n_pages)
        _start_page_copies(pt_ref, cv_hbm, vbuf.at[sl], semv.at[sl], bb * n_pages, n_pages)

    @pl.when(b == 0)
    def _():
        start_all(0, 0)
        key_ref[...] = _float_key(sc_ref[...])
        _selection_bias(key_ref, bias_ref, nch)

    @pl.when(b + 1 < nb)
    def _():
        start_all(b + 1, 1 - slot)

    pltpu.make_async_copy(kbuf.at[slot], kbuf.at[slot], semk.at[slot]).wait()
    pltpu.make_async_copy(vbuf.at[slot], vbuf.at[slot], semv.at[slot]).wait()

    qbd = qbd_ref[0]
    bias_row = jnp.concatenate([bias_ref[c, pl.ds(b, 1), :] for c in range(nch)], axis=1)
    s_parts = [jnp.dot(qbd, kbuf[slot, :, c * SAMPLE_KC:(c + 1) * SAMPLE_KC].astype(BF16),
                       preferred_element_type=F32) for c in range(past // SAMPLE_KC)]
    s_new = jnp.sum(qbd.astype(F32) * knew_ref[0], axis=1, keepdims=True)
    ln = lax.broadcasted_iota(I32, (N_HEADS, ATT_CK), 1)
    s = jnp.concatenate(s_parts + [jnp.where(ln == 0, s_new, 0.0)], axis=1)
    hrow = lax.broadcasted_iota(I32, (N_HEADS, 1), 0)
    slope2 = jnp.exp2(-8.0 * (hrow + 1).astype(F32) / N_HEADS) * LOG2E
    rel = (lax.broadcasted_iota(I32, (1, past + ATT_CK), 1) - past).astype(F32)
    s = s + slope2 * rel + bias_row
    m = jnp.max(s, axis=1, keepdims=True)
    p = jnp.exp2(s - m)
    l = jnp.sum(p, axis=1, keepdims=True)
    o_all = p[:, past:past + 1] * vnew_ref[0]
    for c in range(past // SAMPLE_KC):
        o_all = o_all + _dot_nt(p[:, c * SAMPLE_KC:(c + 1) * SAMPLE_KC].astype(BF16),
                                vbuf[slot, :, c * SAMPLE_KC:(c + 1) * SAMPLE_KC].astype(BF16))
    o_all = o_all / l
    out = jnp.zeros((N_HEADS, HEAD_DIM), F32)
    for g in range(KV_HEADS):
        own = (hrow >= g * HEADS_PER_KV) & (hrow < (g + 1) * HEADS_PER_KV)
        out = out + jnp.where(own, o_all[:, g * HEAD_DIM:(g + 1) * HEAD_DIM], 0.0)
    o_ref[0] = out


def _sample_attn(page_table, scores3, qbd, knew, vnew, ck2, cv2):
    n, n_pages = page_table.shape
    past = n_pages * PAGE_SIZE
    nch = scores3.shape[0]
    grid_spec = pltpu.PrefetchScalarGridSpec(
        num_scalar_prefetch=1,
        grid=(n,),
        in_specs=[pl.BlockSpec((nch, n, ATT_CK), lambda b, pt: (0, 0, 0)),
                  pl.BlockSpec((1, N_HEADS, KV_DIM), lambda b, pt: (b, 0, 0)),
                  pl.BlockSpec((1, 1, KV_DIM), lambda b, pt: (b, 0, 0)),
                  pl.BlockSpec((1, 1, KV_DIM), lambda b, pt: (b, 0, 0)),
                  pl.BlockSpec(memory_space=pl.ANY),
                  pl.BlockSpec(memory_space=pl.ANY)],
        out_specs=pl.BlockSpec((1, N_HEADS, HEAD_DIM), lambda b, pt: (b, 0, 0)),
        scratch_shapes=[pltpu.VMEM((nch, n, ATT_CK), I32),
                        pltpu.VMEM((nch, n, ATT_CK), F32),
                        pltpu.VMEM((2, KV_DIM, past), F32),
                        pltpu.VMEM((2, KV_DIM, past), F32),
                        pltpu.SemaphoreType.DMA((2,)),
                        pltpu.SemaphoreType.DMA((2,))],
    )
    return pl.pallas_call(
        functools.partial(_sample_attn_kernel, n_pages=n_pages),
        grid_spec=grid_spec,
        out_shape=jax.ShapeDtypeStruct((n, N_HEADS, HEAD_DIM), F32),
        compiler_params=_cparams(("arbitrary",)),
        name="sample_attn",
    )(page_table.reshape(-1), scores3, qbd, knew, vnew, ck2, cv2)


N_CT = D_MODEL // LANES
TOK_GRP = SUBLANES * N_CT


def _tile_perm(to_token_major):
    r = lax.broadcasted_iota(I32, (TOK_GRP, TOK_GRP), 1 if to_token_major else 0)
    k = lax.broadcasted_iota(I32, (TOK_GRP, TOK_GRP), 0 if to_token_major else 1)
    return jnp.where((r // SUBLANES == k % N_CT) & (r % SUBLANES == k // N_CT), 1.0, 0.0).astype(BF16)


def _to_token_major(x, perm):
    rows = x.shape[0]
    out = []
    for t in range(rows // SUBLANES):
        xs = x[t * SUBLANES:(t + 1) * SUBLANES]
        cm = jnp.concatenate([xs[:, c * LANES:(c + 1) * LANES] for c in range(N_CT)], axis=0).astype(BF16)
        out.append(jnp.dot(perm, cm, preferred_element_type=F32).astype(BF16))
    return jnp.stack(out).reshape(rows, N_CT, LANES)


def _from_token_major(xt, perm):
    rows = xt.shape[0]
    xg = xt.reshape(rows // SUBLANES, TOK_GRP, LANES)
    out = jnp.stack([jnp.dot(perm, xg[t], preferred_element_type=F32) for t in range(rows // SUBLANES)])
    return jnp.concatenate([out[:, c * SUBLANES:(c + 1) * SUBLANES, :].reshape(rows, LANES) for c in range(N_CT)],
                           axis=1)


def _merge_ln1_kernel(*refs, aliased, n_tiles):
    n_in = 12
    outs = refs[n_in + 1:] if aliased else refs[n_in:]
    xt_ref = outs[0]

    @pl.when(pl.program_id(0) < n_tiles)
    def _():
        _merge_ln1_tile(*refs[:n_in], *outs)

    @pl.when(pl.program_id(0) >= n_tiles)
    def _():
        xt_ref[...] = jnp.zeros(xt_ref.shape, BF16)


def _merge_ln1_tile(y_ref, o_ref, gc_ref, ga_ref, x_ref, wc_ref, wa_ref, wo_ref, g_ref, b_ref, wr_ref,
                    cnt_in_ref, xt_ref, x1_ref, ei_ref, gt_ref, cnt_ref):
    @pl.when(pl.program_id(0) == 0)
    def _():
        cnt_ref[...] = cnt_in_ref[...]

    conv_out = jnp.dot(y_ref[...], wc_ref[...], preferred_element_type=F32)
    attn_out = jnp.dot(o_ref[...], wa_ref[...], preferred_element_type=F32)
    merged = gc_ref[...].astype(F32) * conv_out + ga_ref[...].astype(F32) * attn_out
    mix = jnp.dot(merged.astype(BF16), wo_ref[...], preferred_element_type=F32)
    x1 = _layer_norm(DEEPNORM_ALPHA * x_ref[...] + mix, g_ref[...], b_ref[...])
    x1_ref[...] = x1
    xt_ref[...] = _to_token_major(x1, _tile_perm(True))

    x_hi = x1.astype(BF16)
    x_lo = (x1 - x_hi.astype(F32)).astype(BF16)
    logits = (jnp.dot(x_hi, wr_ref[0], preferred_element_type=F32)
              + jnp.dot(x_lo, wr_ref[0], preferred_element_type=F32)
              + jnp.dot(x_hi, wr_ref[1], preferred_element_type=F32))
    lane = lax.broadcasted_iota(I32, logits.shape, 1)
    is_g = lane < N_GROUPS
    lg = jnp.where(is_g, logits, -jnp.inf)
    eg = jnp.exp(lg - jnp.max(lg, axis=1, keepdims=True))
    pg = eg / jnp.sum(eg, axis=1, keepdims=True)
    g_top = jnp.max(pg, axis=1, keepdims=True)
    g_idx = jnp.min(jnp.where(is_g & (pg == g_top), lane, LANES), axis=1, keepdims=True)
    lo = N_GROUPS + g_idx * EXPERTS_PER_GROUP
    is_e = (lane >= lo) & (lane < lo + EXPERTS_PER_GROUP)
    le = jnp.where(is_e, logits, -jnp.inf)
    ee = jnp.exp(le - jnp.max(le, axis=1, keepdims=True))
    pe = ee / jnp.sum(ee, axis=1, keepdims=True)
    p1 = jnp.max(jnp.where(is_e, pe, -1.0), axis=1, keepdims=True)
    i1 = jnp.min(jnp.where(is_e & (pe == p1), lane, 2 * LANES), axis=1, keepdims=True)
    rest = is_e & (lane != i1)
    p2 = jnp.max(jnp.where(rest, pe, -1.0), axis=1, keepdims=True)
    i2 = jnp.min(jnp.where(rest & (pe == p2), lane, 2 * LANES), axis=1, keepdims=True)
    den = p1 + p2
    gt_ref[...] = jnp.where(lane == 0, g_top * p1 / den, jnp.where(lane == 1, g_top * p2 / den, 0.0))

    e1, e2 = i1 - N_GROUPS, i2 - N_GROUPS
    tm = logits.shape[0]
    oh1 = jnp.where(lane == e1, 1.0, 0.0)
    oh2 = jnp.where(lane == e2, 1.0, 0.0)
    earlier = jnp.where(lax.broadcasted_iota(I32, (tm, tm), 1) < lax.broadcasted_iota(I32, (tm, tm), 0),
                        1.0, 0.0).astype(BF16)
    before1 = jnp.dot(earlier, oh1.astype(BF16), preferred_element_type=F32)
    before2 = jnp.dot(earlier, oh2.astype(BF16), preferred_element_type=F32)
    tot1 = jnp.sum(oh1, axis=0, keepdims=True)
    base = cnt_ref[...]
    r1 = jnp.sum(oh1 * (base + before1), axis=1, keepdims=True)
    r2 = jnp.sum(oh2 * (base + tot1 + before2), axis=1, keepdims=True)
    cnt_ref[...] = base + tot1 + jnp.sum(oh2, axis=0, keepdims=True)
    ei_ref[...] = jnp.where(lane == 0, e1, jnp.where(lane == 1, e2, jnp.where(
        lane == 2, r1.astype(I32), jnp.where(lane == 3, r2.astype(I32), 0))))


def _merge_ln1(yact, o, gc, ga, x, wc, wa, wo, g1, b1, wr, cnt_in, tm, xt_all=None, m_total=None):
    m = x.shape[0]
    aliased = xt_all is not None
    n_tiles = m // tm
    total = xt_all.shape[0] if aliased else m_total
    blk0 = (total - m) // tm if aliased else 0
    n_steps = n_tiles if aliased else pl.cdiv(total, tm)
    row = lambda w: pl.BlockSpec((tm, w), lambda i: (jnp.minimum(i, n_tiles - 1), 0))
    res = lambda a: pl.BlockSpec(a.shape, lambda i: (0,) * a.ndim, pipeline_mode=pl.Buffered(1))
    cnt_spec = pl.BlockSpec((1, LANES), lambda i: (0, 0))
    args = [yact, o, gc, ga, x, wc, wa, wo, g1, b1, wr, cnt_in]
    in_specs = [row(C_CONV), row(N_HEADS * HEAD_DIM), row(D_MODEL), row(D_MODEL), row(D_MODEL),
                res(wc), res(wa), res(wo), res(g1), res(b1), res(wr), cnt_spec]
    if aliased:
        args.append(xt_all)
        in_specs.append(pl.BlockSpec(memory_space=pl.ANY))
    return pl.pallas_call(
        functools.partial(_merge_ln1_kernel, aliased=aliased, n_tiles=n_tiles),
        grid=(n_steps,),
        in_specs=in_specs,
        out_specs=(pl.BlockSpec((tm, N_CT, LANES), lambda i: (blk0 + i, 0, 0)), row(D_MODEL), row(LANES),
                   row(LANES), cnt_spec),
        out_shape=(jax.ShapeDtypeStruct((total, N_CT, LANES), BF16),
                   jax.ShapeDtypeStruct((m, D_MODEL), F32),
                   jax.ShapeDtypeStruct((m, LANES), I32),
                   jax.ShapeDtypeStruct((m, LANES), F32),
                   jax.ShapeDtypeStruct((1, LANES), F32)),
        input_output_aliases={len(args) - 1: 0} if aliased else {},
        compiler_params=_cparams(("arbitrary",)),
        name="merge_ln1",
    )(*args)


N_GSLOT = 4


def _experts_kernel(fb_ref, dest_ref, x_hbm, zeros_hbm, wg_ref, wu_ref, wd_ref, y_hbm,
                    rowtok, xbuf, ybuf, wgb, wub, wdb, gsem, ysem, zsem, *, n_blocks):
    e = pl.program_id(0)
    n_used = fb_ref[N_EXPERTS]
    last = jnp.maximum(n_used - 1, 0)

    def start_gather(blk, sl):
        base = blk * MOE_BLK
        for i in range(MOE_BLK):
            pltpu.make_async_copy(x_hbm.at[rowtok[base + i]], xbuf.at[sl, i], gsem.at[sl]).start(priority=i % 2)

    def wait_gather(sl):
        pltpu.make_async_copy(xbuf.at[sl], xbuf.at[sl], gsem.at[sl]).wait()

    def write_copy(blk, sl):
        r0 = pl.multiple_of(blk * MOE_BLK, MOE_BLK)
        return pltpu.make_async_copy(ybuf.at[sl], y_hbm.at[pl.ds(r0, MOE_BLK)], ysem.at[sl])

    @pl.when(e == 0)
    def _():
        clear = pltpu.make_async_copy(zeros_hbm, rowtok, zsem.at[0])
        clear.start()
        clear.wait()

        def place(t, carry):
            rowtok[dest_ref[2 * t]] = t
            rowtok[dest_ref[2 * t + 1]] = t
            return carry
        lax.fori_loop(0, dest_ref.shape[0] // 2, place, 0, unroll=8)
        for d in range(N_GSLOT - 1):
            start_gather(jnp.minimum(d, last), d)

    b0, b1 = fb_ref[e], fb_ref[e + 1]

    @pl.when(b1 > b0)
    def _():
        wgb[...] = wg_ref[0].astype(BF16)
        wub[...] = wu_ref[0].astype(BF16)
        wdb[...] = wd_ref[0].astype(BF16)

    def block(b, carry):
        sl = lax.rem(b, N_GSLOT)
        ysl = lax.rem(b, 2)

        @pl.when(b >= 2)
        def _():
            write_copy(b - 2, ysl).wait()

        wait_gather(sl)
        start_gather(jnp.minimum(b + N_GSLOT - 1, last), lax.rem(b + N_GSLOT - 1, N_GSLOT))
        xb = _from_token_major(xbuf[sl], _tile_perm(False)).astype(BF16)
        hg = jnp.dot(xb, wgb[...], preferred_element_type=F32)
        hu = jnp.dot(xb, wub[...], preferred_element_type=F32)
        h = (hg * jax.nn.sigmoid(hg) * hu).astype(BF16)
        y = jnp.dot(h, wdb[...], preferred_element_type=F32)
        ybuf[ysl] = _to_token_major(y, _tile_perm(True))
        write_copy(b, ysl).start()
        return carry

    lax.fori_loop(b0, b1, block, 0)

    @pl.when(e == pl.num_programs(0) - 1)
    def _():
        for d in range(1, N_GSLOT):
            wait_gather(lax.rem(last + d, N_GSLOT))

        @pl.when(n_used >= 1)
        def _():
            write_copy(last, lax.rem(last, 2)).wait()

        @pl.when(n_used >= 2)
        def _():
            write_copy(last - 1, lax.rem(last - 1, 2)).wait()

        ybuf[0] = jnp.zeros(ybuf.shape[1:], BF16)

        def fill(blk, carry):
            write_copy(blk, 0).start()
            return carry
        lax.fori_loop(n_used, n_blocks, fill, 0)

        def fill_wait(blk, carry):
            write_copy(blk, 0).wait()
            return carry
        lax.fori_loop(n_used, n_blocks, fill_wait, 0)


def _experts(first_blk, dest, xt_all, wg, wu, wd):
    n_blocks = (dest.shape[0] + N_EXPERTS * (MOE_BLK - 1) + MOE_BLK - 1) // MOE_BLK
    n_rows = n_blocks * MOE_BLK
    grid_spec = pltpu.PrefetchScalarGridSpec(
        num_scalar_prefetch=2,
        grid=(N_EXPERTS,),
        in_specs=[pl.BlockSpec(memory_space=pl.ANY),
                  pl.BlockSpec(memory_space=pl.ANY),
                  pl.BlockSpec((1, D_MODEL, D_EXPERT), lambda e, fb, de: (e, 0, 0)),
                  pl.BlockSpec((1, D_MODEL, D_EXPERT), lambda e, fb, de: (e, 0, 0)),
                  pl.BlockSpec((1, D_EXPERT, D_MODEL), lambda e, fb, de: (e, 0, 0))],
        out_specs=pl.BlockSpec(memory_space=pl.ANY),
        scratch_shapes=[pltpu.SMEM((n_rows,), I32),
                        pltpu.VMEM((N_GSLOT, MOE_BLK, N_CT, LANES), BF16),
                        pltpu.VMEM((2, MOE_BLK, N_CT, LANES), BF16),
                        pltpu.VMEM((D_MODEL, D_EXPERT), BF16),
                        pltpu.VMEM((D_MODEL, D_EXPERT), BF16),
                        pltpu.VMEM((D_EXPERT, D_MODEL), BF16),
                        pltpu.SemaphoreType.DMA((N_GSLOT,)),
                        pltpu.SemaphoreType.DMA((2,)),
                        pltpu.SemaphoreType.DMA((1,))],
    )
    return pl.pallas_call(
        functools.partial(_experts_kernel, n_blocks=n_blocks),
        grid_spec=grid_spec,
        out_shape=jax.ShapeDtypeStruct((n_rows, N_CT, LANES), BF16),
        compiler_params=_cparams(("arbitrary",)),
        name="experts",
    )(first_blk, dest, xt_all, jnp.zeros((n_rows,), I32), wg, wu, wd)


def _combine_ln2_kernel(dest_ref, x1_ref, gt_ref, yb_hbm, g_ref, b_ref, y_ref, ybuf, sem, *, tm, tok0):
    i = pl.program_id(0)
    last = pl.num_programs(0) - 1
    slot = lax.rem(i, 2)

    def row_copy(a, sl, k, r):
        return pltpu.make_async_copy(yb_hbm.at[dest_ref[a]], ybuf.at[sl, k, r], sem.at[sl])

    def start_gather(tile, sl, unrolled):
        base = 2 * (tok0 + tile * tm)
        if unrolled:
            for r in range(tm):
                row_copy(base + 2 * r, sl, 0, r).start(priority=0)
                row_copy(base + 2 * r + 1, sl, 1, r).start(priority=1)
        else:
            def body(r, carry):
                row_copy(base + 2 * r, sl, 0, r).start()
                row_copy(base + 2 * r + 1, sl, 1, r).start()
                return carry
            lax.fori_loop(0, tm, body, 0)

    def wait_gather(sl):
        pltpu.make_async_copy(ybuf.at[sl], ybuf.at[sl], sem.at[sl]).wait()

    @pl.when(i == 0)
    def _():
        start_gather(0, 0, unrolled=False)

    wait_gather(slot)
    start_gather(jnp.minimum(i + 1, last), 1 - slot, unrolled=True)
    gt = gt_ref[...]
    perm = _tile_perm(False)
    z = (DEEPNORM_ALPHA * x1_ref[...] + gt[:, 0:1] * _from_token_major(ybuf[slot, 0], perm)
         + gt[:, 1:2] * _from_token_major(ybuf[slot, 1], perm))
    y_ref[...] = _layer_norm(z, g_ref[...], b_ref[...])

    @pl.when(i == last)
    def _():
        wait_gather(1 - slot)


def _combine_ln2(dest, x1, gt, yb, g2, b2, tm, tok0):
    m = x1.shape[0]
    grid_spec = pltpu.PrefetchScalarGridSpec(
        num_scalar_prefetch=1,
        grid=(m // tm,),
        in_specs=[pl.BlockSpec((tm, D_MODEL), lambda i, d: (i, 0)),
                  pl.BlockSpec((tm, LANES), lambda i, d: (i, 0)),
                  pl.BlockSpec(memory_space=pl.ANY),
                  pl.BlockSpec((1, D_MODEL), lambda i, d: (0, 0)),
                  pl.BlockSpec((1, D_MODEL), lambda i, d: (0, 0))],
        out_specs=pl.BlockSpec((tm, D_MODEL), lambda i, d: (i, 0)),
        scratch_shapes=[pltpu.VMEM((2, 2, tm, N_CT, LANES), BF16),
                        pltpu.SemaphoreType.DMA((2,))],
    )
    return pl.pallas_call(
        functools.partial(_combine_ln2_kernel, tm=tm, tok0=tok0),
        grid_spec=grid_spec,
        out_shape=jax.ShapeDtypeStruct((m, D_MODEL), F32),
        compiler_params=_cparams(("arbitrary",)),
        name="combine_ln2",
    )(dest, x1, gt, yb, g2, b2)


def _dispatch_tables(eid, rank, counts):
    padded = (counts + MOE_BLK - 1) // MOE_BLK * MOE_BLK
    pad_end = jnp.cumsum(padded)
    pad_start = pad_end - padded
    start_of = jnp.sum(jnp.where(eid[..., None] == jnp.arange(N_EXPERTS, dtype=I32), pad_start, 0), axis=-1)
    dest = (start_of + rank).reshape(-1).astype(I32)
    first_blk = (jnp.concatenate([pad_start, pad_end[-1:]]) // MOE_BLK).astype(I32)
    return dest, first_blk


def kernel(x_prompt, x_sample, cache_k, cache_v, cache_kidx, state_conv, page_table, w_in, conv_dw_w, conv_dw_b,
           conv_ln_g, conv_ln_b, w_conv_out, w_attn_out, w_out, ln1_g, ln1_b, w_router_group, w_router_expert,
           w_expert_gate, w_expert_up, w_expert_down, ln2_g, ln2_b):
    assert w_in.shape[0] == 1, "single-layer trunk"
    nb, t, d = x_prompt.shape
    ns = x_sample.shape[0]
    n_pool = cache_k.shape[1]
    mp = nb * t
    row2 = lambda a: a.reshape(1, -1)

    w_arr = _arrange_w_in(w_in[0])
    dww, dwb, lng, lnb = conv_dw_w[0], row2(conv_dw_b[0]), row2(conv_ln_g[0]), row2(conv_ln_b[0])
    wc, wa, wo = w_conv_out[0].astype(BF16), w_attn_out[0].astype(BF16), w_out[0].astype(BF16)
    g1, b1, g2, b2 = row2(ln1_g[0]), row2(ln1_b[0]), row2(ln2_g[0]), row2(ln2_b[0])
    wr32 = jnp.concatenate([w_router_group[0], w_router_expert[0],
                            jnp.zeros((d, LANES - N_GROUPS - N_EXPERTS), F32)], axis=1)
    wr_hi = wr32.astype(BF16)
    wr = jnp.stack([wr_hi, (wr32 - wr_hi.astype(F32)).astype(BF16)])

    xp = x_prompt.reshape(mp, d)
    u_p, q_p, qi_p, _, kvb_p, gc_p, ga_p, kiwi_p, kib_p, yact_p, kt_p, vt_p, kit_p = _in_proj(
        xp, w_arr, tm=512, conv=(dww, dwb, lng, lnb, t))
    o_p = _attn_prompt(qi_p.reshape(nb, t, -1), kiwi_p.reshape(nb, t, LANES), kib_p.reshape(nb, t, LANES),
                       q_p.reshape(nb, t, -1), kvb_p.reshape(nb, t, 2 * KV_DIM))
    xt_all, x1_p, ei_p, gt_p, cnt_p = _merge_ln1(yact_p, o_p.reshape(mp, -1), gc_p, ga_p, xp,
                                                 wc, wa, wo, g1, b1, wr, jnp.zeros((1, LANES), F32), tm=256,
                                                 m_total=mp + ns)

    xs = x_sample.reshape(ns, d)
    u_s, q_s, qi_s, kv_s, _, gc_s, ga_s, kiwi_s, _ = _in_proj(xs, w_arr, tm=ns)
    yact_s = _conv_sample(state_conv[0].transpose(1, 0, 2), u_s, dww, dwb, lng, lnb)
    cki2 = cache_kidx.transpose(0, 1, 3, 2).reshape(n_pool, IDX_DIM, PAGE_SIZE)
    ck2 = cache_k.transpose(0, 1, 3, 4, 2).reshape(n_pool, KV_DIM, PAGE_SIZE)
    cv2 = cache_v.transpose(0, 1, 3, 4, 2).reshape(n_pool, KV_DIM, PAGE_SIZE)
    scores_s = _sample_scores(page_table, qi_s.reshape(ns, IDX_HEADS, IDX_DIM),
                              kiwi_s[:, IDX_DIM:IDX_DIM + IDX_HEADS].reshape(ns, IDX_HEADS, 1),
                              kiwi_s[:, :IDX_DIM].reshape(ns, 1, IDX_DIM), cki2)
    scores3 = scores_s.reshape(ns, -1, ATT_CK).transpose(1, 0, 2)
    own_group = (np.arange(N_HEADS)[:, None] // HEADS_PER_KV) == (np.arange(KV_DIM)[None, :] // HEAD_DIM)
    qbd = jnp.where(own_group[None], jnp.tile(q_s.reshape(ns, N_HEADS, HEAD_DIM), (1, 1, KV_HEADS)), 0).astype(BF16)
    o3 = _sample_attn(page_table, scores3, qbd, kv_s[:, :KV_DIM].reshape(ns, 1, KV_DIM),
                      kv_s[:, KV_DIM:].reshape(ns, 1, KV_DIM), ck2, cv2)
    o_s = o3.reshape(ns, N_HEADS * HEAD_DIM).astype(BF16)
    xt_all, x1_s, ei_s, gt_s, cnt = _merge_ln1(yact_s, o_s, gc_s, ga_s, xs, wc, wa, wo, g1, b1, wr, cnt_p, tm=ns,
                                               xt_all=xt_all)

    eid_rank = jnp.concatenate([ei_p[:, :4], ei_s[:, :4]], axis=0)
    dest, first_blk = _dispatch_tables(eid_rank[:, :2], eid_rank[:, 2:], cnt[0, :N_EXPERTS].astype(I32))
    yb = _experts(first_blk, dest, xt_all, w_expert_gate[0], w_expert_up[0], w_expert_down[0])
    y_p = _combine_ln2(dest, x1_p, gt_p, yb, g2, b2, tm=256, tok0=0)
    y_s = _combine_ln2(dest, x1_s, gt_s, yb, g2, b2, tm=ns, tok0=mp)

    y_prompt = y_p.reshape(nb, t, d)
    y_sample = y_s.reshape(ns, 1, d)
    k_prompt = kt_p.reshape(nb, KV_HEADS, HEAD_DIM, t).transpose(0, 3, 1, 2)[None]
    v_prompt = vt_p.reshape(nb, KV_HEADS, HEAD_DIM, t).transpose(0, 3, 1, 2)[None]
    kidx_prompt = kit_p.transpose(0, 2, 1)[None]
    conv_prompt = u_p.reshape(nb, t, C_CONV)[:, t - (CONV_W - 1):][None]
    k_sample = kv_s[:, :KV_DIM].reshape(1, ns, 1, KV_HEADS, HEAD_DIM)
    v_sample = kv_s[:, KV_DIM:].reshape(1, ns, 1, KV_HEADS, HEAD_DIM)
    kidx_sample = kiwi_s[:, :IDX_DIM].reshape(1, ns, 1, IDX_DIM)
    conv_sample = jnp.concatenate([state_conv[0][:, 1:], u_s[:, None, :]], axis=1)[None]
    return (y_prompt, y_sample, k_prompt, v_prompt, kidx_prompt, conv_prompt, k_sample, v_sample, kidx_sample,
            conv_sample)
```
